```python
import jax, jax.numpy as jnp
from jax import lax
import numpy as np

D_MODEL = 1024
BATCH = 8
SEQ = 8192
DEPTH = 2

PLE_DIM = 256
D_CONF = D_MODEL // 2
D_SC = D_MODEL // 2
N_GROUPS_CONF = 8
N_GROUPS_SC = 8
CONF_KERNEL = 31
SC_KERNEL = 3
FFN_KERNEL = 3
D_FF = 2816
EPS = 1e-6

W_IN_COLS = 2 * D_CONF + 3 * D_SC + 2 * D_MODEL

kernel_name = "hybrid_conformer_shortconv_gated_merge"


def rmsnorm(x, g):
    xf = x.astype(jnp.float32)
    y = xf * lax.rsqrt(jnp.mean(xf * xf, axis=-1, keepdims=True) + EPS)
    return (y * g.astype(jnp.float32)).astype(x.dtype)


def layernorm(x, g, b):
    xf = x.astype(jnp.float32)
    mu = jnp.mean(xf, axis=-1, keepdims=True)
    var = jnp.mean(jnp.square(xf - mu), axis=-1, keepdims=True)
    y = (xf - mu) * lax.rsqrt(var + EPS)
    return (y * g.astype(jnp.float32) + b.astype(jnp.float32)).astype(x.dtype)


def causal_dwconv(u, w):
    k, c = w.shape
    return lax.conv_general_dilated(
        u, w[:, None, :].astype(u.dtype),
        window_strides=(1,), padding=[(k - 1, 0)],
        dimension_numbers=("NWC", "WIO", "NWC"),
        feature_group_count=c)


def _fwd_setup_inputs(seed: int = 0) -> dict:
    key = jax.random.key(seed)
    ks = jax.random.split(key, 24)
    f32 = jnp.float32

    def dense(k, shape, fan_in):
        return jax.random.normal(k, shape, f32) * (fan_in ** -0.5)

    def gain(k, shape):
        return 1.0 + 0.05 * jax.random.normal(k, shape, f32)

    def small(k, shape):
        return 0.02 * jax.random.normal(k, shape, f32)

    L = DEPTH
    return {
        "x": jax.random.normal(ks[0], (BATCH, SEQ, D_MODEL), f32),
        "p": jax.random.normal(ks[1], (DEPTH, BATCH, SEQ, PLE_DIM), f32),
        "g_mix": gain(ks[2], (L, D_MODEL)),
        "w_in": dense(ks[3], (L, D_MODEL, W_IN_COLS), D_MODEL),
        "b_gate": small(ks[4], (L, 2 * D_MODEL)),
        "conv_a_w": dense(ks[5], (L, CONF_KERNEL, D_CONF), CONF_KERNEL),
        "conv_a_b": small(ks[6], (L, D_CONF)),
        "ln_a_g": gain(ks[7], (L, D_CONF)),
        "ln_a_b": small(ks[8], (L, D_CONF)),
        "w_a_out": dense(ks[9], (L, D_CONF, D_MODEL), D_CONF),
        "conv_b_w": dense(ks[10], (L, SC_KERNEL, D_SC), SC_KERNEL),
        "w_b_out": dense(ks[11], (L, D_SC, D_MODEL), D_SC),
        "w_o": dense(ks[12], (L, D_MODEL, D_MODEL), D_MODEL),
        "g_ffn": gain(ks[13], (L, D_MODEL)),
        "w_up": dense(ks[14], (L, D_MODEL, 2 * D_FF), D_MODEL),
        "conv_f_w": dense(ks[15], (L, FFN_KERNEL, D_FF), FFN_KERNEL),
        "conv_f_b": small(ks[16], (L, D_FF)),
        "w_down": dense(ks[17], (L, D_FF, D_MODEL), D_FF),
        "g_ple": gain(ks[18], (L, D_MODEL)),
        "w_ple": dense(ks[19], (L, PLE_DIM, D_MODEL), PLE_DIM),
        "w_ple_gate": dense(ks[20], (L, D_MODEL, D_MODEL), D_MODEL),
        "g_final": gain(ks[21], (D_MODEL,)),
    }


def _fwd_reference(x, p, g_mix, w_in, b_gate, conv_a_w, conv_a_b, ln_a_g, ln_a_b, w_a_out,
              conv_b_w, w_b_out, w_o, g_ffn, w_up, conv_f_w, conv_f_b, w_down,
              g_ple, w_ple, w_ple_gate, g_final):
    o1 = D_CONF
    o2 = o1 + D_CONF
    o3 = o2 + D_SC
    o4 = o3 + D_SC
    o5 = o4 + D_SC
    o6 = o5 + D_MODEL
    for i in range(DEPTH):
        h = rmsnorm(x, g_mix[i])
        z = jnp.einsum("bsd,dn->bsn", h, w_in[i])
        a_val, a_gt = z[..., :o1], z[..., o1:o2]
        sc_b, sc_c, sc_v = z[..., o2:o3], z[..., o3:o4], z[..., o4:o5]
        gate_logits = z[..., o5:] + b_gate[i]
        g_a = jax.nn.sigmoid(gate_logits[..., :D_MODEL])
        g_b = jax.nn.sigmoid(gate_logits[..., D_MODEL:])

        a = a_val * jax.nn.sigmoid(a_gt)
        a = causal_dwconv(a, conv_a_w[i]) + conv_a_b[i]
        a = jax.nn.silu(layernorm(a, ln_a_g[i], ln_a_b[i]))
        y_a = jnp.einsum("bsc,cd->bsd", a, w_a_out[i])

        s = sc_b * causal_dwconv(sc_c * sc_v, conv_b_w[i])
        y_b = jnp.einsum("bsc,cd->bsd", s, w_b_out[i])

        x = x + jnp.einsum("bsd,de->bse", g_a * y_a + g_b * y_b, w_o[i])

        h = rmsnorm(x, g_ffn[i])
        u = jnp.einsum("bsd,df->bsf", h, w_up[i])
        f_gate = causal_dwconv(u[..., :D_FF], conv_f_w[i]) + conv_f_b[i]
        f = jax.nn.gelu(f_gate, approximate=True) * u[..., D_FF:]
        x = x + jnp.einsum("bsf,fd->bsd", f, w_down[i])

        pg = jax.nn.sigmoid(jnp.einsum("bsd,de->bse", rmsnorm(x, g_ple[i]), w_ple_gate[i]))
        x = x + pg * jnp.einsum("bsk,kd->bsd", p[i], w_ple[i])

    return rmsnorm(x, g_final)


import jax as _jax
import jax.numpy as _jnp

TWIN_FORMAT = 'train_step'
FWD_PARAMS = ['x', 'p', 'g_mix', 'w_in', 'b_gate', 'conv_a_w', 'conv_a_b', 'ln_a_g', 'ln_a_b', 'w_a_out', 'conv_b_w', 'w_b_out', 'w_o', 'g_ffn', 'w_up', 'conv_f_w', 'conv_f_b', 'w_down', 'g_ple', 'w_ple', 'w_ple_gate', 'g_final']
TWIN_WEIGHTS = ['g_mix', 'w_in', 'b_gate', 'conv_a_w', 'conv_a_b', 'ln_a_g', 'ln_a_b', 'w_a_out', 'conv_b_w', 'w_b_out', 'w_o', 'g_ffn', 'w_up', 'conv_f_w', 'conv_f_b', 'w_down', 'g_ple', 'w_ple', 'w_ple_gate', 'g_final']
TWIN_DIFF_INPUT = 'x'
TWIN_INPUTS = ['x', 'p', 'g_mix', 'w_in', 'b_gate', 'conv_a_w', 'conv_a_b', 'ln_a_g', 'ln_a_b', 'w_a_out', 'conv_b_w', 'w_b_out', 'w_o', 'g_ffn', 'w_up', 'conv_f_w', 'conv_f_b', 'w_down', 'g_ple', 'w_ple', 'w_ple_gate', 'g_final', 'loss_target', 'm_g_mix', 'm_w_in', 'm_b_gate', 'm_conv_a_w', 'm_conv_a_b', 'm_ln_a_g', 'm_ln_a_b', 'm_w_a_out', 'm_conv_b_w', 'm_w_b_out', 'm_w_o', 'm_g_ffn', 'm_w_up', 'm_conv_f_w', 'm_conv_f_b', 'm_w_down', 'm_g_ple', 'm_w_ple', 'm_w_ple_gate', 'm_g_final', 'v_g_mix', 'v_w_in', 'v_b_gate', 'v_conv_a_w', 'v_conv_a_b', 'v_ln_a_g', 'v_ln_a_b', 'v_w_a_out', 'v_conv_b_w', 'v_w_b_out', 'v_w_o', 'v_g_ffn', 'v_w_up', 'v_conv_f_w', 'v_conv_f_b', 'v_w_down', 'v_g_ple', 'v_w_ple', 'v_w_ple_gate', 'v_g_final']
TWIN_OUTPUTS = ['loss', 'grad_x', 'grad_g_mix', 'grad_w_in', 'grad_b_gate', 'grad_conv_a_w', 'grad_conv_a_b', 'grad_ln_a_g', 'grad_ln_a_b', 'grad_w_a_out', 'grad_conv_b_w', 'grad_w_b_out', 'grad_w_o', 'grad_g_ffn', 'grad_w_up', 'grad_conv_f_w', 'grad_conv_f_b', 'grad_w_down', 'grad_g_ple', 'grad_w_ple', 'grad_w_ple_gate', 'grad_g_final', 'delta_g_mix', 'delta_w_in', 'delta_b_gate', 'delta_conv_a_w', 'delta_conv_a_b', 'delta_ln_a_g', 'delta_ln_a_b', 'delta_w_a_out', 'delta_conv_b_w', 'delta_w_b_out', 'delta_w_o', 'delta_g_ffn', 'delta_w_up', 'delta_conv_f_w', 'delta_conv_f_b', 'delta_w_down', 'delta_g_ple', 'delta_w_ple', 'delta_w_ple_gate', 'delta_g_final', 'new_m_g_mix', 'new_m_w_in', 'new_m_b_gate', 'new_m_conv_a_w', 'new_m_conv_a_b', 'new_m_ln_a_g', 'new_m_ln_a_b', 'new_m_w_a_out', 'new_m_conv_b_w', 'new_m_w_b_out', 'new_m_w_o', 'new_m_g_ffn', 'new_m_w_up', 'new_m_conv_f_w', 'new_m_conv_f_b', 'new_m_w_down', 'new_m_g_ple', 'new_m_w_ple', 'new_m_w_ple_gate', 'new_m_g_final', 'new_v_g_mix', 'new_v_w_in', 'new_v_b_gate', 'new_v_conv_a_w', 'new_v_conv_a_b', 'new_v_ln_a_g', 'new_v_ln_a_b', 'new_v_w_a_out', 'new_v_conv_b_w', 'new_v_w_b_out', 'new_v_w_o', 'new_v_g_ffn', 'new_v_w_up', 'new_v_conv_f_w', 'new_v_conv_f_b', 'new_v_w_down', 'new_v_g_ple', 'new_v_w_ple', 'new_v_w_ple_gate', 'new_v_g_final']
TWIN_LEAF_KINDS = {'loss': 'loss', 'grad_x': 'grad_x', 'grad_g_mix': 'grad_w', 'grad_w_in': 'grad_w', 'grad_b_gate': 'grad_w', 'grad_conv_a_w': 'grad_w', 'grad_conv_a_b': 'grad_w', 'grad_ln_a_g': 'grad_w', 'grad_ln_a_b': 'grad_w', 'grad_w_a_out': 'grad_w', 'grad_conv_b_w': 'grad_w', 'grad_w_b_out': 'grad_w', 'grad_w_o': 'grad_w', 'grad_g_ffn': 'grad_w', 'grad_w_up': 'grad_w', 'grad_conv_f_w': 'grad_w', 'grad_conv_f_b': 'grad_w', 'grad_w_down': 'grad_w', 'grad_g_ple': 'grad_w', 'grad_w_ple': 'grad_w', 'grad_w_ple_gate': 'grad_w', 'grad_g_final': 'grad_w', 'delta_g_mix': 'delta_w', 'delta_w_in': 'delta_w', 'delta_b_gate': 'delta_w', 'delta_conv_a_w': 'delta_w', 'delta_conv_a_b': 'delta_w', 'delta_ln_a_g': 'delta_w', 'delta_ln_a_b': 'delta_w', 'delta_w_a_out': 'delta_w', 'delta_conv_b_w': 'delta_w', 'delta_w_b_out': 'delta_w', 'delta_w_o': 'delta_w', 'delta_g_ffn': 'delta_w', 'delta_w_up': 'delta_w', 'delta_conv_f_w': 'delta_w', 'delta_conv_f_b': 'delta_w', 'delta_w_down': 'delta_w', 'delta_g_ple': 'delta_w', 'delta_w_ple': 'delta_w', 'delta_w_ple_gate': 'delta_w', 'delta_g_final': 'delta_w', 'new_m_g_mix': 'new_m', 'new_m_w_in': 'new_m', 'new_m_b_gate': 'new_m', 'new_m_conv_a_w': 'new_m', 'new_m_conv_a_b': 'new_m', 'new_m_ln_a_g': 'new_m', 'new_m_ln_a_b': 'new_m', 'new_m_w_a_out': 'new_m', 'new_m_conv_b_w': 'new_m', 'new_m_w_b_out': 'new_m', 'new_m_w_o': 'new_m', 'new_m_g_ffn': 'new_m', 'new_m_w_up': 'new_m', 'new_m_conv_f_w': 'new_m', 'new_m_conv_f_b': 'new_m', 'new_m_w_down': 'new_m', 'new_m_g_ple': 'new_m', 'new_m_w_ple': 'new_m', 'new_m_w_ple_gate': 'new_m', 'new_m_g_final': 'new_m', 'new_v_g_mix': 'new_v', 'new_v_w_in': 'new_v', 'new_v_b_gate': 'new_v', 'new_v_conv_a_w': 'new_v', 'new_v_conv_a_b': 'new_v', 'new_v_ln_a_g': 'new_v', 'new_v_ln_a_b': 'new_v', 'new_v_w_a_out': 'new_v', 'new_v_conv_b_w': 'new_v', 'new_v_w_b_out': 'new_v', 'new_v_w_o': 'new_v', 'new_v_g_ffn': 'new_v', 'new_v_w_up': 'new_v', 'new_v_conv_f_w': 'new_v', 'new_v_conv_f_b': 'new_v', 'new_v_w_down': 'new_v', 'new_v_g_ple': 'new_v', 'new_v_w_ple': 'new_v', 'new_v_w_ple_gate': 'new_v', 'new_v_g_final': 'new_v'}


def _forward(args):
    return _fwd_reference(*[args[k] for k in FWD_PARAMS])


def _output_shape():
    def fwd():
        inp = _fwd_setup_inputs(0)
        return _fwd_reference(*[inp[k] for k in FWD_PARAMS])
    out = _jax.eval_shape(fwd)
    return out.shape, out.dtype

N_MICROBATCH = 1
ADAM_LR = 0.001
ADAM_B1 = 0.9
ADAM_B2 = 0.999
ADAM_EPS = 1e-08
ADAM_WD = 0.01
ADAM_STEP = 10
PER_EXAMPLE_BATCH_AXIS = {'x': 0, 'p': 1, 'loss_target': 0}
SHARED_INPUTS = []
_WEIGHT_DTYPES = {'g_mix': _jnp.float32, 'w_in': _jnp.float32, 'b_gate': _jnp.float32, 'conv_a_w': _jnp.float32, 'conv_a_b': _jnp.float32, 'ln_a_g': _jnp.float32, 'ln_a_b': _jnp.float32, 'w_a_out': _jnp.float32, 'conv_b_w': _jnp.float32, 'w_b_out': _jnp.float32, 'w_o': _jnp.float32, 'g_ffn': _jnp.float32, 'w_up': _jnp.float32, 'conv_f_w': _jnp.float32, 'conv_f_b': _jnp.float32, 'w_down': _jnp.float32, 'g_ple': _jnp.float32, 'w_ple': _jnp.float32, 'w_ple_gate': _jnp.float32, 'g_final': _jnp.float32}
MOMENT_SCALE = {'g_mix': 2.210647e-01, 'w_in': 1.066545e-01, 'b_gate': 3.927331e-02, 'conv_a_w': 1.061576e-01, 'conv_a_b': 2.211279e-01, 'ln_a_g': 1.285901e-01, 'ln_a_b': 1.366207e-01, 'w_a_out': 7.446974e-02, 'conv_b_w': 1.720447e-01, 'w_b_out': 1.179208e-01, 'w_o': 1.405876e-01, 'g_ffn': 1.756740e-01, 'w_up': 7.187216e-02, 'conv_f_w': 7.312153e-02, 'conv_f_b': 7.379156e-02, 'w_down': 1.186270e-01, 'g_ple': 3.784936e-02, 'w_ple': 9.568957e-02, 'w_ple_gate': 3.780991e-02, 'g_final': 6.411288e+01}


def _to_microbatches(a, axis):
    t = _jnp.moveaxis(a, axis, 0)
    t = t.reshape((N_MICROBATCH, t.shape[0] // N_MICROBATCH) + t.shape[1:])
    return _jnp.moveaxis(t, 1, axis + 1)


def setup_inputs(seed: int = 0) -> dict:
    inp = _fwd_setup_inputs(seed)
    key = _jax.random.fold_in(_jax.random.key(seed), 7919)
    shape, _ = _output_shape()
    out = dict(inp)
    out["loss_target"] = _jax.random.normal(_jax.random.fold_in(key, 0), shape, _jnp.float32)
    for i, name in enumerate(TWIN_WEIGHTS):
        w = inp[name].astype(_jnp.float32)
        if MOMENT_SCALE is None:
            s = _jnp.sqrt(_jnp.mean(_jnp.square(w)) + 1e-30)
        else:
            s = MOMENT_SCALE[name]
        km, kv = _jax.random.split(_jax.random.fold_in(key, i + 1))
        out[name] = w
        out["m_" + name] = s * _jax.random.normal(km, w.shape, _jnp.float32)
        out["v_" + name] = (s * s) * _jax.random.uniform(kv, w.shape, _jnp.float32, 0.5, 1.5)
    if N_MICROBATCH > 1:
        for name, axis in PER_EXAMPLE_BATCH_AXIS.items():
            out[name] = _to_microbatches(out[name], axis)
    return {'x': out['x'], 'p': out['p'], 'g_mix': out['g_mix'], 'w_in': out['w_in'], 'b_gate': out['b_gate'], 'conv_a_w': out['conv_a_w'], 'conv_a_b': out['conv_a_b'], 'ln_a_g': out['ln_a_g'], 'ln_a_b': out['ln_a_b'], 'w_a_out': out['w_a_out'], 'conv_b_w': out['conv_b_w'], 'w_b_out': out['w_b_out'], 'w_o': out['w_o'], 'g_ffn': out['g_ffn'], 'w_up': out['w_up'], 'conv_f_w': out['conv_f_w'], 'conv_f_b': out['conv_f_b'], 'w_down': out['w_down'], 'g_ple': out['g_ple'], 'w_ple': out['w_ple'], 'w_ple_gate': out['w_ple_gate'], 'g_final': out['g_final'], 'loss_target': out['loss_target'], 'm_g_mix': out['m_g_mix'], 'm_w_in': out['m_w_in'], 'm_b_gate': out['m_b_gate'], 'm_conv_a_w': out['m_conv_a_w'], 'm_conv_a_b': out['m_conv_a_b'], 'm_ln_a_g': out['m_ln_a_g'], 'm_ln_a_b': out['m_ln_a_b'], 'm_w_a_out': out['m_w_a_out'], 'm_conv_b_w': out['m_conv_b_w'], 'm_w_b_out': out['m_w_b_out'], 'm_w_o': out['m_w_o'], 'm_g_ffn': out['m_g_ffn'], 'm_w_up': out['m_w_up'], 'm_conv_f_w': out['m_conv_f_w'], 'm_conv_f_b': out['m_conv_f_b'], 'm_w_down': out['m_w_down'], 'm_g_ple': out['m_g_ple'], 'm_w_ple': out['m_w_ple'], 'm_w_ple_gate': out['m_w_ple_gate'], 'm_g_final': out['m_g_final'], 'v_g_mix': out['v_g_mix'], 'v_w_in': out['v_w_in'], 'v_b_gate': out['v_b_gate'], 'v_conv_a_w': out['v_conv_a_w'], 'v_conv_a_b': out['v_conv_a_b'], 'v_ln_a_g': out['v_ln_a_g'], 'v_ln_a_b': out['v_ln_a_b'], 'v_w_a_out': out['v_w_a_out'], 'v_conv_b_w': out['v_conv_b_w'], 'v_w_b_out': out['v_w_b_out'], 'v_w_o': out['v_w_o'], 'v_g_ffn': out['v_g_ffn'], 'v_w_up': out['v_w_up'], 'v_conv_f_w': out['v_conv_f_w'], 'v_conv_f_b': out['v_conv_f_b'], 'v_w_down': out['v_w_down'], 'v_g_ple': out['v_g_ple'], 'v_w_ple': out['v_w_ple'], 'v_w_ple_gate': out['v_w_ple_gate'], 'v_g_final': out['v_g_final']}


def _loss(weights, diff, rest, loss_target):
    with _jax.named_scope("forward"):
        args = {**rest, TWIN_DIFF_INPUT: diff, **{k: w.astype(_WEIGHT_DTYPES[k]) for k, w in weights.items()}}
        y = _forward(args)
    with _jax.named_scope("loss_head"):
        err = _jnp.square(y.astype(_jnp.float32) - loss_target)
        return 0.5 * _jnp.sum(_jnp.mean(err, axis=-1)) if err.ndim else 0.5 * err


def _adamw(w, g, m, v):
    m = ADAM_B1 * m + (1.0 - ADAM_B1) * g
    v = ADAM_B2 * v + (1.0 - ADAM_B2) * _jnp.square(g)
    m_hat = m / (1.0 - ADAM_B1 ** ADAM_STEP)
    v_hat = v / (1.0 - ADAM_B2 ** ADAM_STEP)
    delta = -ADAM_LR * (m_hat / (_jnp.sqrt(v_hat) + ADAM_EPS) + ADAM_WD * w)
    return delta, m, v


def reference(x, p, g_mix, w_in, b_gate, conv_a_w, conv_a_b, ln_a_g, ln_a_b, w_a_out, conv_b_w, w_b_out, w_o, g_ffn, w_up, conv_f_w, conv_f_b, w_down, g_ple, w_ple, w_ple_gate, g_final, loss_target, m_g_mix, m_w_in, m_b_gate, m_conv_a_w, m_conv_a_b, m_ln_a_g, m_ln_a_b, m_w_a_out, m_conv_b_w, m_w_b_out, m_w_o, m_g_ffn, m_w_up, m_conv_f_w, m_conv_f_b, m_w_down, m_g_ple, m_w_ple, m_w_ple_gate, m_g_final, v_g_mix, v_w_in, v_b_gate, v_conv_a_w, v_conv_a_b, v_ln_a_g, v_ln_a_b, v_w_a_out, v_conv_b_w, v_w_b_out, v_w_o, v_g_ffn, v_w_up, v_conv_f_w, v_conv_f_b, v_w_down, v_g_ple, v_w_ple, v_w_ple_gate, v_g_final):
    given = dict(x=x, p=p, g_mix=g_mix, w_in=w_in, b_gate=b_gate, conv_a_w=conv_a_w, conv_a_b=conv_a_b, ln_a_g=ln_a_g, ln_a_b=ln_a_b, w_a_out=w_a_out, conv_b_w=conv_b_w, w_b_out=w_b_out, w_o=w_o, g_ffn=g_ffn, w_up=w_up, conv_f_w=conv_f_w, conv_f_b=conv_f_b, w_down=w_down, g_ple=g_ple, w_ple=w_ple, w_ple_gate=w_ple_gate, g_final=g_final, loss_target=loss_target, m_g_mix=m_g_mix, m_w_in=m_w_in, m_b_gate=m_b_gate, m_conv_a_w=m_conv_a_w, m_conv_a_b=m_conv_a_b, m_ln_a_g=m_ln_a_g, m_ln_a_b=m_ln_a_b, m_w_a_out=m_w_a_out, m_conv_b_w=m_conv_b_w, m_w_b_out=m_w_b_out, m_w_o=m_w_o, m_g_ffn=m_g_ffn, m_w_up=m_w_up, m_conv_f_w=m_conv_f_w, m_conv_f_b=m_conv_f_b, m_w_down=m_w_down, m_g_ple=m_g_ple, m_w_ple=m_w_ple, m_w_ple_gate=m_w_ple_gate, m_g_final=m_g_final, v_g_mix=v_g_mix, v_w_in=v_w_in, v_b_gate=v_b_gate, v_conv_a_w=v_conv_a_w, v_conv_a_b=v_conv_a_b, v_ln_a_g=v_ln_a_g, v_ln_a_b=v_ln_a_b, v_w_a_out=v_w_a_out, v_conv_b_w=v_conv_b_w, v_w_b_out=v_w_b_out, v_w_o=v_w_o, v_g_ffn=v_g_ffn, v_w_up=v_w_up, v_conv_f_w=v_conv_f_w, v_conv_f_b=v_conv_f_b, v_w_down=v_w_down, v_g_ple=v_g_ple, v_w_ple=v_w_ple, v_w_ple_gate=v_w_ple_gate, v_g_final=v_g_final)
    weights = {n: given[n] for n in TWIN_WEIGHTS}
    shared = {n: given[n] for n in SHARED_INPUTS}
    per_example = {n: given[n] for n in ['x', 'p']}
    grad_fn = _jax.value_and_grad(_loss, argnums=(0, 1))

    def one_microbatch(ex, loss_target):
        ex = dict(ex)
        diff = ex.pop(TWIN_DIFF_INPUT)
        return grad_fn(weights, diff, {**shared, **ex}, loss_target)

    if N_MICROBATCH == 1:
        loss, (grad_w, grad_x) = one_microbatch(per_example, given["loss_target"])
    else:
        def body(carry, xs):
            loss_sum, grad_sum = carry
            l_k, (gw_k, gx_k) = one_microbatch(xs[0], xs[1])
            with _jax.named_scope("update"):
                return (loss_sum + l_k, _jax.tree.map(_jnp.add, grad_sum, gw_k)), gx_k

        init = (_jnp.zeros((), _jnp.float32), _jax.tree.map(_jnp.zeros_like, weights))
        (loss, grad_w), grad_x = _jax.lax.scan(body, init, (per_example, given["loss_target"]))
    with _jax.named_scope("update"):
        delta_w, new_m, new_v = {}, {}, {}
        for n in TWIN_WEIGHTS:
            delta_w[n], new_m[n], new_v[n] = _adamw(weights[n], grad_w[n], given["m_" + n], given["v_" + n])
    return (loss, grad_x, *[grad_w[n] for n in TWIN_WEIGHTS], *[delta_w[n] for n in TWIN_WEIGHTS],
            *[new_m[n] for n in TWIN_WEIGHTS], *[new_v[n] for n in TWIN_WEIGHTS])
```

```python
import functools

import jax
import jax.numpy as jnp
from jax import lax
from jax.experimental import pallas as pl
from jax.experimental.pallas import tpu as pltpu

F32 = jnp.float32
BF16 = jnp.bfloat16
MESH = pl.DeviceIdType.MESH

N_DEV = 8
NORM_EPS = 1e-6
HALO = 32
LANES = 128
SUBLANES = 8
VMEM_LIMIT_BYTES = 56 * 2**20

ADAM_LR = 0.001
ADAM_B1 = 0.9
ADAM_B2 = 0.999
ADAM_EPS = 1e-08
ADAM_WD = 0.01
ADAM_STEP = 10

WEIGHT_NAMES = ('g_mix', 'w_in', 'b_gate', 'conv_a_w', 'conv_a_b', 'ln_a_g', 'ln_a_b', 'w_a_out',
                'conv_b_w', 'w_b_out', 'w_o', 'g_ffn', 'w_up', 'conv_f_w', 'conv_f_b', 'w_down',
                'g_ple', 'w_ple', 'w_ple_gate', 'g_final')
BIG_AXIS = {'w_in': 1, 'w_up': 1, 'w_a_out': 2, 'w_b_out': 2, 'w_o': 1, 'w_down': 1, 'w_ple': 2,
            'w_ple_gate': 1}
TRANSPOSED = ('w_in', 'w_up')
CONV_SHARDED = ('conv_a_w', 'conv_b_w', 'conv_f_w')


def _dot(a, b):
    return jnp.dot(a, b, preferred_element_type=F32)


def _dot_nt(a, b):
    return lax.dot_general(a, b, (((1,), (1,)), ((), ())), preferred_element_type=F32)


def _dot_tn(a, b):
    return lax.dot_general(a, b, (((0,), (0,)), ((), ())), preferred_element_type=F32)


def _sigmoid(v):
    return jax.nn.sigmoid(v)


def _token_tile(t, cap=512):
    return cap if (t % cap == 0 and t > 512) else 128


def _chunk(n, limit=512):
    for c in range(limit - limit % LANES, 0, -LANES):
        if n % c == 0:
            return c
    return n


def _row_tile(rows):
    for c in (512, 256, 128, 64, 32, 16, 8):
        if rows % c == 0:
            return c
    return rows


def _params(n_axes=1):
    return pltpu.CompilerParams(dimension_semantics=("arbitrary",) * n_axes,
                                vmem_limit_bytes=VMEM_LIMIT_BYTES)


def _rows(tm, width):
    return pl.BlockSpec((tm, width), lambda i: (i, 0))


def _rows_rev(tm, width, nt):
    return pl.BlockSpec((tm, width), lambda i: (nt - 1 - i, 0))


def _whole(shape):
    nd = len(shape)
    return pl.BlockSpec(tuple(shape), lambda i: (0,) * nd)


def _layer(shape, layer):
    return pl.BlockSpec((None,) + tuple(shape[1:]), lambda i: (layer, 0, 0), pipeline_mode=pl.Buffered(1))


def _sds(shape, dtype):
    return jax.ShapeDtypeStruct(tuple(shape), dtype)


def _rms_stats(xv):
    r = lax.rsqrt(jnp.mean(xv * xv, axis=-1, keepdims=True) + NORM_EPS)
    return xv * r, r


def _rms_bwd(dy, xh, r, g):
    dxh = dy * g
    dx = r * (dxh - xh * jnp.mean(dxh * xh, axis=-1, keepdims=True))
    return dx, jnp.sum(dy * xh, axis=0, keepdims=True)


def _gelu_tanh(v):
    c = 0.7978845608028654
    t = jnp.tanh(c * (v + 0.044715 * v * v * v))
    return 0.5 * v * (1.0 + t), t


def _gelu_tanh_grad(v, t):
    c = 0.7978845608028654
    return 0.5 * (1.0 + t) + 0.5 * v * (1.0 - t * t) * c * (1.0 + 3.0 * 0.044715 * v * v)


def _causal_conv(ext_ref, w_ref, width, tm):
    acc = None
    for k in range(width):
        term = w_ref[k:k + 1, :] * ext_ref[pl.ds(HALO - (width - 1) + k, tm), :]
        acc = term if acc is None else acc + term
    return acc


def _causal_conv_bwd_input(ext_ref, w_ref, width, tm):
    acc = None
    for k in range(width):
        term = w_ref[k:k + 1, :] * ext_ref[pl.ds(width - 1 - k, tm), :]
        acc = term if acc is None else acc + term
    return acc


def _causal_conv_bwd_weight(dw_ref, dy, ext_ref, width, tm):
    for k in range(width):
        prod = dy * ext_ref[pl.ds(HALO - (width - 1) + k, tm), :]
        dw_ref[k:k + 1, :] += jnp.sum(prod, axis=0, keepdims=True)


def _my_index():
    return 4 * lax.axis_index("x") + 2 * lax.axis_index("y") + lax.axis_index("c")


def _mesh_id(idx):
    return (idx // 4, (idx // 2) % 2, idx % 2)


def _slab(ref, axis, idx, width):
    at = [slice(None)] * len(ref.shape)
    at[axis] = pl.ds(pl.multiple_of(idx * width, width), width)
    return ref.at[tuple(at)]


def _norm_matmul(x, g, wt, layer, name):
    t, d = x.shape
    n = wt.shape[1]
    tm, nc = _token_tile(t), _chunk(n)

    def body(x_ref, g_ref, wt_ref, h_ref, o_ref):
        xh, _ = _rms_stats(x_ref[...])
        h = (xh * g_ref[...]).astype(BF16)
        h_ref[...] = h
        for n0 in range(0, n, nc):
            o_ref[:, n0:n0 + nc] = _dot_nt(h, wt_ref[n0:n0 + nc, :]).astype(BF16)

    return pl.pallas_call(
        body, name=name, grid=(t // tm,),
        in_specs=[_rows(tm, d), _whole(g.shape), _layer(wt.shape, layer)],
        out_specs=[_rows(tm, d), _rows(tm, n)],
        out_shape=[_sds((t, d), BF16), _sds((t, n), BF16)],
        compiler_params=_params())(x, g, wt)


def _fwd_branch(z, caw, cab, lng, lnb, cbw, dc, name):
    t = z.shape[0]
    tm = _token_tile(t)
    ka, kb = caw.shape[0], cbw.shape[0]

    def body(z_ref, caw_ref, cab_ref, lng_ref, lnb_ref, cbw_ref, ac_ref, act_ref, s_ref, a_ext, cv_ext):
        @pl.when(pl.program_id(0) == 0)
        def _():
            a_ext[0:HALO, :] = jnp.zeros((HALO, dc), F32)
            cv_ext[0:HALO, :] = jnp.zeros((HALO, dc), F32)

        a_val = z_ref[:, 0:dc].astype(F32)
        a_gt = z_ref[:, dc:2 * dc].astype(F32)
        a_ext[HALO:HALO + tm, :] = a_val * _sigmoid(a_gt)
        ac = _causal_conv(a_ext, caw_ref, ka, tm) + cab_ref[...]
        ac_ref[...] = ac
        mu = jnp.mean(ac, axis=-1, keepdims=True)
        xc = ac - mu
        var = jnp.mean(xc * xc, axis=-1, keepdims=True)
        ln = xc * lax.rsqrt(var + NORM_EPS) * lng_ref[...] + lnb_ref[...]
        act_ref[...] = (ln * _sigmoid(ln)).astype(BF16)
        a_ext[0:HALO, :] = a_ext[tm:tm + HALO, :]

        sc_b = z_ref[:, 2 * dc:3 * dc].astype(F32)
        sc_c = z_ref[:, 3 * dc:4 * dc].astype(F32)
        sc_v = z_ref[:, 4 * dc:5 * dc].astype(F32)
        cv_ext[HALO:HALO + tm, :] = sc_c * sc_v
        s_ref[...] = (sc_b * _causal_conv(cv_ext, cbw_ref, kb, tm)).astype(BF16)
        cv_ext[0:HALO, :] = cv_ext[tm:tm + HALO, :]

    return pl.pallas_call(
        body, name=name, grid=(t // tm,),
        in_specs=[_rows(tm, 5 * dc), _whole(caw.shape), _whole(cab.shape), _whole(lng.shape),
                  _whole(lnb.shape), _whole(cbw.shape)],
        out_specs=[_rows(tm, dc), _rows(tm, dc), _rows(tm, dc)],
        out_shape=[_sds((t, dc), F32), _sds((t, dc), BF16), _sds((t, dc), BF16)],
        scratch_shapes=[pltpu.VMEM((HALO + tm, dc), F32), pltpu.VMEM((HALO + tm, dc), F32)],
        compiler_params=_params())(z, caw, cab, lng, lnb, cbw)


def _fwd_merge(x, z, bg, a_act, s, wa, wb, wo, layer, name):
    t, d = x.shape
    n = z.shape[1]
    dc = a_act.shape[1]
    tm = _token_tile(t)
    o5 = n - 2 * d

    def body(x_ref, z_ref, bg_ref, act_ref, s_ref, wa_ref, wb_ref, wo_ref, o_ref):
        ya = _dot(act_ref[...], wa_ref[...])
        yb = _dot(s_ref[...], wb_ref[...])
        ga = _sigmoid(z_ref[:, o5:o5 + d].astype(F32) + bg_ref[:, 0:d])
        gb = _sigmoid(z_ref[:, o5 + d:n].astype(F32) + bg_ref[:, d:2 * d])
        m = (ga * ya + gb * yb).astype(BF16)
        o_ref[...] = x_ref[...] + _dot(m, wo_ref[...])

    return pl.pallas_call(
        body, name=name, grid=(t // tm,),
        in_specs=[_rows(tm, d), _rows(tm, n), _whole(bg.shape), _rows(tm, dc), _rows(tm, dc),
                  _layer(wa.shape, layer), _layer(wb.shape, layer), _layer(wo.shape, layer)],
        out_specs=_rows(tm, d), out_shape=_sds((t, d), F32),
        compiler_params=_params())(x, z, bg, a_act, s, wa, wb, wo)


def _fwd_down(x, u, cfw, cfb, wd, layer, name):
    t, d = x.shape
    f = u.shape[1] // 2
    tm = _token_tile(t, 256)
    kf = cfw.shape[0]

    def body(x_ref, u_ref, cfw_ref, cfb_ref, wd_ref, o_ref, act_ref, ug_ext):
        @pl.when(pl.program_id(0) == 0)
        def _():
            ug_ext[0:HALO, :] = jnp.zeros((HALO, f), F32)

        ug_ext[HALO:HALO + tm, :] = u_ref[:, 0:f].astype(F32)
        fg = _causal_conv(ug_ext, cfw_ref, kf, tm) + cfb_ref[...]
        gl, _ = _gelu_tanh(fg)
        act = (gl * u_ref[:, f:2 * f].astype(F32)).astype(BF16)
        act_ref[...] = act
        o_ref[...] = x_ref[...] + _dot(act, wd_ref[...])
        ug_ext[0:HALO, :] = ug_ext[tm:tm + HALO, :]

    return pl.pallas_call(
        body, name=name, grid=(t // tm,),
        in_specs=[_rows(tm, d), _rows(tm, 2 * f), _whole(cfw.shape), _whole(cfb.shape),
                  _layer(wd.shape, layer)],
        out_specs=[_rows(tm, d), _rows(tm, f)], out_shape=[_sds((t, d), F32), _sds((t, f), BF16)],
        scratch_shapes=[pltpu.VMEM((HALO + tm, f), F32)],
        compiler_params=_params())(x, u, cfw, cfb, wd)


def _fwd_ple(x, g, wpg, p, wple, layer, name):
    t, d = x.shape
    pd = p.shape[1]
    tm = _token_tile(t)

    def body(x_ref, g_ref, wpg_ref, p_ref, wple_ref, o_ref):
        xv = x_ref[...]
        xh, _ = _rms_stats(xv)
        lg = _dot((xh * g_ref[...]).astype(BF16), wpg_ref[...])
        pp = _dot(p_ref[...].astype(BF16), wple_ref[...])
        o_ref[...] = xv + _sigmoid(lg) * pp

    return pl.pallas_call(
        body, name=name, grid=(t // tm,),
        in_specs=[_rows(tm, d), _whole(g.shape), _layer(wpg.shape, layer), _rows(tm, pd),
                  _layer(wple.shape, layer)],
        out_specs=_rows(tm, d), out_shape=_sds((t, d), F32),
        compiler_params=_params())(x, g, wpg, p, wple)


def _loss_bwd(x, g, target, name):
    t, d = x.shape
    tm = _token_tile(t)

    def body(x_ref, g_ref, t_ref, dx_ref, dg_ref, loss_ref):
        @pl.when(pl.program_id(0) == 0)
        def _():
            dg_ref[...] = jnp.zeros_like(dg_ref)
            loss_ref[...] = jnp.zeros_like(loss_ref)

        xh, r = _rms_stats(x_ref[...])
        err = xh * g_ref[...] - t_ref[...]
        sq = jnp.sum(jnp.sum(err * err, axis=0, keepdims=True), axis=1, keepdims=True)
        loss_ref[...] += jnp.broadcast_to(0.5 * sq / d, loss_ref.shape)
        dx, dg = _rms_bwd(err / d, xh, r, g_ref[...])
        dx_ref[...] = dx
        dg_ref[...] += dg

    return pl.pallas_call(
        body, name=name, grid=(t // tm,),
        in_specs=[_rows(tm, d), _whole(g.shape), _rows(tm, d)],
        out_specs=[_rows(tm, d), _whole((1, d)), _whole((SUBLANES, LANES))],
        out_shape=[_sds((t, d), F32), _sds((1, d), F32), _sds((SUBLANES, LANES), F32)],
        compiler_params=_params())(x, g, target)


def _bwd_ple(dy, x, g, wpg, p, wple, layer, name):
    t, d = x.shape
    pd = p.shape[1]
    tm = _token_tile(t)
    nt = t // tm

    def body(dy_ref, x_ref, g_ref, wpg_ref, p_ref, wple_ref, dx_ref, dwpg_ref, dwple_ref, dg_ref,
             acc_pg, acc_ple):
        i = pl.program_id(0)

        @pl.when(i == 0)
        def _():
            acc_pg[...] = jnp.zeros_like(acc_pg)
            acc_ple[...] = jnp.zeros_like(acc_ple)
            dg_ref[...] = jnp.zeros_like(dg_ref)

        dyv = dy_ref[...]
        xh, r = _rms_stats(x_ref[...])
        h = (xh * g_ref[...]).astype(BF16)
        pb = p_ref[...].astype(BF16)
        pg = _sigmoid(_dot(h, wpg_ref[...]))
        pp = _dot(pb, wple_ref[...])
        dpp = (dyv * pg).astype(BF16)
        dlg = (dyv * pp * pg * (1.0 - pg)).astype(BF16)
        acc_ple[...] += _dot_tn(pb, dpp)
        acc_pg[...] += _dot_tn(h, dlg)
        dx, dg = _rms_bwd(_dot_nt(dlg, wpg_ref[...]), xh, r, g_ref[...])
        dx_ref[...] = dyv + dx
        dg_ref[...] += dg

        @pl.when(i == nt - 1)
        def _():
            dwpg_ref[...] = acc_pg[...].astype(BF16)
            dwple_ref[...] = acc_ple[...].astype(BF16)

    return pl.pallas_call(
        body, name=name, grid=(nt,),
        in_specs=[_rows(tm, d), _rows(tm, d), _whole(g.shape), _layer(wpg.shape, layer), _rows(tm, pd),
                  _layer(wple.shape, layer)],
        out_specs=[_rows(tm, d), _whole((d, d)), _whole((pd, d)), _whole((1, d))],
        out_shape=[_sds((t, d), F32), _sds((d, d), BF16), _sds((pd, d), BF16), _sds((1, d), F32)],
        scratch_shapes=[pltpu.VMEM((d, d), F32), pltpu.VMEM((pd, d), F32)],
        compiler_params=_params())(dy, x, g, wpg, p, wple)


def _bwd_down(dy, u, cfw, cfb, wd, layer, name):
    t, d = dy.shape
    f = u.shape[1] // 2
    tm = _token_tile(t, 256)
    nt = t // tm
    kf = cfw.shape[0]
    per = tm // HALO

    def body(dy_ref, u_ref, up_ref, cfw_ref, cfb_ref, wd_ref, du_ref, dcw_ref, dcb_ref, ug_ext, dfg_ext):
        i = pl.program_id(0)

        @pl.when(i == 0)
        def _():
            dcw_ref[...] = jnp.zeros_like(dcw_ref)
            dcb_ref[...] = jnp.zeros_like(dcb_ref)
            dfg_ext[tm:tm + HALO, :] = jnp.zeros((HALO, f), F32)

        first = (i == nt - 1).astype(F32)
        ug_ext[0:HALO, :] = up_ref[:, 0:f].astype(F32) * (1.0 - first)
        ug_ext[HALO:HALO + tm, :] = u_ref[:, 0:f].astype(F32)
        uv = u_ref[:, f:2 * f].astype(F32)
        df = _dot_nt(dy_ref[...].astype(BF16), wd_ref[...])
        fg = _causal_conv(ug_ext, cfw_ref, kf, tm) + cfb_ref[...]
        gl, th = _gelu_tanh(fg)
        du_ref[:, f:2 * f] = (df * gl).astype(BF16)
        dfg = df * uv * _gelu_tanh_grad(fg, th)
        dcb_ref[...] += jnp.sum(dfg, axis=0, keepdims=True)
        _causal_conv_bwd_weight(dcw_ref, dfg, ug_ext, kf, tm)
        dfg_ext[0:tm, :] = dfg
        du_ref[:, 0:f] = _causal_conv_bwd_input(dfg_ext, cfw_ref, kf, tm).astype(BF16)
        dfg_ext[tm:tm + HALO, :] = dfg_ext[0:HALO, :]

    prev_rows = pl.BlockSpec((HALO, 2 * f), lambda i: (jnp.maximum((nt - 1 - i) * per - 1, 0), 0))
    return pl.pallas_call(
        body, name=name, grid=(nt,),
        in_specs=[_rows_rev(tm, d, nt), _rows_rev(tm, 2 * f, nt), prev_rows, _whole(cfw.shape),
                  _whole(cfb.shape), _layer(wd.shape, layer)],
        out_specs=[_rows_rev(tm, 2 * f, nt), _whole((SUBLANES, f)), _whole((1, f))],
        out_shape=[_sds((t, 2 * f), BF16), _sds((SUBLANES, f), F32), _sds((1, f), F32)],
        scratch_shapes=[pltpu.VMEM((HALO + tm, f), F32), pltpu.VMEM((tm + HALO, f), F32)],
        compiler_params=_params())(dy, u, u, cfw, cfb, wd)


def _bwd_norm_matmul(dout, wt, layer, x, g, dres, name):
    t, d = x.shape
    n = dout.shape[1]
    tm = _token_tile(t)

    def body(do_ref, wt_ref, x_ref, g_ref, dres_ref, dx_ref, dg_ref):
        @pl.when(pl.program_id(0) == 0)
        def _():
            dg_ref[...] = jnp.zeros_like(dg_ref)

        dh = _dot(do_ref[...], wt_ref[...])
        xh, r = _rms_stats(x_ref[...])
        dx, dg = _rms_bwd(dh, xh, r, g_ref[...])
        dx_ref[...] = dres_ref[...] + dx
        dg_ref[...] += dg

    return pl.pallas_call(
        body, name=name, grid=(t // tm,),
        in_specs=[_rows(tm, n), _layer(wt.shape, layer), _rows(tm, d), _whole(g.shape), _rows(tm, d)],
        out_specs=[_rows(tm, d), _whole((1, d))],
        out_shape=[_sds((t, d), F32), _sds((1, d), F32)],
        compiler_params=_params())(dout, wt, x, g, dres)


def _wgrad_tn(a, b, name):
    t, n = a.shape
    d = b.shape[1]
    tt = 1024 if t % 1024 == 0 else _token_tile(t)
    tn = _chunk(n, 1536)
    nt = t // tt

    def body(a_ref, b_ref, o_ref, acc):
        k = pl.program_id(1)

        @pl.when(k == 0)
        def _():
            acc[...] = jnp.zeros_like(acc)

        acc[...] += _dot_tn(a_ref[...].astype(BF16), b_ref[...].astype(BF16))

        @pl.when(k == nt - 1)
        def _():
            o_ref[...] = acc[...].astype(BF16)

    return pl.pallas_call(
        body, name=name, grid=(n // tn, nt),
        in_specs=[pl.BlockSpec((tt, tn), lambda j, k: (k, j)), pl.BlockSpec((tt, d), lambda j, k: (k, 0))],
        out_specs=pl.BlockSpec((tn, d), lambda j, k: (j, 0)),
        out_shape=_sds((n, d), BF16),
        scratch_shapes=[pltpu.VMEM((tn, d), F32)],
        compiler_params=_params(2))(a, b)


def _bwd_merge(dy, z, bg, a_act, s, wa, wb, wo, layer, name):
    t, d = dy.shape
    n = z.shape[1]
    dc = a_act.shape[1]
    tm = _token_tile(t, 256)
    nt = t // tm
    o5 = n - 2 * d

    def body(dy_ref, z_ref, bg_ref, act_ref, s_ref, wa_ref, wb_ref, wo_ref,
             dact_ref, ds_ref, dgl_ref, dwo_ref, dwa_ref, dwb_ref, dbg_ref, acc_o, acc_a, acc_b):
        i = pl.program_id(0)

        @pl.when(i == 0)
        def _():
            acc_o[...] = jnp.zeros_like(acc_o)
            acc_a[...] = jnp.zeros_like(acc_a)
            acc_b[...] = jnp.zeros_like(acc_b)
            dbg_ref[...] = jnp.zeros_like(dbg_ref)

        dyb = dy_ref[...].astype(BF16)
        dm = _dot_nt(dyb, wo_ref[...])
        ya = _dot(act_ref[...], wa_ref[...])
        yb = _dot(s_ref[...], wb_ref[...])
        ga = _sigmoid(z_ref[:, o5:o5 + d].astype(F32) + bg_ref[:, 0:d])
        gb = _sigmoid(z_ref[:, o5 + d:n].astype(F32) + bg_ref[:, d:2 * d])
        acc_o[...] += _dot_tn((ga * ya + gb * yb).astype(BF16), dyb)
        dya = (dm * ga).astype(BF16)
        dyb2 = (dm * gb).astype(BF16)
        acc_a[...] += _dot_tn(act_ref[...], dya)
        acc_b[...] += _dot_tn(s_ref[...], dyb2)
        dact_ref[...] = _dot_nt(dya, wa_ref[...])
        ds_ref[...] = _dot_nt(dyb2, wb_ref[...])
        dla = dm * ya * ga * (1.0 - ga)
        dlb = dm * yb * gb * (1.0 - gb)
        dgl_ref[:, 0:d] = dla.astype(BF16)
        dgl_ref[:, d:2 * d] = dlb.astype(BF16)
        dbg_ref[:, 0:d] += jnp.sum(dla, axis=0, keepdims=True)
        dbg_ref[:, d:2 * d] += jnp.sum(dlb, axis=0, keepdims=True)

        @pl.when(i == nt - 1)
        def _():
            dwo_ref[...] = acc_o[...].astype(BF16)
            dwa_ref[...] = acc_a[...].astype(BF16)
            dwb_ref[...] = acc_b[...].astype(BF16)

    return pl.pallas_call(
        body, name=name, grid=(nt,),
        in_specs=[_rows(tm, d), _rows(tm, n), _whole(bg.shape), _rows(tm, dc), _rows(tm, dc),
                  _layer(wa.shape, layer), _layer(wb.shape, layer), _layer(wo.shape, layer)],
        out_specs=[_rows(tm, dc), _rows(tm, dc), _rows(tm, 2 * d), _whole((d, d)), _whole((dc, d)),
                   _whole((dc, d)), _whole((1, 2 * d))],
        out_shape=[_sds((t, dc), F32), _sds((t, dc), F32), _sds((t, 2 * d), BF16), _sds((d, d), BF16),
                   _sds((dc, d), BF16), _sds((dc, d), BF16), _sds((1, 2 * d), F32)],
        scratch_shapes=[pltpu.VMEM((d, d), F32), pltpu.VMEM((dc, d), F32), pltpu.VMEM((dc, d), F32)],
        compiler_params=_params())(dy, z, bg, a_act, s, wa, wb, wo)


def _bwd_branch(dact, ds, z, dgl, a_conv, caw, lng, lnb, cbw, name):
    t, n = z.shape
    dc = a_conv.shape[1]
    tm = _token_tile(t, 256)
    nt = t // tm
    ka, kb = caw.shape[0], cbw.shape[0]
    per = tm // HALO

    def body(dact_ref, ds_ref, z_ref, zp_ref, dgl_ref, ac_ref, caw_ref, lng_ref, lnb_ref, cbw_ref,
             dz_ref, dcaw_ref, dcab_ref, dlng_ref, dlnb_ref, dcbw_ref,
             a_ext, dac_ext, cv_ext, dcb_ext):
        i = pl.program_id(0)

        @pl.when(i == 0)
        def _():
            for ref in (dcaw_ref, dcab_ref, dlng_ref, dlnb_ref, dcbw_ref):
                ref[...] = jnp.zeros_like(ref)
            dac_ext[tm:tm + HALO, :] = jnp.zeros((HALO, dc), F32)
            dcb_ext[tm:tm + HALO, :] = jnp.zeros((HALO, dc), F32)

        keep = 1.0 - (i == nt - 1).astype(F32)
        a_val = z_ref[:, 0:dc].astype(F32)
        sg = _sigmoid(z_ref[:, dc:2 * dc].astype(F32))
        a_ext[0:HALO, :] = zp_ref[:, 0:dc].astype(F32) * _sigmoid(zp_ref[:, dc:2 * dc].astype(F32)) * keep
        a_ext[HALO:HALO + tm, :] = a_val * sg

        ac = ac_ref[...]
        mu = jnp.mean(ac, axis=-1, keepdims=True)
        xc = ac - mu
        rstd = lax.rsqrt(jnp.mean(xc * xc, axis=-1, keepdims=True) + NORM_EPS)
        xh = xc * rstd
        ln = xh * lng_ref[...] + lnb_ref[...]
        sl = _sigmoid(ln)
        dln = dact_ref[...] * (sl * (1.0 + ln * (1.0 - sl)))
        dlng_ref[...] += jnp.sum(dln * xh, axis=0, keepdims=True)
        dlnb_ref[...] += jnp.sum(dln, axis=0, keepdims=True)
        dxh = dln * lng_ref[...]
        dac = rstd * (dxh - jnp.mean(dxh, axis=-1, keepdims=True)
                      - xh * jnp.mean(dxh * xh, axis=-1, keepdims=True))
        dcab_ref[...] += jnp.sum(dac, axis=0, keepdims=True)
        _causal_conv_bwd_weight(dcaw_ref, dac, a_ext, ka, tm)
        dac_ext[0:tm, :] = dac
        da = _causal_conv_bwd_input(dac_ext, caw_ref, ka, tm)
        dac_ext[tm:tm + HALO, :] = dac_ext[0:HALO, :]
        dz_ref[:, 0:dc] = (da * sg).astype(BF16)
        dz_ref[:, dc:2 * dc] = (da * a_val * sg * (1.0 - sg)).astype(BF16)

        sc_b = z_ref[:, 2 * dc:3 * dc].astype(F32)
        sc_c = z_ref[:, 3 * dc:4 * dc].astype(F32)
        sc_v = z_ref[:, 4 * dc:5 * dc].astype(F32)
        cv_ext[0:HALO, :] = zp_ref[:, 3 * dc:4 * dc].astype(F32) * zp_ref[:, 4 * dc:5 * dc].astype(F32) * keep
        cv_ext[HALO:HALO + tm, :] = sc_c * sc_v
        dsv = ds_ref[...]
        dz_ref[:, 2 * dc:3 * dc] = (dsv * _causal_conv(cv_ext, cbw_ref, kb, tm)).astype(BF16)
        dcb = dsv * sc_b
        _causal_conv_bwd_weight(dcbw_ref, dcb, cv_ext, kb, tm)
        dcb_ext[0:tm, :] = dcb
        dcv = _causal_conv_bwd_input(dcb_ext, cbw_ref, kb, tm)
        dcb_ext[tm:tm + HALO, :] = dcb_ext[0:HALO, :]
        dz_ref[:, 3 * dc:4 * dc] = (dcv * sc_v).astype(BF16)
        dz_ref[:, 4 * dc:5 * dc] = (dcv * sc_c).astype(BF16)
        dz_ref[:, 5 * dc:n] = dgl_ref[...]

    prev_rows = pl.BlockSpec((HALO, 5 * dc), lambda i: (jnp.maximum((nt - 1 - i) * per - 1, 0), 0))
    return pl.pallas_call(
        body, name=name, grid=(nt,),
        in_specs=[_rows_rev(tm, dc, nt), _rows_rev(tm, dc, nt), _rows_rev(tm, 5 * dc, nt), prev_rows,
                  _rows_rev(tm, n - 5 * dc, nt), _rows_rev(tm, dc, nt), _whole(caw.shape), _whole(lng.shape),
                  _whole(lnb.shape), _whole(cbw.shape)],
        out_specs=[_rows_rev(tm, n, nt), _whole((HALO, dc)), _whole((1, dc)), _whole((1, dc)),
                   _whole((1, dc)), _whole((SUBLANES, dc))],
        out_shape=[_sds((t, n), BF16), _sds((HALO, dc), F32), _sds((1, dc), F32), _sds((1, dc), F32),
                   _sds((1, dc), F32), _sds((SUBLANES, dc), F32)],
        scratch_shapes=[pltpu.VMEM((HALO + tm, dc), F32), pltpu.VMEM((tm + HALO, dc), F32),
                        pltpu.VMEM((HALO + tm, dc), F32), pltpu.VMEM((tm + HALO, dc), F32)],
        compiler_params=_params())(dact, ds, z, z, dgl, a_conv, caw, lng, lnb, cbw)


def _all_gather(shards, axes, name):
    n = len(shards)
    widths = [s.shape[a] for s, a in zip(shards, axes)]
    out_shape = []
    for s, a in zip(shards, axes):
        shape = list(s.shape)
        shape[a] *= N_DEV
        out_shape.append(_sds(shape, s.dtype))

    def body(*refs):
        ins, outs = refs[:n], refs[n:2 * n]
        send_sems, recv_sems, local_sems = refs[2 * n:]
        me = _my_index()
        pending = []
        for k in range(n):
            own = pltpu.make_async_copy(ins[k], _slab(outs[k], axes[k], me, widths[k]), local_sems.at[k])
            own.start()
            pending.append(own)
            for dist in range(1, N_DEV):
                pltpu.make_async_remote_copy(
                    src_ref=ins[k], dst_ref=_slab(outs[k], axes[k], me, widths[k]),
                    send_sem=send_sems.at[k, dist - 1], recv_sem=recv_sems.at[k, dist - 1],
                    device_id=_mesh_id((me + dist) % N_DEV), device_id_type=MESH).start()
        for k in range(n):
            for dist in range(1, N_DEV):
                src_dev = (me + N_DEV - dist) % N_DEV
                cp = pltpu.make_async_remote_copy(
                    src_ref=ins[k], dst_ref=_slab(outs[k], axes[k], src_dev, widths[k]),
                    send_sem=send_sems.at[k, dist - 1], recv_sem=recv_sems.at[k, dist - 1],
                    device_id=_mesh_id(src_dev), device_id_type=MESH)
                cp.wait_send()
                cp.wait_recv()
        for own in pending:
            own.wait()

    any_spec = pl.BlockSpec(memory_space=pl.ANY)
    return pl.pallas_call(
        body, name=name, in_specs=[any_spec] * n, out_specs=[any_spec] * n, out_shape=out_shape,
        scratch_shapes=[pltpu.SemaphoreType.DMA((n, N_DEV - 1)), pltpu.SemaphoreType.DMA((n, N_DEV - 1)),
                        pltpu.SemaphoreType.DMA((n,))])(*shards)


def _reduce_scatter_exchange(partials, axes, name):
    n = len(partials)
    widths = [p.shape[a] // N_DEV for p, a in zip(partials, axes)]
    out_shape = []
    for p, a, w in zip(partials, axes, widths):
        shape = list(p.shape)
        shape[a] = w
        out_shape.append(_sds([N_DEV] + shape, p.dtype))

    def body(*refs):
        ins, outs = refs[:n], refs[n:2 * n]
        send_sems, recv_sems, local_sems = refs[2 * n:]
        me = _my_index()
        pending = []
        for k in range(n):
            own = pltpu.make_async_copy(_slab(ins[k], axes[k], me, widths[k]), outs[k].at[me], local_sems.at[k])
            own.start()
            pending.append(own)
            for dist in range(1, N_DEV):
                peer = (me + dist) % N_DEV
                pltpu.make_async_remote_copy(
                    src_ref=_slab(ins[k], axes[k], peer, widths[k]), dst_ref=outs[k].at[me],
                    send_sem=send_sems.at[k, dist - 1], recv_sem=recv_sems.at[k, dist - 1],
                    device_id=_mesh_id(peer), device_id_type=MESH).start()
        for k in range(n):
            for dist in range(1, N_DEV):
                src_dev = (me + N_DEV - dist) % N_DEV
                cp = pltpu.make_async_remote_copy(
                    src_ref=_slab(ins[k], axes[k], src_dev, widths[k]), dst_ref=outs[k].at[src_dev],
                    send_sem=send_sems.at[k, dist - 1], recv_sem=recv_sems.at[k, dist - 1],
                    device_id=_mesh_id(src_dev), device_id_type=MESH)
                cp.wait_send()
                cp.wait_recv()
        for own in pending:
            own.wait()

    any_spec = pl.BlockSpec(memory_space=pl.ANY)
    return pl.pallas_call(
        body, name=name, in_specs=[any_spec] * n, out_specs=[any_spec] * n, out_shape=out_shape,
        scratch_shapes=[pltpu.SemaphoreType.DMA((n, N_DEV - 1)), pltpu.SemaphoreType.DMA((n, N_DEV - 1)),
                        pltpu.SemaphoreType.DMA((n,))])(*partials)


def _all_reduce_small(pack, name):
    rows = pack.shape[0]

    def body(x_ref, o_ref, land, send_sems, recv_sems):
        me = _my_index()
        land[me] = x_ref[...]
        for dist in range(1, N_DEV):
            pltpu.make_async_remote_copy(
                src_ref=x_ref, dst_ref=land.at[me],
                send_sem=send_sems.at[dist - 1], recv_sem=recv_sems.at[dist - 1],
                device_id=_mesh_id((me + dist) % N_DEV), device_id_type=MESH).start()
        for dist in range(1, N_DEV):
            src_dev = (me + N_DEV - dist) % N_DEV
            cp = pltpu.make_async_remote_copy(
                src_ref=x_ref, dst_ref=land.at[src_dev],
                send_sem=send_sems.at[dist - 1], recv_sem=recv_sems.at[dist - 1],
                device_id=_mesh_id(src_dev), device_id_type=MESH)
            cp.wait_send()
            cp.wait_recv()
        acc = land[0]
        for j in range(1, N_DEV):
            acc = acc + land[j]
        o_ref[...] = acc

    vmem = pl.BlockSpec(memory_space=pltpu.VMEM)
    return pl.pallas_call(
        body, name=name, in_specs=[vmem], out_specs=vmem, out_shape=_sds(pack.shape, F32),
        scratch_shapes=[pltpu.VMEM((N_DEV, rows, LANES), F32), pltpu.SemaphoreType.DMA((N_DEV - 1,)),
                        pltpu.SemaphoreType.DMA((N_DEV - 1,))])(pack)


def _sum_slabs(land, name):
    _, rows, cols = land.shape
    tr = _row_tile(rows)

    def body(l_ref, o_ref):
        acc = l_ref[0].astype(F32)
        for j in range(1, N_DEV):
            acc = acc + l_ref[j].astype(F32)
        o_ref[...] = acc

    return pl.pallas_call(
        body, name=name, grid=(rows // tr,),
        in_specs=[pl.BlockSpec((N_DEV, tr, cols), lambda i: (0, i, 0))],
        out_specs=_rows(tr, cols), out_shape=_sds((rows, cols), F32),
        compiler_params=_params())(land)


def _adamw(w, g, m, v, name):
    rows, cols = w.shape
    tr = _row_tile(rows)

    def body(w_ref, g_ref, m_ref, v_ref, d_ref, nm_ref, nv_ref):
        gv = g_ref[...]
        nm = ADAM_B1 * m_ref[...] + (1.0 - ADAM_B1) * gv
        nv = ADAM_B2 * v_ref[...] + (1.0 - ADAM_B2) * (gv * gv)
        m_hat = nm / (1.0 - ADAM_B1 ** ADAM_STEP)
        v_hat = nv / (1.0 - ADAM_B2 ** ADAM_STEP)
        d_ref[...] = -ADAM_LR * (m_hat / (jnp.sqrt(v_hat) + ADAM_EPS) + ADAM_WD * w_ref[...])
        nm_ref[...] = nm
        nv_ref[...] = nv

    spec = _rows(tr, cols)
    return pl.pallas_call(
        body, name=name, grid=(rows // tr,),
        in_specs=[spec] * 4, out_specs=[spec] * 3, out_shape=[_sds((rows, cols), F32)] * 3,
        compiler_params=_params())(w, g, m, v)


def _pack(arrays):
    parts, meta, row = [], [], 0
    tile = SUBLANES * LANES
    for a in arrays:
        size = a.size
        padded = -(-size // tile) * tile
        flat = jnp.pad(a.reshape(-1).astype(F32), (0, padded - size))
        parts.append(flat.reshape(padded // LANES, LANES))
        meta.append((row, size, a.shape))
        row += padded // LANES
    return jnp.concatenate(parts, axis=0), meta


def _unpack(pack, meta):
    out = []
    for row, size, shape in meta:
        rows = -(-size // LANES)
        out.append(pack[row:row + rows].reshape(-1)[:size].reshape(shape))
    return out


def kernel(x, p, g_mix, w_in, b_gate, conv_a_w, conv_a_b, ln_a_g, ln_a_b, w_a_out, conv_b_w, w_b_out, w_o, g_ffn, w_up, conv_f_w, conv_f_b, w_down, g_ple, w_ple, w_ple_gate, g_final, loss_target, m_g_mix, m_w_in, m_b_gate, m_conv_a_w, m_conv_a_b, m_ln_a_g, m_ln_a_b, m_w_a_out, m_conv_b_w, m_w_b_out, m_w_o, m_g_ffn, m_w_up, m_conv_f_w, m_conv_f_b, m_w_down, m_g_ple, m_w_ple, m_w_ple_gate, m_g_final, v_g_mix, v_w_in, v_b_gate, v_conv_a_w, v_conv_a_b, v_ln_a_g, v_ln_a_b, v_w_a_out, v_conv_b_w, v_w_b_out, v_w_o, v_g_ffn, v_w_up, v_conv_f_w, v_conv_f_b, v_w_down, v_g_ple, v_w_ple, v_w_ple_gate, v_g_final):
    w = dict(zip(WEIGHT_NAMES, (g_mix, w_in, b_gate, conv_a_w, conv_a_b, ln_a_g, ln_a_b, w_a_out, conv_b_w,
                                w_b_out, w_o, g_ffn, w_up, conv_f_w, conv_f_b, w_down, g_ple, w_ple,
                                w_ple_gate, g_final)))
    mom = dict(zip(WEIGHT_NAMES, (m_g_mix, m_w_in, m_b_gate, m_conv_a_w, m_conv_a_b, m_ln_a_g, m_ln_a_b,
                                  m_w_a_out, m_conv_b_w, m_w_b_out, m_w_o, m_g_ffn, m_w_up, m_conv_f_w,
                                  m_conv_f_b, m_w_down, m_g_ple, m_w_ple, m_w_ple_gate, m_g_final)))
    var = dict(zip(WEIGHT_NAMES, (v_g_mix, v_w_in, v_b_gate, v_conv_a_w, v_conv_a_b, v_ln_a_g, v_ln_a_b,
                                  v_w_a_out, v_conv_b_w, v_w_b_out, v_w_o, v_g_ffn, v_w_up, v_conv_f_w,
                                  v_conv_f_b, v_w_down, v_g_ple, v_w_ple, v_w_ple_gate, v_g_final)))
    depth = g_mix.shape[0]
    dc = ln_a_g.shape[1]
    me = _my_index()
    x0 = x[0]
    target = loss_target[0]

    big_names = tuple(BIG_AXIS)
    shards = []
    for name in big_names:
        a = jnp.swapaxes(w[name], 1, 2) if name in TRANSPOSED else w[name]
        shards.append(a.astype(BF16))
    shards += [w[name][None] for name in CONV_SHARDED]
    axes = [BIG_AXIS[name] for name in big_names] + [0] * len(CONV_SHARDED)
    gathered = _all_gather(shards, axes, "all_gather_weights")
    full = dict(zip(big_names, gathered[:len(big_names)]))
    conv_full = {}
    for name, g in zip(CONV_SHARDED, gathered[len(big_names):]):
        conv_full[name] = jnp.transpose(g, (1, 2, 0, 3)).reshape(g.shape[1], g.shape[2], -1)

    def row(name, layer):
        return w[name][layer][None]

    saved = []
    xc = x0
    for l in range(depth):
        h, z = _norm_matmul(xc, row('g_mix', l), full['w_in'], l, f"fwd_in_{l}")
        a_conv, a_act, s = _fwd_branch(z, conv_full['conv_a_w'][l], row('conv_a_b', l), row('ln_a_g', l),
                                       row('ln_a_b', l), conv_full['conv_b_w'][l], dc, f"fwd_branch_{l}")
        x1 = _fwd_merge(xc, z, row('b_gate', l), a_act, s, full['w_a_out'], full['w_b_out'], full['w_o'], l,
                        f"fwd_merge_{l}")
        h2, u = _norm_matmul(x1, row('g_ffn', l), full['w_up'], l, f"fwd_up_{l}")
        x2, act = _fwd_down(x1, u, conv_full['conv_f_w'][l], row('conv_f_b', l), full['w_down'], l,
                            f"fwd_down_{l}")
        x3 = _fwd_ple(x2, row('g_ple', l), full['w_ple_gate'], p[l, 0], full['w_ple'], l, f"fwd_ple_{l}")
        saved.append((xc, h, z, a_conv, a_act, s, x1, h2, u, act, x2))
        xc = x3

    dx, dg_final, loss_part = _loss_bwd(xc, g_final[None], target, "loss_bwd")
    big_grads = {name: [None] * depth for name in big_names}
    small = {name: [None] * depth for name in WEIGHT_NAMES if name not in BIG_AXIS and name != 'g_final'}
    for l in reversed(range(depth)):
        xin, h, z, a_conv, a_act, s, x1, h2, u, act, x2 = saved[l]
        dx2, big_grads['w_ple_gate'][l], big_grads['w_ple'][l], small['g_ple'][l] = _bwd_ple(
            dx, x2, row('g_ple', l), full['w_ple_gate'], p[l, 0], full['w_ple'], l, f"bwd_ple_{l}")
        big_grads['w_down'][l] = _wgrad_tn(act, dx2, f"wgrad_down_{l}")
        du, dcfw, small['conv_f_b'][l] = _bwd_down(
            dx2, u, conv_full['conv_f_w'][l], row('conv_f_b', l), full['w_down'], l, f"bwd_down_{l}")
        small['conv_f_w'][l] = dcfw[:conv_f_w.shape[1]]
        dx1, small['g_ffn'][l] = _bwd_norm_matmul(du, full['w_up'], l, x1, row('g_ffn', l), dx2, f"bwd_up_{l}")
        big_grads['w_up'][l] = _wgrad_tn(du, h2, f"wgrad_up_{l}")
        (dact, ds, dgl, big_grads['w_o'][l], big_grads['w_a_out'][l], big_grads['w_b_out'][l],
         small['b_gate'][l]) = _bwd_merge(dx1, z, row('b_gate', l), a_act, s, full['w_a_out'], full['w_b_out'],
                                          full['w_o'], l, f"bwd_merge_{l}")
        dz, dcaw, small['conv_a_b'][l], small['ln_a_g'][l], small['ln_a_b'][l], dcbw = _bwd_branch(
            dact, ds, z, dgl, a_conv, conv_full['conv_a_w'][l], row('ln_a_g', l), row('ln_a_b', l),
            conv_full['conv_b_w'][l], f"bwd_branch_{l}")
        small['conv_a_w'][l] = dcaw[:conv_a_w.shape[1]]
        small['conv_b_w'][l] = dcbw[:conv_b_w.shape[1]]
        dx, small['g_mix'][l] = _bwd_norm_matmul(dz, full['w_in'], l, xin, row('g_mix', l), dx1, f"bwd_in_{l}")
        big_grads['w_in'][l] = _wgrad_tn(dz, h, f"wgrad_in_{l}")
    grad_x = dx[None]

    small_names = tuple(small)
    small_parts = [jnp.stack([part.reshape(part.shape[-2:]) if part.shape[0] != 1 else part[0]
                              for part in small[name]]) for name in small_names]
    pack, meta = _pack(small_parts + [dg_final[0], loss_part])
    reduced = _unpack(_all_reduce_small(pack, "all_reduce_small"), meta)
    loss = reduced[-1][0, 0]
    grads = dict(zip(small_names, reduced[:len(small_names)]))
    grads['g_final'] = reduced[len(small_names)]
    for name in CONV_SHARDED:
        width = w[name].shape[-1]
        grads[name] = lax.dynamic_slice_in_dim(grads[name], me * width, width, axis=2)

    partials = [jnp.stack(big_grads[name]) for name in big_names]
    landed = _reduce_scatter_exchange(partials, [BIG_AXIS[name] for name in big_names], "reduce_scatter_grads")
    for name, land in zip(big_names, landed):
        slab_shape = land.shape[1:]
        total = _sum_slabs(land.reshape(N_DEV, -1, slab_shape[-1]), f"sum_{name}").reshape(slab_shape)
        grads[name] = jnp.swapaxes(total, 1, 2) if name in TRANSPOSED else total

    delta, new_m, new_v = {}, {}, {}
    for name in big_names:
        shape = w[name].shape
        flat = lambda a: a.reshape(-1, shape[-1])
        d_, m_, v_ = _adamw(flat(w[name]), flat(grads[name]), flat(mom[name]), flat(var[name]), f"adamw_{name}")
        delta[name], new_m[name], new_v[name] = d_.reshape(shape), m_.reshape(shape), v_.reshape(shape)
    rest = tuple(name for name in WEIGHT_NAMES if name not in BIG_AXIS)
    packs = [_pack([src[name] for name in rest]) for src in (w, grads, mom, var)]
    outs = _adamw(packs[0][0], packs[1][0], packs[2][0], packs[3][0], "adamw_small")
    for dst, packed in zip((delta, new_m, new_v), outs):
        dst.update(zip(rest, _unpack(packed, packs[0][1])))

    return (loss, grad_x, *[grads[n] for n in WEIGHT_NAMES], *[delta[n] for n in WEIGHT_NAMES],
            *[new_m[n] for n in WEIGHT_NAMES], *[new_v[n] for n in WEIGHT_NAMES])
```

```python
import jax
import jax.numpy as jnp
from jax import lax
from jax.experimental import pallas as pl
from jax.experimental.pallas import tpu as pltpu

F32 = jnp.float32
BF16 = jnp.bfloat16
MESH = pl.DeviceIdType.MESH

N_DEV = 8
NORM_EPS = 1e-6
HALO = 32
LANES = 128
SUBLANES = 8
VMEM_LIMIT_BYTES = 56 * 2**20

ADAM_LR = 0.001
ADAM_B1 = 0.9
ADAM_B2 = 0.999
ADAM_EPS = 1e-08
ADAM_WD = 0.01
ADAM_STEP = 10

WEIGHT_NAMES = ('g_mix', 'w_in', 'b_gate', 'conv_a_w', 'conv_a_b', 'ln_a_g', 'ln_a_b', 'w_a_out',
                'conv_b_w', 'w_b_out', 'w_o', 'g_ffn', 'w_up', 'conv_f_w', 'conv_f_b', 'w_down',
                'g_ple', 'w_ple', 'w_ple_gate', 'g_final')
BIG_AXIS = {'w_in': 0, 'w_up': 0, 'w_a_out': 1, 'w_b_out': 1, 'w_o': 0, 'w_down': 0, 'w_ple': 1,
            'w_ple_gate': 0}
TRANSPOSED = ('w_in', 'w_up')
CONV_SHARDED = ('conv_a_w', 'conv_b_w', 'conv_f_w')


def _dot(a, b):
    return jnp.dot(a, b, preferred_element_type=F32)


def _dot_nt(a, b):
    return lax.dot_general(a, b, (((1,), (1,)), ((), ())), preferred_element_type=F32)


def _dot_tn(a, b):
    return lax.dot_general(a, b, (((0,), (0,)), ((), ())), preferred_element_type=F32)


def _sigmoid(v):
    return jax.nn.sigmoid(v)


def _token_tile(t, cap=512):
    return cap if (t % cap == 0 and t > 512) else 128


def _chunk(n, limit=512):
    for c in range(limit - limit % LANES, 0, -LANES):
        if n % c == 0:
            return c
    return n


def _row_tile(rows):
    for c in (512, 256, 128, 64, 32, 16, 8):
        if rows % c == 0:
            return c
    return rows


def _rows(tm, width):
    return pl.BlockSpec((tm, width), lambda i: (i, 0))


def _rows_rev(tm, width, nt):
    return pl.BlockSpec((tm, width), lambda i: (nt - 1 - i, 0))


def _whole(shape):
    nd = len(shape)
    return pl.BlockSpec(tuple(shape), lambda i: (0,) * nd)


def _resident(shape):
    nd = len(shape)
    return pl.BlockSpec(tuple(shape), lambda i: (0,) * nd, pipeline_mode=pl.Buffered(1))


def _sds(shape, dtype):
    return jax.ShapeDtypeStruct(tuple(shape), dtype)


def _rms_stats(xv):
    r = lax.rsqrt(jnp.mean(xv * xv, axis=-1, keepdims=True) + NORM_EPS)
    return xv * r, r


def _rms_bwd(dy, xh, r, g):
    dxh = dy * g
    dx = r * (dxh - xh * jnp.mean(dxh * xh, axis=-1, keepdims=True))
    return dx, jnp.sum(dy * xh, axis=0, keepdims=True)


def _gelu_tanh(v):
    c = 0.7978845608028654
    t = jnp.tanh(c * (v + 0.044715 * v * v * v))
    return 0.5 * v * (1.0 + t), t


def _gelu_tanh_grad(v, t):
    c = 0.7978845608028654
    return 0.5 * (1.0 + t) + 0.5 * v * (1.0 - t * t) * c * (1.0 + 3.0 * 0.044715 * v * v)


def _causal_conv(ext_ref, w_ref, width, tm):
    acc = None
    for k in range(width):
        term = w_ref[k:k + 1, :] * ext_ref[pl.ds(HALO - (width - 1) + k, tm), :]
        acc = term if acc is None else acc + term
    return acc


def _causal_conv_bwd_input(ext_ref, w_ref, width, tm):
    acc = None
    for k in range(width):
        term = w_ref[k:k + 1, :] * ext_ref[pl.ds(width - 1 - k, tm), :]
        acc = term if acc is None else acc + term
    return acc


def _causal_conv_bwd_weight(dw_ref, dy, ext_ref, width, tm):
    for k in range(width):
        prod = dy * ext_ref[pl.ds(HALO - (width - 1) + k, tm), :]
        dw_ref[k:k + 1, :] += jnp.sum(prod, axis=0, keepdims=True)


def _my_index():
    return 4 * lax.axis_index("x") + 2 * lax.axis_index("y") + lax.axis_index("c")


def _mesh_id(idx):
    return (idx // 4, (idx // 2) % 2, idx % 2)


def _slab(ref, axis, idx, width):
    at = [slice(None)] * len(ref.shape)
    at[axis] = pl.ds(pl.multiple_of(idx * width, width), width)
    return ref.at[tuple(at)]


class _Exchange:
    def __init__(self, inputs, out_shape):
        n = len(inputs)
        self.inputs = list(inputs)
        self.out_shape = list(out_shape)
        self.sems = [pltpu.SemaphoreType.DMA((n, N_DEV - 1)), pltpu.SemaphoreType.DMA((n, N_DEV - 1)),
                     pltpu.SemaphoreType.DMA((n,))]
        self.results = None

    def _local(self, ins, outs, k, me):
        raise NotImplementedError

    def _remote(self, ins, outs, k, me, sender, receiver):
        raise NotImplementedError

    def start(self, ins, outs, sems):
        send_sems, recv_sems, local_sems = sems
        me = _my_index()
        for k in range(len(self.inputs)):
            src, dst = self._local(ins, outs, k, me)
            pltpu.make_async_copy(src, dst, local_sems.at[k]).start()
            for dist in range(1, N_DEV):
                peer = (me + dist) % N_DEV
                src, dst = self._remote(ins, outs, k, me, me, peer)
                pltpu.make_async_remote_copy(
                    src_ref=src, dst_ref=dst, send_sem=send_sems.at[k, dist - 1],
                    recv_sem=recv_sems.at[k, dist - 1], device_id=_mesh_id(peer), device_id_type=MESH).start()

    def wait(self, ins, outs, sems):
        send_sems, recv_sems, local_sems = sems
        me = _my_index()
        for k in range(len(self.inputs)):
            for dist in range(1, N_DEV):
                sender = (me + N_DEV - dist) % N_DEV
                src, dst = self._remote(ins, outs, k, me, sender, me)
                cp = pltpu.make_async_remote_copy(
                    src_ref=src, dst_ref=dst, send_sem=send_sems.at[k, dist - 1],
                    recv_sem=recv_sems.at[k, dist - 1], device_id=_mesh_id(sender), device_id_type=MESH)
                cp.wait_send()
                cp.wait_recv()
            src, dst = self._local(ins, outs, k, me)
            pltpu.make_async_copy(src, dst, local_sems.at[k]).wait()


class _Gather(_Exchange):
    def __init__(self, items):
        self.items = list(items)
        out_shape = []
        for shards, layer, axis in self.items:
            shape = list(shards.shape if layer is None else shards.shape[1:])
            shape[axis] *= N_DEV
            out_shape.append(_sds(shape, shards.dtype))
        super().__init__([it[0] for it in self.items], out_shape)

    def _src(self, ins, k):
        layer = self.items[k][1]
        return ins[k] if layer is None else ins[k].at[layer]

    def _place(self, outs, k, idx):
        axis = self.items[k][2]
        return _slab(outs[k], axis, idx, self.out_shape[k].shape[axis] // N_DEV)

    def _local(self, ins, outs, k, me):
        return self._src(ins, k), self._place(outs, k, me)

    def _remote(self, ins, outs, k, me, sender, receiver):
        return self._src(ins, k), self._place(outs, k, sender)


class _Scatter(_Exchange):
    def __init__(self, items):
        self.items = list(items)
        out_shape = []
        for partial, axis in self.items:
            shape = list(partial.shape)
            shape[axis] //= N_DEV
            out_shape.append(_sds([N_DEV] + shape, partial.dtype))
        super().__init__([it[0] for it in self.items], out_shape)

    def _take(self, ins, k, idx):
        axis = self.items[k][1]
        return _slab(ins[k], axis, idx, self.items[k][0].shape[axis] // N_DEV)

    def _local(self, ins, outs, k, me):
        return self._take(ins, k, me), outs[k].at[me]

    def _remote(self, ins, outs, k, me, sender, receiver):
        return self._take(ins, k, receiver), outs[k].at[sender]


def _run_exchange(exchange, name):
    n = len(exchange.inputs)

    def body(*refs):
        ins, outs, sems = refs[:n], refs[n:2 * n], refs[2 * n:]
        exchange.start(ins, outs, sems)
        exchange.wait(ins, outs, sems)

    any_spec = pl.BlockSpec(memory_space=pl.ANY)
    exchange.results = pl.pallas_call(
        body, name=name, in_specs=[any_spec] * n, out_specs=[any_spec] * n, out_shape=exchange.out_shape,
        scratch_shapes=exchange.sems)(*exchange.inputs)
    return exchange.results


def _call(body, *, name, grid, in_specs, out_specs, out_shape, args, scratch=(), carry=None):
    in_specs, out_specs, out_shape, scratch = list(in_specs), list(out_specs), list(out_shape), list(scratch)
    params = pltpu.CompilerParams(dimension_semantics=("arbitrary",) * len(grid),
                                  vmem_limit_bytes=VMEM_LIMIT_BYTES)
    if carry is None:
        return pl.pallas_call(body, name=name, grid=grid, in_specs=in_specs, out_specs=out_specs,
                              out_shape=out_shape, scratch_shapes=scratch, compiler_params=params)(*args)
    n_in, n_out, n_scr, n_x = len(in_specs), len(out_specs), len(scratch), len(carry.inputs)

    def with_exchange(*refs):
        core_in, x_in = refs[:n_in], refs[n_in:n_in + n_x]
        refs = refs[n_in + n_x:]
        core_out, x_out = refs[:n_out], refs[n_out:n_out + n_x]
        refs = refs[n_out + n_x:]
        core_scr, sems = refs[:n_scr], refs[n_scr:]
        first = pl.program_id(0) == 0
        last = pl.program_id(0) == grid[0] - 1
        for axis in range(1, len(grid)):
            first = jnp.logical_and(first, pl.program_id(axis) == 0)
            last = jnp.logical_and(last, pl.program_id(axis) == grid[axis] - 1)

        @pl.when(first)
        def _():
            carry.start(x_in, x_out, sems)

        body(*core_in, *core_out, *core_scr)

        @pl.when(last)
        def _():
            carry.wait(x_in, x_out, sems)

    any_spec = pl.BlockSpec(memory_space=pl.ANY)
    outs = pl.pallas_call(
        with_exchange, name=name, grid=grid, in_specs=in_specs + [any_spec] * n_x,
        out_specs=out_specs + [any_spec] * n_x, out_shape=out_shape + carry.out_shape,
        scratch_shapes=scratch + carry.sems, compiler_params=params)(*args, *carry.inputs)
    carry.results = outs[n_out:]
    return outs[:n_out]


def _all_reduce_small(pack, name):
    rows = pack.shape[0]

    def body(x_ref, o_ref, land, send_sems, recv_sems):
        me = _my_index()
        land[me] = x_ref[...]
        for dist in range(1, N_DEV):
            pltpu.make_async_remote_copy(
                src_ref=x_ref, dst_ref=land.at[me],
                send_sem=send_sems.at[dist - 1], recv_sem=recv_sems.at[dist - 1],
                device_id=_mesh_id((me + dist) % N_DEV), device_id_type=MESH).start()
        for dist in range(1, N_DEV):
            src_dev = (me + N_DEV - dist) % N_DEV
            cp = pltpu.make_async_remote_copy(
                src_ref=x_ref, dst_ref=land.at[src_dev],
                send_sem=send_sems.at[dist - 1], recv_sem=recv_sems.at[dist - 1],
                device_id=_mesh_id(src_dev), device_id_type=MESH)
            cp.wait_send()
            cp.wait_recv()
        acc = land[0]
        for j in range(1, N_DEV):
            acc = acc + land[j]
        o_ref[...] = acc

    vmem = pl.BlockSpec(memory_space=pltpu.VMEM)
    return pl.pallas_call(
        body, name=name, in_specs=[vmem], out_specs=vmem, out_shape=_sds(pack.shape, F32),
        scratch_shapes=[pltpu.VMEM((N_DEV, rows, LANES), F32), pltpu.SemaphoreType.DMA((N_DEV - 1,)),
                        pltpu.SemaphoreType.DMA((N_DEV - 1,))])(pack)


def _norm_matmul(x, g, wt, name, carry=None):
    t, d = x.shape
    n = wt.shape[0]
    tm, nc = _token_tile(t), _chunk(n)

    def body(x_ref, g_ref, wt_ref, h_ref, o_ref):
        xh, _ = _rms_stats(x_ref[...])
        h = (xh * g_ref[...]).astype(BF16)
        h_ref[...] = h
        for n0 in range(0, n, nc):
            o_ref[:, n0:n0 + nc] = _dot_nt(h, wt_ref[n0:n0 + nc, :]).astype(BF16)

    return _call(body, name=name, grid=(t // tm,),
                 in_specs=[_rows(tm, d), _whole(g.shape), _resident(wt.shape)],
                 out_specs=[_rows(tm, d), _rows(tm, n)],
                 out_shape=[_sds((t, d), BF16), _sds((t, n), BF16)], args=(x, g, wt), carry=carry)


def _fwd_branch(z, caw, cab, lng, lnb, cbw, dc, name, carry=None):
    t = z.shape[0]
    tm = _token_tile(t)
    ka, kb = caw.shape[0], cbw.shape[0]

    def body(z_ref, caw_ref, cab_ref, lng_ref, lnb_ref, cbw_ref, ac_ref, act_ref, s_ref, a_ext, cv_ext):
        @pl.when(pl.program_id(0) == 0)
        def _():
            a_ext[0:HALO, :] = jnp.zeros((HALO, dc), F32)
            cv_ext[0:HALO, :] = jnp.zeros((HALO, dc), F32)

        a_val = z_ref[:, 0:dc].astype(F32)
        a_gt = z_ref[:, dc:2 * dc].astype(F32)
        a_ext[HALO:HALO + tm, :] = a_val * _sigmoid(a_gt)
        ac = _causal_conv(a_ext, caw_ref, ka, tm) + cab_ref[...]
        ac_ref[...] = ac
        mu = jnp.mean(ac, axis=-1, keepdims=True)
        xc = ac - mu
        var = jnp.mean(xc * xc, axis=-1, keepdims=True)
        ln = xc * lax.rsqrt(var + NORM_EPS) * lng_ref[...] + lnb_ref[...]
        act_ref[...] = (ln * _sigmoid(ln)).astype(BF16)
        a_ext[0:HALO, :] = a_ext[tm:tm + HALO, :]

        sc_b = z_ref[:, 2 * dc:3 * dc].astype(F32)
        sc_c = z_ref[:, 3 * dc:4 * dc].astype(F32)
        sc_v = z_ref[:, 4 * dc:5 * dc].astype(F32)
        cv_ext[HALO:HALO + tm, :] = sc_c * sc_v
        s_ref[...] = (sc_b * _causal_conv(cv_ext, cbw_ref, kb, tm)).astype(BF16)
        cv_ext[0:HALO, :] = cv_ext[tm:tm + HALO, :]

    return _call(body, name=name, grid=(t // tm,),
                 in_specs=[_rows(tm, 5 * dc), _whole(caw.shape), _whole(cab.shape), _whole(lng.shape),
                           _whole(lnb.shape), _whole(cbw.shape)],
                 out_specs=[_rows(tm, dc), _rows(tm, dc), _rows(tm, dc)],
                 out_shape=[_sds((t, dc), F32), _sds((t, dc), BF16), _sds((t, dc), BF16)],
                 scratch=[pltpu.VMEM((HALO + tm, dc), F32), pltpu.VMEM((HALO + tm, dc), F32)],
                 args=(z, caw, cab, lng, lnb, cbw), carry=carry)


def _fwd_merge(x, z, bg, a_act, s, wa, wb, wo, name, carry=None):
    t, d = x.shape
    n = z.shape[1]
    dc = a_act.shape[1]
    tm = _token_tile(t)
    o5 = n - 2 * d

    def body(x_ref, z_ref, bg_ref, act_ref, s_ref, wa_ref, wb_ref, wo_ref, o_ref):
        ya = _dot(act_ref[...], wa_ref[...])
        yb = _dot(s_ref[...], wb_ref[...])
        ga = _sigmoid(z_ref[:, o5:o5 + d].astype(F32) + bg_ref[:, 0:d])
        gb = _sigmoid(z_ref[:, o5 + d:n].astype(F32) + bg_ref[:, d:2 * d])
        m = (ga * ya + gb * yb).astype(BF16)
        o_ref[...] = x_ref[...] + _dot(m, wo_ref[...])

    return _call(body, name=name, grid=(t // tm,),
                 in_specs=[_rows(tm, d), _rows(tm, n), _whole(bg.shape), _rows(tm, dc), _rows(tm, dc),
                           _resident(wa.shape), _resident(wb.shape), _resident(wo.shape)],
                 out_specs=[_rows(tm, d)], out_shape=[_sds((t, d), F32)],
                 args=(x, z, bg, a_act, s, wa, wb, wo), carry=carry)[0]


def _fwd_down(x, u, cfw, cfb, wd, name, carry=None):
    t, d = x.shape
    f = u.shape[1] // 2
    tm = _token_tile(t, 256)
    kf = cfw.shape[0]

    def body(x_ref, u_ref, cfw_ref, cfb_ref, wd_ref, o_ref, act_ref, ug_ext):
        @pl.when(pl.program_id(0) == 0)
        def _():
            ug_ext[0:HALO, :] = jnp.zeros((HALO, f), F32)

        ug_ext[HALO:HALO + tm, :] = u_ref[:, 0:f].astype(F32)
        fg = _causal_conv(ug_ext, cfw_ref, kf, tm) + cfb_ref[...]
        gl, _ = _gelu_tanh(fg)
        act = (gl * u_ref[:, f:2 * f].astype(F32)).astype(BF16)
        act_ref[...] = act
        o_ref[...] = x_ref[...] + _dot(act, wd_ref[...])
        ug_ext[0:HALO, :] = ug_ext[tm:tm + HALO, :]

    return _call(body, name=name, grid=(t // tm,),
                 in_specs=[_rows(tm, d), _rows(tm, 2 * f), _whole(cfw.shape), _whole(cfb.shape),
                           _resident(wd.shape)],
                 out_specs=[_rows(tm, d), _rows(tm, f)], out_shape=[_sds((t, d), F32), _sds((t, f), BF16)],
                 scratch=[pltpu.VMEM((HALO + tm, f), F32)], args=(x, u, cfw, cfb, wd), carry=carry)


def _fwd_ple(x, g, wpg, p, wple, name, carry=None):
    t, d = x.shape
    pd = p.shape[1]
    tm = _token_tile(t)

    def body(x_ref, g_ref, wpg_ref, p_ref, wple_ref, o_ref):
        xv = x_ref[...]
        xh, _ = _rms_stats(xv)
        lg = _dot((xh * g_ref[...]).astype(BF16), wpg_ref[...])
        pp = _dot(p_ref[...].astype(BF16), wple_ref[...])
        o_ref[...] = xv + _sigmoid(lg) * pp

    return _call(body, name=name, grid=(t // tm,),
                 in_specs=[_rows(tm, d), _whole(g.shape), _resident(wpg.shape), _rows(tm, pd),
                           _resident(wple.shape)],
                 out_specs=[_rows(tm, d)], out_shape=[_sds((t, d), F32)],
                 args=(x, g, wpg, p, wple), carry=carry)[0]


def _loss_bwd(x, g, target, name):
    t, d = x.shape
    tm = _token_tile(t)

    def body(x_ref, g_ref, t_ref, dx_ref, dg_ref, loss_ref):
        @pl.when(pl.program_id(0) == 0)
        def _():
            dg_ref[...] = jnp.zeros_like(dg_ref)
            loss_ref[...] = jnp.zeros_like(loss_ref)

        xh, r = _rms_stats(x_ref[...])
        err = xh * g_ref[...] - t_ref[...]
        sq = jnp.sum(jnp.sum(err * err, axis=0, keepdims=True), axis=1, keepdims=True)
        loss_ref[...] += jnp.broadcast_to(0.5 * sq / d, loss_ref.shape)
        dx, dg = _rms_bwd(err / d, xh, r, g_ref[...])
        dx_ref[...] = dx
        dg_ref[...] += dg

    return _call(body, name=name, grid=(t // tm,),
                 in_specs=[_rows(tm, d), _whole(g.shape), _rows(tm, d)],
                 out_specs=[_rows(tm, d), _whole((1, d)), _whole((SUBLANES, LANES))],
                 out_shape=[_sds((t, d), F32), _sds((1, d), F32), _sds((SUBLANES, LANES), F32)],
                 args=(x, g, target))


def _bwd_ple(dy, x, g, wpg, p, wple, name, carry=None):
    t, d = x.shape
    pd = p.shape[1]
    tm = _token_tile(t)
    nt = t // tm

    def body(dy_ref, x_ref, g_ref, wpg_ref, p_ref, wple_ref, dx_ref, dwpg_ref, dwple_ref, dg_ref,
             acc_pg, acc_ple):
        i = pl.program_id(0)

        @pl.when(i == 0)
        def _():
            acc_pg[...] = jnp.zeros_like(acc_pg)
            acc_ple[...] = jnp.zeros_like(acc_ple)
            dg_ref[...] = jnp.zeros_like(dg_ref)

        dyv = dy_ref[...]
        xh, r = _rms_stats(x_ref[...])
        h = (xh * g_ref[...]).astype(BF16)
        pb = p_ref[...].astype(BF16)
        pg = _sigmoid(_dot(h, wpg_ref[...]))
        pp = _dot(pb, wple_ref[...])
        dpp = (dyv * pg).astype(BF16)
        dlg = (dyv * pp * pg * (1.0 - pg)).astype(BF16)
        acc_ple[...] += _dot_tn(pb, dpp)
        acc_pg[...] += _dot_tn(h, dlg)
        dx, dg = _rms_bwd(_dot_nt(dlg, wpg_ref[...]), xh, r, g_ref[...])
        dx_ref[...] = dyv + dx
        dg_ref[...] += dg

        @pl.when(i == nt - 1)
        def _():
            dwpg_ref[...] = acc_pg[...].astype(BF16)
            dwple_ref[...] = acc_ple[...].astype(BF16)

    return _call(body, name=name, grid=(nt,),
                 in_specs=[_rows(tm, d), _rows(tm, d), _whole(g.shape), _resident(wpg.shape), _rows(tm, pd),
                           _resident(wple.shape)],
                 out_specs=[_rows(tm, d), _whole((d, d)), _whole((pd, d)), _whole((1, d))],
                 out_shape=[_sds((t, d), F32), _sds((d, d), BF16), _sds((pd, d), BF16), _sds((1, d), F32)],
                 scratch=[pltpu.VMEM((d, d), F32), pltpu.VMEM((pd, d), F32)],
                 args=(dy, x, g, wpg, p, wple), carry=carry)


def _bwd_down(dy, u, cfw, cfb, wd, name, carry=None):
    t, d = dy.shape
    f = u.shape[1] // 2
    tm = _token_tile(t, 256)
    nt = t // tm
    kf = cfw.shape[0]
    per = tm // HALO

    def body(dy_ref, u_ref, up_ref, cfw_ref, cfb_ref, wd_ref, du_ref, dcw_ref, dcb_ref, ug_ext, dfg_ext):
        i = pl.program_id(0)

        @pl.when(i == 0)
        def _():
            dcw_ref[...] = jnp.zeros_like(dcw_ref)
            dcb_ref[...] = jnp.zeros_like(dcb_ref)
            dfg_ext[tm:tm + HALO, :] = jnp.zeros((HALO, f), F32)

        first = (i == nt - 1).astype(F32)
        ug_ext[0:HALO, :] = up_ref[:, 0:f].astype(F32) * (1.0 - first)
        ug_ext[HALO:HALO + tm, :] = u_ref[:, 0:f].astype(F32)
        uv = u_ref[:, f:2 * f].astype(F32)
        df = _dot_nt(dy_ref[...].astype(BF16), wd_ref[...])
        fg = _causal_conv(ug_ext, cfw_ref, kf, tm) + cfb_ref[...]
        gl, th = _gelu_tanh(fg)
        du_ref[:, f:2 * f] = (df * gl).astype(BF16)
        dfg = df * uv * _gelu_tanh_grad(fg, th)
        dcb_ref[...] += jnp.sum(dfg, axis=0, keepdims=True)
        _causal_conv_bwd_weight(dcw_ref, dfg, ug_ext, kf, tm)
        dfg_ext[0:tm, :] = dfg
        du_ref[:, 0:f] = _causal_conv_bwd_input(dfg_ext, cfw_ref, kf, tm).astype(BF16)
        dfg_ext[tm:tm + HALO, :] = dfg_ext[0:HALO, :]

    prev_rows = pl.BlockSpec((HALO, 2 * f), lambda i: (jnp.maximum((nt - 1 - i) * per - 1, 0), 0))
    return _call(body, name=name, grid=(nt,),
                 in_specs=[_rows_rev(tm, d, nt), _rows_rev(tm, 2 * f, nt), prev_rows, _whole(cfw.shape),
                           _whole(cfb.shape), _resident(wd.shape)],
                 out_specs=[_rows_rev(tm, 2 * f, nt), _whole((SUBLANES, f)), _whole((1, f))],
                 out_shape=[_sds((t, 2 * f), BF16), _sds((SUBLANES, f), F32), _sds((1, f), F32)],
                 scratch=[pltpu.VMEM((HALO + tm, f), F32), pltpu.VMEM((tm + HALO, f), F32)],
                 args=(dy, u, u, cfw, cfb, wd), carry=carry)


def _bwd_norm_matmul(dout, wt, x, g, dres, name, carry=None):
    t, d = x.shape
    n = dout.shape[1]
    tm = _token_tile(t)

    def body(do_ref, wt_ref, x_ref, g_ref, dres_ref, dx_ref, dg_ref):
        @pl.when(pl.program_id(0) == 0)
        def _():
            dg_ref[...] = jnp.zeros_like(dg_ref)

        dh = _dot(do_ref[...], wt_ref[...])
        xh, r = _rms_stats(x_ref[...])
        dx, dg = _rms_bwd(dh, xh, r, g_ref[...])
        dx_ref[...] = dres_ref[...] + dx
        dg_ref[...] += dg

    return _call(body, name=name, grid=(t // tm,),
                 in_specs=[_rows(tm, n), _resident(wt.shape), _rows(tm, d), _whole(g.shape), _rows(tm, d)],
                 out_specs=[_rows(tm, d), _whole((1, d))],
                 out_shape=[_sds((t, d), F32), _sds((1, d), F32)],
                 args=(dout, wt, x, g, dres), carry=carry)


def _wgrad_tn(a, b, name, carry=None):
    t, n = a.shape
    d = b.shape[1]
    tt = 1024 if t % 1024 == 0 else _token_tile(t)
    tn = _chunk(n, 1536)
    nt = t // tt

    def body(a_ref, b_ref, o_ref, acc):
        k = pl.program_id(1)

        @pl.when(k == 0)
        def _():
            acc[...] = jnp.zeros_like(acc)

        acc[...] += _dot_tn(a_ref[...].astype(BF16), b_ref[...].astype(BF16))

        @pl.when(k == nt - 1)
        def _():
            o_ref[...] = acc[...].astype(BF16)

    return _call(body, name=name, grid=(n // tn, nt),
                 in_specs=[pl.BlockSpec((tt, tn), lambda j, k: (k, j)), pl.BlockSpec((tt, d), lambda j, k: (k, 0))],
                 out_specs=[pl.BlockSpec((tn, d), lambda j, k: (j, 0))], out_shape=[_sds((n, d), BF16)],
                 scratch=[pltpu.VMEM((tn, d), F32)], args=(a, b), carry=carry)[0]


def _bwd_merge(dy, z, bg, a_act, s, wa, wb, wo, name, carry=None):
    t, d = dy.shape
    n = z.shape[1]
    dc = a_act.shape[1]
    tm = _token_tile(t, 256)
    nt = t // tm
    o5 = n - 2 * d

    def body(dy_ref, z_ref, bg_ref, act_ref, s_ref, wa_ref, wb_ref, wo_ref,
             dact_ref, ds_ref, dgl_ref, dwo_ref, dwa_ref, dwb_ref, dbg_ref, acc_o, acc_a, acc_b):
        i = pl.program_id(0)

        @pl.when(i == 0)
        def _():
            acc_o[...] = jnp.zeros_like(acc_o)
            acc_a[...] = jnp.zeros_like(acc_a)
            acc_b[...] = jnp.zeros_like(acc_b)
            dbg_ref[...] = jnp.zeros_like(dbg_ref)

        dyb = dy_ref[...].astype(BF16)
        dm = _dot_nt(dyb, wo_ref[...])
        ya = _dot(act_ref[...], wa_ref[...])
        yb = _dot(s_ref[...], wb_ref[...])
        ga = _sigmoid(z_ref[:, o5:o5 + d].astype(F32) + bg_ref[:, 0:d])
        gb = _sigmoid(z_ref[:, o5 + d:n].astype(F32) + bg_ref[:, d:2 * d])
        acc_o[...] += _dot_tn((ga * ya + gb * yb).astype(BF16), dyb)
        dya = (dm * ga).astype(BF16)
        dyb2 = (dm * gb).astype(BF16)
        acc_a[...] += _dot_tn(act_ref[...], dya)
        acc_b[...] += _dot_tn(s_ref[...], dyb2)
        dact_ref[...] = _dot_nt(dya, wa_ref[...])
        ds_ref[...] = _dot_nt(dyb2, wb_ref[...])
        dla = dm * ya * ga * (1.0 - ga)
        dlb = dm * yb * gb * (1.0 - gb)
        dgl_ref[:, 0:d] = dla.astype(BF16)
        dgl_ref[:, d:2 * d] = dlb.astype(BF16)
        dbg_ref[:, 0:d] += jnp.sum(dla, axis=0, keepdims=True)
        dbg_ref[:, d:2 * d] += jnp.sum(dlb, axis=0, keepdims=True)

        @pl.when(i == nt - 1)
        def _():
            dwo_ref[...] = acc_o[...].astype(BF16)
            dwa_ref[...] = acc_a[...].astype(BF16)
            dwb_ref[...] = acc_b[...].astype(BF16)

    return _call(body, name=name, grid=(nt,),
                 in_specs=[_rows(tm, d), _rows(tm, n), _whole(bg.shape), _rows(tm, dc), _rows(tm, dc),
                           _resident(wa.shape), _resident(wb.shape), _resident(wo.shape)],
                 out_specs=[_rows(tm, dc), _rows(tm, dc), _rows(tm, 2 * d), _whole((d, d)), _whole((dc, d)),
                            _whole((dc, d)), _whole((1, 2 * d))],
                 out_shape=[_sds((t, dc), F32), _sds((t, dc), F32), _sds((t, 2 * d), BF16), _sds((d, d), BF16),
                            _sds((dc, d), BF16), _sds((dc, d), BF16), _sds((1, 2 * d), F32)],
                 scratch=[pltpu.VMEM((d, d), F32), pltpu.VMEM((dc, d), F32), pltpu.VMEM((dc, d), F32)],
                 args=(dy, z, bg, a_act, s, wa, wb, wo), carry=carry)


def _bwd_branch(dact, ds, z, dgl, a_conv, caw, lng, lnb, cbw, name, carry=None):
    t, n = z.shape
    dc = a_conv.shape[1]
    tm = _token_tile(t, 256)
    nt = t // tm
    ka, kb = caw.shape[0], cbw.shape[0]
    per = tm // HALO

    def body(dact_ref, ds_ref, z_ref, zp_ref, dgl_ref, ac_ref, caw_ref, lng_ref, lnb_ref, cbw_ref,
             dz_ref, dcaw_ref, dcab_ref, dlng_ref, dlnb_ref, dcbw_ref,
             a_ext, dac_ext, cv_ext, dcb_ext):
        i = pl.program_id(0)

        @pl.when(i == 0)
        def _():
            for ref in (dcaw_ref, dcab_ref, dlng_ref, dlnb_ref, dcbw_ref):
                ref[...] = jnp.zeros_like(ref)
            dac_ext[tm:tm + HALO, :] = jnp.zeros((HALO, dc), F32)
            dcb_ext[tm:tm + HALO, :] = jnp.zeros((HALO, dc), F32)

        keep = 1.0 - (i == nt - 1).astype(F32)
        a_val = z_ref[:, 0:dc].astype(F32)
        sg = _sigmoid(z_ref[:, dc:2 * dc].astype(F32))
        a_ext[0:HALO, :] = zp_ref[:, 0:dc].astype(F32) * _sigmoid(zp_ref[:, dc:2 * dc].astype(F32)) * keep
        a_ext[HALO:HALO + tm, :] = a_val * sg

        ac = ac_ref[...]
        mu = jnp.mean(ac, axis=-1, keepdims=True)
        xc = ac - mu
        rstd = lax.rsqrt(jnp.mean(xc * xc, axis=-1, keepdims=True) + NORM_EPS)
        xh = xc * rstd
        ln = xh * lng_ref[...] + lnb_ref[...]
        sl = _sigmoid(ln)
        dln = dact_ref[...] * (sl * (1.0 + ln * (1.0 - sl)))
        dlng_ref[...] += jnp.sum(dln * xh, axis=0, keepdims=True)
        dlnb_ref[...] += jnp.sum(dln, axis=0, keepdims=True)
        dxh = dln * lng_ref[...]
        dac = rstd * (dxh - jnp.mean(dxh, axis=-1, keepdims=True)
                      - xh * jnp.mean(dxh * xh, axis=-1, keepdims=True))
        dcab_ref[...] += jnp.sum(dac, axis=0, keepdims=True)
        _causal_conv_bwd_weight(dcaw_ref, dac, a_ext, ka, tm)
        dac_ext[0:tm, :] = dac
        da = _causal_conv_bwd_input(dac_ext, caw_ref, ka, tm)
        dac_ext[tm:tm + HALO, :] = dac_ext[0:HALO, :]
        dz_ref[:, 0:dc] = (da * sg).astype(BF16)
        dz_ref[:, dc:2 * dc] = (da * a_val * sg * (1.0 - sg)).astype(BF16)

        sc_b = z_ref[:, 2 * dc:3 * dc].astype(F32)
        sc_c = z_ref[:, 3 * dc:4 * dc].astype(F32)
        sc_v = z_ref[:, 4 * dc:5 * dc].astype(F32)
        cv_ext[0:HALO, :] = zp_ref[:, 3 * dc:4 * dc].astype(F32) * zp_ref[:, 4 * dc:5 * dc].astype(F32) * keep
        cv_ext[HALO:HALO + tm, :] = sc_c * sc_v
        dsv = ds_ref[...]
        dz_ref[:, 2 * dc:3 * dc] = (dsv * _causal_conv(cv_ext, cbw_ref, kb, tm)).astype(BF16)
        dcb = dsv * sc_b
        _causal_conv_bwd_weight(dcbw_ref, dcb, cv_ext, kb, tm)
        dcb_ext[0:tm, :] = dcb
        dcv = _causal_conv_bwd_input(dcb_ext, cbw_ref, kb, tm)
        dcb_ext[tm:tm + HALO, :] = dcb_ext[0:HALO, :]
        dz_ref[:, 3 * dc:4 * dc] = (dcv * sc_v).astype(BF16)
        dz_ref[:, 4 * dc:5 * dc] = (dcv * sc_c).astype(BF16)
        dz_ref[:, 5 * dc:n] = dgl_ref[...]

    prev_rows = pl.BlockSpec((HALO, 5 * dc), lambda i: (jnp.maximum((nt - 1 - i) * per - 1, 0), 0))
    return _call(body, name=name, grid=(nt,),
                 in_specs=[_rows_rev(tm, dc, nt), _rows_rev(tm, dc, nt), _rows_rev(tm, 5 * dc, nt), prev_rows,
                           _rows_rev(tm, n - 5 * dc, nt), _rows_rev(tm, dc, nt), _whole(caw.shape),
                           _whole(lng.shape), _whole(lnb.shape), _whole(cbw.shape)],
                 out_specs=[_rows_rev(tm, n, nt), _whole((HALO, dc)), _whole((1, dc)), _whole((1, dc)),
                            _whole((1, dc)), _whole((SUBLANES, dc))],
                 out_shape=[_sds((t, n), BF16), _sds((HALO, dc), F32), _sds((1, dc), F32), _sds((1, dc), F32),
                            _sds((1, dc), F32), _sds((SUBLANES, dc), F32)],
                 scratch=[pltpu.VMEM((HALO + tm, dc), F32), pltpu.VMEM((tm + HALO, dc), F32),
                          pltpu.VMEM((HALO + tm, dc), F32), pltpu.VMEM((tm + HALO, dc), F32)],
                 args=(dact, ds, z, z, dgl, a_conv, caw, lng, lnb, cbw), carry=carry)


def _sum_slabs(lands, name):
    _, rows, cols = lands[0].shape
    tr = _row_tile(rows)
    nr = rows // tr
    depth = len(lands)

    def body(*refs):
        o_ref = refs[depth]
        i = pl.program_id(0)
        for k in range(depth):
            @pl.when(i // nr == k)
            def _(k=k):
                acc = refs[k][0].astype(F32)
                for j in range(1, N_DEV):
                    acc = acc + refs[k][j].astype(F32)
                o_ref[...] = acc

    def land_spec(k):
        return pl.BlockSpec((N_DEV, tr, cols), lambda i: (0, jnp.clip(i - k * nr, 0, nr - 1), 0))

    return _call(body, name=name, grid=(depth * nr,), in_specs=[land_spec(k) for k in range(depth)],
                 out_specs=[_rows(tr, cols)], out_shape=[_sds((depth * rows, cols), F32)], args=lands)[0]


def _adamw(w, g, m, v, name):
    rows, cols = w.shape
    tr = _row_tile(rows)

    def body(w_ref, g_ref, m_ref, v_ref, d_ref, nm_ref, nv_ref):
        gv = g_ref[...]
        nm = ADAM_B1 * m_ref[...] + (1.0 - ADAM_B1) * gv
        nv = ADAM_B2 * v_ref[...] + (1.0 - ADAM_B2) * (gv * gv)
        m_hat = nm / (1.0 - ADAM_B1 ** ADAM_STEP)
        v_hat = nv / (1.0 - ADAM_B2 ** ADAM_STEP)
        d_ref[...] = -ADAM_LR * (m_hat / (jnp.sqrt(v_hat) + ADAM_EPS) + ADAM_WD * w_ref[...])
        nm_ref[...] = nm
        nv_ref[...] = nv

    spec = _rows(tr, cols)
    return _call(body, name=name, grid=(rows // tr,), in_specs=[spec] * 4, out_specs=[spec] * 3,
                 out_shape=[_sds((rows, cols), F32)] * 3, args=(w, g, m, v))


def _pack(arrays):
    parts, meta, row = [], [], 0
    tile = SUBLANES * LANES
    for a in arrays:
        size = a.size
        padded = -(-size // tile) * tile
        flat = jnp.pad(a.reshape(-1).astype(F32), (0, padded - size))
        parts.append(flat.reshape(padded // LANES, LANES))
        meta.append((row, size, a.shape))
        row += padded // LANES
    return jnp.concatenate(parts, axis=0), meta


def _unpack(pack, meta):
    out = []
    for row, size, shape in meta:
        rows = -(-size // LANES)
        out.append(pack[row:row + rows].reshape(-1)[:size].reshape(shape))
    return out


def kernel(x, p, g_mix, w_in, b_gate, conv_a_w, conv_a_b, ln_a_g, ln_a_b, w_a_out, conv_b_w, w_b_out, w_o, g_ffn, w_up, conv_f_w, conv_f_b, w_down, g_ple, w_ple, w_ple_gate, g_final, loss_target, m_g_mix, m_w_in, m_b_gate, m_conv_a_w, m_conv_a_b, m_ln_a_g, m_ln_a_b, m_w_a_out, m_conv_b_w, m_w_b_out, m_w_o, m_g_ffn, m_w_up, m_conv_f_w, m_conv_f_b, m_w_down, m_g_ple, m_w_ple, m_w_ple_gate, m_g_final, v_g_mix, v_w_in, v_b_gate, v_conv_a_w, v_conv_a_b, v_ln_a_g, v_ln_a_b, v_w_a_out, v_conv_b_w, v_w_b_out, v_w_o, v_g_ffn, v_w_up, v_conv_f_w, v_conv_f_b, v_w_down, v_g_ple, v_w_ple, v_w_ple_gate, v_g_final):
    w = dict(zip(WEIGHT_NAMES, (g_mix, w_in, b_gate, conv_a_w, conv_a_b, ln_a_g, ln_a_b, w_a_out, conv_b_w,
                                w_b_out, w_o, g_ffn, w_up, conv_f_w, conv_f_b, w_down, g_ple, w_ple,
                                w_ple_gate, g_final)))
    mom = dict(zip(WEIGHT_NAMES, (m_g_mix, m_w_in, m_b_gate, m_conv_a_w, m_conv_a_b, m_ln_a_g, m_ln_a_b,
                                  m_w_a_out, m_conv_b_w, m_w_b_out, m_w_o, m_g_ffn, m_w_up, m_conv_f_w,
                                  m_conv_f_b, m_w_down, m_g_ple, m_w_ple, m_w_ple_gate, m_g_final)))
    var = dict(zip(WEIGHT_NAMES, (v_g_mix, v_w_in, v_b_gate, v_conv_a_w, v_conv_a_b, v_ln_a_g, v_ln_a_b,
                                  v_w_a_out, v_conv_b_w, v_w_b_out, v_w_o, v_g_ffn, v_w_up, v_conv_f_w,
                                  v_conv_f_b, v_w_down, v_g_ple, v_w_ple, v_w_ple_gate, v_g_final)))
    depth = g_mix.shape[0]
    dc = ln_a_g.shape[1]
    me = _my_index()
    x0 = x[0]
    target = loss_target[0]
    big_names = tuple(BIG_AXIS)

    shard = {name: (jnp.swapaxes(w[name], 1, 2) if name in TRANSPOSED else w[name]).astype(BF16)
             for name in big_names}

    def gather_of(layer, *names):
        return _Gather([(shard[name], layer, BIG_AXIS[name]) for name in names])

    def row(name, layer):
        return w[name][layer][None]

    first = _Gather([(shard['w_in'], 0, BIG_AXIS['w_in'])] + [(w[name][None], None, 0) for name in CONV_SHARDED])
    gathered = _run_exchange(first, "gather_first")
    w_in_full = gathered[0]
    conv_full = {name: jnp.transpose(g, (1, 2, 0, 3)).reshape(g.shape[1], g.shape[2], -1)
                 for name, g in zip(CONV_SHARDED, gathered[1:])}
    saved = []
    xc = x0
    for l in range(depth):
        carry = gather_of(l, 'w_a_out', 'w_b_out', 'w_o')
        h, z = _norm_matmul(xc, row('g_mix', l), w_in_full, f"fwd_in_{l}", carry)
        wa_full, wb_full, wo_full = carry.results
        carry = gather_of(l, 'w_up')
        a_conv, a_act, s = _fwd_branch(z, conv_full['conv_a_w'][l], row('conv_a_b', l), row('ln_a_g', l),
                                       row('ln_a_b', l), conv_full['conv_b_w'][l], dc, f"fwd_branch_{l}", carry)
        w_up_full, = carry.results
        carry = gather_of(l, 'w_down')
        x1 = _fwd_merge(xc, z, row('b_gate', l), a_act, s, wa_full, wb_full, wo_full, f"fwd_merge_{l}", carry)
        w_down_full, = carry.results
        carry = gather_of(l, 'w_ple', 'w_ple_gate')
        h2, u = _norm_matmul(x1, row('g_ffn', l), w_up_full, f"fwd_up_{l}", carry)
        w_ple_full, w_pg_full = carry.results
        carry = gather_of(l + 1, 'w_in') if l + 1 < depth else None
        x2, act = _fwd_down(x1, u, conv_full['conv_f_w'][l], row('conv_f_b', l), w_down_full, f"fwd_down_{l}",
                            carry)
        x3 = _fwd_ple(x2, row('g_ple', l), w_pg_full, p[l, 0], w_ple_full, f"fwd_ple_{l}")
        saved.append((xc, h, z, a_conv, a_act, s, x1, h2, u, act, x2,
                      dict(w_in=w_in_full, w_a_out=wa_full, w_b_out=wb_full, w_o=wo_full, w_up=w_up_full,
                           w_down=w_down_full, w_ple=w_ple_full, w_ple_gate=w_pg_full)))
        if carry is not None:
            w_in_full, = carry.results
        xc = x3

    dx, dg_final, loss_part = _loss_bwd(xc, g_final[None], target, "loss_bwd")
    landed = {name: [None] * depth for name in big_names}
    small = {name: [None] * depth for name in WEIGHT_NAMES if name not in BIG_AXIS and name != 'g_final'}

    def scatter_of(layer, **partials):
        ex = _Scatter([(part, BIG_AXIS[name]) for name, part in partials.items()])
        ex.names, ex.layer = tuple(partials), layer
        return ex

    def keep(ex):
        for name, land in zip(ex.names, ex.results):
            landed[name][ex.layer] = land

    for l in reversed(range(depth)):
        xin, h, z, a_conv, a_act, s, x1, h2, u, act, x2, full = saved[l]
        dx2, d_wpg, d_wple, small['g_ple'][l] = _bwd_ple(
            dx, x2, row('g_ple', l), full['w_ple_gate'], p[l, 0], full['w_ple'], f"bwd_ple_{l}")
        carry = scatter_of(l, w_ple_gate=d_wpg, w_ple=d_wple)
        d_wdown = _wgrad_tn(act, dx2, f"wgrad_down_{l}", carry)
        keep(carry)
        carry = scatter_of(l, w_down=d_wdown)
        du, dcfw, small['conv_f_b'][l] = _bwd_down(
            dx2, u, conv_full['conv_f_w'][l], row('conv_f_b', l), full['w_down'], f"bwd_down_{l}", carry)
        keep(carry)
        small['conv_f_w'][l] = dcfw[:conv_f_w.shape[1]]
        dx1, small['g_ffn'][l] = _bwd_norm_matmul(du, full['w_up'], x1, row('g_ffn', l), dx2, f"bwd_up_{l}")
        d_wup = _wgrad_tn(du, h2, f"wgrad_up_{l}")
        carry = scatter_of(l, w_up=d_wup)
        dact, ds, dgl, d_wo, d_wa, d_wb, small['b_gate'][l] = _bwd_merge(
            dx1, z, row('b_gate', l), a_act, s, full['w_a_out'], full['w_b_out'], full['w_o'],
            f"bwd_merge_{l}", carry)
        keep(carry)
        carry = scatter_of(l, w_o=d_wo, w_a_out=d_wa, w_b_out=d_wb)
        dz, dcaw, small['conv_a_b'][l], small['ln_a_g'][l], small['ln_a_b'][l], dcbw = _bwd_branch(
            dact, ds, z, dgl, a_conv, conv_full['conv_a_w'][l], row('ln_a_g', l), row('ln_a_b', l),
            conv_full['conv_b_w'][l], f"bwd_branch_{l}", carry)
        keep(carry)
        small['conv_a_w'][l] = dcaw[:conv_a_w.shape[1]]
        small['conv_b_w'][l] = dcbw[:conv_b_w.shape[1]]
        d_win = _wgrad_tn(dz, h, f"wgrad_in_{l}")
        carry = scatter_of(l, w_in=d_win)
        dx, small['g_mix'][l] = _bwd_norm_matmul(dz, full['w_in'], xin, row('g_mix', l), dx1, f"bwd_in_{l}", carry)
        keep(carry)
    grad_x = dx[None]

    small_names = tuple(small)
    small_parts = [jnp.stack([part[0] if part.shape[0] == 1 else part for part in small[name]])
                   for name in small_names]
    pack, meta = _pack(small_parts + [dg_final[0], loss_part])
    reduced = _unpack(_all_reduce_small(pack, "all_reduce_small"), meta)
    loss = reduced[-1][0, 0]
    grads = dict(zip(small_names, reduced[:len(small_names)]))
    grads['g_final'] = reduced[len(small_names)]
    for name in CONV_SHARDED:
        width = w[name].shape[-1]
        grads[name] = lax.dynamic_slice_in_dim(grads[name], me * width, width, axis=2)

    for name in big_names:
        slab_shape = landed[name][0].shape[1:]
        lands = [land.reshape(N_DEV, -1, slab_shape[-1]) for land in landed[name]]
        total = _sum_slabs(lands, f"sum_{name}").reshape((depth,) + slab_shape)
        grads[name] = jnp.swapaxes(total, 1, 2) if name in TRANSPOSED else total

    delta, new_m, new_v = {}, {}, {}
    for name in big_names:
        shape = w[name].shape
        flat = lambda a: a.reshape(-1, shape[-1])
        d_, m_, v_ = _adamw(flat(w[name]), flat(grads[name]), flat(mom[name]), flat(var[name]), f"adamw_{name}")
        delta[name], new_m[name], new_v[name] = d_.reshape(shape), m_.reshape(shape), v_.reshape(shape)
    rest = tuple(name for name in WEIGHT_NAMES if name not in BIG_AXIS)
    packs = [_pack([src[name] for name in rest]) for src in (w, grads, mom, var)]
    outs = _adamw(packs[0][0], packs[1][0], packs[2][0], packs[3][0], "adamw_small")
    for dst, packed in zip((delta, new_m, new_v), outs):
        dst.update(zip(rest, _unpack(packed, packs[0][1])))

    return (loss, grad_x, *[grads[n] for n in WEIGHT_NAMES], *[delta[n] for n in WEIGHT_NAMES],
            *[new_m[n] for n in WEIGHT_NAMES], *[new_v[n] for n in WEIGHT_NAMES])
```

```python
import jax
import jax.numpy as jnp
from jax import lax
from jax.experimental import pallas as pl
from jax.experimental.pallas import tpu as pltpu

F32 = jnp.float32
BF16 = jnp.bfloat16
MESH = pl.DeviceIdType.MESH

N_DEV = 8
NORM_EPS = 1e-6
HALO = 32
LANES = 128
SUBLANES = 8
VMEM_LIMIT_BYTES = 56 * 2**20

ADAM_LR = 0.001
ADAM_B1 = 0.9
ADAM_B2 = 0.999
ADAM_EPS = 1e-08
ADAM_WD = 0.01
ADAM_STEP = 10

WEIGHT_NAMES = ('g_mix', 'w_in', 'b_gate', 'conv_a_w', 'conv_a_b', 'ln_a_g', 'ln_a_b', 'w_a_out',
                'conv_b_w', 'w_b_out', 'w_o', 'g_ffn', 'w_up', 'conv_f_w', 'conv_f_b', 'w_down',
                'g_ple', 'w_ple', 'w_ple_gate', 'g_final')
BIG_AXIS = {'w_in': 0, 'w_up': 0, 'w_a_out': 1, 'w_b_out': 1, 'w_o': 0, 'w_down': 0, 'w_ple': 1,
            'w_ple_gate': 0}
TRANSPOSED = ('w_in', 'w_up')
CONV_SHARDED = ('conv_a_w', 'conv_b_w', 'conv_f_w')


def _dot(a, b):
    return jnp.dot(a, b, preferred_element_type=F32)


def _dot_nt(a, b):
    return lax.dot_general(a, b, (((1,), (1,)), ((), ())), preferred_element_type=F32)


def _dot_tn(a, b):
    return lax.dot_general(a, b, (((0,), (0,)), ((), ())), preferred_element_type=F32)


def _sigmoid(v):
    return jax.nn.sigmoid(v)


def _token_tile(t, cap=512):
    return cap if (t % cap == 0 and t > 512) else 128


def _chunk(n, limit=512):
    for c in range(limit - limit % LANES, 0, -LANES):
        if n % c == 0:
            return c
    return n


def _row_tile(rows):
    for c in (512, 256, 128, 64, 32, 16, 8):
        if rows % c == 0:
            return c
    return rows


def _rows(tm, width):
    return pl.BlockSpec((tm, width), lambda i: (i, 0))


def _rows_rev(tm, width, nt):
    return pl.BlockSpec((tm, width), lambda i: (nt - 1 - i, 0))


def _whole(shape):
    nd = len(shape)
    return pl.BlockSpec(tuple(shape), lambda i: (0,) * nd)


def _resident(shape):
    nd = len(shape)
    return pl.BlockSpec(tuple(shape), lambda i: (0,) * nd, pipeline_mode=pl.Buffered(1))


def _sds(shape, dtype):
    return jax.ShapeDtypeStruct(tuple(shape), dtype)


def _rms_stats(xv):
    r = lax.rsqrt(jnp.mean(xv * xv, axis=-1, keepdims=True) + NORM_EPS)
    return xv * r, r


def _rms_bwd(dy, xh, r, g):
    dxh = dy * g
    dx = r * (dxh - xh * jnp.mean(dxh * xh, axis=-1, keepdims=True))
    return dx, jnp.sum(dy * xh, axis=0, keepdims=True)


GELU_C0 = 0.7978845608028654
GELU_C1 = GELU_C0 * 0.044715


def _gelu_tanh(v):
    v2 = v * v
    t = jnp.tanh(v * (GELU_C0 + GELU_C1 * v2))
    q = 1.0 + t
    hv = 0.5 * v
    grad = 0.5 * q + hv * (1.0 - t * t) * (GELU_C0 + (3.0 * GELU_C1) * v2)
    return hv * q, grad


ROW_CHUNK = 32
LANE_CHUNK = 512


def _residues(taps):
    return [0] + sorted({off % SUBLANES for _, off in taps} - {0})


def _fill_rotations(rot_ref, residues, length):
    for plane, r in enumerate(residues):
        if r:
            rot_ref[plane, 0:length, :] = rot_ref[0, pl.ds(r, length), :]


def _broadcast_rows(dst_ref, src_ref, count):
    for k in range(count):
        dst_ref[k] = jnp.broadcast_to(src_ref[k:k + 1, :], dst_ref.shape[1:])


def _lane_chunks(width):
    return [(c0, min(LANE_CHUNK, width - c0)) for c0 in range(0, width, LANE_CHUNK)]


def _tap_conv(rot_ref, taps, wb_ref, out_ref, tm, bias_plane=None):
    residues = _residues(taps)
    plane = {r: p for p, r in enumerate(residues)}
    blocks = ROW_CHUNK // SUBLANES
    width = out_ref.shape[1]

    def chunk(c, state):
        r0 = c * ROW_CHUNK
        for c0, cw in _lane_chunks(width):
            accs = [None if bias_plane is None else wb_ref[bias_plane, :, c0:c0 + cw]] * blocks
            for k, off in taps:
                wk = wb_ref[k, :, c0:c0 + cw]
                base = off - off % SUBLANES
                for j in range(blocks):
                    at = pl.multiple_of(r0 + base + SUBLANES * j, SUBLANES)
                    term = wk * rot_ref[plane[off % SUBLANES], pl.ds(at, SUBLANES), c0:c0 + cw]
                    accs[j] = term if accs[j] is None else accs[j] + term
            for j in range(blocks):
                at = pl.multiple_of(r0 + SUBLANES * j, SUBLANES)
                out_ref[pl.ds(at, SUBLANES), c0:c0 + cw] = accs[j]
        return state

    lax.fori_loop(0, tm // ROW_CHUNK, chunk, 0)


def _tap_wgrad(rot_ref, taps, dy_ref, dy_plane, acc_ref, tm):
    residues = _residues(taps)
    plane = {r: p for p, r in enumerate(residues)}
    blocks = ROW_CHUNK // SUBLANES
    width = acc_ref.shape[2]

    def chunk(c, state):
        r0 = c * ROW_CHUNK
        for c0, cw in _lane_chunks(width):
            dys = [dy_ref[dy_plane, pl.ds(pl.multiple_of(r0 + SUBLANES * j, SUBLANES), SUBLANES), c0:c0 + cw]
                   for j in range(blocks)]
            for k, off in taps:
                base = off - off % SUBLANES
                part = None
                for j in range(blocks):
                    at = pl.multiple_of(r0 + base + SUBLANES * j, SUBLANES)
                    term = dys[j] * rot_ref[plane[off % SUBLANES], pl.ds(at, SUBLANES), c0:c0 + cw]
                    part = term if part is None else part + term
                acc_ref[k, :, c0:c0 + cw] += part
        return state

    lax.fori_loop(0, tm // ROW_CHUNK, chunk, 0)


def _fwd_taps(width):
    return [(k, HALO - (width - 1) + k) for k in range(width)]


def _bwd_taps(width):
    return [(k, width - 1 - k) for k in range(width)]


def _my_index():
    return 4 * lax.axis_index("x") + 2 * lax.axis_index("y") + lax.axis_index("c")


def _mesh_id(idx):
    return (idx // 4, (idx // 2) % 2, idx % 2)


def _slab(ref, axis, idx, width):
    at = [slice(None)] * len(ref.shape)
    at[axis] = pl.ds(pl.multiple_of(idx * width, width), width)
    return ref.at[tuple(at)]


class _Exchange:
    def __init__(self, inputs, out_shape):
        n = len(inputs)
        self.inputs = list(inputs)
        self.out_shape = list(out_shape)
        self.sems = [pltpu.SemaphoreType.DMA((n, N_DEV - 1)), pltpu.SemaphoreType.DMA((n, N_DEV - 1)),
                     pltpu.SemaphoreType.DMA((n,))]
        self.results = None

    def _local(self, ins, outs, k, me):
        raise NotImplementedError

    def _remote(self, ins, outs, k, me, sender, receiver):
        raise NotImplementedError

    def start(self, ins, outs, sems):
        send_sems, recv_sems, local_sems = sems
        me = _my_index()
        for k in range(len(self.inputs)):
            src, dst = self._local(ins, outs, k, me)
            pltpu.make_async_copy(src, dst, local_sems.at[k]).start()
            for dist in range(1, N_DEV):
                peer = (me + dist) % N_DEV
                src, dst = self._remote(ins, outs, k, me, me, peer)
                pltpu.make_async_remote_copy(
                    src_ref=src, dst_ref=dst, send_sem=send_sems.at[k, dist - 1],
                    recv_sem=recv_sems.at[k, dist - 1], device_id=_mesh_id(peer), device_id_type=MESH).start()

    def wait(self, ins, outs, sems):
        send_sems, recv_sems, local_sems = sems
        me = _my_index()
        for k in range(len(self.inputs)):
            for dist in range(1, N_DEV):
                sender = (me + N_DEV - dist) % N_DEV
                src, dst = self._remote(ins, outs, k, me, sender, me)
                cp = pltpu.make_async_remote_copy(
                    src_ref=src, dst_ref=dst, send_sem=send_sems.at[k, dist - 1],
                    recv_sem=recv_sems.at[k, dist - 1], device_id=_mesh_id(sender), device_id_type=MESH)
                cp.wait_send()
                cp.wait_recv()
            src, dst = self._local(ins, outs, k, me)
            pltpu.make_async_copy(src, dst, local_sems.at[k]).wait()


class _Gather(_Exchange):
    def __init__(self, items):
        self.items = list(items)
        out_shape = []
        for shards, layer, axis in self.items:
            shape = list(shards.shape if layer is None else shards.shape[1:])
            shape[axis] *= N_DEV
            out_shape.append(_sds(shape, shards.dtype))
        super().__init__([it[0] for it in self.items], out_shape)

    def _src(self, ins, k):
        layer = self.items[k][1]
        return ins[k] if layer is None else ins[k].at[layer]

    def _place(self, outs, k, idx):
        axis = self.items[k][2]
        return _slab(outs[k], axis, idx, self.out_shape[k].shape[axis] // N_DEV)

    def _local(self, ins, outs, k, me):
        return self._src(ins, k), self._place(outs, k, me)

    def _remote(self, ins, outs, k, me, sender, receiver):
        return self._src(ins, k), self._place(outs, k, sender)


class _Scatter(_Exchange):
    def __init__(self, items):
        self.items = list(items)
        out_shape = []
        for partial, axis in self.items:
            shape = list(partial.shape)
            shape[axis] //= N_DEV
            out_shape.append(_sds([N_DEV] + shape, partial.dtype))
        super().__init__([it[0] for it in self.items], out_shape)

    def _take(self, ins, k, idx):
        axis = self.items[k][1]
        return _slab(ins[k], axis, idx, self.items[k][0].shape[axis] // N_DEV)

    def _local(self, ins, outs, k, me):
        return self._take(ins, k, me), outs[k].at[me]

    def _remote(self, ins, outs, k, me, sender, receiver):
        return self._take(ins, k, receiver), outs[k].at[sender]


def _run_exchange(exchange, name):
    n = len(exchange.inputs)

    def body(*refs):
        ins, outs, sems = refs[:n], refs[n:2 * n], refs[2 * n:]
        exchange.start(ins, outs, sems)
        exchange.wait(ins, outs, sems)

    any_spec = pl.BlockSpec(memory_space=pl.ANY)
    exchange.results = pl.pallas_call(
        body, name=name, in_specs=[any_spec] * n, out_specs=[any_spec] * n, out_shape=exchange.out_shape,
        scratch_shapes=exchange.sems)(*exchange.inputs)
    return exchange.results


def _call(body, *, name, grid, in_specs, out_specs, out_shape, args, scratch=(), carry=None):
    in_specs, out_specs, out_shape, scratch = list(in_specs), list(out_specs), list(out_shape), list(scratch)
    params = pltpu.CompilerParams(dimension_semantics=("arbitrary",) * len(grid),
                                  vmem_limit_bytes=VMEM_LIMIT_BYTES)
    if carry is None:
        return pl.pallas_call(body, name=name, grid=grid, in_specs=in_specs, out_specs=out_specs,
                              out_shape=out_shape, scratch_shapes=scratch, compiler_params=params)(*args)
    n_in, n_out, n_scr, n_x = len(in_specs), len(out_specs), len(scratch), len(carry.inputs)

    def with_exchange(*refs):
        core_in, x_in = refs[:n_in], refs[n_in:n_in + n_x]
        refs = refs[n_in + n_x:]
        core_out, x_out = refs[:n_out], refs[n_out:n_out + n_x]
        refs = refs[n_out + n_x:]
        core_scr, sems = refs[:n_scr], refs[n_scr:]
        first = pl.program_id(0) == 0
        last = pl.program_id(0) == grid[0] - 1
        for axis in range(1, len(grid)):
            first = jnp.logical_and(first, pl.program_id(axis) == 0)
            last = jnp.logical_and(last, pl.program_id(axis) == grid[axis] - 1)

        @pl.when(first)
        def _():
            carry.start(x_in, x_out, sems)

        body(*core_in, *core_out, *core_scr)

        @pl.when(last)
        def _():
            carry.wait(x_in, x_out, sems)

    any_spec = pl.BlockSpec(memory_space=pl.ANY)
    outs = pl.pallas_call(
        with_exchange, name=name, grid=grid, in_specs=in_specs + [any_spec] * n_x,
        out_specs=out_specs + [any_spec] * n_x, out_shape=out_shape + carry.out_shape,
        scratch_shapes=scratch + carry.sems, compiler_params=params)(*args, *carry.inputs)
    carry.results = outs[n_out:]
    return outs[:n_out]


def _all_reduce_small(pack, name):
    rows = pack.shape[0]

    def body(x_ref, o_ref, land, send_sems, recv_sems):
        me = _my_index()
        land[me] = x_ref[...]
        for dist in range(1, N_DEV):
            pltpu.make_async_remote_copy(
                src_ref=x_ref, dst_ref=land.at[me],
                send_sem=send_sems.at[dist - 1], recv_sem=recv_sems.at[dist - 1],
                device_id=_mesh_id((me + dist) % N_DEV), device_id_type=MESH).start()
        for dist in range(1, N_DEV):
            src_dev = (me + N_DEV - dist) % N_DEV
            cp = pltpu.make_async_remote_copy(
                src_ref=x_ref, dst_ref=land.at[src_dev],
                send_sem=send_sems.at[dist - 1], recv_sem=recv_sems.at[dist - 1],
                device_id=_mesh_id(src_dev), device_id_type=MESH)
            cp.wait_send()
            cp.wait_recv()
        acc = land[0]
        for j in range(1, N_DEV):
            acc = acc + land[j]
        o_ref[...] = acc

    vmem = pl.BlockSpec(memory_space=pltpu.VMEM)
    return pl.pallas_call(
        body, name=name, in_specs=[vmem], out_specs=vmem, out_shape=_sds(pack.shape, F32),
        scratch_shapes=[pltpu.VMEM((N_DEV, rows, LANES), F32), pltpu.SemaphoreType.DMA((N_DEV - 1,)),
                        pltpu.SemaphoreType.DMA((N_DEV - 1,))])(pack)


def _norm_matmul(x, g, wt, name, carry=None):
    t, d = x.shape
    n = wt.shape[0]
    tm, nc = _token_tile(t), _chunk(n)

    def body(x_ref, g_ref, wt_ref, h_ref, o_ref):
        xh, _ = _rms_stats(x_ref[...])
        h = (xh * g_ref[...]).astype(BF16)
        h_ref[...] = h
        for n0 in range(0, n, nc):
            o_ref[:, n0:n0 + nc] = _dot_nt(h, wt_ref[n0:n0 + nc, :]).astype(BF16)

    return _call(body, name=name, grid=(t // tm,),
                 in_specs=[_rows(tm, d), _whole(g.shape), _resident(wt.shape)],
                 out_specs=[_rows(tm, d), _rows(tm, n)],
                 out_shape=[_sds((t, d), BF16), _sds((t, n), BF16)], args=(x, g, wt), carry=carry)


def _fwd_branch(z, caw, cab, lng, lnb, cbw, dc, name, carry=None):
    t = z.shape[0]
    tm = _token_tile(t)
    ka, kb = caw.shape[0], cbw.shape[0]
    taps_a, taps_b = _fwd_taps(ka), _fwd_taps(kb)
    res_a, res_b = _residues(taps_a), _residues(taps_b)
    span = HALO + tm - SUBLANES

    def body(z_ref, caw_ref, cab_ref, lng_ref, lnb_ref, cbw_ref, ac_ref, act_ref, s_ref,
             rot_a, rot_b, wb_a, wb_b, cb):
        @pl.when(pl.program_id(0) == 0)
        def _():
            rot_a[0, 0:HALO, :] = jnp.zeros((HALO, dc), F32)
            rot_b[0, 0:HALO, :] = jnp.zeros((HALO, dc), F32)
            _broadcast_rows(wb_a, caw_ref, ka)
            wb_a[ka] = jnp.broadcast_to(cab_ref[...], (SUBLANES, dc))
            _broadcast_rows(wb_b, cbw_ref, kb)

        a_val = z_ref[:, 0:dc].astype(F32)
        a_gt = z_ref[:, dc:2 * dc].astype(F32)
        rot_a[0, HALO:HALO + tm, :] = a_val * _sigmoid(a_gt)
        _fill_rotations(rot_a, res_a, span)
        _tap_conv(rot_a, taps_a, wb_a, ac_ref, tm, bias_plane=ka)
        ac = ac_ref[...]
        mu = jnp.mean(ac, axis=-1, keepdims=True)
        xc = ac - mu
        var = jnp.mean(xc * xc, axis=-1, keepdims=True)
        ln = xc * lax.rsqrt(var + NORM_EPS) * lng_ref[...] + lnb_ref[...]
        act_ref[...] = (ln * _sigmoid(ln)).astype(BF16)
        rot_a[0, 0:HALO, :] = rot_a[0, tm:tm + HALO, :]

        sc_c = z_ref[:, 3 * dc:4 * dc].astype(F32)
        sc_v = z_ref[:, 4 * dc:5 * dc].astype(F32)
        rot_b[0, HALO:HALO + tm, :] = sc_c * sc_v
        _fill_rotations(rot_b, res_b, span)
        _tap_conv(rot_b, taps_b, wb_b, cb, tm)
        s_ref[...] = (z_ref[:, 2 * dc:3 * dc].astype(F32) * cb[...]).astype(BF16)
        rot_b[0, 0:HALO, :] = rot_b[0, tm:tm + HALO, :]

    return _call(body, name=name, grid=(t // tm,),
                 in_specs=[_rows(tm, 5 * dc), _whole(caw.shape), _whole(cab.shape), _whole(lng.shape),
                           _whole(lnb.shape), _whole(cbw.shape)],
                 out_specs=[_rows(tm, dc), _rows(tm, dc), _rows(tm, dc)],
                 out_shape=[_sds((t, dc), F32), _sds((t, dc), BF16), _sds((t, dc), BF16)],
                 scratch=[pltpu.VMEM((len(res_a), HALO + tm, dc), F32), pltpu.VMEM((len(res_b), HALO + tm, dc), F32),
                          pltpu.VMEM((ka + 1, SUBLANES, dc), F32), pltpu.VMEM((kb, SUBLANES, dc), F32),
                          pltpu.VMEM((tm, dc), F32)],
                 args=(z, caw, cab, lng, lnb, cbw), carry=carry)


def _fwd_merge(x, z, bg, a_act, s, wa, wb, wo, name, carry=None):
    t, d = x.shape
    n = z.shape[1]
    dc = a_act.shape[1]
    tm = _token_tile(t)
    o5 = n - 2 * d

    def body(x_ref, z_ref, bg_ref, act_ref, s_ref, wa_ref, wb_ref, wo_ref, o_ref):
        ya = _dot(act_ref[...], wa_ref[...])
        yb = _dot(s_ref[...], wb_ref[...])
        ga = _sigmoid(z_ref[:, o5:o5 + d].astype(F32) + bg_ref[:, 0:d])
        gb = _sigmoid(z_ref[:, o5 + d:n].astype(F32) + bg_ref[:, d:2 * d])
        m = (ga * ya + gb * yb).astype(BF16)
        o_ref[...] = x_ref[...] + _dot(m, wo_ref[...])

    return _call(body, name=name, grid=(t // tm,),
                 in_specs=[_rows(tm, d), _rows(tm, n), _whole(bg.shape), _rows(tm, dc), _rows(tm, dc),
                           _resident(wa.shape), _resident(wb.shape), _resident(wo.shape)],
                 out_specs=[_rows(tm, d)], out_shape=[_sds((t, d), F32)],
                 args=(x, z, bg, a_act, s, wa, wb, wo), carry=carry)[0]


def _fwd_down(x, u, cfw, cfb, wd, name, carry=None):
    t, d = x.shape
    f = u.shape[1] // 2
    tm = _token_tile(t, 256)
    kf = cfw.shape[0]

    taps = _fwd_taps(kf)
    residues = _residues(taps)

    def body(x_ref, u_ref, cfw_ref, cfb_ref, wd_ref, o_ref, act_ref, rot_u, wb, fg):
        @pl.when(pl.program_id(0) == 0)
        def _():
            rot_u[0, 0:HALO, :] = jnp.zeros((HALO, f), F32)
            _broadcast_rows(wb, cfw_ref, kf)
            wb[kf] = jnp.broadcast_to(cfb_ref[...], (SUBLANES, f))

        rot_u[0, HALO:HALO + tm, :] = u_ref[:, 0:f].astype(F32)
        _fill_rotations(rot_u, residues, HALO + tm - SUBLANES)
        _tap_conv(rot_u, taps, wb, fg, tm, bias_plane=kf)
        gl, _ = _gelu_tanh(fg[...])
        act = (gl * u_ref[:, f:2 * f].astype(F32)).astype(BF16)
        act_ref[...] = act
        o_ref[...] = x_ref[...] + _dot(act, wd_ref[...])
        rot_u[0, 0:HALO, :] = rot_u[0, tm:tm + HALO, :]

    return _call(body, name=name, grid=(t // tm,),
                 in_specs=[_rows(tm, d), _rows(tm, 2 * f), _whole(cfw.shape), _whole(cfb.shape),
                           _resident(wd.shape)],
                 out_specs=[_rows(tm, d), _rows(tm, f)], out_shape=[_sds((t, d), F32), _sds((t, f), BF16)],
                 scratch=[pltpu.VMEM((len(residues), HALO + tm, f), F32), pltpu.VMEM((kf + 1, SUBLANES, f), F32),
                          pltpu.VMEM((tm, f), F32)],
                 args=(x, u, cfw, cfb, wd), carry=carry)


def _fwd_ple(x, g, wpg, p, wple, name, carry=None):
    t, d = x.shape
    pd = p.shape[1]
    tm = _token_tile(t)

    def body(x_ref, g_ref, wpg_ref, p_ref, wple_ref, o_ref):
        xv = x_ref[...]
        xh, _ = _rms_stats(xv)
        lg = _dot((xh * g_ref[...]).astype(BF16), wpg_ref[...])
        pp = _dot(p_ref[...].astype(BF16), wple_ref[...])
        o_ref[...] = xv + _sigmoid(lg) * pp

    return _call(body, name=name, grid=(t // tm,),
                 in_specs=[_rows(tm, d), _whole(g.shape), _resident(wpg.shape), _rows(tm, pd),
                           _resident(wple.shape)],
                 out_specs=[_rows(tm, d)], out_shape=[_sds((t, d), F32)],
                 args=(x, g, wpg, p, wple), carry=carry)[0]


def _loss_bwd(x, g, target, name):
    t, d = x.shape
    tm = _token_tile(t)

    def body(x_ref, g_ref, t_ref, dx_ref, dg_ref, loss_ref):
        @pl.when(pl.program_id(0) == 0)
        def _():
            dg_ref[...] = jnp.zeros_like(dg_ref)
            loss_ref[...] = jnp.zeros_like(loss_ref)

        xh, r = _rms_stats(x_ref[...])
        err = xh * g_ref[...] - t_ref[...]
        sq = jnp.sum(jnp.sum(err * err, axis=0, keepdims=True), axis=1, keepdims=True)
        loss_ref[...] += jnp.broadcast_to(0.5 * sq / d, loss_ref.shape)
        dx, dg = _rms_bwd(err / d, xh, r, g_ref[...])
        dx_ref[...] = dx
        dg_ref[...] += dg

    return _call(body, name=name, grid=(t // tm,),
                 in_specs=[_rows(tm, d), _whole(g.shape), _rows(tm, d)],
                 out_specs=[_rows(tm, d), _whole((1, d)), _whole((SUBLANES, LANES))],
                 out_shape=[_sds((t, d), F32), _sds((1, d), F32), _sds((SUBLANES, LANES), F32)],
                 args=(x, g, target))


def _bwd_ple(dy, x, g, wpg, p, wple, name, carry=None):
    t, d = x.shape
    pd = p.shape[1]
    tm = _token_tile(t)
    nt = t // tm

    def body(dy_ref, x_ref, g_ref, wpg_ref, p_ref, wple_ref, dx_ref, dwpg_ref, dwple_ref, dg_ref,
             acc_pg, acc_ple):
        i = pl.program_id(0)

        @pl.when(i == 0)
        def _():
            acc_pg[...] = jnp.zeros_like(acc_pg)
            acc_ple[...] = jnp.zeros_like(acc_ple)
            dg_ref[...] = jnp.zeros_like(dg_ref)

        dyv = dy_ref[...]
        xh, r = _rms_stats(x_ref[...])
        h = (xh * g_ref[...]).astype(BF16)
        pb = p_ref[...].astype(BF16)
        pg = _sigmoid(_dot(h, wpg_ref[...]))
        pp = _dot(pb, wple_ref[...])
        dpp = (dyv * pg).astype(BF16)
        dlg = (dyv * pp * pg * (1.0 - pg)).astype(BF16)
        acc_ple[...] += _dot_tn(pb, dpp)
        acc_pg[...] += _dot_tn(h, dlg)
        dx, dg = _rms_bwd(_dot_nt(dlg, wpg_ref[...]), xh, r, g_ref[...])
        dx_ref[...] = dyv + dx
        dg_ref[...] += dg

        @pl.when(i == nt - 1)
        def _():
            dwpg_ref[...] = acc_pg[...].astype(BF16)
            dwple_ref[...] = acc_ple[...].astype(BF16)

    return _call(body, name=name, grid=(nt,),
                 in_specs=[_rows(tm, d), _rows(tm, d), _whole(g.shape), _resident(wpg.shape), _rows(tm, pd),
                           _resident(wple.shape)],
                 out_specs=[_rows(tm, d), _whole((d, d)), _whole((pd, d)), _whole((1, d))],
                 out_shape=[_sds((t, d), F32), _sds((d, d), BF16), _sds((pd, d), BF16), _sds((1, d), F32)],
                 scratch=[pltpu.VMEM((d, d), F32), pltpu.VMEM((pd, d), F32)],
                 args=(dy, x, g, wpg, p, wple), carry=carry)


def _bwd_down(dy, u, cfw, cfb, wd, name, carry=None):
    t, d = dy.shape
    f = u.shape[1] // 2
    tm = _token_tile(t, 256)
    nt = t // tm
    kf = cfw.shape[0]
    per = tm // HALO
    fwd, bwd = _fwd_taps(kf), _bwd_taps(kf)
    span = HALO + tm - SUBLANES

    def body(dy_ref, u_ref, up_ref, cfw_ref, cfb_ref, wd_ref, du_ref, dcw_ref, dcb_ref,
             rot_u, rot_g, wb, acc, conv_out):
        i = pl.program_id(0)

        @pl.when(i == 0)
        def _():
            for ref in (dcw_ref, dcb_ref, acc):
                ref[...] = jnp.zeros_like(ref)
            rot_g[0, tm:tm + HALO, :] = jnp.zeros((HALO, f), F32)
            _broadcast_rows(wb, cfw_ref, kf)
            wb[kf] = jnp.broadcast_to(cfb_ref[...], (SUBLANES, f))

        first = (i == nt - 1).astype(F32)
        rot_u[0, 0:HALO, :] = up_ref[:, 0:f].astype(F32) * (1.0 - first)
        rot_u[0, HALO:HALO + tm, :] = u_ref[:, 0:f].astype(F32)
        _fill_rotations(rot_u, _residues(fwd), span)
        _tap_conv(rot_u, fwd, wb, conv_out, tm, bias_plane=kf)
        uv = u_ref[:, f:2 * f].astype(F32)
        df = _dot_nt(dy_ref[...].astype(BF16), wd_ref[...])
        gl, dgl = _gelu_tanh(conv_out[...])
        du_ref[:, f:2 * f] = (df * gl).astype(BF16)
        dfg = df * uv * dgl
        dcb_ref[...] += jnp.sum(dfg, axis=0, keepdims=True)
        rot_g[0, 0:tm, :] = dfg
        _tap_wgrad(rot_u, fwd, rot_g, 0, acc, tm)
        _fill_rotations(rot_g, _residues(bwd), span)
        _tap_conv(rot_g, bwd, wb, conv_out, tm)
        du_ref[:, 0:f] = conv_out[...].astype(BF16)
        rot_g[0, tm:tm + HALO, :] = rot_g[0, 0:HALO, :]

        @pl.when(i == nt - 1)
        def _():
            dcw_ref[0:kf, :] = jnp.sum(acc[...], axis=1)

    def planes(taps):
        return pltpu.VMEM((len(_residues(taps)), HALO + tm, f), F32)

    prev_rows = pl.BlockSpec((HALO, 2 * f), lambda i: (jnp.maximum((nt - 1 - i) * per - 1, 0), 0))
    return _call(body, name=name, grid=(nt,),
                 in_specs=[_rows_rev(tm, d, nt), _rows_rev(tm, 2 * f, nt), prev_rows, _whole(cfw.shape),
                           _whole(cfb.shape), _resident(wd.shape)],
                 out_specs=[_rows_rev(tm, 2 * f, nt), _whole((SUBLANES, f)), _whole((1, f))],
                 out_shape=[_sds((t, 2 * f), BF16), _sds((SUBLANES, f), F32), _sds((1, f), F32)],
                 scratch=[planes(fwd), planes(bwd), pltpu.VMEM((kf + 1, SUBLANES, f), F32),
                          pltpu.VMEM((kf, SUBLANES, f), F32), pltpu.VMEM((tm, f), F32)],
                 args=(dy, u, u, cfw, cfb, wd), carry=carry)


def _bwd_norm_matmul(dout, wt, x, g, dres, name, carry=None):
    t, d = x.shape
    n = dout.shape[1]
    tm = _token_tile(t)

    def body(do_ref, wt_ref, x_ref, g_ref, dres_ref, dx_ref, dg_ref):
        @pl.when(pl.program_id(0) == 0)
        def _():
            dg_ref[...] = jnp.zeros_like(dg_ref)

        dh = _dot(do_ref[...], wt_ref[...])
        xh, r = _rms_stats(x_ref[...])
        dx, dg = _rms_bwd(dh, xh, r, g_ref[...])
        dx_ref[...] = dres_ref[...] + dx
        dg_ref[...] += dg

    return _call(body, name=name, grid=(t // tm,),
                 in_specs=[_rows(tm, n), _resident(wt.shape), _rows(tm, d), _whole(g.shape), _rows(tm, d)],
                 out_specs=[_rows(tm, d), _whole((1, d))],
                 out_shape=[_sds((t, d), F32), _sds((1, d), F32)],
                 args=(dout, wt, x, g, dres), carry=carry)


def _wgrad_tn(a, b, name, carry=None):
    t, n = a.shape
    d = b.shape[1]
    tt = 1024 if t % 1024 == 0 else _token_tile(t)
    tn = _chunk(n, 1536)
    nt = t // tt

    def body(a_ref, b_ref, o_ref, acc):
        k = pl.program_id(1)

        @pl.when(k == 0)
        def _():
            acc[...] = jnp.zeros_like(acc)

        acc[...] += _dot_tn(a_ref[...].astype(BF16), b_ref[...].astype(BF16))

        @pl.when(k == nt - 1)
        def _():
            o_ref[...] = acc[...].astype(BF16)

    return _call(body, name=name, grid=(n // tn, nt),
                 in_specs=[pl.BlockSpec((tt, tn), lambda j, k: (k, j)), pl.BlockSpec((tt, d), lambda j, k: (k, 0))],
                 out_specs=[pl.BlockSpec((tn, d), lambda j, k: (j, 0))], out_shape=[_sds((n, d), BF16)],
                 scratch=[pltpu.VMEM((tn, d), F32)], args=(a, b), carry=carry)[0]


def _bwd_merge(dy, z, bg, a_act, s, wa, wb, wo, name, carry=None):
    t, d = dy.shape
    n = z.shape[1]
    dc = a_act.shape[1]
    tm = _token_tile(t, 256)
    nt = t // tm
    o5 = n - 2 * d

    def body(dy_ref, z_ref, bg_ref, act_ref, s_ref, wa_ref, wb_ref, wo_ref,
             dact_ref, ds_ref, dgl_ref, dwo_ref, dwa_ref, dwb_ref, dbg_ref, acc_o, acc_a, acc_b):
        i = pl.program_id(0)

        @pl.when(i == 0)
        def _():
            acc_o[...] = jnp.zeros_like(acc_o)
            acc_a[...] = jnp.zeros_like(acc_a)
            acc_b[...] = jnp.zeros_like(acc_b)
            dbg_ref[...] = jnp.zeros_like(dbg_ref)

        dyb = dy_ref[...].astype(BF16)
        dm = _dot_nt(dyb, wo_ref[...])
        ya = _dot(act_ref[...], wa_ref[...])
        yb = _dot(s_ref[...], wb_ref[...])
        ga = _sigmoid(z_ref[:, o5:o5 + d].astype(F32) + bg_ref[:, 0:d])
        gb = _sigmoid(z_ref[:, o5 + d:n].astype(F32) + bg_ref[:, d:2 * d])
        acc_o[...] += _dot_tn((ga * ya + gb * yb).astype(BF16), dyb)
        dya = (dm * ga).astype(BF16)
        dyb2 = (dm * gb).astype(BF16)
        acc_a[...] += _dot_tn(act_ref[...], dya)
        acc_b[...] += _dot_tn(s_ref[...], dyb2)
        dact_ref[...] = _dot_nt(dya, wa_ref[...])
        ds_ref[...] = _dot_nt(dyb2, wb_ref[...])
        dla = dm * ya * ga * (1.0 - ga)
        dlb = dm * yb * gb * (1.0 - gb)
        dgl_ref[:, 0:d] = dla.astype(BF16)
        dgl_ref[:, d:2 * d] = dlb.astype(BF16)
        dbg_ref[:, 0:d] += jnp.sum(dla, axis=0, keepdims=True)
        dbg_ref[:, d:2 * d] += jnp.sum(dlb, axis=0, keepdims=True)

        @pl.when(i == nt - 1)
        def _():
            dwo_ref[...] = acc_o[...].astype(BF16)
            dwa_ref[...] = acc_a[...].astype(BF16)
            dwb_ref[...] = acc_b[...].astype(BF16)

    return _call(body, name=name, grid=(nt,),
                 in_specs=[_rows(tm, d), _rows(tm, n), _whole(bg.shape), _rows(tm, dc), _rows(tm, dc),
                           _resident(wa.shape), _resident(wb.shape), _resident(wo.shape)],
                 out_specs=[_rows(tm, dc), _rows(tm, dc), _rows(tm, 2 * d), _whole((d, d)), _whole((dc, d)),
                            _whole((dc, d)), _whole((1, 2 * d))],
                 out_shape=[_sds((t, dc), F32), _sds((t, dc), F32), _sds((t, 2 * d), BF16), _sds((d, d), BF16),
                            _sds((dc, d), BF16), _sds((dc, d), BF16), _sds((1, 2 * d), F32)],
                 scratch=[pltpu.VMEM((d, d), F32), pltpu.VMEM((dc, d), F32), pltpu.VMEM((dc, d), F32)],
                 args=(dy, z, bg, a_act, s, wa, wb, wo), carry=carry)


def _bwd_branch(dact, ds, z, dgl, a_conv, caw, lng, lnb, cbw, name, carry=None):
    t, n = z.shape
    dc = a_conv.shape[1]
    tm = _token_tile(t, 256)
    nt = t // tm
    ka, kb = caw.shape[0], cbw.shape[0]
    per = tm // HALO
    fwd_a, fwd_b, bwd_a, bwd_b = _fwd_taps(ka), _fwd_taps(kb), _bwd_taps(ka), _bwd_taps(kb)
    span = HALO + tm - SUBLANES

    def body(dact_ref, ds_ref, z_ref, zp_ref, dgl_ref, ac_ref, caw_ref, lng_ref, lnb_ref, cbw_ref,
             dz_ref, dcaw_ref, dcab_ref, dlng_ref, dlnb_ref, dcbw_ref,
             rot_a, rot_da, rot_c, rot_dc, wb_a, wb_b, acc_a, acc_b, conv_out):
        i = pl.program_id(0)

        @pl.when(i == 0)
        def _():
            for ref in (dcaw_ref, dcab_ref, dlng_ref, dlnb_ref, dcbw_ref, acc_a, acc_b):
                ref[...] = jnp.zeros_like(ref)
            rot_da[0, tm:tm + HALO, :] = jnp.zeros((HALO, dc), F32)
            rot_dc[0, tm:tm + HALO, :] = jnp.zeros((HALO, dc), F32)
            _broadcast_rows(wb_a, caw_ref, ka)
            _broadcast_rows(wb_b, cbw_ref, kb)

        keep = 1.0 - (i == nt - 1).astype(F32)
        a_val = z_ref[:, 0:dc].astype(F32)
        sg = _sigmoid(z_ref[:, dc:2 * dc].astype(F32))
        rot_a[0, 0:HALO, :] = zp_ref[:, 0:dc].astype(F32) * _sigmoid(zp_ref[:, dc:2 * dc].astype(F32)) * keep
        rot_a[0, HALO:HALO + tm, :] = a_val * sg
        _fill_rotations(rot_a, _residues(fwd_a), span)

        ac = ac_ref[...]
        mu = jnp.mean(ac, axis=-1, keepdims=True)
        xc = ac - mu
        rstd = lax.rsqrt(jnp.mean(xc * xc, axis=-1, keepdims=True) + NORM_EPS)
        xh = xc * rstd
        ln = xh * lng_ref[...] + lnb_ref[...]
        sl = _sigmoid(ln)
        dln = dact_ref[...] * (sl * (1.0 + ln * (1.0 - sl)))
        dlng_ref[...] += jnp.sum(dln * xh, axis=0, keepdims=True)
        dlnb_ref[...] += jnp.sum(dln, axis=0, keepdims=True)
        dxh = dln * lng_ref[...]
        dac = rstd * (dxh - jnp.mean(dxh, axis=-1, keepdims=True)
                      - xh * jnp.mean(dxh * xh, axis=-1, keepdims=True))
        dcab_ref[...] += jnp.sum(dac, axis=0, keepdims=True)
        rot_da[0, 0:tm, :] = dac
        _tap_wgrad(rot_a, fwd_a, rot_da, 0, acc_a, tm)
        _fill_rotations(rot_da, _residues(bwd_a), span)
        _tap_conv(rot_da, bwd_a, wb_a, conv_out, tm)
        rot_da[0, tm:tm + HALO, :] = rot_da[0, 0:HALO, :]
        da = conv_out[...]
        dz_ref[:, 0:dc] = (da * sg).astype(BF16)
        dz_ref[:, dc:2 * dc] = (da * a_val * sg * (1.0 - sg)).astype(BF16)

        sc_b = z_ref[:, 2 * dc:3 * dc].astype(F32)
        sc_c = z_ref[:, 3 * dc:4 * dc].astype(F32)
        sc_v = z_ref[:, 4 * dc:5 * dc].astype(F32)
        rot_c[0, 0:HALO, :] = zp_ref[:, 3 * dc:4 * dc].astype(F32) * zp_ref[:, 4 * dc:5 * dc].astype(F32) * keep
        rot_c[0, HALO:HALO + tm, :] = sc_c * sc_v
        _fill_rotations(rot_c, _residues(fwd_b), span)
        _tap_conv(rot_c, fwd_b, wb_b, conv_out, tm)
        dsv = ds_ref[...]
        dz_ref[:, 2 * dc:3 * dc] = (dsv * conv_out[...]).astype(BF16)
        rot_dc[0, 0:tm, :] = dsv * sc_b
        _tap_wgrad(rot_c, fwd_b, rot_dc, 0, acc_b, tm)
        _fill_rotations(rot_dc, _residues(bwd_b), span)
        _tap_conv(rot_dc, bwd_b, wb_b, conv_out, tm)
        rot_dc[0, tm:tm + HALO, :] = rot_dc[0, 0:HALO, :]
        dcv = conv_out[...]
        dz_ref[:, 3 * dc:4 * dc] = (dcv * sc_v).astype(BF16)
        dz_ref[:, 4 * dc:5 * dc] = (dcv * sc_c).astype(BF16)
        dz_ref[:, 5 * dc:n] = dgl_ref[...]

        @pl.when(i == nt - 1)
        def _():
            dcaw_ref[0:ka, :] = jnp.sum(acc_a[...], axis=1)
            dcbw_ref[0:kb, :] = jnp.sum(acc_b[...], axis=1)

    def planes(taps):
        return pltpu.VMEM((len(_residues(taps)), HALO + tm, dc), F32)

    prev_rows = pl.BlockSpec((HALO, 5 * dc), lambda i: (jnp.maximum((nt - 1 - i) * per - 1, 0), 0))
    return _call(body, name=name, grid=(nt,),
                 in_specs=[_rows_rev(tm, dc, nt), _rows_rev(tm, dc, nt), _rows_rev(tm, 5 * dc, nt), prev_rows,
                           _rows_rev(tm, n - 5 * dc, nt), _rows_rev(tm, dc, nt), _whole(caw.shape),
                           _whole(lng.shape), _whole(lnb.shape), _whole(cbw.shape)],
                 out_specs=[_rows_rev(tm, n, nt), _whole((HALO, dc)), _whole((1, dc)), _whole((1, dc)),
                            _whole((1, dc)), _whole((SUBLANES, dc))],
                 out_shape=[_sds((t, n), BF16), _sds((HALO, dc), F32), _sds((1, dc), F32), _sds((1, dc), F32),
                            _sds((1, dc), F32), _sds((SUBLANES, dc), F32)],
                 scratch=[planes(fwd_a), planes(bwd_a), planes(fwd_b), planes(bwd_b),
                          pltpu.VMEM((ka, SUBLANES, dc), F32), pltpu.VMEM((kb, SUBLANES, dc), F32),
                          pltpu.VMEM((ka, SUBLANES, dc), F32), pltpu.VMEM((kb, SUBLANES, dc), F32),
                          pltpu.VMEM((tm, dc), F32)],
                 args=(dact, ds, z, z, dgl, a_conv, caw, lng, lnb, cbw), carry=carry)


def _sum_slabs(lands, name):
    _, rows, cols = lands[0].shape
    tr = _row_tile(rows)
    nr = rows // tr
    depth = len(lands)

    def body(*refs):
        o_ref = refs[depth]
        i = pl.program_id(0)
        for k in range(depth):
            @pl.when(i // nr == k)
            def _(k=k):
                acc = refs[k][0].astype(F32)
                for j in range(1, N_DEV):
                    acc = acc + refs[k][j].astype(F32)
                o_ref[...] = acc

    def land_spec(k):
        return pl.BlockSpec((N_DEV, tr, cols), lambda i: (0, jnp.clip(i - k * nr, 0, nr - 1), 0))

    return _call(body, name=name, grid=(depth * nr,), in_specs=[land_spec(k) for k in range(depth)],
                 out_specs=[_rows(tr, cols)], out_shape=[_sds((depth * rows, cols), F32)], args=lands)[0]


def _adamw(w, g, m, v, name):
    rows, cols = w.shape
    tr = _row_tile(rows)

    def body(w_ref, g_ref, m_ref, v_ref, d_ref, nm_ref, nv_ref):
        gv = g_ref[...]
        nm = ADAM_B1 * m_ref[...] + (1.0 - ADAM_B1) * gv
        nv = ADAM_B2 * v_ref[...] + (1.0 - ADAM_B2) * (gv * gv)
        m_hat = nm / (1.0 - ADAM_B1 ** ADAM_STEP)
        v_hat = nv / (1.0 - ADAM_B2 ** ADAM_STEP)
        d_ref[...] = -ADAM_LR * (m_hat / (jnp.sqrt(v_hat) + ADAM_EPS) + ADAM_WD * w_ref[...])
        nm_ref[...] = nm
        nv_ref[...] = nv

    spec = _rows(tr, cols)
    return _call(body, name=name, grid=(rows // tr,), in_specs=[spec] * 4, out_specs=[spec] * 3,
                 out_shape=[_sds((rows, cols), F32)] * 3, args=(w, g, m, v))


def _pack(arrays):
    parts, meta, row = [], [], 0
    tile = SUBLANES * LANES
    for a in arrays:
        size = a.size
        padded = -(-size // tile) * tile
        flat = jnp.pad(a.reshape(-1).astype(F32), (0, padded - size))
        parts.append(flat.reshape(padded // LANES, LANES))
        meta.append((row, size, a.shape))
        row += padded // LANES
    return jnp.concatenate(parts, axis=0), meta


def _unpack(pack, meta):
    out = []
    for row, size, shape in meta:
        rows = -(-size // LANES)
        out.append(pack[row:row + rows].reshape(-1)[:size].reshape(shape))
    return out


def kernel(x, p, g_mix, w_in, b_gate, conv_a_w, conv_a_b, ln_a_g, ln_a_b, w_a_out, conv_b_w, w_b_out, w_o, g_ffn, w_up, conv_f_w, conv_f_b, w_down, g_ple, w_ple, w_ple_gate, g_final, loss_target, m_g_mix, m_w_in, m_b_gate, m_conv_a_w, m_conv_a_b, m_ln_a_g, m_ln_a_b, m_w_a_out, m_conv_b_w, m_w_b_out, m_w_o, m_g_ffn, m_w_up, m_conv_f_w, m_conv_f_b, m_w_down, m_g_ple, m_w_ple, m_w_ple_gate, m_g_final, v_g_mix, v_w_in, v_b_gate, v_conv_a_w, v_conv_a_b, v_ln_a_g, v_ln_a_b, v_w_a_out, v_conv_b_w, v_w_b_out, v_w_o, v_g_ffn, v_w_up, v_conv_f_w, v_conv_f_b, v_w_down, v_g_ple, v_w_ple, v_w_ple_gate, v_g_final):
    w = dict(zip(WEIGHT_NAMES, (g_mix, w_in, b_gate, conv_a_w, conv_a_b, ln_a_g, ln_a_b, w_a_out, conv_b_w,
                                w_b_out, w_o, g_ffn, w_up, conv_f_w, conv_f_b, w_down, g_ple, w_ple,
                                w_ple_gate, g_final)))
    mom = dict(zip(WEIGHT_NAMES, (m_g_mix, m_w_in, m_b_gate, m_conv_a_w, m_conv_a_b, m_ln_a_g, m_ln_a_b,
                                  m_w_a_out, m_conv_b_w, m_w_b_out, m_w_o, m_g_ffn, m_w_up, m_conv_f_w,
                                  m_conv_f_b, m_w_down, m_g_ple, m_w_ple, m_w_ple_gate, m_g_final)))
    var = dict(zip(WEIGHT_NAMES, (v_g_mix, v_w_in, v_b_gate, v_conv_a_w, v_conv_a_b, v_ln_a_g, v_ln_a_b,
                                  v_w_a_out, v_conv_b_w, v_w_b_out, v_w_o, v_g_ffn, v_w_up, v_conv_f_w,
                                  v_conv_f_b, v_w_down, v_g_ple, v_w_ple, v_w_ple_gate, v_g_final)))
    depth = g_mix.shape[0]
    dc = ln_a_g.shape[1]
    me = _my_index()
    x0 = x[0]
    target = loss_target[0]
    big_names = tuple(BIG_AXIS)

    shard = {name: (jnp.swapaxes(w[name], 1, 2) if name in TRANSPOSED else w[name]).astype(BF16)
             for name in big_names}

    def gather_of(layer, *names):
        return _Gather([(shard[name], layer, BIG_AXIS[name]) for name in names])

    def row(name, layer):
        return w[name][layer][None]

    first = _Gather([(shard['w_in'], 0, BIG_AXIS['w_in'])] + [(w[name][None], None, 0) for name in CONV_SHARDED])
    gathered = _run_exchange(first, "gather_first")
    w_in_full = gathered[0]
    conv_full = {name: jnp.transpose(g, (1, 2, 0, 3)).reshape(g.shape[1], g.shape[2], -1)
                 for name, g in zip(CONV_SHARDED, gathered[1:])}
    saved = []
    xc = x0
    for l in range(depth):
        carry = gather_of(l, 'w_a_out', 'w_b_out', 'w_o')
        h, z = _norm_matmul(xc, row('g_mix', l), w_in_full, f"fwd_in_{l}", carry)
        wa_full, wb_full, wo_full = carry.results
        carry = gather_of(l, 'w_up')
        a_conv, a_act, s = _fwd_branch(z, conv_full['conv_a_w'][l], row('conv_a_b', l), row('ln_a_g', l),
                                       row('ln_a_b', l), conv_full['conv_b_w'][l], dc, f"fwd_branch_{l}", carry)
        w_up_full, = carry.results
        carry = gather_of(l, 'w_down')
        x1 = _fwd_merge(xc, z, row('b_gate', l), a_act, s, wa_full, wb_full, wo_full, f"fwd_merge_{l}", carry)
        w_down_full, = carry.results
        carry = gather_of(l, 'w_ple', 'w_ple_gate')
        h2, u = _norm_matmul(x1, row('g_ffn', l), w_up_full, f"fwd_up_{l}", carry)
        w_ple_full, w_pg_full = carry.results
        carry = gather_of(l + 1, 'w_in') if l + 1 < depth else None
        x2, act = _fwd_down(x1, u, conv_full['conv_f_w'][l], row('conv_f_b', l), w_down_full, f"fwd_down_{l}",
                            carry)
        x3 = _fwd_ple(x2, row('g_ple', l), w_pg_full, p[l, 0], w_ple_full, f"fwd_ple_{l}")
        saved.append((xc, h, z, a_conv, a_act, s, x1, h2, u, act, x2,
                      dict(w_in=w_in_full, w_a_out=wa_full, w_b_out=wb_full, w_o=wo_full, w_up=w_up_full,
                           w_down=w_down_full, w_ple=w_ple_full, w_ple_gate=w_pg_full)))
        if carry is not None:
            w_in_full, = carry.results
        xc = x3

    dx, dg_final, loss_part = _loss_bwd(xc, g_final[None], target, "loss_bwd")
    landed = {name: [None] * depth for name in big_names}
    small = {name: [None] * depth for name in WEIGHT_NAMES if name not in BIG_AXIS and name != 'g_final'}

    def scatter_of(layer, **partials):
        ex = _Scatter([(part, BIG_AXIS[name]) for name, part in partials.items()])
        ex.names, ex.layer = tuple(partials), layer
        return ex

    def keep(ex):
        for name, land in zip(ex.names, ex.results):
            landed[name][ex.layer] = land

    for l in reversed(range(depth)):
        xin, h, z, a_conv, a_act, s, x1, h2, u, act, x2, full = saved[l]
        dx2, d_wpg, d_wple, small['g_ple'][l] = _bwd_ple(
            dx, x2, row('g_ple', l), full['w_ple_gate'], p[l, 0], full['w_ple'], f"bwd_ple_{l}")
        carry = scatter_of(l, w_ple_gate=d_wpg, w_ple=d_wple)
        d_wdown = _wgrad_tn(act, dx2, f"wgrad_down_{l}", carry)
        keep(carry)
        carry = scatter_of(l, w_down=d_wdown)
        du, dcfw, small['conv_f_b'][l] = _bwd_down(
            dx2, u, conv_full['conv_f_w'][l], row('conv_f_b', l), full['w_down'], f"bwd_down_{l}", carry)
        keep(carry)
        small['conv_f_w'][l] = dcfw[:conv_f_w.shape[1]]
        dx1, small['g_ffn'][l] = _bwd_norm_matmul(du, full['w_up'], x1, row('g_ffn', l), dx2, f"bwd_up_{l}")
        d_wup = _wgrad_tn(du, h2, f"wgrad_up_{l}")
        carry = scatter_of(l, w_up=d_wup)
        dact, ds, dgl, d_wo, d_wa, d_wb, small['b_gate'][l] = _bwd_merge(
            dx1, z, row('b_gate', l), a_act, s, full['w_a_out'], full['w_b_out'], full['w_o'],
            f"bwd_merge_{l}", carry)
        keep(carry)
        carry = scatter_of(l, w_o=d_wo, w_a_out=d_wa, w_b_out=d_wb)
        dz, dcaw, small['conv_a_b'][l], small['ln_a_g'][l], small['ln_a_b'][l], dcbw = _bwd_branch(
            dact, ds, z, dgl, a_conv, conv_full['conv_a_w'][l], row('ln_a_g', l), row('ln_a_b', l),
            conv_full['conv_b_w'][l], f"bwd_branch_{l}", carry)
        keep(carry)
        small['conv_a_w'][l] = dcaw[:conv_a_w.shape[1]]
        small['conv_b_w'][l] = dcbw[:conv_b_w.shape[1]]
        d_win = _wgrad_tn(dz, h, f"wgrad_in_{l}")
        carry = scatter_of(l, w_in=d_win)
        dx, small['g_mix'][l] = _bwd_norm_matmul(dz, full['w_in'], xin, row('g_mix', l), dx1, f"bwd_in_{l}", carry)
        keep(carry)
    grad_x = dx[None]

    small_names = tuple(small)
    small_parts = [jnp.stack([part[0] if part.shape[0] == 1 else part for part in small[name]])
                   for name in small_names]
    pack, meta = _pack(small_parts + [dg_final[0], loss_part])
    reduced = _unpack(_all_reduce_small(pack, "all_reduce_small"), meta)
    loss = reduced[-1][0, 0]
    grads = dict(zip(small_names, reduced[:len(small_names)]))
    grads['g_final'] = reduced[len(small_names)]
    for name in CONV_SHARDED:
        width = w[name].shape[-1]
        grads[name] = lax.dynamic_slice_in_dim(grads[name], me * width, width, axis=2)

    for name in big_names:
        slab_shape = landed[name][0].shape[1:]
        lands = [land.reshape(N_DEV, -1, slab_shape[-1]) for land in landed[name]]
        total = _sum_slabs(lands, f"sum_{name}").reshape((depth,) + slab_shape)
        grads[name] = jnp.swapaxes(total, 1, 2) if name in TRANSPOSED else total

    delta, new_m, new_v = {}, {}, {}
    for name in big_names:
        shape = w[name].shape
        flat = lambda a: a.reshape(-1, shape[-1])
        d_, m_, v_ = _adamw(flat(w[name]), flat(grads[name]), flat(mom[name]), flat(var[name]), f"adamw_{name}")
        delta[name], new_m[name], new_v[name] = d_.reshape(shape), m_.reshape(shape), v_.reshape(shape)
    rest = tuple(name for name in WEIGHT_NAMES if name not in BIG_AXIS)
    packs = [_pack([src[name] for name in rest]) for src in (w, grads, mom, var)]
    outs = _adamw(packs[0][0], packs[1][0], packs[2][0], packs[3][0], "adamw_small")
    for dst, packed in zip((delta, new_m, new_v), outs):
        dst.update(zip(rest, _unpack(packed, packs[0][1])))

    return (loss, grad_x, *[grads[n] for n in WEIGHT_NAMES], *[delta[n] for n in WEIGHT_NAMES],
            *[new_m[n] for n in WEIGHT_NAMES], *[new_v[n] for n in WEIGHT_NAMES])
```

```python
import jax
import jax.numpy as jnp
from jax import lax
from jax.experimental import pallas as pl
from jax.experimental.pallas import tpu as pltpu

F32 = jnp.float32
BF16 = jnp.bfloat16
MESH = pl.DeviceIdType.MESH

N_DEV = 8
NORM_EPS = 1e-6
HALO = 32
LANES = 128
SUBLANES = 8
VMEM_LIMIT_BYTES = 56 * 2**20

ADAM_LR = 0.001
ADAM_B1 = 0.9
ADAM_B2 = 0.999
ADAM_EPS = 1e-08
ADAM_WD = 0.01
ADAM_STEP = 10

WEIGHT_NAMES = ('g_mix', 'w_in', 'b_gate', 'conv_a_w', 'conv_a_b', 'ln_a_g', 'ln_a_b', 'w_a_out',
                'conv_b_w', 'w_b_out', 'w_o', 'g_ffn', 'w_up', 'conv_f_w', 'conv_f_b', 'w_down',
                'g_ple', 'w_ple', 'w_ple_gate', 'g_final')
BIG_AXIS = {'w_in': 0, 'w_up': 0, 'w_a_out': 1, 'w_b_out': 1, 'w_o': 0, 'w_down': 0, 'w_ple': 1,
            'w_ple_gate': 0}
TRANSPOSED = ('w_in', 'w_up')
CONV_SHARDED = ('conv_a_w', 'conv_b_w', 'conv_f_w')


def _dot(a, b):
    return jnp.dot(a, b, preferred_element_type=F32)


def _dot_nt(a, b):
    return lax.dot_general(a, b, (((1,), (1,)), ((), ())), preferred_element_type=F32)


def _dot_tn(a, b):
    return lax.dot_general(a, b, (((0,), (0,)), ((), ())), preferred_element_type=F32)


def _sigmoid(v):
    return jax.nn.sigmoid(v)


def _token_tile(t, cap=512):
    return cap if (t % cap == 0 and t > 512) else 128


def _chunk(n, limit=512):
    for c in range(limit - limit % LANES, 0, -LANES):
        if n % c == 0:
            return c
    return n


def _row_tile(rows):
    for c in (512, 256, 128, 64, 32, 16, 8):
        if rows % c == 0:
            return c
    return rows


def _rows(tm, width):
    return pl.BlockSpec((tm, width), lambda i: (i, 0))


def _rows_rev(tm, width, nt):
    return pl.BlockSpec((tm, width), lambda i: (nt - 1 - i, 0))


def _whole(shape):
    nd = len(shape)
    return pl.BlockSpec(tuple(shape), lambda i: (0,) * nd)


def _resident(shape):
    nd = len(shape)
    return pl.BlockSpec(tuple(shape), lambda i: (0,) * nd, pipeline_mode=pl.Buffered(1))


def _sds(shape, dtype):
    return jax.ShapeDtypeStruct(tuple(shape), dtype)


def _rms_stats(xv):
    r = lax.rsqrt(jnp.mean(xv * xv, axis=-1, keepdims=True) + NORM_EPS)
    return xv * r, r


def _rms_bwd(dy, xh, r, g):
    dxh = dy * g
    dx = r * (dxh - xh * jnp.mean(dxh * xh, axis=-1, keepdims=True))
    return dx, jnp.sum(dy * xh, axis=0, keepdims=True)


GELU_C0 = 0.7978845608028654
GELU_C1 = GELU_C0 * 0.044715


def _gelu_tanh(v):
    v2 = v * v
    t = jnp.tanh(v * (GELU_C0 + GELU_C1 * v2))
    q = 1.0 + t
    hv = 0.5 * v
    grad = 0.5 * q + hv * (1.0 - t * t) * (GELU_C0 + (3.0 * GELU_C1) * v2)
    return hv * q, grad


ROW_CHUNK = 32
LANE_CHUNK = 512


def _residues(taps):
    return [0] + sorted({off % SUBLANES for _, off in taps} - {0})


def _fill_rotations(rot_ref, residues, length):
    for plane, r in enumerate(residues):
        if r:
            rot_ref[plane, 0:length, :] = rot_ref[0, pl.ds(r, length), :]


def _broadcast_rows(dst_ref, src_ref, count):
    for k in range(count):
        dst_ref[k] = jnp.broadcast_to(src_ref[k:k + 1, :], dst_ref.shape[1:])


def _lane_chunks(width):
    return [(c0, min(LANE_CHUNK, width - c0)) for c0 in range(0, width, LANE_CHUNK)]


def _tap_conv(rot_ref, taps, wb_ref, out_ref, tm, bias_plane=None):
    residues = _residues(taps)
    plane = {r: p for p, r in enumerate(residues)}
    blocks = ROW_CHUNK // SUBLANES
    width = out_ref.shape[1]

    def chunk(c, state):
        r0 = c * ROW_CHUNK
        for c0, cw in _lane_chunks(width):
            accs = [None if bias_plane is None else wb_ref[bias_plane, :, c0:c0 + cw]] * blocks
            for k, off in taps:
                wk = wb_ref[k, :, c0:c0 + cw]
                base = off - off % SUBLANES
                for j in range(blocks):
                    at = pl.multiple_of(r0 + base + SUBLANES * j, SUBLANES)
                    term = wk * rot_ref[plane[off % SUBLANES], pl.ds(at, SUBLANES), c0:c0 + cw]
                    accs[j] = term if accs[j] is None else accs[j] + term
            for j in range(blocks):
                at = pl.multiple_of(r0 + SUBLANES * j, SUBLANES)
                out_ref[pl.ds(at, SUBLANES), c0:c0 + cw] = accs[j]
        return state

    lax.fori_loop(0, tm // ROW_CHUNK, chunk, 0)


def _tap_wgrad(rot_ref, taps, dy_ref, dy_plane, acc_ref, tm):
    residues = _residues(taps)
    plane = {r: p for p, r in enumerate(residues)}
    blocks = ROW_CHUNK // SUBLANES
    width = acc_ref.shape[2]

    def chunk(c, state):
        r0 = c * ROW_CHUNK
        for c0, cw in _lane_chunks(width):
            dys = [dy_ref[dy_plane, pl.ds(pl.multiple_of(r0 + SUBLANES * j, SUBLANES), SUBLANES), c0:c0 + cw]
                   for j in range(blocks)]
            for k, off in taps:
                base = off - off % SUBLANES
                part = None
                for j in range(blocks):
                    at = pl.multiple_of(r0 + base + SUBLANES * j, SUBLANES)
                    term = dys[j] * rot_ref[plane[off % SUBLANES], pl.ds(at, SUBLANES), c0:c0 + cw]
                    part = term if part is None else part + term
                acc_ref[k, :, c0:c0 + cw] += part
        return state

    lax.fori_loop(0, tm // ROW_CHUNK, chunk, 0)


def _fwd_taps(width):
    return [(k, HALO - (width - 1) + k) for k in range(width)]


def _bwd_taps(width):
    return [(k, width - 1 - k) for k in range(width)]


def _my_index():
    return 4 * lax.axis_index("x") + 2 * lax.axis_index("y") + lax.axis_index("c")


def _mesh_id(idx):
    return (idx // 4, (idx // 2) % 2, idx % 2)


def _slab(ref, axis, idx, width):
    at = [slice(None)] * len(ref.shape)
    at[axis] = pl.ds(pl.multiple_of(idx * width, width), width)
    return ref.at[tuple(at)]


class _Exchange:
    def __init__(self, inputs, out_shape):
        n = len(inputs)
        self.inputs = list(inputs)
        self.out_shape = list(out_shape)
        self.sems = [pltpu.SemaphoreType.DMA((n, N_DEV - 1)), pltpu.SemaphoreType.DMA((n, N_DEV - 1)),
                     pltpu.SemaphoreType.DMA((n,))]
        self.results = None

    def _local(self, ins, outs, k, me):
        raise NotImplementedError

    def _remote(self, ins, outs, k, me, sender, receiver):
        raise NotImplementedError

    def start(self, ins, outs, sems):
        send_sems, recv_sems, local_sems = sems
        me = _my_index()
        for k in range(len(self.inputs)):
            src, dst = self._local(ins, outs, k, me)
            pltpu.make_async_copy(src, dst, local_sems.at[k]).start()
            for dist in range(1, N_DEV):
                peer = (me + dist) % N_DEV
                src, dst = self._remote(ins, outs, k, me, me, peer)
                pltpu.make_async_remote_copy(
                    src_ref=src, dst_ref=dst, send_sem=send_sems.at[k, dist - 1],
                    recv_sem=recv_sems.at[k, dist - 1], device_id=_mesh_id(peer), device_id_type=MESH).start()

    def wait(self, ins, outs, sems):
        send_sems, recv_sems, local_sems = sems
        me = _my_index()
        for k in range(len(self.inputs)):
            for dist in range(1, N_DEV):
                sender = (me + N_DEV - dist) % N_DEV
                src, dst = self._remote(ins, outs, k, me, sender, me)
                cp = pltpu.make_async_remote_copy(
                    src_ref=src, dst_ref=dst, send_sem=send_sems.at[k, dist - 1],
                    recv_sem=recv_sems.at[k, dist - 1], device_id=_mesh_id(sender), device_id_type=MESH)
                cp.wait_send()
                cp.wait_recv()
            src, dst = self._local(ins, outs, k, me)
            pltpu.make_async_copy(src, dst, local_sems.at[k]).wait()

    def forward(self, ins, outs, sems):
        pass


class _Gather(_Exchange):
    FLIPS = ((1, 0), (0, 1), (1, 1))

    def __init__(self, items):
        self.items = list(items)
        out_shape = []
        for shards, layer, axis in self.items:
            shape = list(shards.shape if layer is None else shards.shape[1:])
            shape[axis] *= N_DEV
            out_shape.append(_sds(shape, shards.dtype))
        super().__init__([it[0] for it in self.items], out_shape)

    def _src(self, ins, k):
        layer = self.items[k][1]
        return ins[k] if layer is None else ins[k].at[layer]

    def _place(self, outs, k, idx):
        axis = self.items[k][2]
        return _slab(outs[k], axis, idx, self.out_shape[k].shape[axis] // N_DEV)

    def _copy(self, sems, k, j, src, dst, to):
        return pltpu.make_async_remote_copy(src_ref=src, dst_ref=dst, send_sem=sems[0].at[k, j],
                                            recv_sem=sems[1].at[k, j], device_id=to, device_id_type=MESH)

    @staticmethod
    def _places():
        x, y, c = lax.axis_index("x"), lax.axis_index("y"), lax.axis_index("c")
        chips = [(1 - x if fx else x, 1 - y if fy else y) for fx, fy in _Gather.FLIPS]
        return (x, y, c), (x, y, 1 - c), chips

    @staticmethod
    def _index(place):
        return 4 * place[0] + 2 * place[1] + place[2]

    def start(self, ins, outs, sems):
        me, sibling, chips = self._places()
        for k in range(len(self.inputs)):
            src, mine = self._src(ins, k), self._place(outs, k, self._index(me))
            pltpu.make_async_copy(src, mine, sems[2].at[k]).start()
            self._copy(sems, k, 0, src, mine, sibling).start()
            for j, chip in enumerate(chips):
                self._copy(sems, k, 1 + j, src, mine, (*chip, me[2])).start()

    def forward(self, ins, outs, sems):
        me, sibling, chips = self._places()
        for k in range(len(self.inputs)):
            for j, chip in enumerate(chips):
                got = self._place(outs, k, self._index((*chip, me[2])))
                self._copy(sems, k, 1 + j, got, got, (*chip, me[2])).wait_recv()
                self._copy(sems, k, 4 + j, got, got, sibling).start()

    def wait(self, ins, outs, sems):
        me, sibling, chips = self._places()
        for k in range(len(self.inputs)):
            src, mine = self._src(ins, k), self._place(outs, k, self._index(me))
            self._copy(sems, k, 0, src, self._place(outs, k, self._index(sibling)), sibling).wait_recv()
            for j, chip in enumerate(chips):
                got = self._place(outs, k, self._index((*chip, sibling[2])))
                self._copy(sems, k, 4 + j, got, got, sibling).wait_recv()
            for j in range(N_DEV - 1):
                self._copy(sems, k, j, src, mine, sibling).wait_send()
            pltpu.make_async_copy(src, mine, sems[2].at[k]).wait()


class _Scatter(_Exchange):
    def __init__(self, items):
        self.items = list(items)
        out_shape = []
        for partial, axis in self.items:
            shape = list(partial.shape)
            shape[axis] //= N_DEV
            out_shape.append(_sds([N_DEV] + shape, partial.dtype))
        super().__init__([it[0] for it in self.items], out_shape)

    def _take(self, ins, k, idx):
        axis = self.items[k][1]
        return _slab(ins[k], axis, idx, self.items[k][0].shape[axis] // N_DEV)

    def _local(self, ins, outs, k, me):
        return self._take(ins, k, me), outs[k].at[me]

    def _remote(self, ins, outs, k, me, sender, receiver):
        return self._take(ins, k, receiver), outs[k].at[sender]


def _run_exchange(exchange, name):
    n = len(exchange.inputs)

    def body(*refs):
        ins, outs, sems = refs[:n], refs[n:2 * n], refs[2 * n:]
        exchange.start(ins, outs, sems)
        exchange.forward(ins, outs, sems)
        exchange.wait(ins, outs, sems)

    any_spec = pl.BlockSpec(memory_space=pl.ANY)
    exchange.results = pl.pallas_call(
        body, name=name, in_specs=[any_spec] * n, out_specs=[any_spec] * n, out_shape=exchange.out_shape,
        scratch_shapes=exchange.sems)(*exchange.inputs)
    return exchange.results


def _call(body, *, name, grid, in_specs, out_specs, out_shape, args, scratch=(), carry=None):
    in_specs, out_specs, out_shape, scratch = list(in_specs), list(out_specs), list(out_shape), list(scratch)
    params = pltpu.CompilerParams(dimension_semantics=("arbitrary",) * len(grid),
                                  vmem_limit_bytes=VMEM_LIMIT_BYTES)
    if carry is None:
        return pl.pallas_call(body, name=name, grid=grid, in_specs=in_specs, out_specs=out_specs,
                              out_shape=out_shape, scratch_shapes=scratch, compiler_params=params)(*args)
    n_in, n_out, n_scr, n_x = len(in_specs), len(out_specs), len(scratch), len(carry.inputs)
    steps = 1
    for extent in grid:
        steps *= extent
    assert steps >= 3, "a carrier needs a step each for start, second stage and wait"

    def with_exchange(*refs):
        core_in, x_in = refs[:n_in], refs[n_in:n_in + n_x]
        refs = refs[n_in + n_x:]
        core_out, x_out = refs[:n_out], refs[n_out:n_out + n_x]
        refs = refs[n_out + n_x:]
        core_scr, sems = refs[:n_scr], refs[n_scr:]
        step = pl.program_id(0)
        for axis in range(1, len(grid)):
            step = step * grid[axis] + pl.program_id(axis)

        @pl.when(step == 0)
        def _():
            carry.start(x_in, x_out, sems)

        @pl.when(step == steps - 2)
        def _():
            carry.forward(x_in, x_out, sems)

        body(*core_in, *core_out, *core_scr)

        @pl.when(step == steps - 1)
        def _():
            carry.wait(x_in, x_out, sems)

    any_spec = pl.BlockSpec(memory_space=pl.ANY)
    outs = pl.pallas_call(
        with_exchange, name=name, grid=grid, in_specs=in_specs + [any_spec] * n_x,
        out_specs=out_specs + [any_spec] * n_x, out_shape=out_shape + carry.out_shape,
        scratch_shapes=scratch + carry.sems, compiler_params=params)(*args, *carry.inputs)
    carry.results = outs[n_out:]
    return outs[:n_out]


def _all_reduce_small(pack, name):
    rows = pack.shape[0]

    def body(x_ref, o_ref, land, send_sems, recv_sems):
        me = _my_index()
        land[me] = x_ref[...]
        for dist in range(1, N_DEV):
            pltpu.make_async_remote_copy(
                src_ref=x_ref, dst_ref=land.at[me],
                send_sem=send_sems.at[dist - 1], recv_sem=recv_sems.at[dist - 1],
                device_id=_mesh_id((me + dist) % N_DEV), device_id_type=MESH).start()
        for dist in range(1, N_DEV):
            src_dev = (me + N_DEV - dist) % N_DEV
            cp = pltpu.make_async_remote_copy(
                src_ref=x_ref, dst_ref=land.at[src_dev],
                send_sem=send_sems.at[dist - 1], recv_sem=recv_sems.at[dist - 1],
                device_id=_mesh_id(src_dev), device_id_type=MESH)
            cp.wait_send()
            cp.wait_recv()
        acc = land[0]
        for j in range(1, N_DEV):
            acc = acc + land[j]
        o_ref[...] = acc

    vmem = pl.BlockSpec(memory_space=pltpu.VMEM)
    return pl.pallas_call(
        body, name=name, in_specs=[vmem], out_specs=vmem, out_shape=_sds(pack.shape, F32),
        scratch_shapes=[pltpu.VMEM((N_DEV, rows, LANES), F32), pltpu.SemaphoreType.DMA((N_DEV - 1,)),
                        pltpu.SemaphoreType.DMA((N_DEV - 1,))])(pack)


def _norm_matmul(x, g, wt, name, carry=None):
    t, d = x.shape
    n = wt.shape[0]
    tm, nc = _token_tile(t), _chunk(n)

    def body(x_ref, g_ref, wt_ref, h_ref, o_ref):
        xh, _ = _rms_stats(x_ref[...])
        h = (xh * g_ref[...]).astype(BF16)
        h_ref[...] = h
        for n0 in range(0, n, nc):
            o_ref[:, n0:n0 + nc] = _dot_nt(h, wt_ref[n0:n0 + nc, :]).astype(BF16)

    return _call(body, name=name, grid=(t // tm,),
                 in_specs=[_rows(tm, d), _whole(g.shape), _resident(wt.shape)],
                 out_specs=[_rows(tm, d), _rows(tm, n)],
                 out_shape=[_sds((t, d), BF16), _sds((t, n), BF16)], args=(x, g, wt), carry=carry)


def _fwd_branch(z, caw, cab, lng, lnb, cbw, dc, name, carry=None):
    t = z.shape[0]
    tm = _token_tile(t)
    ka, kb = caw.shape[0], cbw.shape[0]
    taps_a, taps_b = _fwd_taps(ka), _fwd_taps(kb)
    res_a, res_b = _residues(taps_a), _residues(taps_b)
    span = HALO + tm - SUBLANES

    def body(z_ref, caw_ref, cab_ref, lng_ref, lnb_ref, cbw_ref, ac_ref, act_ref, s_ref,
             rot_a, rot_b, wb_a, wb_b, cb):
        @pl.when(pl.program_id(0) == 0)
        def _():
            rot_a[0, 0:HALO, :] = jnp.zeros((HALO, dc), F32)
            rot_b[0, 0:HALO, :] = jnp.zeros((HALO, dc), F32)
            _broadcast_rows(wb_a, caw_ref, ka)
            wb_a[ka] = jnp.broadcast_to(cab_ref[...], (SUBLANES, dc))
            _broadcast_rows(wb_b, cbw_ref, kb)

        a_val = z_ref[:, 0:dc].astype(F32)
        a_gt = z_ref[:, dc:2 * dc].astype(F32)
        rot_a[0, HALO:HALO + tm, :] = a_val * _sigmoid(a_gt)
        _fill_rotations(rot_a, res_a, span)
        _tap_conv(rot_a, taps_a, wb_a, ac_ref, tm, bias_plane=ka)
        ac = ac_ref[...]
        mu = jnp.mean(ac, axis=-1, keepdims=True)
        xc = ac - mu
        var = jnp.mean(xc * xc, axis=-1, keepdims=True)
        ln = xc * lax.rsqrt(var + NORM_EPS) * lng_ref[...] + lnb_ref[...]
        act_ref[...] = (ln * _sigmoid(ln)).astype(BF16)
        rot_a[0, 0:HALO, :] = rot_a[0, tm:tm + HALO, :]

        sc_c = z_ref[:, 3 * dc:4 * dc].astype(F32)
        sc_v = z_ref[:, 4 * dc:5 * dc].astype(F32)
        rot_b[0, HALO:HALO + tm, :] = sc_c * sc_v
        _fill_rotations(rot_b, res_b, span)
        _tap_conv(rot_b, taps_b, wb_b, cb, tm)
        s_ref[...] = (z_ref[:, 2 * dc:3 * dc].astype(F32) * cb[...]).astype(BF16)
        rot_b[0, 0:HALO, :] = rot_b[0, tm:tm + HALO, :]

    return _call(body, name=name, grid=(t // tm,),
                 in_specs=[_rows(tm, 5 * dc), _whole(caw.shape), _whole(cab.shape), _whole(lng.shape),
                           _whole(lnb.shape), _whole(cbw.shape)],
                 out_specs=[_rows(tm, dc), _rows(tm, dc), _rows(tm, dc)],
                 out_shape=[_sds((t, dc), F32), _sds((t, dc), BF16), _sds((t, dc), BF16)],
                 scratch=[pltpu.VMEM((len(res_a), HALO + tm, dc), F32), pltpu.VMEM((len(res_b), HALO + tm, dc), F32),
                          pltpu.VMEM((ka + 1, SUBLANES, dc), F32), pltpu.VMEM((kb, SUBLANES, dc), F32),
                          pltpu.VMEM((tm, dc), F32)],
                 args=(z, caw, cab, lng, lnb, cbw), carry=carry)


def _fwd_merge(x, z, bg, a_act, s, wa, wb, wo, name, carry=None):
    t, d = x.shape
    n = z.shape[1]
    dc = a_act.shape[1]
    tm = _token_tile(t)
    o5 = n - 2 * d

    def body(x_ref, z_ref, bg_ref, act_ref, s_ref, wa_ref, wb_ref, wo_ref, o_ref):
        ya = _dot(act_ref[...], wa_ref[...])
        yb = _dot(s_ref[...], wb_ref[...])
        ga = _sigmoid(z_ref[:, o5:o5 + d].astype(F32) + bg_ref[:, 0:d])
        gb = _sigmoid(z_ref[:, o5 + d:n].astype(F32) + bg_ref[:, d:2 * d])
        m = (ga * ya + gb * yb).astype(BF16)
        o_ref[...] = x_ref[...] + _dot(m, wo_ref[...])

    return _call(body, name=name, grid=(t // tm,),
                 in_specs=[_rows(tm, d), _rows(tm, n), _whole(bg.shape), _rows(tm, dc), _rows(tm, dc),
                           _resident(wa.shape), _resident(wb.shape), _resident(wo.shape)],
                 out_specs=[_rows(tm, d)], out_shape=[_sds((t, d), F32)],
                 args=(x, z, bg, a_act, s, wa, wb, wo), carry=carry)[0]


def _fwd_down(x, u, cfw, cfb, wd, name, carry=None):
    t, d = x.shape
    f = u.shape[1] // 2
    tm = _token_tile(t, 256)
    kf = cfw.shape[0]

    taps = _fwd_taps(kf)
    residues = _residues(taps)

    def body(x_ref, u_ref, cfw_ref, cfb_ref, wd_ref, o_ref, act_ref, rot_u, wb, fg):
        @pl.when(pl.program_id(0) == 0)
        def _():
            rot_u[0, 0:HALO, :] = jnp.zeros((HALO, f), F32)
            _broadcast_rows(wb, cfw_ref, kf)
            wb[kf] = jnp.broadcast_to(cfb_ref[...], (SUBLANES, f))

        rot_u[0, HALO:HALO + tm, :] = u_ref[:, 0:f].astype(F32)
        _fill_rotations(rot_u, residues, HALO + tm - SUBLANES)
        _tap_conv(rot_u, taps, wb, fg, tm, bias_plane=kf)
        gl, _ = _gelu_tanh(fg[...])
        act = (gl * u_ref[:, f:2 * f].astype(F32)).astype(BF16)
        act_ref[...] = act
        o_ref[...] = x_ref[...] + _dot(act, wd_ref[...])
        rot_u[0, 0:HALO, :] = rot_u[0, tm:tm + HALO, :]

    return _call(body, name=name, grid=(t // tm,),
                 in_specs=[_rows(tm, d), _rows(tm, 2 * f), _whole(cfw.shape), _whole(cfb.shape),
                           _resident(wd.shape)],
                 out_specs=[_rows(tm, d), _rows(tm, f)], out_shape=[_sds((t, d), F32), _sds((t, f), BF16)],
                 scratch=[pltpu.VMEM((len(residues), HALO + tm, f), F32), pltpu.VMEM((kf + 1, SUBLANES, f), F32),
                          pltpu.VMEM((tm, f), F32)],
                 args=(x, u, cfw, cfb, wd), carry=carry)


def _fwd_ple(x, g, wpg, p, wple, name, carry=None):
    t, d = x.shape
    pd = p.shape[1]
    tm = _token_tile(t)

    def body(x_ref, g_ref, wpg_ref, p_ref, wple_ref, o_ref):
        xv = x_ref[...]
        xh, _ = _rms_stats(xv)
        lg = _dot((xh * g_ref[...]).astype(BF16), wpg_ref[...])
        pp = _dot(p_ref[...].astype(BF16), wple_ref[...])
        o_ref[...] = xv + _sigmoid(lg) * pp

    return _call(body, name=name, grid=(t // tm,),
                 in_specs=[_rows(tm, d), _whole(g.shape), _resident(wpg.shape), _rows(tm, pd),
                           _resident(wple.shape)],
                 out_specs=[_rows(tm, d)], out_shape=[_sds((t, d), F32)],
                 args=(x, g, wpg, p, wple), carry=carry)[0]


def _loss_bwd(x, g, target, name):
    t, d = x.shape
    tm = _token_tile(t)

    def body(x_ref, g_ref, t_ref, dx_ref, dg_ref, loss_ref):
        @pl.when(pl.program_id(0) == 0)
        def _():
            dg_ref[...] = jnp.zeros_like(dg_ref)
            loss_ref[...] = jnp.zeros_like(loss_ref)

        xh, r = _rms_stats(x_ref[...])
        err = xh * g_ref[...] - t_ref[...]
        sq = jnp.sum(jnp.sum(err * err, axis=0, keepdims=True), axis=1, keepdims=True)
        loss_ref[...] += jnp.broadcast_to(0.5 * sq / d, loss_ref.shape)
        dx, dg = _rms_bwd(err / d, xh, r, g_ref[...])
        dx_ref[...] = dx
        dg_ref[...] += dg

    return _call(body, name=name, grid=(t // tm,),
                 in_specs=[_rows(tm, d), _whole(g.shape), _rows(tm, d)],
                 out_specs=[_rows(tm, d), _whole((1, d)), _whole((SUBLANES, LANES))],
                 out_shape=[_sds((t, d), F32), _sds((1, d), F32), _sds((SUBLANES, LANES), F32)],
                 args=(x, g, target))


def _bwd_ple(dy, x, g, wpg, p, wple, name, carry=None):
    t, d = x.shape
    pd = p.shape[1]
    tm = _token_tile(t)
    nt = t // tm

    def body(dy_ref, x_ref, g_ref, wpg_ref, p_ref, wple_ref, dx_ref, dwpg_ref, dwple_ref, dg_ref,
             acc_pg, acc_ple):
        i = pl.program_id(0)

        @pl.when(i == 0)
        def _():
            acc_pg[...] = jnp.zeros_like(acc_pg)
            acc_ple[...] = jnp.zeros_like(acc_ple)
            dg_ref[...] = jnp.zeros_like(dg_ref)

        dyv = dy_ref[...]
        xh, r = _rms_stats(x_ref[...])
        h = (xh * g_ref[...]).astype(BF16)
        pb = p_ref[...].astype(BF16)
        pg = _sigmoid(_dot(h, wpg_ref[...]))
        pp = _dot(pb, wple_ref[...])
        dpp = (dyv * pg).astype(BF16)
        dlg = (dyv * pp * pg * (1.0 - pg)).astype(BF16)
        acc_ple[...] += _dot_tn(pb, dpp)
        acc_pg[...] += _dot_tn(h, dlg)
        dx, dg = _rms_bwd(_dot_nt(dlg, wpg_ref[...]), xh, r, g_ref[...])
        dx_ref[...] = dyv + dx
        dg_ref[...] += dg

        @pl.when(i == nt - 1)
        def _():
            dwpg_ref[...] = acc_pg[...].astype(BF16)
            dwple_ref[...] = acc_ple[...].astype(BF16)

    return _call(body, name=name, grid=(nt,),
                 in_specs=[_rows(tm, d), _rows(tm, d), _whole(g.shape), _resident(wpg.shape), _rows(tm, pd),
                           _resident(wple.shape)],
                 out_specs=[_rows(tm, d), _whole((d, d)), _whole((pd, d)), _whole((1, d))],
                 out_shape=[_sds((t, d), F32), _sds((d, d), BF16), _sds((pd, d), BF16), _sds((1, d), F32)],
                 scratch=[pltpu.VMEM((d, d), F32), pltpu.VMEM((pd, d), F32)],
                 args=(dy, x, g, wpg, p, wple), carry=carry)


def _bwd_down(dy, u, cfw, cfb, wd, name, carry=None):
    t, d = dy.shape
    f = u.shape[1] // 2
    tm = _token_tile(t, 256)
    nt = t // tm
    kf = cfw.shape[0]
    per = tm // HALO
    fwd, bwd = _fwd_taps(kf), _bwd_taps(kf)
    span = HALO + tm - SUBLANES

    def body(dy_ref, u_ref, up_ref, cfw_ref, cfb_ref, wd_ref, du_ref, dcw_ref, dcb_ref,
             rot_u, rot_g, wb, acc, conv_out):
        i = pl.program_id(0)

        @pl.when(i == 0)
        def _():
            for ref in (dcw_ref, dcb_ref, acc):
                ref[...] = jnp.zeros_like(ref)
            rot_g[0, tm:tm + HALO, :] = jnp.zeros((HALO, f), F32)
            _broadcast_rows(wb, cfw_ref, kf)
            wb[kf] = jnp.broadcast_to(cfb_ref[...], (SUBLANES, f))

        first = (i == nt - 1).astype(F32)
        rot_u[0, 0:HALO, :] = up_ref[:, 0:f].astype(F32) * (1.0 - first)
        rot_u[0, HALO:HALO + tm, :] = u_ref[:, 0:f].astype(F32)
        _fill_rotations(rot_u, _residues(fwd), span)
        _tap_conv(rot_u, fwd, wb, conv_out, tm, bias_plane=kf)
        uv = u_ref[:, f:2 * f].astype(F32)
        df = _dot_nt(dy_ref[...].astype(BF16), wd_ref[...])
        gl, dgl = _gelu_tanh(conv_out[...])
        du_ref[:, f:2 * f] = (df * gl).astype(BF16)
        dfg = df * uv * dgl
        dcb_ref[...] += jnp.sum(dfg, axis=0, keepdims=True)
        rot_g[0, 0:tm, :] = dfg
        _tap_wgrad(rot_u, fwd, rot_g, 0, acc, tm)
        _fill_rotations(rot_g, _residues(bwd), span)
        _tap_conv(rot_g, bwd, wb, conv_out, tm)
        du_ref[:, 0:f] = conv_out[...].astype(BF16)
        rot_g[0, tm:tm + HALO, :] = rot_g[0, 0:HALO, :]

        @pl.when(i == nt - 1)
        def _():
            dcw_ref[0:kf, :] = jnp.sum(acc[...], axis=1)

    def planes(taps):
        return pltpu.VMEM((len(_residues(taps)), HALO + tm, f), F32)

    prev_rows = pl.BlockSpec((HALO, 2 * f), lambda i: (jnp.maximum((nt - 1 - i) * per - 1, 0), 0))
    return _call(body, name=name, grid=(nt,),
                 in_specs=[_rows_rev(tm, d, nt), _rows_rev(tm, 2 * f, nt), prev_rows, _whole(cfw.shape),
                           _whole(cfb.shape), _resident(wd.shape)],
                 out_specs=[_rows_rev(tm, 2 * f, nt), _whole((SUBLANES, f)), _whole((1, f))],
                 out_shape=[_sds((t, 2 * f), BF16), _sds((SUBLANES, f), F32), _sds((1, f), F32)],
                 scratch=[planes(fwd), planes(bwd), pltpu.VMEM((kf + 1, SUBLANES, f), F32),
                          pltpu.VMEM((kf, SUBLANES, f), F32), pltpu.VMEM((tm, f), F32)],
                 args=(dy, u, u, cfw, cfb, wd), carry=carry)


def _bwd_norm_matmul(dout, wt, x, g, dres, name, carry=None):
    t, d = x.shape
    n = dout.shape[1]
    tm = _token_tile(t)

    def body(do_ref, wt_ref, x_ref, g_ref, dres_ref, dx_ref, dg_ref):
        @pl.when(pl.program_id(0) == 0)
        def _():
            dg_ref[...] = jnp.zeros_like(dg_ref)

        dh = _dot(do_ref[...], wt_ref[...])
        xh, r = _rms_stats(x_ref[...])
        dx, dg = _rms_bwd(dh, xh, r, g_ref[...])
        dx_ref[...] = dres_ref[...] + dx
        dg_ref[...] += dg

    return _call(body, name=name, grid=(t // tm,),
                 in_specs=[_rows(tm, n), _resident(wt.shape), _rows(tm, d), _whole(g.shape), _rows(tm, d)],
                 out_specs=[_rows(tm, d), _whole((1, d))],
                 out_shape=[_sds((t, d), F32), _sds((1, d), F32)],
                 args=(dout, wt, x, g, dres), carry=carry)


def _wgrad_tn(a, b, name, carry=None):
    t, n = a.shape
    d = b.shape[1]
    tt = 1024 if t % 1024 == 0 else _token_tile(t)
    tn = _chunk(n, 1536)
    nt = t // tt

    def body(a_ref, b_ref, o_ref, acc):
        k = pl.program_id(1)

        @pl.when(k == 0)
        def _():
            acc[...] = jnp.zeros_like(acc)

        acc[...] += _dot_tn(a_ref[...].astype(BF16), b_ref[...].astype(BF16))

        @pl.when(k == nt - 1)
        def _():
            o_ref[...] = acc[...].astype(BF16)

    return _call(body, name=name, grid=(n // tn, nt),
                 in_specs=[pl.BlockSpec((tt, tn), lambda j, k: (k, j)), pl.BlockSpec((tt, d), lambda j, k: (k, 0))],
                 out_specs=[pl.BlockSpec((tn, d), lambda j, k: (j, 0))], out_shape=[_sds((n, d), BF16)],
                 scratch=[pltpu.VMEM((tn, d), F32)], args=(a, b), carry=carry)[0]


def _bwd_merge(dy, z, bg, a_act, s, wa, wb, wo, name, carry=None):
    t, d = dy.shape
    n = z.shape[1]
    dc = a_act.shape[1]
    tm = _token_tile(t, 256)
    nt = t // tm
    o5 = n - 2 * d

    def body(dy_ref, z_ref, bg_ref, act_ref, s_ref, wa_ref, wb_ref, wo_ref,
             dact_ref, ds_ref, dgl_ref, dwo_ref, dwa_ref, dwb_ref, dbg_ref, acc_o, acc_a, acc_b):
        i = pl.program_id(0)

        @pl.when(i == 0)
        def _():
            acc_o[...] = jnp.zeros_like(acc_o)
            acc_a[...] = jnp.zeros_like(acc_a)
            acc_b[...] = jnp.zeros_like(acc_b)
            dbg_ref[...] = jnp.zeros_like(dbg_ref)

        dyb = dy_ref[...].astype(BF16)
        dm = _dot_nt(dyb, wo_ref[...])
        ya = _dot(act_ref[...], wa_ref[...])
        yb = _dot(s_ref[...], wb_ref[...])
        ga = _sigmoid(z_ref[:, o5:o5 + d].astype(F32) + bg_ref[:, 0:d])
        gb = _sigmoid(z_ref[:, o5 + d:n].astype(F32) + bg_ref[:, d:2 * d])
        acc_o[...] += _dot_tn((ga * ya + gb * yb).astype(BF16), dyb)
        dya = (dm * ga).astype(BF16)
        dyb2 = (dm * gb).astype(BF16)
        acc_a[...] += _dot_tn(act_ref[...], dya)
        acc_b[...] += _dot_tn(s_ref[...], dyb2)
        dact_ref[...] = _dot_nt(dya, wa_ref[...])
        ds_ref[...] = _dot_nt(dyb2, wb_ref[...])
        dla = dm * ya * ga * (1.0 - ga)
        dlb = dm * yb * gb * (1.0 - gb)
        dgl_ref[:, 0:d] = dla.astype(BF16)
        dgl_ref[:, d:2 * d] = dlb.astype(BF16)
        dbg_ref[:, 0:d] += jnp.sum(dla, axis=0, keepdims=True)
        dbg_ref[:, d:2 * d] += jnp.sum(dlb, axis=0, keepdims=True)

        @pl.when(i == nt - 1)
        def _():
            dwo_ref[...] = acc_o[...].astype(BF16)
            dwa_ref[...] = acc_a[...].astype(BF16)
            dwb_ref[...] = acc_b[...].astype(BF16)

    return _call(body, name=name, grid=(nt,),
                 in_specs=[_rows(tm, d), _rows(tm, n), _whole(bg.shape), _rows(tm, dc), _rows(tm, dc),
                           _resident(wa.shape), _resident(wb.shape), _resident(wo.shape)],
                 out_specs=[_rows(tm, dc), _rows(tm, dc), _rows(tm, 2 * d), _whole((d, d)), _whole((dc, d)),
                            _whole((dc, d)), _whole((1, 2 * d))],
                 out_shape=[_sds((t, dc), F32), _sds((t, dc), F32), _sds((t, 2 * d), BF16), _sds((d, d), BF16),
                            _sds((dc, d), BF16), _sds((dc, d), BF16), _sds((1, 2 * d), F32)],
                 scratch=[pltpu.VMEM((d, d), F32), pltpu.VMEM((dc, d), F32), pltpu.VMEM((dc, d), F32)],
                 args=(dy, z, bg, a_act, s, wa, wb, wo), carry=carry)


def _bwd_branch(dact, ds, z, dgl, a_conv, caw, lng, lnb, cbw, name, carry=None):
    t, n = z.shape
    dc = a_conv.shape[1]
    tm = _token_tile(t, 256)
    nt = t // tm
    ka, kb = caw.shape[0], cbw.shape[0]
    per = tm // HALO
    fwd_a, fwd_b, bwd_a, bwd_b = _fwd_taps(ka), _fwd_taps(kb), _bwd_taps(ka), _bwd_taps(kb)
    span = HALO + tm - SUBLANES

    def body(dact_ref, ds_ref, z_ref, zp_ref, dgl_ref, ac_ref, caw_ref, lng_ref, lnb_ref, cbw_ref,
             dz_ref, dcaw_ref, dcab_ref, dlng_ref, dlnb_ref, dcbw_ref,
             rot_a, rot_da, rot_c, rot_dc, wb_a, wb_b, acc_a, acc_b, conv_out):
        i = pl.program_id(0)

        @pl.when(i == 0)
        def _():
            for ref in (dcaw_ref, dcab_ref, dlng_ref, dlnb_ref, dcbw_ref, acc_a, acc_b):
                ref[...] = jnp.zeros_like(ref)
            rot_da[0, tm:tm + HALO, :] = jnp.zeros((HALO, dc), F32)
            rot_dc[0, tm:tm + HALO, :] = jnp.zeros((HALO, dc), F32)
            _broadcast_rows(wb_a, caw_ref, ka)
            _broadcast_rows(wb_b, cbw_ref, kb)

        keep = 1.0 - (i == nt - 1).astype(F32)
        a_val = z_ref[:, 0:dc].astype(F32)
        sg = _sigmoid(z_ref[:, dc:2 * dc].astype(F32))
        rot_a[0, 0:HALO, :] = zp_ref[:, 0:dc].astype(F32) * _sigmoid(zp_ref[:, dc:2 * dc].astype(F32)) * keep
        rot_a[0, HALO:HALO + tm, :] = a_val * sg
        _fill_rotations(rot_a, _residues(fwd_a), span)

        ac = ac_ref[...]
        mu = jnp.mean(ac, axis=-1, keepdims=True)
        xc = ac - mu
        rstd = lax.rsqrt(jnp.mean(xc * xc, axis=-1, keepdims=True) + NORM_EPS)
        xh = xc * rstd
        ln = xh * lng_ref[...] + lnb_ref[...]
        sl = _sigmoid(ln)
        dln = dact_ref[...] * (sl * (1.0 + ln * (1.0 - sl)))
        dlng_ref[...] += jnp.sum(dln * xh, axis=0, keepdims=True)
        dlnb_ref[...] += jnp.sum(dln, axis=0, keepdims=True)
        dxh = dln * lng_ref[...]
        dac = rstd * (dxh - jnp.mean(dxh, axis=-1, keepdims=True)
                      - xh * jnp.mean(dxh * xh, axis=-1, keepdims=True))
        dcab_ref[...] += jnp.sum(dac, axis=0, keepdims=True)
        rot_da[0, 0:tm, :] = dac
        _tap_wgrad(rot_a, fwd_a, rot_da, 0, acc_a, tm)
        _fill_rotations(rot_da, _residues(bwd_a), span)
        _tap_conv(rot_da, bwd_a, wb_a, conv_out, tm)
        rot_da[0, tm:tm + HALO, :] = rot_da[0, 0:HALO, :]
        da = conv_out[...]
        dz_ref[:, 0:dc] = (da * sg).astype(BF16)
        dz_ref[:, dc:2 * dc] = (da * a_val * sg * (1.0 - sg)).astype(BF16)

        sc_b = z_ref[:, 2 * dc:3 * dc].astype(F32)
        sc_c = z_ref[:, 3 * dc:4 * dc].astype(F32)
        sc_v = z_ref[:, 4 * dc:5 * dc].astype(F32)
        rot_c[0, 0:HALO, :] = zp_ref[:, 3 * dc:4 * dc].astype(F32) * zp_ref[:, 4 * dc:5 * dc].astype(F32) * keep
        rot_c[0, HALO:HALO + tm, :] = sc_c * sc_v
        _fill_rotations(rot_c, _residues(fwd_b), span)
        _tap_conv(rot_c, fwd_b, wb_b, conv_out, tm)
        dsv = ds_ref[...]
        dz_ref[:, 2 * dc:3 * dc] = (dsv * conv_out[...]).astype(BF16)
        rot_dc[0, 0:tm, :] = dsv * sc_b
        _tap_wgrad(rot_c, fwd_b, rot_dc, 0, acc_b, tm)
        _fill_rotations(rot_dc, _residues(bwd_b), span)
        _tap_conv(rot_dc, bwd_b, wb_b, conv_out, tm)
        rot_dc[0, tm:tm + HALO, :] = rot_dc[0, 0:HALO, :]
        dcv = conv_out[...]
        dz_ref[:, 3 * dc:4 * dc] = (dcv * sc_v).astype(BF16)
        dz_ref[:, 4 * dc:5 * dc] = (dcv * sc_c).astype(BF16)
        dz_ref[:, 5 * dc:n] = dgl_ref[...]

        @pl.when(i == nt - 1)
        def _():
            dcaw_ref[0:ka, :] = jnp.sum(acc_a[...], axis=1)
            dcbw_ref[0:kb, :] = jnp.sum(acc_b[...], axis=1)

    def planes(taps):
        return pltpu.VMEM((len(_residues(taps)), HALO + tm, dc), F32)

    prev_rows = pl.BlockSpec((HALO, 5 * dc), lambda i: (jnp.maximum((nt - 1 - i) * per - 1, 0), 0))
    return _call(body, name=name, grid=(nt,),
                 in_specs=[_rows_rev(tm, dc, nt), _rows_rev(tm, dc, nt), _rows_rev(tm, 5 * dc, nt), prev_rows,
                           _rows_rev(tm, n - 5 * dc, nt), _rows_rev(tm, dc, nt), _whole(caw.shape),
                           _whole(lng.shape), _whole(lnb.shape), _whole(cbw.shape)],
                 out_specs=[_rows_rev(tm, n, nt), _whole((HALO, dc)), _whole((1, dc)), _whole((1, dc)),
                            _whole((1, dc)), _whole((SUBLANES, dc))],
                 out_shape=[_sds((t, n), BF16), _sds((HALO, dc), F32), _sds((1, dc), F32), _sds((1, dc), F32),
                            _sds((1, dc), F32), _sds((SUBLANES, dc), F32)],
                 scratch=[planes(fwd_a), planes(bwd_a), planes(fwd_b), planes(bwd_b),
                          pltpu.VMEM((ka, SUBLANES, dc), F32), pltpu.VMEM((kb, SUBLANES, dc), F32),
                          pltpu.VMEM((ka, SUBLANES, dc), F32), pltpu.VMEM((kb, SUBLANES, dc), F32),
                          pltpu.VMEM((tm, dc), F32)],
                 args=(dact, ds, z, z, dgl, a_conv, caw, lng, lnb, cbw), carry=carry)


def _sum_slabs(lands, name):
    _, rows, cols = lands[0].shape
    tr = _row_tile(rows)
    nr = rows // tr
    depth = len(lands)

    def body(*refs):
        o_ref = refs[depth]
        i = pl.program_id(0)
        for k in range(depth):
            @pl.when(i // nr == k)
            def _(k=k):
                acc = refs[k][0].astype(F32)
                for j in range(1, N_DEV):
                    acc = acc + refs[k][j].astype(F32)
                o_ref[...] = acc

    def land_spec(k):
        return pl.BlockSpec((N_DEV, tr, cols), lambda i: (0, jnp.clip(i - k * nr, 0, nr - 1), 0))

    return _call(body, name=name, grid=(depth * nr,), in_specs=[land_spec(k) for k in range(depth)],
                 out_specs=[_rows(tr, cols)], out_shape=[_sds((depth * rows, cols), F32)], args=lands)[0]


def _adamw(w, g, m, v, name):
    rows, cols = w.shape
    tr = _row_tile(rows)

    def body(w_ref, g_ref, m_ref, v_ref, d_ref, nm_ref, nv_ref):
        gv = g_ref[...]
        nm = ADAM_B1 * m_ref[...] + (1.0 - ADAM_B1) * gv
        nv = ADAM_B2 * v_ref[...] + (1.0 - ADAM_B2) * (gv * gv)
        m_hat = nm / (1.0 - ADAM_B1 ** ADAM_STEP)
        v_hat = nv / (1.0 - ADAM_B2 ** ADAM_STEP)
        d_ref[...] = -ADAM_LR * (m_hat / (jnp.sqrt(v_hat) + ADAM_EPS) + ADAM_WD * w_ref[...])
        nm_ref[...] = nm
        nv_ref[...] = nv

    spec = _rows(tr, cols)
    return _call(body, name=name, grid=(rows // tr,), in_specs=[spec] * 4, out_specs=[spec] * 3,
                 out_shape=[_sds((rows, cols), F32)] * 3, args=(w, g, m, v))


def _pack(arrays):
    parts, meta, row = [], [], 0
    tile = SUBLANES * LANES
    for a in arrays:
        size = a.size
        padded = -(-size // tile) * tile
        flat = jnp.pad(a.reshape(-1).astype(F32), (0, padded - size))
        parts.append(flat.reshape(padded // LANES, LANES))
        meta.append((row, size, a.shape))
        row += padded // LANES
    return jnp.concatenate(parts, axis=0), meta


def _unpack(pack, meta):
    out = []
    for row, size, shape in meta:
        rows = -(-size // LANES)
        out.append(pack[row:row + rows].reshape(-1)[:size].reshape(shape))
    return out


def kernel(x, p, g_mix, w_in, b_gate, conv_a_w, conv_a_b, ln_a_g, ln_a_b, w_a_out, conv_b_w, w_b_out, w_o, g_ffn, w_up, conv_f_w, conv_f_b, w_down, g_ple, w_ple, w_ple_gate, g_final, loss_target, m_g_mix, m_w_in, m_b_gate, m_conv_a_w, m_conv_a_b, m_ln_a_g, m_ln_a_b, m_w_a_out, m_conv_b_w, m_w_b_out, m_w_o, m_g_ffn, m_w_up, m_conv_f_w, m_conv_f_b, m_w_down, m_g_ple, m_w_ple, m_w_ple_gate, m_g_final, v_g_mix, v_w_in, v_b_gate, v_conv_a_w, v_conv_a_b, v_ln_a_g, v_ln_a_b, v_w_a_out, v_conv_b_w, v_w_b_out, v_w_o, v_g_ffn, v_w_up, v_conv_f_w, v_conv_f_b, v_w_down, v_g_ple, v_w_ple, v_w_ple_gate, v_g_final):
    w = dict(zip(WEIGHT_NAMES, (g_mix, w_in, b_gate, conv_a_w, conv_a_b, ln_a_g, ln_a_b, w_a_out, conv_b_w,
                                w_b_out, w_o, g_ffn, w_up, conv_f_w, conv_f_b, w_down, g_ple, w_ple,
                                w_ple_gate, g_final)))
    mom = dict(zip(WEIGHT_NAMES, (m_g_mix, m_w_in, m_b_gate, m_conv_a_w, m_conv_a_b, m_ln_a_g, m_ln_a_b,
                                  m_w_a_out, m_conv_b_w, m_w_b_out, m_w_o, m_g_ffn, m_w_up, m_conv_f_w,
                                  m_conv_f_b, m_w_down, m_g_ple, m_w_ple, m_w_ple_gate, m_g_final)))
    var = dict(zip(WEIGHT_NAMES, (v_g_mix, v_w_in, v_b_gate, v_conv_a_w, v_conv_a_b, v_ln_a_g, v_ln_a_b,
                                  v_w_a_out, v_conv_b_w, v_w_b_out, v_w_o, v_g_ffn, v_w_up, v_conv_f_w,
                                  v_conv_f_b, v_w_down, v_g_ple, v_w_ple, v_w_ple_gate, v_g_final)))
    depth = g_mix.shape[0]
    dc = ln_a_g.shape[1]
    me = _my_index()
    x0 = x[0]
    target = loss_target[0]
    big_names = tuple(BIG_AXIS)

    shard = {name: (jnp.swapaxes(w[name], 1, 2) if name in TRANSPOSED else w[name]).astype(BF16)
             for name in big_names}

    def gather_of(layer, *names):
        return _Gather([(shard[name], layer, BIG_AXIS[name]) for name in names])

    def row(name, layer):
        return w[name][layer][None]

    first = _Gather([(shard['w_in'], 0, BIG_AXIS['w_in'])] + [(w[name][None], None, 0) for name in CONV_SHARDED])
    gathered = _run_exchange(first, "gather_first")
    w_in_full = gathered[0]
    conv_full = {name: jnp.transpose(g, (1, 2, 0, 3)).reshape(g.shape[1], g.shape[2], -1)
                 for name, g in zip(CONV_SHARDED, gathered[1:])}
    saved = []
    xc = x0
    for l in range(depth):
        carry = gather_of(l, 'w_a_out', 'w_b_out', 'w_o')
        h, z = _norm_matmul(xc, row('g_mix', l), w_in_full, f"fwd_in_{l}", carry)
        wa_full, wb_full, wo_full = carry.results
        carry = gather_of(l, 'w_up')
        a_conv, a_act, s = _fwd_branch(z, conv_full['conv_a_w'][l], row('conv_a_b', l), row('ln_a_g', l),
                                       row('ln_a_b', l), conv_full['conv_b_w'][l], dc, f"fwd_branch_{l}", carry)
        w_up_full, = carry.results
        carry = gather_of(l, 'w_down')
        x1 = _fwd_merge(xc, z, row('b_gate', l), a_act, s, wa_full, wb_full, wo_full, f"fwd_merge_{l}", carry)
        w_down_full, = carry.results
        carry = gather_of(l, 'w_ple', 'w_ple_gate')
        h2, u = _norm_matmul(x1, row('g_ffn', l), w_up_full, f"fwd_up_{l}", carry)
        w_ple_full, w_pg_full = carry.results
        carry = gather_of(l + 1, 'w_in') if l + 1 < depth else None
        x2, act = _fwd_down(x1, u, conv_full['conv_f_w'][l], row('conv_f_b', l), w_down_full, f"fwd_down_{l}",
                            carry)
        x3 = _fwd_ple(x2, row('g_ple', l), w_pg_full, p[l, 0], w_ple_full, f"fwd_ple_{l}")
        saved.append((xc, h, z, a_conv, a_act, s, x1, h2, u, act, x2,
                      dict(w_in=w_in_full, w_a_out=wa_full, w_b_out=wb_full, w_o=wo_full, w_up=w_up_full,
                           w_down=w_down_full, w_ple=w_ple_full, w_ple_gate=w_pg_full)))
        if carry is not None:
            w_in_full, = carry.results
        xc = x3

    dx, dg_final, loss_part = _loss_bwd(xc, g_final[None], target, "loss_bwd")
    landed = {name: [None] * depth for name in big_names}
    small = {name: [None] * depth for name in WEIGHT_NAMES if name not in BIG_AXIS and name != 'g_final'}

    def scatter_of(*partials):
        ex = _Scatter([(part, BIG_AXIS[name]) for name, _, part in partials])
        ex.places = [(name, layer) for name, layer, _ in partials]
        return ex

    def keep(ex):
        for (name, layer), land in zip(ex.places, ex.results):
            landed[name][layer] = land

    pending = []
    for l in reversed(range(depth)):
        xin, h, z, a_conv, a_act, s, x1, h2, u, act, x2, full = saved[l]
        dx2, d_wpg, d_wple, small['g_ple'][l] = _bwd_ple(
            dx, x2, row('g_ple', l), full['w_ple_gate'], p[l, 0], full['w_ple'], f"bwd_ple_{l}")
        carry = scatter_of(('w_ple_gate', l, d_wpg), ('w_ple', l, d_wple))
        d_wdown = _wgrad_tn(act, dx2, f"wgrad_down_{l}", carry)
        keep(carry)
        carry = scatter_of(('w_down', l, d_wdown), *pending)
        pending = []
        du, dcfw, small['conv_f_b'][l] = _bwd_down(
            dx2, u, conv_full['conv_f_w'][l], row('conv_f_b', l), full['w_down'], f"bwd_down_{l}", carry)
        keep(carry)
        small['conv_f_w'][l] = dcfw[:conv_f_w.shape[1]]
        dx1, small['g_ffn'][l] = _bwd_norm_matmul(du, full['w_up'], x1, row('g_ffn', l), dx2, f"bwd_up_{l}")
        d_wup = _wgrad_tn(du, h2, f"wgrad_up_{l}")
        dact, ds, dgl, d_wo, d_wa, d_wb, small['b_gate'][l] = _bwd_merge(
            dx1, z, row('b_gate', l), a_act, s, full['w_a_out'], full['w_b_out'], full['w_o'], f"bwd_merge_{l}")
        carry = scatter_of(('w_up', l, d_wup))
        dz, dcaw, small['conv_a_b'][l], small['ln_a_g'][l], small['ln_a_b'][l], dcbw = _bwd_branch(
            dact, ds, z, dgl, a_conv, conv_full['conv_a_w'][l], row('ln_a_g', l), row('ln_a_b', l),
            conv_full['conv_b_w'][l], f"bwd_branch_{l}", carry)
        keep(carry)
        small['conv_a_w'][l] = dcaw[:conv_a_w.shape[1]]
        small['conv_b_w'][l] = dcbw[:conv_b_w.shape[1]]
        carry = scatter_of(('w_o', l, d_wo), ('w_a_out', l, d_wa), ('w_b_out', l, d_wb))
        d_win = _wgrad_tn(dz, h, f"wgrad_in_{l}", carry)
        keep(carry)
        carry = scatter_of(('w_in', l, d_win)) if l == 0 else None
        if l > 0:
            pending = [('w_in', l, d_win)]
        dx, small['g_mix'][l] = _bwd_norm_matmul(dz, full['w_in'], xin, row('g_mix', l), dx1, f"bwd_in_{l}", carry)
        if carry is not None:
            keep(carry)
    grad_x = dx[None]

    small_names = tuple(small)
    small_parts = [jnp.stack([part[0] if part.shape[0] == 1 else part for part in small[name]])
                   for name in small_names]
    pack, meta = _pack(small_parts + [dg_final[0], loss_part])
    reduced = _unpack(_all_reduce_small(pack, "all_reduce_small"), meta)
    loss = reduced[-1][0, 0]
    grads = dict(zip(small_names, reduced[:len(small_names)]))
    grads['g_final'] = reduced[len(small_names)]
    for name in CONV_SHARDED:
        width = w[name].shape[-1]
        grads[name] = lax.dynamic_slice_in_dim(grads[name], me * width, width, axis=2)

    for name in big_names:
        slab_shape = landed[name][0].shape[1:]
        lands = [land.reshape(N_DEV, -1, slab_shape[-1]) for land in landed[name]]
        total = _sum_slabs(lands, f"sum_{name}").reshape((depth,) + slab_shape)
        grads[name] = jnp.swapaxes(total, 1, 2) if name in TRANSPOSED else total

    delta, new_m, new_v = {}, {}, {}
    for name in big_names:
        shape = w[name].shape
        flat = lambda a: a.reshape(-1, shape[-1])
        d_, m_, v_ = _adamw(flat(w[name]), flat(grads[name]), flat(mom[name]), flat(var[name]), f"adamw_{name}")
        delta[name], new_m[name], new_v[name] = d_.reshape(shape), m_.reshape(shape), v_.reshape(shape)
    rest = tuple(name for name in WEIGHT_NAMES if name not in BIG_AXIS)
    packs = [_pack([src[name] for name in rest]) for src in (w, grads, mom, var)]
    outs = _adamw(packs[0][0], packs[1][0], packs[2][0], packs[3][0], "adamw_small")
    for dst, packed in zip((delta, new_m, new_v), outs):
        dst.update(zip(rest, _unpack(packed, packs[0][1])))

    return (loss, grad_x, *[grads[n] for n in WEIGHT_NAMES], *[delta[n] for n in WEIGHT_NAMES],
            *[new_m[n] for n in WEIGHT_NAMES], *[new_v[n] for n in WEIGHT_NAMES])
```

```python
import jax
import jax.numpy as jnp
from jax import lax
from jax.experimental import pallas as pl
from jax.experimental.pallas import tpu as pltpu

F32 = jnp.float32
BF16 = jnp.bfloat16
MESH = pl.DeviceIdType.MESH

N_DEV = 8
NORM_EPS = 1e-6
HALO = 32
LANES = 128
SUBLANES = 8
VMEM_LIMIT_BYTES = 56 * 2**20

ADAM_LR = 0.001
ADAM_B1 = 0.9
ADAM_B2 = 0.999
ADAM_EPS = 1e-08
ADAM_WD = 0.01
ADAM_STEP = 10

WEIGHT_NAMES = ('g_mix', 'w_in', 'b_gate', 'conv_a_w', 'conv_a_b', 'ln_a_g', 'ln_a_b', 'w_a_out',
                'conv_b_w', 'w_b_out', 'w_o', 'g_ffn', 'w_up', 'conv_f_w', 'conv_f_b', 'w_down',
                'g_ple', 'w_ple', 'w_ple_gate', 'g_final')
BIG_AXIS = {'w_in': 0, 'w_up': 0, 'w_a_out': 1, 'w_b_out': 1, 'w_o': 0, 'w_down': 0, 'w_ple': 1,
            'w_ple_gate': 0}
TRANSPOSED = ('w_in', 'w_up')
CONV_SHARDED = ('conv_a_w', 'conv_b_w', 'conv_f_w')


def _dot(a, b):
    return jnp.dot(a, b, preferred_element_type=F32)


def _dot_nt(a, b):
    return lax.dot_general(a, b, (((1,), (1,)), ((), ())), preferred_element_type=F32)


def _dot_tn(a, b):
    return lax.dot_general(a, b, (((0,), (0,)), ((), ())), preferred_element_type=F32)


def _sigmoid(v):
    return jax.nn.sigmoid(v)


def _token_tile(t, cap=512):
    return cap if (t % cap == 0 and t > 512) else 128


def _chunk(n, limit=512):
    for c in range(limit - limit % LANES, 0, -LANES):
        if n % c == 0:
            return c
    return n


def _row_tile(rows):
    for c in (512, 256, 128, 64, 32, 16, 8):
        if rows % c == 0:
            return c
    return rows


def _rows(tm, width):
    return pl.BlockSpec((tm, width), lambda i: (i, 0))


def _rows_rev(tm, width, nt):
    return pl.BlockSpec((tm, width), lambda i: (nt - 1 - i, 0))


def _whole(shape):
    nd = len(shape)
    return pl.BlockSpec(tuple(shape), lambda i: (0,) * nd)


def _resident(shape):
    nd = len(shape)
    return pl.BlockSpec(tuple(shape), lambda i: (0,) * nd, pipeline_mode=pl.Buffered(1))


def _sds(shape, dtype):
    return jax.ShapeDtypeStruct(tuple(shape), dtype)


def _rms_stats(xv):
    r = lax.rsqrt(jnp.mean(xv * xv, axis=-1, keepdims=True) + NORM_EPS)
    return xv * r, r


def _rms_bwd(dy, xh, r, g):
    dxh = dy * g
    dx = r * (dxh - xh * jnp.mean(dxh * xh, axis=-1, keepdims=True))
    return dx, jnp.sum(dy * xh, axis=0, keepdims=True)


GELU_C0 = 0.7978845608028654
GELU_C1 = GELU_C0 * 0.044715


def _gelu_tanh(v):
    v2 = v * v
    t = jnp.tanh(v * (GELU_C0 + GELU_C1 * v2))
    q = 1.0 + t
    hv = 0.5 * v
    grad = 0.5 * q + hv * (1.0 - t * t) * (GELU_C0 + (3.0 * GELU_C1) * v2)
    return hv * q, grad


ROW_CHUNK = 32
LANE_CHUNK = 512


def _residues(taps):
    return [0] + sorted({off % SUBLANES for _, off in taps} - {0})


def _fill_rotations(rot_ref, residues, length):
    for plane, r in enumerate(residues):
        if r:
            rot_ref[plane, 0:length, :] = rot_ref[0, pl.ds(r, length), :]


def _broadcast_rows(dst_ref, src_ref, count):
    for k in range(count):
        dst_ref[k] = jnp.broadcast_to(src_ref[k:k + 1, :], dst_ref.shape[1:])


def _lane_chunks(width):
    return [(c0, min(LANE_CHUNK, width - c0)) for c0 in range(0, width, LANE_CHUNK)]


def _tap_conv(rot_ref, taps, wb_ref, out_ref, tm, bias_plane=None):
    residues = _residues(taps)
    plane = {r: p for p, r in enumerate(residues)}
    blocks = ROW_CHUNK // SUBLANES
    width = out_ref.shape[1]

    def chunk(c, state):
        r0 = c * ROW_CHUNK
        for c0, cw in _lane_chunks(width):
            accs = [None if bias_plane is None else wb_ref[bias_plane, :, c0:c0 + cw]] * blocks
            for k, off in taps:
                wk = wb_ref[k, :, c0:c0 + cw]
                base = off - off % SUBLANES
                for j in range(blocks):
                    at = pl.multiple_of(r0 + base + SUBLANES * j, SUBLANES)
                    term = wk * rot_ref[plane[off % SUBLANES], pl.ds(at, SUBLANES), c0:c0 + cw]
                    accs[j] = term if accs[j] is None else accs[j] + term
            for j in range(blocks):
                at = pl.multiple_of(r0 + SUBLANES * j, SUBLANES)
                out_ref[pl.ds(at, SUBLANES), c0:c0 + cw] = accs[j]
        return state

    lax.fori_loop(0, tm // ROW_CHUNK, chunk, 0)


def _tap_wgrad(rot_ref, taps, dy_ref, dy_plane, acc_ref, tm):
    residues = _residues(taps)
    plane = {r: p for p, r in enumerate(residues)}
    blocks = ROW_CHUNK // SUBLANES
    width = acc_ref.shape[2]

    def chunk(c, state):
        r0 = c * ROW_CHUNK
        for c0, cw in _lane_chunks(width):
            dys = [dy_ref[dy_plane, pl.ds(pl.multiple_of(r0 + SUBLANES * j, SUBLANES), SUBLANES), c0:c0 + cw]
                   for j in range(blocks)]
            for k, off in taps:
                base = off - off % SUBLANES
                part = None
                for j in range(blocks):
                    at = pl.multiple_of(r0 + base + SUBLANES * j, SUBLANES)
                    term = dys[j] * rot_ref[plane[off % SUBLANES], pl.ds(at, SUBLANES), c0:c0 + cw]
                    part = term if part is None else part + term
                acc_ref[k, :, c0:c0 + cw] += part
        return state

    lax.fori_loop(0, tm // ROW_CHUNK, chunk, 0)


def _fwd_taps(width):
    return [(k, HALO - (width - 1) + k) for k in range(width)]


def _bwd_taps(width):
    return [(k, width - 1 - k) for k in range(width)]


def _my_index():
    return 4 * lax.axis_index("x") + 2 * lax.axis_index("y") + lax.axis_index("c")


def _mesh_id(idx):
    return (idx // 4, (idx // 2) % 2, idx % 2)


def _slab(ref, axis, idx, width):
    at = [slice(None)] * len(ref.shape)
    at[axis] = pl.ds(pl.multiple_of(idx * width, width), width)
    return ref.at[tuple(at)]


class _Exchange:
    def __init__(self, inputs, out_shape):
        n = len(inputs)
        self.inputs = list(inputs)
        self.out_shape = list(out_shape)
        self.sems = [pltpu.SemaphoreType.DMA((n, N_DEV - 1)), pltpu.SemaphoreType.DMA((n, N_DEV - 1)),
                     pltpu.SemaphoreType.DMA((n,))]
        self.results = None

    def _local(self, ins, outs, k, me):
        raise NotImplementedError

    def _remote(self, ins, outs, k, me, sender, receiver):
        raise NotImplementedError

    def start(self, ins, outs, sems):
        send_sems, recv_sems, local_sems = sems
        me = _my_index()
        for k in range(len(self.inputs)):
            src, dst = self._local(ins, outs, k, me)
            pltpu.make_async_copy(src, dst, local_sems.at[k]).start()
            for dist in range(1, N_DEV):
                peer = (me + dist) % N_DEV
                src, dst = self._remote(ins, outs, k, me, me, peer)
                pltpu.make_async_remote_copy(
                    src_ref=src, dst_ref=dst, send_sem=send_sems.at[k, dist - 1],
                    recv_sem=recv_sems.at[k, dist - 1], device_id=_mesh_id(peer), device_id_type=MESH).start()

    def wait(self, ins, outs, sems):
        send_sems, recv_sems, local_sems = sems
        me = _my_index()
        for k in range(len(self.inputs)):
            for dist in range(1, N_DEV):
                sender = (me + N_DEV - dist) % N_DEV
                src, dst = self._remote(ins, outs, k, me, sender, me)
                cp = pltpu.make_async_remote_copy(
                    src_ref=src, dst_ref=dst, send_sem=send_sems.at[k, dist - 1],
                    recv_sem=recv_sems.at[k, dist - 1], device_id=_mesh_id(sender), device_id_type=MESH)
                cp.wait_send()
                cp.wait_recv()
            src, dst = self._local(ins, outs, k, me)
            pltpu.make_async_copy(src, dst, local_sems.at[k]).wait()

    def forward(self, ins, outs, sems):
        pass


class _Gather(_Exchange):
    FLIPS = ((1, 0), (0, 1), (1, 1))

    def __init__(self, items):
        self.items = list(items)
        out_shape = []
        for shards, layer, axis in self.items:
            shape = list(shards.shape if layer is None else shards.shape[1:])
            shape[axis] *= N_DEV
            out_shape.append(_sds(shape, shards.dtype))
        super().__init__([it[0] for it in self.items], out_shape)

    def _src(self, ins, k):
        layer = self.items[k][1]
        return ins[k] if layer is None else ins[k].at[layer]

    def _place(self, outs, k, idx):
        axis = self.items[k][2]
        return _slab(outs[k], axis, idx, self.out_shape[k].shape[axis] // N_DEV)

    def _copy(self, sems, k, j, src, dst, to):
        return pltpu.make_async_remote_copy(src_ref=src, dst_ref=dst, send_sem=sems[0].at[k, j],
                                            recv_sem=sems[1].at[k, j], device_id=to, device_id_type=MESH)

    @staticmethod
    def _places():
        x, y, c = lax.axis_index("x"), lax.axis_index("y"), lax.axis_index("c")
        chips = [(1 - x if fx else x, 1 - y if fy else y) for fx, fy in _Gather.FLIPS]
        return (x, y, c), (x, y, 1 - c), chips

    @staticmethod
    def _index(place):
        return 4 * place[0] + 2 * place[1] + place[2]

    def start(self, ins, outs, sems):
        me, sibling, chips = self._places()
        for k in range(len(self.inputs)):
            src, mine = self._src(ins, k), self._place(outs, k, self._index(me))
            pltpu.make_async_copy(src, mine, sems[2].at[k]).start()
            self._copy(sems, k, 0, src, mine, sibling).start()
            for j, chip in enumerate(chips):
                self._copy(sems, k, 1 + j, src, mine, (*chip, me[2])).start()

    def forward(self, ins, outs, sems):
        me, sibling, chips = self._places()
        for k in range(len(self.inputs)):
            for j, chip in enumerate(chips):
                got = self._place(outs, k, self._index((*chip, me[2])))
                self._copy(sems, k, 1 + j, got, got, (*chip, me[2])).wait_recv()
                self._copy(sems, k, 4 + j, got, got, sibling).start()

    def wait(self, ins, outs, sems):
        me, sibling, chips = self._places()
        for k in range(len(self.inputs)):
            src, mine = self._src(ins, k), self._place(outs, k, self._index(me))
            self._copy(sems, k, 0, src, self._place(outs, k, self._index(sibling)), sibling).wait_recv()
            for j, chip in enumerate(chips):
                got = self._place(outs, k, self._index((*chip, sibling[2])))
                self._copy(sems, k, 4 + j, got, got, sibling).wait_recv()
            for j in range(N_DEV - 1):
                self._copy(sems, k, j, src, mine, sibling).wait_send()
            pltpu.make_async_copy(src, mine, sems[2].at[k]).wait()


class _Scatter(_Exchange):
    def __init__(self, items):
        self.items = list(items)
        out_shape = []
        for partial, axis in self.items:
            shape = list(partial.shape)
            shape[axis] //= N_DEV
            out_shape.append(_sds([N_DEV] + shape, partial.dtype))
        super().__init__([it[0] for it in self.items], out_shape)

    def _take(self, ins, k, idx):
        axis = self.items[k][1]
        return _slab(ins[k], axis, idx, self.items[k][0].shape[axis] // N_DEV)

    def _local(self, ins, outs, k, me):
        return self._take(ins, k, me), outs[k].at[me]

    def _remote(self, ins, outs, k, me, sender, receiver):
        return self._take(ins, k, receiver), outs[k].at[sender]


def _run_exchange(exchange, name):
    n = len(exchange.inputs)

    def body(*refs):
        ins, outs, sems = refs[:n], refs[n:2 * n], refs[2 * n:]
        exchange.start(ins, outs, sems)
        exchange.forward(ins, outs, sems)
        exchange.wait(ins, outs, sems)

    any_spec = pl.BlockSpec(memory_space=pl.ANY)
    exchange.results = pl.pallas_call(
        body, name=name, in_specs=[any_spec] * n, out_specs=[any_spec] * n, out_shape=exchange.out_shape,
        scratch_shapes=exchange.sems)(*exchange.inputs)
    return exchange.results


class _Layer:
    def __init__(self, stack, index):
        self.stack, self.index = stack, index
        self.shape = (1,) + stack.shape[1:] if stack.ndim == 2 else stack.shape[1:]

    def view(self, ref):
        return ref.at[pl.ds(self.index, 1)] if self.stack.ndim == 2 else ref.at[self.index]


def _call(body, *, name, grid, in_specs, out_specs, out_shape, args, scratch=(), carry=None):
    in_specs, out_specs, out_shape, scratch = list(in_specs), list(out_specs), list(out_shape), list(scratch)
    args = list(args)
    layers = {k: a for k, a in enumerate(args) if isinstance(a, _Layer)}
    for k, a in layers.items():
        in_specs[k], args[k] = _whole(a.stack.shape), a.stack
    params = pltpu.CompilerParams(dimension_semantics=("arbitrary",) * len(grid),
                                  vmem_limit_bytes=VMEM_LIMIT_BYTES)
    n_in, n_out, n_scr = len(in_specs), len(out_specs), len(scratch)
    n_x = 0 if carry is None else len(carry.inputs)
    steps = 1
    for extent in grid:
        steps *= extent
    assert carry is None or steps >= 3, "a carrier needs a step each for start, second stage and wait"

    def whole_body(*refs):
        core_in, x_in = list(refs[:n_in]), refs[n_in:n_in + n_x]
        refs = refs[n_in + n_x:]
        core_out, x_out = refs[:n_out], refs[n_out:n_out + n_x]
        refs = refs[n_out + n_x:]
        core_scr, sems = refs[:n_scr], refs[n_scr:]
        for k, a in layers.items():
            core_in[k] = a.view(core_in[k])
        if carry is None:
            body(*core_in, *core_out, *core_scr)
            return
        step = pl.program_id(0)
        for axis in range(1, len(grid)):
            step = step * grid[axis] + pl.program_id(axis)

        @pl.when(step == 0)
        def _():
            carry.start(x_in, x_out, sems)

        @pl.when(step == steps - 2)
        def _():
            carry.forward(x_in, x_out, sems)

        body(*core_in, *core_out, *core_scr)

        @pl.when(step == steps - 1)
        def _():
            carry.wait(x_in, x_out, sems)

    any_spec = pl.BlockSpec(memory_space=pl.ANY)
    extra_in = [] if carry is None else carry.inputs
    extra_shape = [] if carry is None else carry.out_shape
    extra_sems = [] if carry is None else carry.sems
    outs = pl.pallas_call(
        whole_body, name=name, grid=grid, in_specs=in_specs + [any_spec] * n_x,
        out_specs=out_specs + [any_spec] * n_x, out_shape=out_shape + extra_shape,
        scratch_shapes=scratch + extra_sems, compiler_params=params)(*args, *extra_in)
    if carry is not None:
        carry.results = outs[n_out:]
    return outs[:n_out]


def _all_reduce(parts, plan, name):
    n = len(parts)
    out_shape = []
    for group in plan:
        shape = parts[group[0]].shape
        if len(group) > 1:
            shape = (len(group),) + (shape[1:] if shape[0] == 1 else shape)
        out_shape.append(_sds(shape, F32))

    def body(*refs):
        ins, outs, lands = refs[:n], refs[n:n + len(plan)], refs[n + len(plan):2 * n + len(plan)]
        send_sems, recv_sems = refs[2 * n + len(plan):]
        me = _my_index()
        for k in range(n):
            lands[k][me] = ins[k][...]
            for dist in range(1, N_DEV):
                pltpu.make_async_remote_copy(
                    src_ref=ins[k], dst_ref=lands[k].at[me],
                    send_sem=send_sems.at[k, dist - 1], recv_sem=recv_sems.at[k, dist - 1],
                    device_id=_mesh_id((me + dist) % N_DEV), device_id_type=MESH).start()
        for k in range(n):
            for dist in range(1, N_DEV):
                sender = (me + N_DEV - dist) % N_DEV
                cp = pltpu.make_async_remote_copy(
                    src_ref=ins[k], dst_ref=lands[k].at[sender],
                    send_sem=send_sems.at[k, dist - 1], recv_sem=recv_sems.at[k, dist - 1],
                    device_id=_mesh_id(sender), device_id_type=MESH)
                cp.wait_send()
                cp.wait_recv()
        for o_ref, group in zip(outs, plan):
            for j, k in enumerate(group):
                total = lands[k][0]
                for dev in range(1, N_DEV):
                    total = total + lands[k][dev]
                if len(group) == 1:
                    o_ref[...] = total
                elif parts[k].shape[0] == 1:
                    o_ref[j:j + 1, :] = total
                else:
                    o_ref[j] = total

    vmem = pl.BlockSpec(memory_space=pltpu.VMEM)
    return pl.pallas_call(
        body, name=name, in_specs=[vmem] * n, out_specs=[vmem] * len(plan), out_shape=out_shape,
        scratch_shapes=[pltpu.VMEM((N_DEV,) + part.shape, F32) for part in parts]
        + [pltpu.SemaphoreType.DMA((n, N_DEV - 1)), pltpu.SemaphoreType.DMA((n, N_DEV - 1))])(*parts)


def _norm_matmul(x, g, wt, name, carry=None):
    t, d = x.shape
    n = wt.shape[0]
    tm, nc = _token_tile(t), _chunk(n)

    def body(x_ref, g_ref, wt_ref, h_ref, o_ref):
        xh, _ = _rms_stats(x_ref[...])
        h = (xh * g_ref[...]).astype(BF16)
        h_ref[...] = h
        for n0 in range(0, n, nc):
            o_ref[:, n0:n0 + nc] = _dot_nt(h, wt_ref[n0:n0 + nc, :]).astype(BF16)

    return _call(body, name=name, grid=(t // tm,),
                 in_specs=[_rows(tm, d), _whole(g.shape), _resident(wt.shape)],
                 out_specs=[_rows(tm, d), _rows(tm, n)],
                 out_shape=[_sds((t, d), BF16), _sds((t, n), BF16)], args=(x, g, wt), carry=carry)


def _fwd_branch(z, caw, cab, lng, lnb, cbw, dc, name, carry=None):
    t = z.shape[0]
    tm = _token_tile(t)
    ka, kb = caw.shape[0], cbw.shape[0]
    taps_a, taps_b = _fwd_taps(ka), _fwd_taps(kb)
    res_a, res_b = _residues(taps_a), _residues(taps_b)
    span = HALO + tm - SUBLANES

    def body(z_ref, caw_ref, cab_ref, lng_ref, lnb_ref, cbw_ref, ac_ref, act_ref, s_ref,
             rot_a, rot_b, wb_a, wb_b, cb):
        @pl.when(pl.program_id(0) == 0)
        def _():
            rot_a[0, 0:HALO, :] = jnp.zeros((HALO, dc), F32)
            rot_b[0, 0:HALO, :] = jnp.zeros((HALO, dc), F32)
            _broadcast_rows(wb_a, caw_ref, ka)
            wb_a[ka] = jnp.broadcast_to(cab_ref[...], (SUBLANES, dc))
            _broadcast_rows(wb_b, cbw_ref, kb)

        a_val = z_ref[:, 0:dc].astype(F32)
        a_gt = z_ref[:, dc:2 * dc].astype(F32)
        rot_a[0, HALO:HALO + tm, :] = a_val * _sigmoid(a_gt)
        _fill_rotations(rot_a, res_a, span)
        _tap_conv(rot_a, taps_a, wb_a, ac_ref, tm, bias_plane=ka)
        ac = ac_ref[...]
        mu = jnp.mean(ac, axis=-1, keepdims=True)
        xc = ac - mu
        var = jnp.mean(xc * xc, axis=-1, keepdims=True)
        ln = xc * lax.rsqrt(var + NORM_EPS) * lng_ref[...] + lnb_ref[...]
        act_ref[...] = (ln * _sigmoid(ln)).astype(BF16)
        rot_a[0, 0:HALO, :] = rot_a[0, tm:tm + HALO, :]

        sc_c = z_ref[:, 3 * dc:4 * dc].astype(F32)
        sc_v = z_ref[:, 4 * dc:5 * dc].astype(F32)
        rot_b[0, HALO:HALO + tm, :] = sc_c * sc_v
        _fill_rotations(rot_b, res_b, span)
        _tap_conv(rot_b, taps_b, wb_b, cb, tm)
        s_ref[...] = (z_ref[:, 2 * dc:3 * dc].astype(F32) * cb[...]).astype(BF16)
        rot_b[0, 0:HALO, :] = rot_b[0, tm:tm + HALO, :]

    return _call(body, name=name, grid=(t // tm,),
                 in_specs=[_rows(tm, 5 * dc), _whole(caw.shape), _whole(cab.shape), _whole(lng.shape),
                           _whole(lnb.shape), _whole(cbw.shape)],
                 out_specs=[_rows(tm, dc), _rows(tm, dc), _rows(tm, dc)],
                 out_shape=[_sds((t, dc), F32), _sds((t, dc), BF16), _sds((t, dc), BF16)],
                 scratch=[pltpu.VMEM((len(res_a), HALO + tm, dc), F32), pltpu.VMEM((len(res_b), HALO + tm, dc), F32),
                          pltpu.VMEM((ka + 1, SUBLANES, dc), F32), pltpu.VMEM((kb, SUBLANES, dc), F32),
                          pltpu.VMEM((tm, dc), F32)],
                 args=(z, caw, cab, lng, lnb, cbw), carry=carry)


def _fwd_merge(x, z, bg, a_act, s, wa, wb, wo, name, carry=None):
    t, d = x.shape
    n = z.shape[1]
    dc = a_act.shape[1]
    tm = _token_tile(t)
    o5 = n - 2 * d

    def body(x_ref, z_ref, bg_ref, act_ref, s_ref, wa_ref, wb_ref, wo_ref, o_ref):
        ya = _dot(act_ref[...], wa_ref[...])
        yb = _dot(s_ref[...], wb_ref[...])
        ga = _sigmoid(z_ref[:, o5:o5 + d].astype(F32) + bg_ref[:, 0:d])
        gb = _sigmoid(z_ref[:, o5 + d:n].astype(F32) + bg_ref[:, d:2 * d])
        m = (ga * ya + gb * yb).astype(BF16)
        o_ref[...] = x_ref[...] + _dot(m, wo_ref[...])

    return _call(body, name=name, grid=(t // tm,),
                 in_specs=[_rows(tm, d), _rows(tm, n), _whole(bg.shape), _rows(tm, dc), _rows(tm, dc),
                           _resident(wa.shape), _resident(wb.shape), _resident(wo.shape)],
                 out_specs=[_rows(tm, d)], out_shape=[_sds((t, d), F32)],
                 args=(x, z, bg, a_act, s, wa, wb, wo), carry=carry)[0]


def _fwd_down(x, u, cfw, cfb, wd, name, carry=None):
    t, d = x.shape
    f = u.shape[1] // 2
    tm = _token_tile(t, 256)
    kf = cfw.shape[0]

    taps = _fwd_taps(kf)
    residues = _residues(taps)

    def body(x_ref, u_ref, cfw_ref, cfb_ref, wd_ref, o_ref, act_ref, gl_ref, dgl_ref, rot_u, wb, fg):
        @pl.when(pl.program_id(0) == 0)
        def _():
            rot_u[0, 0:HALO, :] = jnp.zeros((HALO, f), F32)
            _broadcast_rows(wb, cfw_ref, kf)
            wb[kf] = jnp.broadcast_to(cfb_ref[...], (SUBLANES, f))

        rot_u[0, HALO:HALO + tm, :] = u_ref[:, 0:f].astype(F32)
        _fill_rotations(rot_u, residues, HALO + tm - SUBLANES)
        _tap_conv(rot_u, taps, wb, fg, tm, bias_plane=kf)
        gl, dgl = _gelu_tanh(fg[...])
        gl_ref[...] = gl.astype(BF16)
        dgl_ref[...] = dgl.astype(BF16)
        act = (gl * u_ref[:, f:2 * f].astype(F32)).astype(BF16)
        act_ref[...] = act
        o_ref[...] = x_ref[...] + _dot(act, wd_ref[...])
        rot_u[0, 0:HALO, :] = rot_u[0, tm:tm + HALO, :]

    return _call(body, name=name, grid=(t // tm,),
                 in_specs=[_rows(tm, d), _rows(tm, 2 * f), _whole(cfw.shape), _whole(cfb.shape),
                           _resident(wd.shape)],
                 out_specs=[_rows(tm, d), _rows(tm, f), _rows(tm, f), _rows(tm, f)],
                 out_shape=[_sds((t, d), F32), _sds((t, f), BF16), _sds((t, f), BF16), _sds((t, f), BF16)],
                 scratch=[pltpu.VMEM((len(residues), HALO + tm, f), F32), pltpu.VMEM((kf + 1, SUBLANES, f), F32),
                          pltpu.VMEM((tm, f), F32)],
                 args=(x, u, cfw, cfb, wd), carry=carry)


def _fwd_ple(x, g, wpg, p, wple, name, carry=None):
    t, d = x.shape
    pd = p.shape[1]
    tm = _token_tile(t)

    def body(x_ref, g_ref, wpg_ref, p_ref, wple_ref, o_ref):
        xv = x_ref[...]
        xh, _ = _rms_stats(xv)
        lg = _dot((xh * g_ref[...]).astype(BF16), wpg_ref[...])
        pp = _dot(p_ref[...].astype(BF16), wple_ref[...])
        o_ref[...] = xv + _sigmoid(lg) * pp

    return _call(body, name=name, grid=(t // tm,),
                 in_specs=[_rows(tm, d), _whole(g.shape), _resident(wpg.shape), _rows(tm, pd),
                           _resident(wple.shape)],
                 out_specs=[_rows(tm, d)], out_shape=[_sds((t, d), F32)],
                 args=(x, g, wpg, p, wple), carry=carry)[0]


def _loss_bwd(x, g, target, name):
    t, d = x.shape
    tm = _token_tile(t)

    def body(x_ref, g_ref, t_ref, dx_ref, dg_ref, loss_ref):
        @pl.when(pl.program_id(0) == 0)
        def _():
            dg_ref[...] = jnp.zeros_like(dg_ref)
            loss_ref[...] = jnp.zeros_like(loss_ref)

        xh, r = _rms_stats(x_ref[...])
        err = xh * g_ref[...] - t_ref[...]
        sq = jnp.sum(jnp.sum(err * err, axis=0, keepdims=True), axis=1, keepdims=True)
        loss_ref[...] += jnp.broadcast_to(0.5 * sq / d, loss_ref.shape)
        dx, dg = _rms_bwd(err / d, xh, r, g_ref[...])
        dx_ref[...] = dx
        dg_ref[...] += dg

    return _call(body, name=name, grid=(t // tm,),
                 in_specs=[_rows(tm, d), _whole(g.shape), _rows(tm, d)],
                 out_specs=[_rows(tm, d), _whole((1, d)), _whole((SUBLANES, LANES))],
                 out_shape=[_sds((t, d), F32), _sds((1, d), F32), _sds((SUBLANES, LANES), F32)],
                 args=(x, g, target))


def _bwd_ple(dy, x, g, wpg, p, wple, name, carry=None):
    t, d = x.shape
    pd = p.shape[1]
    tm = _token_tile(t)
    nt = t // tm

    def body(dy_ref, x_ref, g_ref, wpg_ref, p_ref, wple_ref, dx_ref, dwpg_ref, dwple_ref, dg_ref,
             acc_pg, acc_ple):
        i = pl.program_id(0)

        @pl.when(i == 0)
        def _():
            acc_pg[...] = jnp.zeros_like(acc_pg)
            acc_ple[...] = jnp.zeros_like(acc_ple)
            dg_ref[...] = jnp.zeros_like(dg_ref)

        dyv = dy_ref[...]
        xh, r = _rms_stats(x_ref[...])
        h = (xh * g_ref[...]).astype(BF16)
        pb = p_ref[...].astype(BF16)
        pg = _sigmoid(_dot(h, wpg_ref[...]))
        pp = _dot(pb, wple_ref[...])
        dpp = (dyv * pg).astype(BF16)
        dlg = (dyv * pp * pg * (1.0 - pg)).astype(BF16)
        acc_ple[...] += _dot_tn(pb, dpp)
        acc_pg[...] += _dot_tn(h, dlg)
        dx, dg = _rms_bwd(_dot_nt(dlg, wpg_ref[...]), xh, r, g_ref[...])
        dx_ref[...] = dyv + dx
        dg_ref[...] += dg

        @pl.when(i == nt - 1)
        def _():
            dwpg_ref[...] = acc_pg[...].astype(BF16)
            dwple_ref[...] = acc_ple[...].astype(BF16)

    return _call(body, name=name, grid=(nt,),
                 in_specs=[_rows(tm, d), _rows(tm, d), _whole(g.shape), _resident(wpg.shape), _rows(tm, pd),
                           _resident(wple.shape)],
                 out_specs=[_rows(tm, d), _whole((d, d)), _whole((pd, d)), _whole((1, d))],
                 out_shape=[_sds((t, d), F32), _sds((d, d), BF16), _sds((pd, d), BF16), _sds((1, d), F32)],
                 scratch=[pltpu.VMEM((d, d), F32), pltpu.VMEM((pd, d), F32)],
                 args=(dy, x, g, wpg, p, wple), carry=carry)


def _bwd_down(dy, u, gl, dgl, cfw, wd, name, carry=None):
    t, d = dy.shape
    f = u.shape[1] // 2
    tm = _token_tile(t, 256)
    nt = t // tm
    kf = cfw.shape[0]
    per = tm // HALO
    fwd, bwd = _fwd_taps(kf), _bwd_taps(kf)
    span = HALO + tm - SUBLANES

    def body(dy_ref, u_ref, up_ref, gl_ref, dgl_ref, cfw_ref, wd_ref, du_ref, dcw_ref, dcb_ref,
             rot_u, rot_g, wb, acc, conv_out):
        i = pl.program_id(0)

        @pl.when(i == 0)
        def _():
            for ref in (dcw_ref, dcb_ref, acc):
                ref[...] = jnp.zeros_like(ref)
            rot_g[0, tm:tm + HALO, :] = jnp.zeros((HALO, f), F32)
            _broadcast_rows(wb, cfw_ref, kf)

        first = (i == nt - 1).astype(F32)
        rot_u[0, 0:HALO, :] = up_ref[:, 0:f].astype(F32) * (1.0 - first)
        rot_u[0, HALO:HALO + tm, :] = u_ref[:, 0:f].astype(F32)
        _fill_rotations(rot_u, _residues(fwd), span)
        df = _dot_nt(dy_ref[...].astype(BF16), wd_ref[...])
        du_ref[:, f:2 * f] = (df * gl_ref[...].astype(F32)).astype(BF16)
        dfg = df * u_ref[:, f:2 * f].astype(F32) * dgl_ref[...].astype(F32)
        dcb_ref[...] += jnp.sum(dfg, axis=0, keepdims=True)
        rot_g[0, 0:tm, :] = dfg
        _tap_wgrad(rot_u, fwd, rot_g, 0, acc, tm)
        _fill_rotations(rot_g, _residues(bwd), span)
        _tap_conv(rot_g, bwd, wb, conv_out, tm)
        du_ref[:, 0:f] = conv_out[...].astype(BF16)
        rot_g[0, tm:tm + HALO, :] = rot_g[0, 0:HALO, :]

        @pl.when(i == nt - 1)
        def _():
            dcw_ref[0:kf, :] = jnp.sum(acc[...], axis=1)

    def planes(taps):
        return pltpu.VMEM((len(_residues(taps)), HALO + tm, f), F32)

    prev_rows = pl.BlockSpec((HALO, 2 * f), lambda i: (jnp.maximum((nt - 1 - i) * per - 1, 0), 0))
    return _call(body, name=name, grid=(nt,),
                 in_specs=[_rows_rev(tm, d, nt), _rows_rev(tm, 2 * f, nt), prev_rows, _rows_rev(tm, f, nt),
                           _rows_rev(tm, f, nt), _whole(cfw.shape), _resident(wd.shape)],
                 out_specs=[_rows_rev(tm, 2 * f, nt), _whole((SUBLANES, f)), _whole((1, f))],
                 out_shape=[_sds((t, 2 * f), BF16), _sds((SUBLANES, f), F32), _sds((1, f), F32)],
                 scratch=[planes(fwd), planes(bwd), pltpu.VMEM((kf, SUBLANES, f), F32),
                          pltpu.VMEM((kf, SUBLANES, f), F32), pltpu.VMEM((tm, f), F32)],
                 args=(dy, u, u, gl, dgl, cfw, wd), carry=carry)


def _bwd_norm_matmul(dout, wt, x, g, dres, name, carry=None):
    t, d = x.shape
    n = dout.shape[1]
    tm = _token_tile(t)

    def body(do_ref, wt_ref, x_ref, g_ref, dres_ref, dx_ref, dg_ref):
        @pl.when(pl.program_id(0) == 0)
        def _():
            dg_ref[...] = jnp.zeros_like(dg_ref)

        dh = _dot(do_ref[...], wt_ref[...])
        xh, r = _rms_stats(x_ref[...])
        dx, dg = _rms_bwd(dh, xh, r, g_ref[...])
        dx_ref[...] = dres_ref[...] + dx
        dg_ref[...] += dg

    return _call(body, name=name, grid=(t // tm,),
                 in_specs=[_rows(tm, n), _resident(wt.shape), _rows(tm, d), _whole(g.shape), _rows(tm, d)],
                 out_specs=[_rows(tm, d), _whole((1, d))],
                 out_shape=[_sds((t, d), F32), _sds((1, d), F32)],
                 args=(dout, wt, x, g, dres), carry=carry)


def _wgrad_tn(a, b, name, carry=None):
    t, n = a.shape
    d = b.shape[1]
    tt = 1024 if t % 1024 == 0 else _token_tile(t)
    tn = _chunk(n, 1536)
    nt = t // tt

    def body(a_ref, b_ref, o_ref, acc):
        k = pl.program_id(1)

        @pl.when(k == 0)
        def _():
            acc[...] = jnp.zeros_like(acc)

        acc[...] += _dot_tn(a_ref[...].astype(BF16), b_ref[...].astype(BF16))

        @pl.when(k == nt - 1)
        def _():
            o_ref[...] = acc[...].astype(BF16)

    return _call(body, name=name, grid=(n // tn, nt),
                 in_specs=[pl.BlockSpec((tt, tn), lambda j, k: (k, j)), pl.BlockSpec((tt, d), lambda j, k: (k, 0))],
                 out_specs=[pl.BlockSpec((tn, d), lambda j, k: (j, 0))], out_shape=[_sds((n, d), BF16)],
                 scratch=[pltpu.VMEM((tn, d), F32)], args=(a, b), carry=carry)[0]


def _bwd_merge(dy, z, bg, a_act, s, wa, wb, wo, name, carry=None):
    t, d = dy.shape
    n = z.shape[1]
    dc = a_act.shape[1]
    tm = _token_tile(t, 256)
    nt = t // tm
    o5 = n - 2 * d

    def body(dy_ref, z_ref, bg_ref, act_ref, s_ref, wa_ref, wb_ref, wo_ref,
             dact_ref, ds_ref, dgl_ref, dwo_ref, dwa_ref, dwb_ref, dbg_ref, acc_o, acc_a, acc_b):
        i = pl.program_id(0)

        @pl.when(i == 0)
        def _():
            acc_o[...] = jnp.zeros_like(acc_o)
            acc_a[...] = jnp.zeros_like(acc_a)
            acc_b[...] = jnp.zeros_like(acc_b)
            dbg_ref[...] = jnp.zeros_like(dbg_ref)

        dyb = dy_ref[...].astype(BF16)
        dm = _dot_nt(dyb, wo_ref[...])
        ya = _dot(act_ref[...], wa_ref[...])
        yb = _dot(s_ref[...], wb_ref[...])
        ga = _sigmoid(z_ref[:, o5:o5 + d].astype(F32) + bg_ref[:, 0:d])
        gb = _sigmoid(z_ref[:, o5 + d:n].astype(F32) + bg_ref[:, d:2 * d])
        acc_o[...] += _dot_tn((ga * ya + gb * yb).astype(BF16), dyb)
        dya = (dm * ga).astype(BF16)
        dyb2 = (dm * gb).astype(BF16)
        acc_a[...] += _dot_tn(act_ref[...], dya)
        acc_b[...] += _dot_tn(s_ref[...], dyb2)
        dact_ref[...] = _dot_nt(dya, wa_ref[...])
        ds_ref[...] = _dot_nt(dyb2, wb_ref[...])
        dla = dm * ya * ga * (1.0 - ga)
        dlb = dm * yb * gb * (1.0 - gb)
        dgl_ref[:, 0:d] = dla.astype(BF16)
        dgl_ref[:, d:2 * d] = dlb.astype(BF16)
        dbg_ref[:, 0:d] += jnp.sum(dla, axis=0, keepdims=True)
        dbg_ref[:, d:2 * d] += jnp.sum(dlb, axis=0, keepdims=True)

        @pl.when(i == nt - 1)
        def _():
            dwo_ref[...] = acc_o[...].astype(BF16)
            dwa_ref[...] = acc_a[...].astype(BF16)
            dwb_ref[...] = acc_b[...].astype(BF16)

    return _call(body, name=name, grid=(nt,),
                 in_specs=[_rows(tm, d), _rows(tm, n), _whole(bg.shape), _rows(tm, dc), _rows(tm, dc),
                           _resident(wa.shape), _resident(wb.shape), _resident(wo.shape)],
                 out_specs=[_rows(tm, dc), _rows(tm, dc), _rows(tm, 2 * d), _whole((d, d)), _whole((dc, d)),
                            _whole((dc, d)), _whole((1, 2 * d))],
                 out_shape=[_sds((t, dc), F32), _sds((t, dc), F32), _sds((t, 2 * d), BF16), _sds((d, d), BF16),
                            _sds((dc, d), BF16), _sds((dc, d), BF16), _sds((1, 2 * d), F32)],
                 scratch=[pltpu.VMEM((d, d), F32), pltpu.VMEM((dc, d), F32), pltpu.VMEM((dc, d), F32)],
                 args=(dy, z, bg, a_act, s, wa, wb, wo), carry=carry)


def _bwd_branch(dact, ds, z, dgl, a_conv, caw, lng, lnb, cbw, name, carry=None):
    t, n = z.shape
    dc = a_conv.shape[1]
    tm = _token_tile(t, 256)
    nt = t // tm
    ka, kb = caw.shape[0], cbw.shape[0]
    per = tm // HALO
    fwd_a, fwd_b, bwd_a, bwd_b = _fwd_taps(ka), _fwd_taps(kb), _bwd_taps(ka), _bwd_taps(kb)
    span = HALO + tm - SUBLANES

    def body(dact_ref, ds_ref, z_ref, zp_ref, dgl_ref, ac_ref, caw_ref, lng_ref, lnb_ref, cbw_ref,
             dz_ref, dcaw_ref, dcab_ref, dlng_ref, dlnb_ref, dcbw_ref,
             rot_a, rot_da, rot_c, rot_dc, wb_a, wb_b, acc_a, acc_b, conv_out):
        i = pl.program_id(0)

        @pl.when(i == 0)
        def _():
            for ref in (dcaw_ref, dcab_ref, dlng_ref, dlnb_ref, dcbw_ref, acc_a, acc_b):
                ref[...] = jnp.zeros_like(ref)
            rot_da[0, tm:tm + HALO, :] = jnp.zeros((HALO, dc), F32)
            rot_dc[0, tm:tm + HALO, :] = jnp.zeros((HALO, dc), F32)
            _broadcast_rows(wb_a, caw_ref, ka)
            _broadcast_rows(wb_b, cbw_ref, kb)

        keep = 1.0 - (i == nt - 1).astype(F32)
        a_val = z_ref[:, 0:dc].astype(F32)
        sg = _sigmoid(z_ref[:, dc:2 * dc].astype(F32))
        rot_a[0, 0:HALO, :] = zp_ref[:, 0:dc].astype(F32) * _sigmoid(zp_ref[:, dc:2 * dc].astype(F32)) * keep
        rot_a[0, HALO:HALO + tm, :] = a_val * sg
        _fill_rotations(rot_a, _residues(fwd_a), span)

        ac = ac_ref[...]
        mu = jnp.mean(ac, axis=-1, keepdims=True)
        xc = ac - mu
        rstd = lax.rsqrt(jnp.mean(xc * xc, axis=-1, keepdims=True) + NORM_EPS)
        xh = xc * rstd
        ln = xh * lng_ref[...] + lnb_ref[...]
        sl = _sigmoid(ln)
        dln = dact_ref[...] * (sl * (1.0 + ln * (1.0 - sl)))
        dlng_ref[...] += jnp.sum(dln * xh, axis=0, keepdims=True)
        dlnb_ref[...] += jnp.sum(dln, axis=0, keepdims=True)
        dxh = dln * lng_ref[...]
        dac = rstd * (dxh - jnp.mean(dxh, axis=-1, keepdims=True)
                      - xh * jnp.mean(dxh * xh, axis=-1, keepdims=True))
        dcab_ref[...] += jnp.sum(dac, axis=0, keepdims=True)
        rot_da[0, 0:tm, :] = dac
        _tap_wgrad(rot_a, fwd_a, rot_da, 0, acc_a, tm)
        _fill_rotations(rot_da, _residues(bwd_a), span)
        _tap_conv(rot_da, bwd_a, wb_a, conv_out, tm)
        rot_da[0, tm:tm + HALO, :] = rot_da[0, 0:HALO, :]
        da = conv_out[...]
        dz_ref[:, 0:dc] = (da * sg).astype(BF16)
        dz_ref[:, dc:2 * dc] = (da * a_val * sg * (1.0 - sg)).astype(BF16)

        sc_b = z_ref[:, 2 * dc:3 * dc].astype(F32)
        sc_c = z_ref[:, 3 * dc:4 * dc].astype(F32)
        sc_v = z_ref[:, 4 * dc:5 * dc].astype(F32)
        rot_c[0, 0:HALO, :] = zp_ref[:, 3 * dc:4 * dc].astype(F32) * zp_ref[:, 4 * dc:5 * dc].astype(F32) * keep
        rot_c[0, HALO:HALO + tm, :] = sc_c * sc_v
        _fill_rotations(rot_c, _residues(fwd_b), span)
        _tap_conv(rot_c, fwd_b, wb_b, conv_out, tm)
        dsv = ds_ref[...]
        dz_ref[:, 2 * dc:3 * dc] = (dsv * conv_out[...]).astype(BF16)
        rot_dc[0, 0:tm, :] = dsv * sc_b
        _tap_wgrad(rot_c, fwd_b, rot_dc, 0, acc_b, tm)
        _fill_rotations(rot_dc, _residues(bwd_b), span)
        _tap_conv(rot_dc, bwd_b, wb_b, conv_out, tm)
        rot_dc[0, tm:tm + HALO, :] = rot_dc[0, 0:HALO, :]
        dcv = conv_out[...]
        dz_ref[:, 3 * dc:4 * dc] = (dcv * sc_v).astype(BF16)
        dz_ref[:, 4 * dc:5 * dc] = (dcv * sc_c).astype(BF16)
        dz_ref[:, 5 * dc:n] = dgl_ref[...]

        @pl.when(i == nt - 1)
        def _():
            dcaw_ref[0:ka, :] = jnp.sum(acc_a[...], axis=1)
            dcbw_ref[0:kb, :] = jnp.sum(acc_b[...], axis=1)

    def planes(taps):
        return pltpu.VMEM((len(_residues(taps)), HALO + tm, dc), F32)

    prev_rows = pl.BlockSpec((HALO, 5 * dc), lambda i: (jnp.maximum((nt - 1 - i) * per - 1, 0), 0))
    return _call(body, name=name, grid=(nt,),
                 in_specs=[_rows_rev(tm, dc, nt), _rows_rev(tm, dc, nt), _rows_rev(tm, 5 * dc, nt), prev_rows,
                           _rows_rev(tm, n - 5 * dc, nt), _rows_rev(tm, dc, nt), _whole(caw.shape),
                           _whole(lng.shape), _whole(lnb.shape), _whole(cbw.shape)],
                 out_specs=[_rows_rev(tm, n, nt), _whole((HALO, dc)), _whole((1, dc)), _whole((1, dc)),
                            _whole((1, dc)), _whole((SUBLANES, dc))],
                 out_shape=[_sds((t, n), BF16), _sds((HALO, dc), F32), _sds((1, dc), F32), _sds((1, dc), F32),
                            _sds((1, dc), F32), _sds((SUBLANES, dc), F32)],
                 scratch=[planes(fwd_a), planes(bwd_a), planes(fwd_b), planes(bwd_b),
                          pltpu.VMEM((ka, SUBLANES, dc), F32), pltpu.VMEM((kb, SUBLANES, dc), F32),
                          pltpu.VMEM((ka, SUBLANES, dc), F32), pltpu.VMEM((kb, SUBLANES, dc), F32),
                          pltpu.VMEM((tm, dc), F32)],
                 args=(dact, ds, z, z, dgl, a_conv, caw, lng, lnb, cbw), carry=carry)


def _sum_slabs(lands, name):
    _, rows, cols = lands[0].shape
    tr = _row_tile(rows)
    nr = rows // tr
    depth = len(lands)

    def body(*refs):
        o_ref = refs[depth]
        i = pl.program_id(0)
        for k in range(depth):
            @pl.when(i // nr == k)
            def _(k=k):
                acc = refs[k][0].astype(F32)
                for j in range(1, N_DEV):
                    acc = acc + refs[k][j].astype(F32)
                o_ref[...] = acc

    return _call(body, name=name, grid=(depth * nr,), in_specs=_land_specs(depth, nr, tr, cols),
                 out_specs=[_rows(tr, cols)], out_shape=[_sds((depth * rows, cols), F32)], args=lands)[0]


def _land_specs(depth, nr, tr, cols):
    def spec(k):
        return pl.BlockSpec((N_DEV, tr, cols), lambda i: (0, jnp.clip(i - k * nr, 0, nr - 1), 0))
    return [spec(k) for k in range(depth)]


def _adamw_math(w, g, m, v):
    nm = ADAM_B1 * m + (1.0 - ADAM_B1) * g
    nv = ADAM_B2 * v + (1.0 - ADAM_B2) * (g * g)
    m_hat = nm / (1.0 - ADAM_B1 ** ADAM_STEP)
    v_hat = nv / (1.0 - ADAM_B2 ** ADAM_STEP)
    return -ADAM_LR * (m_hat / (jnp.sqrt(v_hat) + ADAM_EPS) + ADAM_WD * w), nm, nv


def _adamw(w, g, m, v, name):
    rows, cols = w.shape
    tr = _row_tile(rows)

    def body(w_ref, g_ref, m_ref, v_ref, d_ref, nm_ref, nv_ref):
        d_ref[...], nm_ref[...], nv_ref[...] = _adamw_math(w_ref[...], g_ref[...], m_ref[...], v_ref[...])

    spec = _rows(tr, cols)
    return _call(body, name=name, grid=(rows // tr,), in_specs=[spec] * 4, out_specs=[spec] * 3,
                 out_shape=[_sds((rows, cols), F32)] * 3, args=(w, g, m, v))


def _sum_adamw(lands, w, m, v, name):
    _, rows, cols = lands[0].shape
    tr = _row_tile(rows)
    nr = rows // tr
    depth = len(lands)

    def body(*refs):
        w_ref, m_ref, v_ref, g_ref, d_ref, nm_ref, nv_ref = refs[depth:]
        i = pl.program_id(0)
        for k in range(depth):
            @pl.when(i // nr == k)
            def _(k=k):
                acc = refs[k][0].astype(F32)
                for j in range(1, N_DEV):
                    acc = acc + refs[k][j].astype(F32)
                g_ref[...] = acc
                d_ref[...], nm_ref[...], nv_ref[...] = _adamw_math(w_ref[...], acc, m_ref[...], v_ref[...])

    spec = _rows(tr, cols)
    return _call(body, name=name, grid=(depth * nr,), in_specs=_land_specs(depth, nr, tr, cols) + [spec] * 3,
                 out_specs=[spec] * 4, out_shape=[_sds((depth * rows, cols), F32)] * 4, args=(*lands, w, m, v))


def _adamw_small(ws, gs, ms, vs, name):
    n = len(ws)

    def body(*refs):
        w_refs, g_refs, m_refs, v_refs = refs[:n], refs[n:2 * n], refs[2 * n:3 * n], refs[3 * n:4 * n]
        d_refs, nm_refs, nv_refs = refs[4 * n:5 * n], refs[5 * n:6 * n], refs[6 * n:]
        for k in range(n):
            d_refs[k][...], nm_refs[k][...], nv_refs[k][...] = _adamw_math(
                w_refs[k][...], g_refs[k][...], m_refs[k][...], v_refs[k][...])

    vmem = pl.BlockSpec(memory_space=pltpu.VMEM)
    outs = pl.pallas_call(
        body, name=name, in_specs=[vmem] * (4 * n), out_specs=[vmem] * (3 * n),
        out_shape=[_sds(a.shape, F32) for a in ws] * 3)(*ws, *gs, *ms, *vs)
    return outs[:n], outs[n:2 * n], outs[2 * n:]


def kernel(x, p, g_mix, w_in, b_gate, conv_a_w, conv_a_b, ln_a_g, ln_a_b, w_a_out, conv_b_w, w_b_out, w_o, g_ffn, w_up, conv_f_w, conv_f_b, w_down, g_ple, w_ple, w_ple_gate, g_final, loss_target, m_g_mix, m_w_in, m_b_gate, m_conv_a_w, m_conv_a_b, m_ln_a_g, m_ln_a_b, m_w_a_out, m_conv_b_w, m_w_b_out, m_w_o, m_g_ffn, m_w_up, m_conv_f_w, m_conv_f_b, m_w_down, m_g_ple, m_w_ple, m_w_ple_gate, m_g_final, v_g_mix, v_w_in, v_b_gate, v_conv_a_w, v_conv_a_b, v_ln_a_g, v_ln_a_b, v_w_a_out, v_conv_b_w, v_w_b_out, v_w_o, v_g_ffn, v_w_up, v_conv_f_w, v_conv_f_b, v_w_down, v_g_ple, v_w_ple, v_w_ple_gate, v_g_final):
    w = dict(zip(WEIGHT_NAMES, (g_mix, w_in, b_gate, conv_a_w, conv_a_b, ln_a_g, ln_a_b, w_a_out, conv_b_w,
                                w_b_out, w_o, g_ffn, w_up, conv_f_w, conv_f_b, w_down, g_ple, w_ple,
                                w_ple_gate, g_final)))
    mom = dict(zip(WEIGHT_NAMES, (m_g_mix, m_w_in, m_b_gate, m_conv_a_w, m_conv_a_b, m_ln_a_g, m_ln_a_b,
                                  m_w_a_out, m_conv_b_w, m_w_b_out, m_w_o, m_g_ffn, m_w_up, m_conv_f_w,
                                  m_conv_f_b, m_w_down, m_g_ple, m_w_ple, m_w_ple_gate, m_g_final)))
    var = dict(zip(WEIGHT_NAMES, (v_g_mix, v_w_in, v_b_gate, v_conv_a_w, v_conv_a_b, v_ln_a_g, v_ln_a_b,
                                  v_w_a_out, v_conv_b_w, v_w_b_out, v_w_o, v_g_ffn, v_w_up, v_conv_f_w,
                                  v_conv_f_b, v_w_down, v_g_ple, v_w_ple, v_w_ple_gate, v_g_final)))
    depth = g_mix.shape[0]
    dc = ln_a_g.shape[1]
    me = _my_index()
    x0 = x[0]
    target = loss_target[0]
    big_names = tuple(BIG_AXIS)

    shard = {name: (jnp.swapaxes(w[name], 1, 2) if name in TRANSPOSED else w[name]).astype(BF16)
             for name in big_names}

    def gather_of(layer, *names):
        return _Gather([(shard[name], layer, BIG_AXIS[name]) for name in names])

    def row(name, layer):
        return _Layer(w[name], layer)

    first = _Gather([(shard['w_in'], 0, BIG_AXIS['w_in'])] + [(w[name][None], None, 0) for name in CONV_SHARDED])
    gathered = _run_exchange(first, "gather_first")
    w_in_full = gathered[0]
    conv_full = {name: jnp.transpose(g, (1, 2, 0, 3)).reshape(g.shape[1], g.shape[2], -1)
                 for name, g in zip(CONV_SHARDED, gathered[1:])}
    saved = []
    xc = x0
    for l in range(depth):
        carry = gather_of(l, 'w_a_out', 'w_b_out', 'w_o')
        h, z = _norm_matmul(xc, row('g_mix', l), w_in_full, f"fwd_in_{l}", carry)
        wa_full, wb_full, wo_full = carry.results
        carry = gather_of(l, 'w_up')
        a_conv, a_act, s = _fwd_branch(z, _Layer(conv_full['conv_a_w'], l), row('conv_a_b', l), row('ln_a_g', l),
                                       row('ln_a_b', l), _Layer(conv_full['conv_b_w'], l), dc, f"fwd_branch_{l}", carry)
        w_up_full, = carry.results
        carry = gather_of(l, 'w_down')
        x1 = _fwd_merge(xc, z, row('b_gate', l), a_act, s, wa_full, wb_full, wo_full, f"fwd_merge_{l}", carry)
        w_down_full, = carry.results
        carry = gather_of(l, 'w_ple', 'w_ple_gate')
        h2, u = _norm_matmul(x1, row('g_ffn', l), w_up_full, f"fwd_up_{l}", carry)
        w_ple_full, w_pg_full = carry.results
        carry = gather_of(l + 1, 'w_in') if l + 1 < depth else None
        x2, act, gl, dgl = _fwd_down(x1, u, _Layer(conv_full['conv_f_w'], l), row('conv_f_b', l), w_down_full,
                                     f"fwd_down_{l}", carry)
        x3 = _fwd_ple(x2, row('g_ple', l), w_pg_full, p[l, 0], w_ple_full, f"fwd_ple_{l}")
        saved.append((xc, h, z, a_conv, a_act, s, x1, h2, u, act, gl, dgl, x2,
                      dict(w_in=w_in_full, w_a_out=wa_full, w_b_out=wb_full, w_o=wo_full, w_up=w_up_full,
                           w_down=w_down_full, w_ple=w_ple_full, w_ple_gate=w_pg_full)))
        if carry is not None:
            w_in_full, = carry.results
        xc = x3

    dx, dg_final, loss_part = _loss_bwd(xc, g_final[None], target, "loss_bwd")
    landed = {name: [None] * depth for name in big_names}
    small = {name: [None] * depth for name in WEIGHT_NAMES if name not in BIG_AXIS and name != 'g_final'}

    def scatter_of(*partials):
        ex = _Scatter([(part, BIG_AXIS[name]) for name, _, part in partials])
        ex.places = [(name, layer) for name, layer, _ in partials]
        return ex

    def keep(ex):
        for (name, layer), land in zip(ex.places, ex.results):
            landed[name][layer] = land

    pending = []
    for l in reversed(range(depth)):
        xin, h, z, a_conv, a_act, s, x1, h2, u, act, gl, dgl, x2, full = saved[l]
        dx2, d_wpg, d_wple, small['g_ple'][l] = _bwd_ple(
            dx, x2, row('g_ple', l), full['w_ple_gate'], p[l, 0], full['w_ple'], f"bwd_ple_{l}")
        carry = scatter_of(('w_ple_gate', l, d_wpg), ('w_ple', l, d_wple))
        d_wdown = _wgrad_tn(act, dx2, f"wgrad_down_{l}", carry)
        keep(carry)
        carry = scatter_of(('w_down', l, d_wdown), *pending)
        pending = []
        du, small['conv_f_w'][l], small['conv_f_b'][l] = _bwd_down(
            dx2, u, gl, dgl, _Layer(conv_full['conv_f_w'], l), full['w_down'], f"bwd_down_{l}", carry)
        keep(carry)
        dx1, small['g_ffn'][l] = _bwd_norm_matmul(du, full['w_up'], x1, row('g_ffn', l), dx2, f"bwd_up_{l}")
        d_wup = _wgrad_tn(du, h2, f"wgrad_up_{l}")
        dact, ds, dgate, d_wo, d_wa, d_wb, small['b_gate'][l] = _bwd_merge(
            dx1, z, row('b_gate', l), a_act, s, full['w_a_out'], full['w_b_out'], full['w_o'], f"bwd_merge_{l}")
        carry = scatter_of(('w_up', l, d_wup))
        (dz, small['conv_a_w'][l], small['conv_a_b'][l], small['ln_a_g'][l], small['ln_a_b'][l],
         small['conv_b_w'][l]) = _bwd_branch(
            dact, ds, z, dgate, a_conv, _Layer(conv_full['conv_a_w'], l), row('ln_a_g', l), row('ln_a_b', l),
            _Layer(conv_full['conv_b_w'], l), f"bwd_branch_{l}", carry)
        keep(carry)
        carry = scatter_of(('w_o', l, d_wo), ('w_a_out', l, d_wa), ('w_b_out', l, d_wb))
        d_win = _wgrad_tn(dz, h, f"wgrad_in_{l}", carry)
        keep(carry)
        carry = scatter_of(('w_in', l, d_win)) if l == 0 else None
        if l > 0:
            pending = [('w_in', l, d_win)]
        dx, small['g_mix'][l] = _bwd_norm_matmul(dz, full['w_in'], xin, row('g_mix', l), dx1, f"bwd_in_{l}", carry)
        if carry is not None:
            keep(carry)
    grad_x = dx[None]

    small_names = tuple(small)
    parts = [part for name in small_names for part in small[name]] + [dg_final, loss_part]
    plan = [tuple(range(k * depth, (k + 1) * depth)) for k in range(len(small_names))]
    plan += [(len(parts) - 2,), (len(parts) - 1,)]
    reduced = _all_reduce(parts, plan, "all_reduce_small")
    loss = reduced[-1][0, 0]
    grads = dict(zip(small_names, reduced[:len(small_names)]))
    grads['g_final'] = reduced[len(small_names)].reshape(g_final.shape)
    for name in CONV_SHARDED:
        _, taps, width = w[name].shape
        grads[name] = lax.dynamic_slice(grads[name], (0, 0, me * width), (depth, taps, width))

    delta, new_m, new_v = {}, {}, {}
    for name in big_names:
        shape = w[name].shape
        flat = lambda a: a.reshape(-1, shape[-1])
        slab_shape = landed[name][0].shape[1:]
        lands = [land.reshape(N_DEV, -1, slab_shape[-1]) for land in landed[name]]
        if name in TRANSPOSED:
            total = _sum_slabs(lands, f"sum_{name}").reshape((depth,) + slab_shape)
            grads[name] = jnp.swapaxes(total, 1, 2)
            d_, m_, v_ = _adamw(flat(w[name]), flat(grads[name]), flat(mom[name]), flat(var[name]),
                                f"adamw_{name}")
        else:
            g_, d_, m_, v_ = _sum_adamw(lands, flat(w[name]), flat(mom[name]), flat(var[name]), f"adamw_{name}")
            grads[name] = g_.reshape(shape)
        delta[name], new_m[name], new_v[name] = d_.reshape(shape), m_.reshape(shape), v_.reshape(shape)
    rest = tuple(name for name in WEIGHT_NAMES if name not in BIG_AXIS)
    as_2d = lambda a: a.reshape(1, -1) if a.ndim == 1 else a
    outs = _adamw_small(*[[as_2d(src[name]) for name in rest] for src in (w, grads, mom, var)], "adamw_small")
    for dst, values in zip((delta, new_m, new_v), outs):
        dst.update({name: value.reshape(w[name].shape) for name, value in zip(rest, values)})

    return (loss, grad_x, *[grads[n] for n in WEIGHT_NAMES], *[delta[n] for n in WEIGHT_NAMES],
            *[new_m[n] for n in WEIGHT_NAMES], *[new_v[n] for n in WEIGHT_NAMES])
```

```python
import jax
import jax.numpy as jnp
from jax import lax
from jax.experimental import pallas as pl
from jax.experimental.pallas import tpu as pltpu

F32 = jnp.float32
BF16 = jnp.bfloat16
MESH = pl.DeviceIdType.MESH

N_DEV = 8
NORM_EPS = 1e-6
HALO = 32
LANES = 128
SUBLANES = 8
VMEM_LIMIT_BYTES = 56 * 2**20

ADAM_LR = 0.001
ADAM_B1 = 0.9
ADAM_B2 = 0.999
ADAM_EPS = 1e-08
ADAM_WD = 0.01
ADAM_STEP = 10

WEIGHT_NAMES = ('g_mix', 'w_in', 'b_gate', 'conv_a_w', 'conv_a_b', 'ln_a_g', 'ln_a_b', 'w_a_out',
                'conv_b_w', 'w_b_out', 'w_o', 'g_ffn', 'w_up', 'conv_f_w', 'conv_f_b', 'w_down',
                'g_ple', 'w_ple', 'w_ple_gate', 'g_final')
BIG_AXIS = {'w_in': 0, 'w_up': 0, 'w_a_out': 1, 'w_b_out': 1, 'w_o': 0, 'w_down': 0, 'w_ple': 1,
            'w_ple_gate': 0}
TRANSPOSED = ('w_in', 'w_up')
CONV_SHARDED = ('conv_a_w', 'conv_b_w', 'conv_f_w')


def _dot(a, b):
    return jnp.dot(a, b, preferred_element_type=F32)


def _dot_nt(a, b):
    return lax.dot_general(a, b, (((1,), (1,)), ((), ())), preferred_element_type=F32)


def _dot_tn(a, b):
    return lax.dot_general(a, b, (((0,), (0,)), ((), ())), preferred_element_type=F32)


def _sigmoid(v):
    return jax.nn.sigmoid(v)


def _token_tile(t, cap=512):
    return cap if (t % cap == 0 and t > 512) else 128


def _chunk(n, limit=512):
    for c in range(limit - limit % LANES, 0, -LANES):
        if n % c == 0:
            return c
    return n


def _row_tile(rows):
    for c in (512, 256, 128, 64, 32, 16, 8):
        if rows % c == 0:
            return c
    return rows


def _rows(tm, width):
    return pl.BlockSpec((tm, width), lambda i: (i, 0))


def _rows_rev(tm, width, nt):
    return pl.BlockSpec((tm, width), lambda i: (nt - 1 - i, 0))


def _whole(shape):
    nd = len(shape)
    return pl.BlockSpec(tuple(shape), lambda i: (0,) * nd)


def _resident(shape):
    nd = len(shape)
    return pl.BlockSpec(tuple(shape), lambda i: (0,) * nd, pipeline_mode=pl.Buffered(1))


def _sds(shape, dtype):
    return jax.ShapeDtypeStruct(tuple(shape), dtype)


def _rms_stats(xv):
    r = lax.rsqrt(jnp.mean(xv * xv, axis=-1, keepdims=True) + NORM_EPS)
    return xv * r, r


def _rms_bwd(dy, xh, r, g):
    dxh = dy * g
    dx = r * (dxh - xh * jnp.mean(dxh * xh, axis=-1, keepdims=True))
    return dx, jnp.sum(dy * xh, axis=0, keepdims=True)


GELU_C0 = 0.7978845608028654
GELU_C1 = GELU_C0 * 0.044715


def _gelu_tanh(v):
    v2 = v * v
    t = jnp.tanh(v * (GELU_C0 + GELU_C1 * v2))
    q = 1.0 + t
    hv = 0.5 * v
    grad = 0.5 * q + hv * (1.0 - t * t) * (GELU_C0 + (3.0 * GELU_C1) * v2)
    return hv * q, grad


ROW_CHUNK = 32
LANE_CHUNK = 512


def _residues(taps):
    return [0] + sorted({off % SUBLANES for _, off in taps} - {0})


def _fill_rotations(rot_ref, residues, length):
    for plane, r in enumerate(residues):
        if r:
            rot_ref[plane, 0:length, :] = rot_ref[0, pl.ds(r, length), :]


def _broadcast_rows(dst_ref, src_ref, count):
    for k in range(count):
        dst_ref[k] = jnp.broadcast_to(src_ref[k:k + 1, :], dst_ref.shape[1:])


def _lane_chunks(width):
    return [(c0, min(LANE_CHUNK, width - c0)) for c0 in range(0, width, LANE_CHUNK)]


def _tap_conv(rot_ref, taps, wb_ref, out_ref, tm, bias_plane=None):
    residues = _residues(taps)
    plane = {r: p for p, r in enumerate(residues)}
    blocks = ROW_CHUNK // SUBLANES
    width = out_ref.shape[1]

    def chunk(c, state):
        r0 = c * ROW_CHUNK
        for c0, cw in _lane_chunks(width):
            accs = [None if bias_plane is None else wb_ref[bias_plane, :, c0:c0 + cw]] * blocks
            for k, off in taps:
                wk = wb_ref[k, :, c0:c0 + cw]
                base = off - off % SUBLANES
                for j in range(blocks):
                    at = pl.multiple_of(r0 + base + SUBLANES * j, SUBLANES)
                    term = wk * rot_ref[plane[off % SUBLANES], pl.ds(at, SUBLANES), c0:c0 + cw]
                    accs[j] = term if accs[j] is None else accs[j] + term
            for j in range(blocks):
                at = pl.multiple_of(r0 + SUBLANES * j, SUBLANES)
                out_ref[pl.ds(at, SUBLANES), c0:c0 + cw] = accs[j]
        return state

    lax.fori_loop(0, tm // ROW_CHUNK, chunk, 0)


def _tap_wgrad(rot_ref, taps, dy_ref, dy_plane, acc_ref, tm):
    residues = _residues(taps)
    plane = {r: p for p, r in enumerate(residues)}
    blocks = ROW_CHUNK // SUBLANES
    width = acc_ref.shape[2]

    def chunk(c, state):
        r0 = c * ROW_CHUNK
        for c0, cw in _lane_chunks(width):
            dys = [dy_ref[dy_plane, pl.ds(pl.multiple_of(r0 + SUBLANES * j, SUBLANES), SUBLANES), c0:c0 + cw]
                   for j in range(blocks)]
            for k, off in taps:
                base = off - off % SUBLANES
                part = None
                for j in range(blocks):
                    at = pl.multiple_of(r0 + base + SUBLANES * j, SUBLANES)
                    term = dys[j] * rot_ref[plane[off % SUBLANES], pl.ds(at, SUBLANES), c0:c0 + cw]
                    part = term if part is None else part + term
                acc_ref[k, :, c0:c0 + cw] += part
        return state

    lax.fori_loop(0, tm // ROW_CHUNK, chunk, 0)


def _fwd_taps(width):
    return [(k, HALO - (width - 1) + k) for k in range(width)]


def _bwd_taps(width):
    return [(k, width - 1 - k) for k in range(width)]


def _my_index():
    return 4 * lax.axis_index("x") + 2 * lax.axis_index("y") + lax.axis_index("c")


def _mesh_id(idx):
    return (idx // 4, (idx // 2) % 2, idx % 2)


def _slab(ref, axis, idx, width):
    at = [slice(None)] * len(ref.shape)
    at[axis] = pl.ds(pl.multiple_of(idx * width, width), width)
    return ref.at[tuple(at)]


class _Exchange:
    def __init__(self, inputs, out_shape):
        n = len(inputs)
        self.inputs = list(inputs)
        self.out_shape = list(out_shape)
        self.sems = [pltpu.SemaphoreType.DMA((n, N_DEV - 1)), pltpu.SemaphoreType.DMA((n, N_DEV - 1)),
                     pltpu.SemaphoreType.DMA((n,))]
        self.results = None

    def _local(self, ins, outs, k, me):
        raise NotImplementedError

    def _remote(self, ins, outs, k, me, sender, receiver):
        raise NotImplementedError

    def start(self, ins, outs, sems):
        send_sems, recv_sems, local_sems = sems
        me = _my_index()
        for k in range(len(self.inputs)):
            src, dst = self._local(ins, outs, k, me)
            pltpu.make_async_copy(src, dst, local_sems.at[k]).start()
            for dist in range(1, N_DEV):
                peer = (me + dist) % N_DEV
                src, dst = self._remote(ins, outs, k, me, me, peer)
                pltpu.make_async_remote_copy(
                    src_ref=src, dst_ref=dst, send_sem=send_sems.at[k, dist - 1],
                    recv_sem=recv_sems.at[k, dist - 1], device_id=_mesh_id(peer), device_id_type=MESH).start()

    def wait(self, ins, outs, sems):
        send_sems, recv_sems, local_sems = sems
        me = _my_index()
        for k in range(len(self.inputs)):
            for dist in range(1, N_DEV):
                sender = (me + N_DEV - dist) % N_DEV
                src, dst = self._remote(ins, outs, k, me, sender, me)
                cp = pltpu.make_async_remote_copy(
                    src_ref=src, dst_ref=dst, send_sem=send_sems.at[k, dist - 1],
                    recv_sem=recv_sems.at[k, dist - 1], device_id=_mesh_id(sender), device_id_type=MESH)
                cp.wait_send()
                cp.wait_recv()
            src, dst = self._local(ins, outs, k, me)
            pltpu.make_async_copy(src, dst, local_sems.at[k]).wait()

    def forward(self, ins, outs, sems):
        pass


class _Gather(_Exchange):
    FLIPS = ((1, 0), (0, 1), (1, 1))

    def __init__(self, items):
        self.items = list(items)
        out_shape = []
        for shards, layer, axis in self.items:
            shape = list(shards.shape if layer is None else shards.shape[1:])
            shape[axis] *= N_DEV
            out_shape.append(_sds(shape, shards.dtype))
        super().__init__([it[0] for it in self.items], out_shape)

    def _src(self, ins, k):
        layer = self.items[k][1]
        return ins[k] if layer is None else ins[k].at[layer]

    def _place(self, outs, k, idx):
        axis = self.items[k][2]
        return _slab(outs[k], axis, idx, self.out_shape[k].shape[axis] // N_DEV)

    def _copy(self, sems, k, j, src, dst, to):
        return pltpu.make_async_remote_copy(src_ref=src, dst_ref=dst, send_sem=sems[0].at[k, j],
                                            recv_sem=sems[1].at[k, j], device_id=to, device_id_type=MESH)

    @staticmethod
    def _places():
        x, y, c = lax.axis_index("x"), lax.axis_index("y"), lax.axis_index("c")
        chips = [(1 - x if fx else x, 1 - y if fy else y) for fx, fy in _Gather.FLIPS]
        return (x, y, c), (x, y, 1 - c), chips

    @staticmethod
    def _index(place):
        return 4 * place[0] + 2 * place[1] + place[2]

    def start(self, ins, outs, sems):
        me, sibling, chips = self._places()
        for k in range(len(self.inputs)):
            src, mine = self._src(ins, k), self._place(outs, k, self._index(me))
            pltpu.make_async_copy(src, mine, sems[2].at[k]).start()
            self._copy(sems, k, 0, src, mine, sibling).start()
            for j, chip in enumerate(chips):
                self._copy(sems, k, 1 + j, src, mine, (*chip, me[2])).start()

    def forward(self, ins, outs, sems):
        me, sibling, chips = self._places()
        for k in range(len(self.inputs)):
            for j, chip in enumerate(chips):
                got = self._place(outs, k, self._index((*chip, me[2])))
                self._copy(sems, k, 1 + j, got, got, (*chip, me[2])).wait_recv()
                self._copy(sems, k, 4 + j, got, got, sibling).start()

    def wait(self, ins, outs, sems):
        me, sibling, chips = self._places()
        for k in range(len(self.inputs)):
            src, mine = self._src(ins, k), self._place(outs, k, self._index(me))
            self._copy(sems, k, 0, src, self._place(outs, k, self._index(sibling)), sibling).wait_recv()
            for j, chip in enumerate(chips):
                got = self._place(outs, k, self._index((*chip, sibling[2])))
                self._copy(sems, k, 4 + j, got, got, sibling).wait_recv()
            for j in range(N_DEV - 1):
                self._copy(sems, k, j, src, mine, sibling).wait_send()
            pltpu.make_async_copy(src, mine, sems[2].at[k]).wait()


class _Scatter(_Exchange):
    def __init__(self, items):
        self.items = list(items)
        out_shape = []
        for partial, axis in self.items:
            shape = list(partial.shape)
            shape[axis] //= N_DEV
            out_shape.append(_sds([N_DEV] + shape, partial.dtype))
        super().__init__([it[0] for it in self.items], out_shape)

    def _take(self, ins, k, idx):
        axis = self.items[k][1]
        return _slab(ins[k], axis, idx, self.items[k][0].shape[axis] // N_DEV)

    def _local(self, ins, outs, k, me):
        return self._take(ins, k, me), outs[k].at[me]

    def _remote(self, ins, outs, k, me, sender, receiver):
        return self._take(ins, k, receiver), outs[k].at[sender]


def _run_exchange(exchange, name):
    n = len(exchange.inputs)

    def body(*refs):
        ins, outs, sems = refs[:n], refs[n:2 * n], refs[2 * n:]
        exchange.start(ins, outs, sems)
        exchange.forward(ins, outs, sems)
        exchange.wait(ins, outs, sems)

    any_spec = pl.BlockSpec(memory_space=pl.ANY)
    exchange.results = pl.pallas_call(
        body, name=name, in_specs=[any_spec] * n, out_specs=[any_spec] * n, out_shape=exchange.out_shape,
        scratch_shapes=exchange.sems)(*exchange.inputs)
    return exchange.results


class _Layer:
    def __init__(self, stack, index):
        self.stack, self.index = stack, index
        self.shape = (1,) + stack.shape[1:] if stack.ndim == 2 else stack.shape[1:]

    def view(self, ref):
        return ref.at[pl.ds(self.index, 1)] if self.stack.ndim == 2 else ref.at[self.index]


class _LayerTokens:
    def __init__(self, stack, index):
        self.stack, self.index = stack, index

    def rows(self, tm):
        return pl.BlockSpec((None, None, tm, self.stack.shape[-1]), lambda i: (self.index, 0, i, 0))


def _call(body, *, name, grid, in_specs, out_specs, out_shape, args, scratch=(), carry=None):
    in_specs, out_specs, out_shape, scratch = list(in_specs), list(out_specs), list(out_shape), list(scratch)
    args = [a.stack if isinstance(a, _LayerTokens) else a for a in args]
    layers = {k: a for k, a in enumerate(args) if isinstance(a, _Layer)}
    for k, a in layers.items():
        in_specs[k], args[k] = _whole(a.stack.shape), a.stack
    params = pltpu.CompilerParams(dimension_semantics=("arbitrary",) * len(grid),
                                  vmem_limit_bytes=VMEM_LIMIT_BYTES)
    n_in, n_out, n_scr = len(in_specs), len(out_specs), len(scratch)
    n_x = 0 if carry is None else len(carry.inputs)
    steps = 1
    for extent in grid:
        steps *= extent
    assert carry is None or steps >= 3, "a carrier needs a step each for start, second stage and wait"

    def whole_body(*refs):
        core_in, x_in = list(refs[:n_in]), refs[n_in:n_in + n_x]
        refs = refs[n_in + n_x:]
        core_out, x_out = refs[:n_out], refs[n_out:n_out + n_x]
        refs = refs[n_out + n_x:]
        core_scr, sems = refs[:n_scr], refs[n_scr:]
        for k, a in layers.items():
            core_in[k] = a.view(core_in[k])
        if carry is None:
            body(*core_in, *core_out, *core_scr)
            return
        step = pl.program_id(0)
        for axis in range(1, len(grid)):
            step = step * grid[axis] + pl.program_id(axis)

        @pl.when(step == 0)
        def _():
            carry.start(x_in, x_out, sems)

        @pl.when(step == steps - 2)
        def _():
            carry.forward(x_in, x_out, sems)

        body(*core_in, *core_out, *core_scr)

        @pl.when(step == steps - 1)
        def _():
            carry.wait(x_in, x_out, sems)

    any_spec = pl.BlockSpec(memory_space=pl.ANY)
    extra_in = [] if carry is None else carry.inputs
    extra_shape = [] if carry is None else carry.out_shape
    extra_sems = [] if carry is None else carry.sems
    outs = pl.pallas_call(
        whole_body, name=name, grid=grid, in_specs=in_specs + [any_spec] * n_x,
        out_specs=out_specs + [any_spec] * n_x, out_shape=out_shape + extra_shape,
        scratch_shapes=scratch + extra_sems, compiler_params=params)(*args, *extra_in)
    if carry is not None:
        carry.results = outs[n_out:]
    return outs[:n_out]


def _all_reduce(parts, plan, name):
    n = len(parts)
    out_shape = []
    for group in plan:
        shape = parts[group[0]].shape
        if len(group) > 1:
            shape = (len(group),) + (shape[1:] if shape[0] == 1 else shape)
        out_shape.append(_sds(shape, F32))

    def body(*refs):
        ins, outs, lands = refs[:n], refs[n:n + len(plan)], refs[n + len(plan):2 * n + len(plan)]
        send_sems, recv_sems = refs[2 * n + len(plan):]
        me = _my_index()
        for k in range(n):
            lands[k][me] = ins[k][...]
            for dist in range(1, N_DEV):
                pltpu.make_async_remote_copy(
                    src_ref=ins[k], dst_ref=lands[k].at[me],
                    send_sem=send_sems.at[k, dist - 1], recv_sem=recv_sems.at[k, dist - 1],
                    device_id=_mesh_id((me + dist) % N_DEV), device_id_type=MESH).start()
        for k in range(n):
            for dist in range(1, N_DEV):
                sender = (me + N_DEV - dist) % N_DEV
                cp = pltpu.make_async_remote_copy(
                    src_ref=ins[k], dst_ref=lands[k].at[sender],
                    send_sem=send_sems.at[k, dist - 1], recv_sem=recv_sems.at[k, dist - 1],
                    device_id=_mesh_id(sender), device_id_type=MESH)
                cp.wait_send()
                cp.wait_recv()
        for o_ref, group in zip(outs, plan):
            for j, k in enumerate(group):
                total = lands[k][0]
                for dev in range(1, N_DEV):
                    total = total + lands[k][dev]
                if len(group) == 1:
                    o_ref[...] = total
                elif parts[k].shape[0] == 1:
                    o_ref[j:j + 1, :] = total
                else:
                    o_ref[j] = total

    vmem = pl.BlockSpec(memory_space=pltpu.VMEM)
    return pl.pallas_call(
        body, name=name, in_specs=[vmem] * n, out_specs=[vmem] * len(plan), out_shape=out_shape,
        scratch_shapes=[pltpu.VMEM((N_DEV,) + part.shape, F32) for part in parts]
        + [pltpu.SemaphoreType.DMA((n, N_DEV - 1)), pltpu.SemaphoreType.DMA((n, N_DEV - 1))])(*parts)


def _norm_matmul(x, g, wt, name, carry=None):
    t, d = x.shape
    n = wt.shape[0]
    tm, nc = _token_tile(t), _chunk(n)

    def body(x_ref, g_ref, wt_ref, h_ref, o_ref):
        xh, _ = _rms_stats(x_ref[...])
        h = (xh * g_ref[...]).astype(BF16)
        h_ref[...] = h
        for n0 in range(0, n, nc):
            o_ref[:, n0:n0 + nc] = _dot_nt(h, wt_ref[n0:n0 + nc, :]).astype(BF16)

    return _call(body, name=name, grid=(t // tm,),
                 in_specs=[_rows(tm, d), _whole(g.shape), _resident(wt.shape)],
                 out_specs=[_rows(tm, d), _rows(tm, n)],
                 out_shape=[_sds((t, d), BF16), _sds((t, n), BF16)], args=(x, g, wt), carry=carry)


def _fwd_branch(z, caw, cab, lng, lnb, cbw, dc, name, carry=None):
    t = z.shape[0]
    tm = _token_tile(t)
    ka, kb = caw.shape[0], cbw.shape[0]
    taps_a, taps_b = _fwd_taps(ka), _fwd_taps(kb)
    res_a, res_b = _residues(taps_a), _residues(taps_b)
    span = HALO + tm - SUBLANES

    def body(z_ref, caw_ref, cab_ref, lng_ref, lnb_ref, cbw_ref, ac_ref, act_ref, s_ref,
             rot_a, rot_b, wb_a, wb_b, cb):
        @pl.when(pl.program_id(0) == 0)
        def _():
            rot_a[0, 0:HALO, :] = jnp.zeros((HALO, dc), F32)
            rot_b[0, 0:HALO, :] = jnp.zeros((HALO, dc), F32)
            _broadcast_rows(wb_a, caw_ref, ka)
            wb_a[ka] = jnp.broadcast_to(cab_ref[...], (SUBLANES, dc))
            _broadcast_rows(wb_b, cbw_ref, kb)

        a_val = z_ref[:, 0:dc].astype(F32)
        a_gt = z_ref[:, dc:2 * dc].astype(F32)
        rot_a[0, HALO:HALO + tm, :] = a_val * _sigmoid(a_gt)
        _fill_rotations(rot_a, res_a, span)
        _tap_conv(rot_a, taps_a, wb_a, ac_ref, tm, bias_plane=ka)
        ac = ac_ref[...]
        mu = jnp.mean(ac, axis=-1, keepdims=True)
        xc = ac - mu
        var = jnp.mean(xc * xc, axis=-1, keepdims=True)
        ln = xc * lax.rsqrt(var + NORM_EPS) * lng_ref[...] + lnb_ref[...]
        act_ref[...] = (ln * _sigmoid(ln)).astype(BF16)
        rot_a[0, 0:HALO, :] = rot_a[0, tm:tm + HALO, :]

        sc_c = z_ref[:, 3 * dc:4 * dc].astype(F32)
        sc_v = z_ref[:, 4 * dc:5 * dc].astype(F32)
        rot_b[0, HALO:HALO + tm, :] = sc_c * sc_v
        _fill_rotations(rot_b, res_b, span)
        _tap_conv(rot_b, taps_b, wb_b, cb, tm)
        s_ref[...] = (z_ref[:, 2 * dc:3 * dc].astype(F32) * cb[...]).astype(BF16)
        rot_b[0, 0:HALO, :] = rot_b[0, tm:tm + HALO, :]

    return _call(body, name=name, grid=(t // tm,),
                 in_specs=[_rows(tm, 5 * dc), _whole(caw.shape), _whole(cab.shape), _whole(lng.shape),
                           _whole(lnb.shape), _whole(cbw.shape)],
                 out_specs=[_rows(tm, dc), _rows(tm, dc), _rows(tm, dc)],
                 out_shape=[_sds((t, dc), F32), _sds((t, dc), BF16), _sds((t, dc), BF16)],
                 scratch=[pltpu.VMEM((len(res_a), HALO + tm, dc), F32), pltpu.VMEM((len(res_b), HALO + tm, dc), F32),
                          pltpu.VMEM((ka + 1, SUBLANES, dc), F32), pltpu.VMEM((kb, SUBLANES, dc), F32),
                          pltpu.VMEM((tm, dc), F32)],
                 args=(z, caw, cab, lng, lnb, cbw), carry=carry)


def _fwd_merge(x, z, bg, a_act, s, wa, wb, wo, name, carry=None):
    t, d = x.shape
    n = z.shape[1]
    dc = a_act.shape[1]
    tm = _token_tile(t)
    o5 = n - 2 * d

    def body(x_ref, z_ref, bg_ref, act_ref, s_ref, wa_ref, wb_ref, wo_ref, o_ref):
        ya = _dot(act_ref[...], wa_ref[...])
        yb = _dot(s_ref[...], wb_ref[...])
        ga = _sigmoid(z_ref[:, o5:o5 + d].astype(F32) + bg_ref[:, 0:d])
        gb = _sigmoid(z_ref[:, o5 + d:n].astype(F32) + bg_ref[:, d:2 * d])
        m = (ga * ya + gb * yb).astype(BF16)
        o_ref[...] = x_ref[...] + _dot(m, wo_ref[...])

    return _call(body, name=name, grid=(t // tm,),
                 in_specs=[_rows(tm, d), _rows(tm, n), _whole(bg.shape), _rows(tm, dc), _rows(tm, dc),
                           _resident(wa.shape), _resident(wb.shape), _resident(wo.shape)],
                 out_specs=[_rows(tm, d)], out_shape=[_sds((t, d), F32)],
                 args=(x, z, bg, a_act, s, wa, wb, wo), carry=carry)[0]


def _fwd_down(x, u, cfw, cfb, wd, name, carry=None):
    t, d = x.shape
    f = u.shape[1] // 2
    tm = _token_tile(t, 256)
    kf = cfw.shape[0]

    taps = _fwd_taps(kf)
    residues = _residues(taps)

    def body(x_ref, u_ref, cfw_ref, cfb_ref, wd_ref, o_ref, act_ref, gl_ref, dgl_ref, rot_u, wb, fg):
        @pl.when(pl.program_id(0) == 0)
        def _():
            rot_u[0, 0:HALO, :] = jnp.zeros((HALO, f), F32)
            _broadcast_rows(wb, cfw_ref, kf)
            wb[kf] = jnp.broadcast_to(cfb_ref[...], (SUBLANES, f))

        rot_u[0, HALO:HALO + tm, :] = u_ref[:, 0:f].astype(F32)
        _fill_rotations(rot_u, residues, HALO + tm - SUBLANES)
        _tap_conv(rot_u, taps, wb, fg, tm, bias_plane=kf)
        gl, dgl = _gelu_tanh(fg[...])
        gl_ref[...] = gl.astype(BF16)
        dgl_ref[...] = dgl.astype(BF16)
        act = (gl * u_ref[:, f:2 * f].astype(F32)).astype(BF16)
        act_ref[...] = act
        o_ref[...] = x_ref[...] + _dot(act, wd_ref[...])
        rot_u[0, 0:HALO, :] = rot_u[0, tm:tm + HALO, :]

    return _call(body, name=name, grid=(t // tm,),
                 in_specs=[_rows(tm, d), _rows(tm, 2 * f), _whole(cfw.shape), _whole(cfb.shape),
                           _resident(wd.shape)],
                 out_specs=[_rows(tm, d), _rows(tm, f), _rows(tm, f), _rows(tm, f)],
                 out_shape=[_sds((t, d), F32), _sds((t, f), BF16), _sds((t, f), BF16), _sds((t, f), BF16)],
                 scratch=[pltpu.VMEM((len(residues), HALO + tm, f), F32), pltpu.VMEM((kf + 1, SUBLANES, f), F32),
                          pltpu.VMEM((tm, f), F32)],
                 args=(x, u, cfw, cfb, wd), carry=carry)


def _fwd_ple(x, g, wpg, p, wple, name, carry=None):
    t, d = x.shape
    pd = p.stack.shape[-1]
    tm = _token_tile(t)

    def body(x_ref, g_ref, wpg_ref, p_ref, wple_ref, o_ref):
        xv = x_ref[...]
        xh, _ = _rms_stats(xv)
        lg = _dot((xh * g_ref[...]).astype(BF16), wpg_ref[...])
        pp = _dot(p_ref[...].astype(BF16), wple_ref[...])
        o_ref[...] = xv + _sigmoid(lg) * pp

    return _call(body, name=name, grid=(t // tm,),
                 in_specs=[_rows(tm, d), _whole(g.shape), _resident(wpg.shape), p.rows(tm),
                           _resident(wple.shape)],
                 out_specs=[_rows(tm, d)], out_shape=[_sds((t, d), F32)],
                 args=(x, g, wpg, p, wple), carry=carry)[0]


def _loss_bwd(x, g, target, name):
    t, d = x.shape
    tm = _token_tile(t)

    def body(x_ref, g_ref, t_ref, dx_ref, dg_ref, loss_ref):
        @pl.when(pl.program_id(0) == 0)
        def _():
            dg_ref[...] = jnp.zeros_like(dg_ref)
            loss_ref[...] = jnp.zeros_like(loss_ref)

        xh, r = _rms_stats(x_ref[...])
        err = xh * g_ref[...] - t_ref[...]
        sq = jnp.sum(jnp.sum(err * err, axis=0, keepdims=True), axis=1, keepdims=True)
        loss_ref[...] += jnp.broadcast_to(0.5 * sq / d, loss_ref.shape)
        dx, dg = _rms_bwd(err / d, xh, r, g_ref[...])
        dx_ref[...] = dx
        dg_ref[...] += dg

    return _call(body, name=name, grid=(t // tm,),
                 in_specs=[_rows(tm, d), _whole(g.shape), _rows(tm, d)],
                 out_specs=[_rows(tm, d), _whole((1, d)), _whole((SUBLANES, LANES))],
                 out_shape=[_sds((t, d), F32), _sds((1, d), F32), _sds((SUBLANES, LANES), F32)],
                 args=(x, g, target))


def _bwd_ple(dy, x, g, wpg, p, wple, name, carry=None):
    t, d = x.shape
    pd = p.stack.shape[-1]
    tm = _token_tile(t)
    nt = t // tm

    def body(dy_ref, x_ref, g_ref, wpg_ref, p_ref, wple_ref, dx_ref, dwpg_ref, dwple_ref, dg_ref,
             acc_pg, acc_ple):
        i = pl.program_id(0)

        @pl.when(i == 0)
        def _():
            acc_pg[...] = jnp.zeros_like(acc_pg)
            acc_ple[...] = jnp.zeros_like(acc_ple)
            dg_ref[...] = jnp.zeros_like(dg_ref)

        dyv = dy_ref[...]
        xh, r = _rms_stats(x_ref[...])
        h = (xh * g_ref[...]).astype(BF16)
        pb = p_ref[...].astype(BF16)
        pg = _sigmoid(_dot(h, wpg_ref[...]))
        pp = _dot(pb, wple_ref[...])
        dpp = (dyv * pg).astype(BF16)
        dlg = (dyv * pp * pg * (1.0 - pg)).astype(BF16)
        acc_ple[...] += _dot_tn(pb, dpp)
        acc_pg[...] += _dot_tn(h, dlg)
        dx, dg = _rms_bwd(_dot_nt(dlg, wpg_ref[...]), xh, r, g_ref[...])
        dx_ref[...] = dyv + dx
        dg_ref[...] += dg

        @pl.when(i == nt - 1)
        def _():
            dwpg_ref[...] = acc_pg[...].astype(BF16)
            dwple_ref[...] = acc_ple[...].astype(BF16)

    return _call(body, name=name, grid=(nt,),
                 in_specs=[_rows(tm, d), _rows(tm, d), _whole(g.shape), _resident(wpg.shape), p.rows(tm),
                           _resident(wple.shape)],
                 out_specs=[_rows(tm, d), _whole((d, d)), _whole((pd, d)), _whole((1, d))],
                 out_shape=[_sds((t, d), F32), _sds((d, d), BF16), _sds((pd, d), BF16), _sds((1, d), F32)],
                 scratch=[pltpu.VMEM((d, d), F32), pltpu.VMEM((pd, d), F32)],
                 args=(dy, x, g, wpg, p, wple), carry=carry)


def _bwd_down(dy, u, gl, dgl, cfw, wd, name, carry=None):
    t, d = dy.shape
    f = u.shape[1] // 2
    tm = _token_tile(t, 256)
    nt = t // tm
    kf = cfw.shape[0]
    per = tm // HALO
    fwd, bwd = _fwd_taps(kf), _bwd_taps(kf)
    span = HALO + tm - SUBLANES

    def body(dy_ref, u_ref, up_ref, gl_ref, dgl_ref, cfw_ref, wd_ref, du_ref, dcw_ref, dcb_ref,
             rot_u, rot_g, wb, acc, conv_out):
        i = pl.program_id(0)

        @pl.when(i == 0)
        def _():
            for ref in (dcw_ref, dcb_ref, acc):
                ref[...] = jnp.zeros_like(ref)
            rot_g[0, tm:tm + HALO, :] = jnp.zeros((HALO, f), F32)
            _broadcast_rows(wb, cfw_ref, kf)

        first = (i == nt - 1).astype(F32)
        rot_u[0, 0:HALO, :] = up_ref[:, 0:f].astype(F32) * (1.0 - first)
        rot_u[0, HALO:HALO + tm, :] = u_ref[:, 0:f].astype(F32)
        _fill_rotations(rot_u, _residues(fwd), span)
        df = _dot_nt(dy_ref[...].astype(BF16), wd_ref[...])
        du_ref[:, f:2 * f] = (df * gl_ref[...].astype(F32)).astype(BF16)
        dfg = df * u_ref[:, f:2 * f].astype(F32) * dgl_ref[...].astype(F32)
        dcb_ref[...] += jnp.sum(dfg, axis=0, keepdims=True)
        rot_g[0, 0:tm, :] = dfg
        _tap_wgrad(rot_u, fwd, rot_g, 0, acc, tm)
        _fill_rotations(rot_g, _residues(bwd), span)
        _tap_conv(rot_g, bwd, wb, conv_out, tm)
        du_ref[:, 0:f] = conv_out[...].astype(BF16)
        rot_g[0, tm:tm + HALO, :] = rot_g[0, 0:HALO, :]

        @pl.when(i == nt - 1)
        def _():
            dcw_ref[0:kf, :] = jnp.sum(acc[...], axis=1)

    def planes(taps):
        return pltpu.VMEM((len(_residues(taps)), HALO + tm, f), F32)

    prev_rows = pl.BlockSpec((HALO, 2 * f), lambda i: (jnp.maximum((nt - 1 - i) * per - 1, 0), 0))
    return _call(body, name=name, grid=(nt,),
                 in_specs=[_rows_rev(tm, d, nt), _rows_rev(tm, 2 * f, nt), prev_rows, _rows_rev(tm, f, nt),
                           _rows_rev(tm, f, nt), _whole(cfw.shape), _resident(wd.shape)],
                 out_specs=[_rows_rev(tm, 2 * f, nt), _whole((SUBLANES, f)), _whole((1, f))],
                 out_shape=[_sds((t, 2 * f), BF16), _sds((SUBLANES, f), F32), _sds((1, f), F32)],
                 scratch=[planes(fwd), planes(bwd), pltpu.VMEM((kf, SUBLANES, f), F32),
                          pltpu.VMEM((kf, SUBLANES, f), F32), pltpu.VMEM((tm, f), F32)],
                 args=(dy, u, u, gl, dgl, cfw, wd), carry=carry)


def _bwd_norm_matmul(dout, wt, x, g, dres, name, carry=None):
    t, d = x.shape
    n = dout.shape[1]
    tm = _token_tile(t)

    def body(do_ref, wt_ref, x_ref, g_ref, dres_ref, dx_ref, dg_ref):
        @pl.when(pl.program_id(0) == 0)
        def _():
            dg_ref[...] = jnp.zeros_like(dg_ref)

        dh = _dot(do_ref[...], wt_ref[...])
        xh, r = _rms_stats(x_ref[...])
        dx, dg = _rms_bwd(dh, xh, r, g_ref[...])
        dx_ref[...] = dres_ref[...] + dx
        dg_ref[...] += dg

    return _call(body, name=name, grid=(t // tm,),
                 in_specs=[_rows(tm, n), _resident(wt.shape), _rows(tm, d), _whole(g.shape), _rows(tm, d)],
                 out_specs=[_rows(tm, d), _whole((1, d))],
                 out_shape=[_sds((t, d), F32), _sds((1, d), F32)],
                 args=(dout, wt, x, g, dres), carry=carry)


def _wgrad_tn(a, b, name, carry=None):
    t, n = a.shape
    d = b.shape[1]
    tt = 1024 if t % 1024 == 0 else _token_tile(t)
    tn = _chunk(n, 1536)
    nt = t // tt

    def body(a_ref, b_ref, o_ref, acc):
        k = pl.program_id(1)

        @pl.when(k == 0)
        def _():
            acc[...] = jnp.zeros_like(acc)

        acc[...] += _dot_tn(a_ref[...].astype(BF16), b_ref[...].astype(BF16))

        @pl.when(k == nt - 1)
        def _():
            o_ref[...] = acc[...].astype(BF16)

    return _call(body, name=name, grid=(n // tn, nt),
                 in_specs=[pl.BlockSpec((tt, tn), lambda j, k: (k, j)), pl.BlockSpec((tt, d), lambda j, k: (k, 0))],
                 out_specs=[pl.BlockSpec((tn, d), lambda j, k: (j, 0))], out_shape=[_sds((n, d), BF16)],
                 scratch=[pltpu.VMEM((tn, d), F32)], args=(a, b), carry=carry)[0]


def _bwd_merge(dy, z, bg, a_act, s, wa, wb, wo, name, carry=None):
    t, d = dy.shape
    n = z.shape[1]
    dc = a_act.shape[1]
    tm = _token_tile(t, 256)
    nt = t // tm
    o5 = n - 2 * d

    def body(dy_ref, z_ref, bg_ref, act_ref, s_ref, wa_ref, wb_ref, wo_ref,
             dact_ref, ds_ref, dgl_ref, dwo_ref, dwa_ref, dwb_ref, dbg_ref, acc_o, acc_a, acc_b):
        i = pl.program_id(0)

        @pl.when(i == 0)
        def _():
            acc_o[...] = jnp.zeros_like(acc_o)
            acc_a[...] = jnp.zeros_like(acc_a)
            acc_b[...] = jnp.zeros_like(acc_b)
            dbg_ref[...] = jnp.zeros_like(dbg_ref)

        dyb = dy_ref[...].astype(BF16)
        dm = _dot_nt(dyb, wo_ref[...])
        ya = _dot(act_ref[...], wa_ref[...])
        yb = _dot(s_ref[...], wb_ref[...])
        ga = _sigmoid(z_ref[:, o5:o5 + d].astype(F32) + bg_ref[:, 0:d])
        gb = _sigmoid(z_ref[:, o5 + d:n].astype(F32) + bg_ref[:, d:2 * d])
        acc_o[...] += _dot_tn((ga * ya + gb * yb).astype(BF16), dyb)
        dya = (dm * ga).astype(BF16)
        dyb2 = (dm * gb).astype(BF16)
        acc_a[...] += _dot_tn(act_ref[...], dya)
        acc_b[...] += _dot_tn(s_ref[...], dyb2)
        dact_ref[...] = _dot_nt(dya, wa_ref[...])
        ds_ref[...] = _dot_nt(dyb2, wb_ref[...])
        dla = dm * ya * ga * (1.0 - ga)
        dlb = dm * yb * gb * (1.0 - gb)
        dgl_ref[:, 0:d] = dla.astype(BF16)
        dgl_ref[:, d:2 * d] = dlb.astype(BF16)
        dbg_ref[:, 0:d] += jnp.sum(dla, axis=0, keepdims=True)
        dbg_ref[:, d:2 * d] += jnp.sum(dlb, axis=0, keepdims=True)

        @pl.when(i == nt - 1)
        def _():
            dwo_ref[...] = acc_o[...].astype(BF16)
            dwa_ref[...] = acc_a[...].astype(BF16)
            dwb_ref[...] = acc_b[...].astype(BF16)

    return _call(body, name=name, grid=(nt,),
                 in_specs=[_rows(tm, d), _rows(tm, n), _whole(bg.shape), _rows(tm, dc), _rows(tm, dc),
                           _resident(wa.shape), _resident(wb.shape), _resident(wo.shape)],
                 out_specs=[_rows(tm, dc), _rows(tm, dc), _rows(tm, 2 * d), _whole((d, d)), _whole((dc, d)),
                            _whole((dc, d)), _whole((1, 2 * d))],
                 out_shape=[_sds((t, dc), F32), _sds((t, dc), F32), _sds((t, 2 * d), BF16), _sds((d, d), BF16),
                            _sds((dc, d), BF16), _sds((dc, d), BF16), _sds((1, 2 * d), F32)],
                 scratch=[pltpu.VMEM((d, d), F32), pltpu.VMEM((dc, d), F32), pltpu.VMEM((dc, d), F32)],
                 args=(dy, z, bg, a_act, s, wa, wb, wo), carry=carry)


def _bwd_branch(dact, ds, z, dgl, a_conv, caw, lng, lnb, cbw, name, carry=None):
    t, n = z.shape
    dc = a_conv.shape[1]
    tm = _token_tile(t, 256)
    nt = t // tm
    ka, kb = caw.shape[0], cbw.shape[0]
    per = tm // HALO
    fwd_a, fwd_b, bwd_a, bwd_b = _fwd_taps(ka), _fwd_taps(kb), _bwd_taps(ka), _bwd_taps(kb)
    span = HALO + tm - SUBLANES

    def body(dact_ref, ds_ref, z_ref, zp_ref, dgl_ref, ac_ref, caw_ref, lng_ref, lnb_ref, cbw_ref,
             dz_ref, dcaw_ref, dcab_ref, dlng_ref, dlnb_ref, dcbw_ref,
             rot_a, rot_da, rot_c, rot_dc, wb_a, wb_b, acc_a, acc_b, conv_out):
        i = pl.program_id(0)

        @pl.when(i == 0)
        def _():
            for ref in (dcaw_ref, dcab_ref, dlng_ref, dlnb_ref, dcbw_ref, acc_a, acc_b):
                ref[...] = jnp.zeros_like(ref)
            rot_da[0, tm:tm + HALO, :] = jnp.zeros((HALO, dc), F32)
            rot_dc[0, tm:tm + HALO, :] = jnp.zeros((HALO, dc), F32)
            _broadcast_rows(wb_a, caw_ref, ka)
            _broadcast_rows(wb_b, cbw_ref, kb)

        keep = 1.0 - (i == nt - 1).astype(F32)
        a_val = z_ref[:, 0:dc].astype(F32)
        sg = _sigmoid(z_ref[:, dc:2 * dc].astype(F32))
        rot_a[0, 0:HALO, :] = zp_ref[:, 0:dc].astype(F32) * _sigmoid(zp_ref[:, dc:2 * dc].astype(F32)) * keep
        rot_a[0, HALO:HALO + tm, :] = a_val * sg
        _fill_rotations(rot_a, _residues(fwd_a), span)

        ac = ac_ref[...]
        mu = jnp.mean(ac, axis=-1, keepdims=True)
        xc = ac - mu
        rstd = lax.rsqrt(jnp.mean(xc * xc, axis=-1, keepdims=True) + NORM_EPS)
        xh = xc * rstd
        ln = xh * lng_ref[...] + lnb_ref[...]
        sl = _sigmoid(ln)
        dln = dact_ref[...] * (sl * (1.0 + ln * (1.0 - sl)))
        dlng_ref[...] += jnp.sum(dln * xh, axis=0, keepdims=True)
        dlnb_ref[...] += jnp.sum(dln, axis=0, keepdims=True)
        dxh = dln * lng_ref[...]
        dac = rstd * (dxh - jnp.mean(dxh, axis=-1, keepdims=True)
                      - xh * jnp.mean(dxh * xh, axis=-1, keepdims=True))
        dcab_ref[...] += jnp.sum(dac, axis=0, keepdims=True)
        rot_da[0, 0:tm, :] = dac
        _tap_wgrad(rot_a, fwd_a, rot_da, 0, acc_a, tm)
        _fill_rotations(rot_da, _residues(bwd_a), span)
        _tap_conv(rot_da, bwd_a, wb_a, conv_out, tm)
        rot_da[0, tm:tm + HALO, :] = rot_da[0, 0:HALO, :]
        da = conv_out[...]
        dz_ref[:, 0:dc] = (da * sg).astype(BF16)
        dz_ref[:, dc:2 * dc] = (da * a_val * sg * (1.0 - sg)).astype(BF16)

        sc_b = z_ref[:, 2 * dc:3 * dc].astype(F32)
        sc_c = z_ref[:, 3 * dc:4 * dc].astype(F32)
        sc_v = z_ref[:, 4 * dc:5 * dc].astype(F32)
        rot_c[0, 0:HALO, :] = zp_ref[:, 3 * dc:4 * dc].astype(F32) * zp_ref[:, 4 * dc:5 * dc].astype(F32) * keep
        rot_c[0, HALO:HALO + tm, :] = sc_c * sc_v
        _fill_rotations(rot_c, _residues(fwd_b), span)
        _tap_conv(rot_c, fwd_b, wb_b, conv_out, tm)
        dsv = ds_ref[...]
        dz_ref[:, 2 * dc:3 * dc] = (dsv * conv_out[...]).astype(BF16)
        rot_dc[0, 0:tm, :] = dsv * sc_b
        _tap_wgrad(rot_c, fwd_b, rot_dc, 0, acc_b, tm)
        _fill_rotations(rot_dc, _residues(bwd_b), span)
        _tap_conv(rot_dc, bwd_b, wb_b, conv_out, tm)
        rot_dc[0, tm:tm + HALO, :] = rot_dc[0, 0:HALO, :]
        dcv = conv_out[...]
        dz_ref[:, 3 * dc:4 * dc] = (dcv * sc_v).astype(BF16)
        dz_ref[:, 4 * dc:5 * dc] = (dcv * sc_c).astype(BF16)
        dz_ref[:, 5 * dc:n] = dgl_ref[...]

        @pl.when(i == nt - 1)
        def _():
            dcaw_ref[0:ka, :] = jnp.sum(acc_a[...], axis=1)
            dcbw_ref[0:kb, :] = jnp.sum(acc_b[...], axis=1)

    def planes(taps):
        return pltpu.VMEM((len(_residues(taps)), HALO + tm, dc), F32)

    prev_rows = pl.BlockSpec((HALO, 5 * dc), lambda i: (jnp.maximum((nt - 1 - i) * per - 1, 0), 0))
    return _call(body, name=name, grid=(nt,),
                 in_specs=[_rows_rev(tm, dc, nt), _rows_rev(tm, dc, nt), _rows_rev(tm, 5 * dc, nt), prev_rows,
                           _rows_rev(tm, n - 5 * dc, nt), _rows_rev(tm, dc, nt), _whole(caw.shape),
                           _whole(lng.shape), _whole(lnb.shape), _whole(cbw.shape)],
                 out_specs=[_rows_rev(tm, n, nt), _whole((HALO, dc)), _whole((1, dc)), _whole((1, dc)),
                            _whole((1, dc)), _whole((SUBLANES, dc))],
                 out_shape=[_sds((t, n), BF16), _sds((HALO, dc), F32), _sds((1, dc), F32), _sds((1, dc), F32),
                            _sds((1, dc), F32), _sds((SUBLANES, dc), F32)],
                 scratch=[planes(fwd_a), planes(bwd_a), planes(fwd_b), planes(bwd_b),
                          pltpu.VMEM((ka, SUBLANES, dc), F32), pltpu.VMEM((kb, SUBLANES, dc), F32),
                          pltpu.VMEM((ka, SUBLANES, dc), F32), pltpu.VMEM((kb, SUBLANES, dc), F32),
                          pltpu.VMEM((tm, dc), F32)],
                 args=(dact, ds, z, z, dgl, a_conv, caw, lng, lnb, cbw), carry=carry)


def _land_specs(depth, nr, tr, cols):
    def spec(k):
        return pl.BlockSpec((N_DEV, tr, cols), lambda i: (0, jnp.clip(i - k * nr, 0, nr - 1), 0))
    return [spec(k) for k in range(depth)]


def _adamw_math(w, g, m, v):
    nm = ADAM_B1 * m + (1.0 - ADAM_B1) * g
    nv = ADAM_B2 * v + (1.0 - ADAM_B2) * (g * g)
    m_hat = nm / (1.0 - ADAM_B1 ** ADAM_STEP)
    v_hat = nv / (1.0 - ADAM_B2 ** ADAM_STEP)
    return -ADAM_LR * (m_hat / (jnp.sqrt(v_hat) + ADAM_EPS) + ADAM_WD * w), nm, nv


def _sum_adamw(lands, w, m, v, name):
    _, rows, cols = lands[0].shape
    tr = _row_tile(rows)
    nr = rows // tr
    depth = len(lands)

    def body(*refs):
        w_ref, m_ref, v_ref, g_ref, d_ref, nm_ref, nv_ref = refs[depth:]
        i = pl.program_id(0)
        for k in range(depth):
            @pl.when(i // nr == k)
            def _(k=k):
                acc = refs[k][0].astype(F32)
                for j in range(1, N_DEV):
                    acc = acc + refs[k][j].astype(F32)
                g_ref[...] = acc
                d_ref[...], nm_ref[...], nv_ref[...] = _adamw_math(w_ref[...], acc, m_ref[...], v_ref[...])

    spec = _rows(tr, cols)
    return _call(body, name=name, grid=(depth * nr,), in_specs=_land_specs(depth, nr, tr, cols) + [spec] * 3,
                 out_specs=[spec] * 4, out_shape=[_sds((depth * rows, cols), F32)] * 4, args=(*lands, w, m, v))


def _adamw_small(ws, gs, ms, vs, name):
    n = len(ws)

    def body(*refs):
        w_refs, g_refs, m_refs, v_refs = refs[:n], refs[n:2 * n], refs[2 * n:3 * n], refs[3 * n:4 * n]
        d_refs, nm_refs, nv_refs = refs[4 * n:5 * n], refs[5 * n:6 * n], refs[6 * n:]
        for k in range(n):
            d_refs[k][...], nm_refs[k][...], nv_refs[k][...] = _adamw_math(
                w_refs[k][...], g_refs[k][...], m_refs[k][...], v_refs[k][...])

    vmem = pl.BlockSpec(memory_space=pltpu.VMEM)
    outs = pl.pallas_call(
        body, name=name, in_specs=[vmem] * (4 * n), out_specs=[vmem] * (3 * n),
        out_shape=[_sds(a.shape, F32) for a in ws] * 3)(*ws, *gs, *ms, *vs)
    return outs[:n], outs[n:2 * n], outs[2 * n:]


def kernel(x, p, g_mix, w_in, b_gate, conv_a_w, conv_a_b, ln_a_g, ln_a_b, w_a_out, conv_b_w, w_b_out, w_o, g_ffn, w_up, conv_f_w, conv_f_b, w_down, g_ple, w_ple, w_ple_gate, g_final, loss_target, m_g_mix, m_w_in, m_b_gate, m_conv_a_w, m_conv_a_b, m_ln_a_g, m_ln_a_b, m_w_a_out, m_conv_b_w, m_w_b_out, m_w_o, m_g_ffn, m_w_up, m_conv_f_w, m_conv_f_b, m_w_down, m_g_ple, m_w_ple, m_w_ple_gate, m_g_final, v_g_mix, v_w_in, v_b_gate, v_conv_a_w, v_conv_a_b, v_ln_a_g, v_ln_a_b, v_w_a_out, v_conv_b_w, v_w_b_out, v_w_o, v_g_ffn, v_w_up, v_conv_f_w, v_conv_f_b, v_w_down, v_g_ple, v_w_ple, v_w_ple_gate, v_g_final):
    w = dict(zip(WEIGHT_NAMES, (g_mix, w_in, b_gate, conv_a_w, conv_a_b, ln_a_g, ln_a_b, w_a_out, conv_b_w,
                                w_b_out, w_o, g_ffn, w_up, conv_f_w, conv_f_b, w_down, g_ple, w_ple,
                                w_ple_gate, g_final)))
    mom = dict(zip(WEIGHT_NAMES, (m_g_mix, m_w_in, m_b_gate, m_conv_a_w, m_conv_a_b, m_ln_a_g, m_ln_a_b,
                                  m_w_a_out, m_conv_b_w, m_w_b_out, m_w_o, m_g_ffn, m_w_up, m_conv_f_w,
                                  m_conv_f_b, m_w_down, m_g_ple, m_w_ple, m_w_ple_gate, m_g_final)))
    var = dict(zip(WEIGHT_NAMES, (v_g_mix, v_w_in, v_b_gate, v_conv_a_w, v_conv_a_b, v_ln_a_g, v_ln_a_b,
                                  v_w_a_out, v_conv_b_w, v_w_b_out, v_w_o, v_g_ffn, v_w_up, v_conv_f_w,
                                  v_conv_f_b, v_w_down, v_g_ple, v_w_ple, v_w_ple_gate, v_g_final)))
    depth = g_mix.shape[0]
    dc = ln_a_g.shape[1]
    me = _my_index()
    x0 = x[0]
    target = loss_target[0]
    big_names = tuple(BIG_AXIS)

    shard = {name: (jnp.swapaxes(w[name], 1, 2) if name in TRANSPOSED else w[name]).astype(BF16)
             for name in big_names}

    def gather_of(layer, *names):
        return _Gather([(shard[name], layer, BIG_AXIS[name]) for name in names])

    def row(name, layer):
        return _Layer(w[name], layer)

    first = _Gather([(shard['w_in'], 0, BIG_AXIS['w_in'])] + [(w[name][None], None, 0) for name in CONV_SHARDED])
    gathered = _run_exchange(first, "gather_first")
    w_in_full = gathered[0]
    conv_full = {name: jnp.transpose(g, (1, 2, 0, 3)).reshape(g.shape[1], g.shape[2], -1)
                 for name, g in zip(CONV_SHARDED, gathered[1:])}
    saved = []
    xc = x0
    for l in range(depth):
        carry = gather_of(l, 'w_a_out', 'w_b_out', 'w_o', 'w_up')
        h, z = _norm_matmul(xc, row('g_mix', l), w_in_full, f"fwd_in_{l}", carry)
        wa_full, wb_full, wo_full, w_up_full = carry.results
        carry = gather_of(l, 'w_down')
        a_conv, a_act, s = _fwd_branch(z, _Layer(conv_full['conv_a_w'], l), row('conv_a_b', l), row('ln_a_g', l),
                                       row('ln_a_b', l), _Layer(conv_full['conv_b_w'], l), dc, f"fwd_branch_{l}", carry)
        w_down_full, = carry.results
        carry = gather_of(l, 'w_ple', 'w_ple_gate')
        x1 = _fwd_merge(xc, z, row('b_gate', l), a_act, s, wa_full, wb_full, wo_full, f"fwd_merge_{l}", carry)
        w_ple_full, w_pg_full = carry.results
        carry = gather_of(l + 1, 'w_in') if l + 1 < depth else None
        h2, u = _norm_matmul(x1, row('g_ffn', l), w_up_full, f"fwd_up_{l}", carry)
        x2, act, gl, dgl = _fwd_down(x1, u, _Layer(conv_full['conv_f_w'], l), row('conv_f_b', l), w_down_full,
                                     f"fwd_down_{l}")
        x3 = _fwd_ple(x2, row('g_ple', l), w_pg_full, _LayerTokens(p, l), w_ple_full, f"fwd_ple_{l}")
        saved.append((xc, h, z, a_conv, a_act, s, x1, h2, u, act, gl, dgl, x2,
                      dict(w_in=w_in_full, w_a_out=wa_full, w_b_out=wb_full, w_o=wo_full, w_up=w_up_full,
                           w_down=w_down_full, w_ple=w_ple_full, w_ple_gate=w_pg_full)))
        if carry is not None:
            w_in_full, = carry.results
        xc = x3

    dx, dg_final, loss_part = _loss_bwd(xc, g_final[None], target, "loss_bwd")
    landed = {name: [None] * depth for name in big_names}
    small = {name: [None] * depth for name in WEIGHT_NAMES if name not in BIG_AXIS and name != 'g_final'}

    def scatter_of(*partials):
        ex = _Scatter([(part, BIG_AXIS[name]) for name, _, part in partials])
        ex.places = [(name, layer) for name, layer, _ in partials]
        return ex

    def keep(ex):
        for (name, layer), land in zip(ex.places, ex.results):
            landed[name][layer] = land

    pending = []
    for l in reversed(range(depth)):
        xin, h, z, a_conv, a_act, s, x1, h2, u, act, gl, dgl, x2, full = saved[l]
        dx2, d_wpg, d_wple, small['g_ple'][l] = _bwd_ple(
            dx, x2, row('g_ple', l), full['w_ple_gate'], _LayerTokens(p, l), full['w_ple'], f"bwd_ple_{l}")
        carry = scatter_of(('w_ple_gate', l, d_wpg), ('w_ple', l, d_wple))
        d_wdown = _wgrad_tn(act, dx2, f"wgrad_down_{l}", carry)
        keep(carry)
        carry = scatter_of(('w_down', l, d_wdown), *pending)
        pending = []
        du, small['conv_f_w'][l], small['conv_f_b'][l] = _bwd_down(
            dx2, u, gl, dgl, _Layer(conv_full['conv_f_w'], l), full['w_down'], f"bwd_down_{l}", carry)
        keep(carry)
        dx1, small['g_ffn'][l] = _bwd_norm_matmul(du, full['w_up'], x1, row('g_ffn', l), dx2, f"bwd_up_{l}")
        d_wup = _wgrad_tn(du, h2, f"wgrad_up_{l}")
        dact, ds, dgate, d_wo, d_wa, d_wb, small['b_gate'][l] = _bwd_merge(
            dx1, z, row('b_gate', l), a_act, s, full['w_a_out'], full['w_b_out'], full['w_o'], f"bwd_merge_{l}")
        carry = scatter_of(('w_up', l, d_wup))
        (dz, small['conv_a_w'][l], small['conv_a_b'][l], small['ln_a_g'][l], small['ln_a_b'][l],
         small['conv_b_w'][l]) = _bwd_branch(
            dact, ds, z, dgate, a_conv, _Layer(conv_full['conv_a_w'], l), row('ln_a_g', l), row('ln_a_b', l),
            _Layer(conv_full['conv_b_w'], l), f"bwd_branch_{l}", carry)
        keep(carry)
        carry = scatter_of(('w_o', l, d_wo), ('w_a_out', l, d_wa), ('w_b_out', l, d_wb))
        d_win = _wgrad_tn(dz, h, f"wgrad_in_{l}", carry)
        keep(carry)
        carry = scatter_of(('w_in', l, d_win)) if l == 0 else None
        if l > 0:
            pending = [('w_in', l, d_win)]
        dx, small['g_mix'][l] = _bwd_norm_matmul(dz, full['w_in'], xin, row('g_mix', l), dx1, f"bwd_in_{l}", carry)
        if carry is not None:
            keep(carry)
    grad_x = dx[None]

    small_names = tuple(small)
    parts = [part for name in small_names for part in small[name]] + [dg_final, loss_part]
    plan = [tuple(range(k * depth, (k + 1) * depth)) for k in range(len(small_names))]
    plan += [(len(parts) - 2,), (len(parts) - 1,)]
    reduced = _all_reduce(parts, plan, "all_reduce_small")
    loss = reduced[-1][0, 0]
    grads = dict(zip(small_names, reduced[:len(small_names)]))
    grads['g_final'] = reduced[len(small_names)].reshape(g_final.shape)
    for name in CONV_SHARDED:
        _, taps, width = w[name].shape
        grads[name] = lax.dynamic_slice(grads[name], (0, 0, me * width), (depth, taps, width))

    delta, new_m, new_v = {}, {}, {}
    for name in big_names:
        view = (lambda a: jnp.swapaxes(a, 1, 2)) if name in TRANSPOSED else (lambda a: a)
        shape = view(w[name]).shape
        flat = lambda a: view(a).reshape(-1, shape[-1])
        lands = [land.reshape(N_DEV, -1, shape[-1]) for land in landed[name]]
        outs = _sum_adamw(lands, flat(w[name]), flat(mom[name]), flat(var[name]), f"adamw_{name}")
        grads[name], delta[name], new_m[name], new_v[name] = [view(a.reshape(shape)) for a in outs]
    rest = tuple(name for name in WEIGHT_NAMES if name not in BIG_AXIS)
    as_2d = lambda a: a.reshape(1, -1) if a.ndim == 1 else a
    outs = _adamw_small(*[[as_2d(src[name]) for name in rest] for src in (w, grads, mom, var)], "adamw_small")
    for dst, values in zip((delta, new_m, new_v), outs):
        dst.update({name: value.reshape(w[name].shape) for name, value in zip(rest, values)})

    return (loss, grad_x, *[grads[n] for n in WEIGHT_NAMES], *[delta[n] for n in WEIGHT_NAMES],
            *[new_m[n] for n in WEIGHT_NAMES], *[new_v[n] for n in WEIGHT_NAMES])
```

```python
import jax
import jax.numpy as jnp
from jax import lax
from jax.experimental import pallas as pl
from jax.experimental.pallas import tpu as pltpu

F32 = jnp.float32
BF16 = jnp.bfloat16
MESH = pl.DeviceIdType.MESH

N_DEV = 8
NORM_EPS = 1e-6
HALO = 32
LANES = 128
SUBLANES = 8
VMEM_LIMIT_BYTES = 56 * 2**20

ADAM_LR = 0.001
ADAM_B1 = 0.9
ADAM_B2 = 0.999
ADAM_EPS = 1e-08
ADAM_WD = 0.01
ADAM_STEP = 10

WEIGHT_NAMES = ('g_mix', 'w_in', 'b_gate', 'conv_a_w', 'conv_a_b', 'ln_a_g', 'ln_a_b', 'w_a_out',
                'conv_b_w', 'w_b_out', 'w_o', 'g_ffn', 'w_up', 'conv_f_w', 'conv_f_b', 'w_down',
                'g_ple', 'w_ple', 'w_ple_gate', 'g_final')
BIG_AXIS = {'w_in': 0, 'w_up': 0, 'w_a_out': 1, 'w_b_out': 1, 'w_o': 0, 'w_down': 0, 'w_ple': 1,
            'w_ple_gate': 0}
TRANSPOSED = ('w_in', 'w_up')
CONV_SHARDED = ('conv_a_w', 'conv_b_w', 'conv_f_w')


def _dot(a, b):
    return jnp.dot(a, b, preferred_element_type=F32)


def _dot_nt(a, b):
    return lax.dot_general(a, b, (((1,), (1,)), ((), ())), preferred_element_type=F32)


def _dot_tn(a, b):
    return lax.dot_general(a, b, (((0,), (0,)), ((), ())), preferred_element_type=F32)


def _sigmoid(v):
    return jax.nn.sigmoid(v)


def _token_tile(t, cap=512):
    return cap if (t % cap == 0 and t > 512) else 128


def _chunk(n, limit=512):
    for c in range(limit - limit % LANES, 0, -LANES):
        if n % c == 0:
            return c
    return n


def _row_tile(rows):
    for c in (512, 256, 128, 64, 32, 16, 8):
        if rows % c == 0:
            return c
    return rows


def _rows(tm, width):
    return pl.BlockSpec((tm, width), lambda i: (i, 0))


def _rows_rev(tm, width, nt):
    return pl.BlockSpec((tm, width), lambda i: (nt - 1 - i, 0))


def _whole(shape):
    nd = len(shape)
    return pl.BlockSpec(tuple(shape), lambda i: (0,) * nd)


def _resident(shape):
    nd = len(shape)
    return pl.BlockSpec(tuple(shape), lambda i: (0,) * nd, pipeline_mode=pl.Buffered(1))


def _sds(shape, dtype):
    return jax.ShapeDtypeStruct(tuple(shape), dtype)


def _rms_stats(xv):
    r = lax.rsqrt(jnp.mean(xv * xv, axis=-1, keepdims=True) + NORM_EPS)
    return xv * r, r


def _rms_bwd(dy, xh, r, g):
    dxh = dy * g
    dx = r * (dxh - xh * jnp.mean(dxh * xh, axis=-1, keepdims=True))
    return dx, jnp.sum(dy * xh, axis=0, keepdims=True)


GELU_C0 = 0.7978845608028654
GELU_C1 = GELU_C0 * 0.044715


def _gelu_tanh(v):
    v2 = v * v
    t = jnp.tanh(v * (GELU_C0 + GELU_C1 * v2))
    q = 1.0 + t
    hv = 0.5 * v
    grad = 0.5 * q + hv * (1.0 - t * t) * (GELU_C0 + (3.0 * GELU_C1) * v2)
    return hv * q, grad


ROW_CHUNK = 32
LANE_CHUNK = 512


def _residues(taps):
    return [0] + sorted({off % SUBLANES for _, off in taps} - {0})


def _fill_rotations(rot_ref, residues, length):
    for plane, r in enumerate(residues):
        if r:
            rot_ref[plane, 0:length, :] = rot_ref[0, pl.ds(r, length), :]


def _broadcast_rows(dst_ref, src_ref, count):
    for k in range(count):
        dst_ref[k] = jnp.broadcast_to(src_ref[k:k + 1, :], dst_ref.shape[1:])


def _lane_chunks(width):
    return [(c0, min(LANE_CHUNK, width - c0)) for c0 in range(0, width, LANE_CHUNK)]


def _tap_conv(rot_ref, taps, wb_ref, out_ref, tm, bias_plane=None):
    residues = _residues(taps)
    plane = {r: p for p, r in enumerate(residues)}
    blocks = ROW_CHUNK // SUBLANES
    width = out_ref.shape[1]

    def chunk(c, state):
        r0 = c * ROW_CHUNK
        for c0, cw in _lane_chunks(width):
            accs = [None if bias_plane is None else wb_ref[bias_plane, :, c0:c0 + cw]] * blocks
            for k, off in taps:
                wk = wb_ref[k, :, c0:c0 + cw]
                base = off - off % SUBLANES
                for j in range(blocks):
                    at = pl.multiple_of(r0 + base + SUBLANES * j, SUBLANES)
                    term = wk * rot_ref[plane[off % SUBLANES], pl.ds(at, SUBLANES), c0:c0 + cw]
                    accs[j] = term if accs[j] is None else accs[j] + term
            for j in range(blocks):
                at = pl.multiple_of(r0 + SUBLANES * j, SUBLANES)
                out_ref[pl.ds(at, SUBLANES), c0:c0 + cw] = accs[j]
        return state

    lax.fori_loop(0, tm // ROW_CHUNK, chunk, 0)


def _tap_wgrad(rot_ref, taps, x_ref, acc_ref, tm):
    residues = _residues(taps)
    plane = {r: p for p, r in enumerate(residues)}
    blocks = ROW_CHUNK // SUBLANES
    width = acc_ref.shape[2]

    def chunk(c, state):
        r0 = c * ROW_CHUNK
        for c0, cw in _lane_chunks(width):
            xs = [x_ref[pl.ds(pl.multiple_of(r0 + SUBLANES * j, SUBLANES), SUBLANES), c0:c0 + cw]
                  for j in range(blocks)]
            for k, off in taps:
                base = off - off % SUBLANES
                part = None
                for j in range(blocks):
                    at = pl.multiple_of(r0 + base + SUBLANES * j, SUBLANES)
                    term = xs[j] * rot_ref[plane[off % SUBLANES], pl.ds(at, SUBLANES), c0:c0 + cw]
                    part = term if part is None else part + term
                acc_ref[k, :, c0:c0 + cw] += part
        return state

    lax.fori_loop(0, tm // ROW_CHUNK, chunk, 0)


def _fwd_taps(width):
    return [(k, HALO - (width - 1) + k) for k in range(width)]


def _bwd_taps(width):
    return [(k, width - 1 - k) for k in range(width)]


def _my_index():
    return 4 * lax.axis_index("x") + 2 * lax.axis_index("y") + lax.axis_index("c")


def _mesh_id(idx):
    return (idx // 4, (idx // 2) % 2, idx % 2)


def _slab(ref, axis, idx, width):
    at = [slice(None)] * len(ref.shape)
    at[axis] = pl.ds(pl.multiple_of(idx * width, width), width)
    return ref.at[tuple(at)]


class _Exchange:
    def __init__(self, inputs, out_shape):
        n = len(inputs)
        self.inputs = list(inputs)
        self.out_shape = list(out_shape)
        self.sems = [pltpu.SemaphoreType.DMA((n, N_DEV - 1)), pltpu.SemaphoreType.DMA((n, N_DEV - 1)),
                     pltpu.SemaphoreType.DMA((n,))]
        self.results = None

    def _local(self, ins, outs, k, me):
        raise NotImplementedError

    def _remote(self, ins, outs, k, me, sender, receiver):
        raise NotImplementedError

    def start(self, ins, outs, sems):
        send_sems, recv_sems, local_sems = sems
        me = _my_index()
        for k in range(len(self.inputs)):
            src, dst = self._local(ins, outs, k, me)
            pltpu.make_async_copy(src, dst, local_sems.at[k]).start()
            for dist in range(1, N_DEV):
                peer = (me + dist) % N_DEV
                src, dst = self._remote(ins, outs, k, me, me, peer)
                pltpu.make_async_remote_copy(
                    src_ref=src, dst_ref=dst, send_sem=send_sems.at[k, dist - 1],
                    recv_sem=recv_sems.at[k, dist - 1], device_id=_mesh_id(peer), device_id_type=MESH).start()

    def wait(self, ins, outs, sems):
        send_sems, recv_sems, local_sems = sems
        me = _my_index()
        for k in range(len(self.inputs)):
            for dist in range(1, N_DEV):
                sender = (me + N_DEV - dist) % N_DEV
                src, dst = self._remote(ins, outs, k, me, sender, me)
                cp = pltpu.make_async_remote_copy(
                    src_ref=src, dst_ref=dst, send_sem=send_sems.at[k, dist - 1],
                    recv_sem=recv_sems.at[k, dist - 1], device_id=_mesh_id(sender), device_id_type=MESH)
                cp.wait_send()
                cp.wait_recv()
            src, dst = self._local(ins, outs, k, me)
            pltpu.make_async_copy(src, dst, local_sems.at[k]).wait()

    def forward(self, ins, outs, sems):
        pass


class _Gather(_Exchange):
    FLIPS = ((1, 0), (0, 1), (1, 1))

    def __init__(self, items):
        self.items = list(items)
        out_shape = []
        for shards, layer, axis in self.items:
            shape = list(shards.shape if layer is None else shards.shape[1:])
            shape[axis] *= N_DEV
            out_shape.append(_sds(shape, shards.dtype))
        super().__init__([it[0] for it in self.items], out_shape)

    def _src(self, ins, k):
        layer = self.items[k][1]
        return ins[k] if layer is None else ins[k].at[layer]

    def _place(self, outs, k, idx):
        axis = self.items[k][2]
        return _slab(outs[k], axis, idx, self.out_shape[k].shape[axis] // N_DEV)

    def _copy(self, sems, k, j, src, dst, to):
        return pltpu.make_async_remote_copy(src_ref=src, dst_ref=dst, send_sem=sems[0].at[k, j],
                                            recv_sem=sems[1].at[k, j], device_id=to, device_id_type=MESH)

    @staticmethod
    def _places():
        x, y, c = lax.axis_index("x"), lax.axis_index("y"), lax.axis_index("c")
        chips = [(1 - x if fx else x, 1 - y if fy else y) for fx, fy in _Gather.FLIPS]
        return (x, y, c), (x, y, 1 - c), chips

    @staticmethod
    def _index(place):
        return 4 * place[0] + 2 * place[1] + place[2]

    def start(self, ins, outs, sems):
        me, sibling, chips = self._places()
        for k in range(len(self.inputs)):
            src, mine = self._src(ins, k), self._place(outs, k, self._index(me))
            pltpu.make_async_copy(src, mine, sems[2].at[k]).start()
            self._copy(sems, k, 0, src, mine, sibling).start()
            for j, chip in enumerate(chips):
                self._copy(sems, k, 1 + j, src, mine, (*chip, me[2])).start()

    def forward(self, ins, outs, sems):
        me, sibling, chips = self._places()
        for k in range(len(self.inputs)):
            for j, chip in enumerate(chips):
                got = self._place(outs, k, self._index((*chip, me[2])))
                self._copy(sems, k, 1 + j, got, got, (*chip, me[2])).wait_recv()
                self._copy(sems, k, 4 + j, got, got, sibling).start()

    def wait(self, ins, outs, sems):
        me, sibling, chips = self._places()
        for k in range(len(self.inputs)):
            src, mine = self._src(ins, k), self._place(outs, k, self._index(me))
            self._copy(sems, k, 0, src, self._place(outs, k, self._index(sibling)), sibling).wait_recv()
            for j, chip in enumerate(chips):
                got = self._place(outs, k, self._index((*chip, sibling[2])))
                self._copy(sems, k, 4 + j, got, got, sibling).wait_recv()
            for j in range(N_DEV - 1):
                self._copy(sems, k, j, src, mine, sibling).wait_send()
            pltpu.make_async_copy(src, mine, sems[2].at[k]).wait()


class _Scatter(_Exchange):
    def __init__(self, items):
        self.items = list(items)
        out_shape = []
        for partial, axis in self.items:
            shape = list(partial.shape)
            shape[axis] //= N_DEV
            out_shape.append(_sds([N_DEV] + shape, partial.dtype))
        super().__init__([it[0] for it in self.items], out_shape)

    def _take(self, ins, k, idx):
        axis = self.items[k][1]
        return _slab(ins[k], axis, idx, self.items[k][0].shape[axis] // N_DEV)

    def _local(self, ins, outs, k, me):
        return self._take(ins, k, me), outs[k].at[me]

    def _remote(self, ins, outs, k, me, sender, receiver):
        return self._take(ins, k, receiver), outs[k].at[sender]


def _run_exchange(exchange, name):
    n = len(exchange.inputs)

    def body(*refs):
        ins, outs, sems = refs[:n], refs[n:2 * n], refs[2 * n:]
        exchange.start(ins, outs, sems)
        exchange.forward(ins, outs, sems)
        exchange.wait(ins, outs, sems)

    any_spec = pl.BlockSpec(memory_space=pl.ANY)
    exchange.results = pl.pallas_call(
        body, name=name, in_specs=[any_spec] * n, out_specs=[any_spec] * n, out_shape=exchange.out_shape,
        scratch_shapes=exchange.sems)(*exchange.inputs)
    return exchange.results


class _Layer:
    def __init__(self, stack, index):
        self.stack, self.index = stack, index
        self.shape = (1,) + stack.shape[1:] if stack.ndim == 2 else stack.shape[1:]

    def view(self, ref):
        return ref.at[pl.ds(self.index, 1)] if self.stack.ndim == 2 else ref.at[self.index]


class _LayerTokens:
    def __init__(self, stack, index):
        self.stack, self.index = stack, index

    def rows(self, tm):
        return pl.BlockSpec((None, None, tm, self.stack.shape[-1]), lambda i: (self.index, 0, i, 0))


def _call(body, *, name, grid, in_specs, out_specs, out_shape, args, scratch=(), carry=None):
    in_specs, out_specs, out_shape, scratch = list(in_specs), list(out_specs), list(out_shape), list(scratch)
    args = [a.stack if isinstance(a, _LayerTokens) else a for a in args]
    layers = {k: a for k, a in enumerate(args) if isinstance(a, _Layer)}
    for k, a in layers.items():
        in_specs[k], args[k] = _whole(a.stack.shape), a.stack
    params = pltpu.CompilerParams(dimension_semantics=("arbitrary",) * len(grid),
                                  vmem_limit_bytes=VMEM_LIMIT_BYTES)
    n_in, n_out, n_scr = len(in_specs), len(out_specs), len(scratch)
    n_x = 0 if carry is None else len(carry.inputs)
    steps = 1
    for extent in grid:
        steps *= extent
    assert carry is None or steps >= 3, "a carrier needs a step each for start, second stage and wait"

    def whole_body(*refs):
        core_in, x_in = list(refs[:n_in]), refs[n_in:n_in + n_x]
        refs = refs[n_in + n_x:]
        core_out, x_out = refs[:n_out], refs[n_out:n_out + n_x]
        refs = refs[n_out + n_x:]
        core_scr, sems = refs[:n_scr], refs[n_scr:]
        for k, a in layers.items():
            core_in[k] = a.view(core_in[k])
        if carry is None:
            body(*core_in, *core_out, *core_scr)
            return
        step = pl.program_id(0)
        for axis in range(1, len(grid)):
            step = step * grid[axis] + pl.program_id(axis)

        @pl.when(step == 0)
        def _():
            carry.start(x_in, x_out, sems)

        @pl.when(step == steps - 2)
        def _():
            carry.forward(x_in, x_out, sems)

        body(*core_in, *core_out, *core_scr)

        @pl.when(step == steps - 1)
        def _():
            carry.wait(x_in, x_out, sems)

    any_spec = pl.BlockSpec(memory_space=pl.ANY)
    extra_in = [] if carry is None else carry.inputs
    extra_shape = [] if carry is None else carry.out_shape
    extra_sems = [] if carry is None else carry.sems
    outs = pl.pallas_call(
        whole_body, name=name, grid=grid, in_specs=in_specs + [any_spec] * n_x,
        out_specs=out_specs + [any_spec] * n_x, out_shape=out_shape + extra_shape,
        scratch_shapes=scratch + extra_sems, compiler_params=params)(*args, *extra_in)
    if carry is not None:
        carry.results = outs[n_out:]
    return outs[:n_out]


def _all_reduce(parts, plan, name):
    n = len(parts)
    out_shape = []
    for group in plan:
        shape = parts[group[0]].shape
        if len(group) > 1:
            shape = (len(group),) + (shape[1:] if shape[0] == 1 else shape)
        out_shape.append(_sds(shape, F32))

    def body(*refs):
        ins, outs, lands = refs[:n], refs[n:n + len(plan)], refs[n + len(plan):2 * n + len(plan)]
        send_sems, recv_sems = refs[2 * n + len(plan):]
        me = _my_index()
        for k in range(n):
            lands[k][me] = ins[k][...]
            for dist in range(1, N_DEV):
                pltpu.make_async_remote_copy(
                    src_ref=ins[k], dst_ref=lands[k].at[me],
                    send_sem=send_sems.at[k, dist - 1], recv_sem=recv_sems.at[k, dist - 1],
                    device_id=_mesh_id((me + dist) % N_DEV), device_id_type=MESH).start()
        for k in range(n):
            for dist in range(1, N_DEV):
                sender = (me + N_DEV - dist) % N_DEV
                cp = pltpu.make_async_remote_copy(
                    src_ref=ins[k], dst_ref=lands[k].at[sender],
                    send_sem=send_sems.at[k, dist - 1], recv_sem=recv_sems.at[k, dist - 1],
                    device_id=_mesh_id(sender), device_id_type=MESH)
                cp.wait_send()
                cp.wait_recv()
        for o_ref, group in zip(outs, plan):
            for j, k in enumerate(group):
                total = lands[k][0]
                for dev in range(1, N_DEV):
                    total = total + lands[k][dev]
                if len(group) == 1:
                    o_ref[...] = total
                elif parts[k].shape[0] == 1:
                    o_ref[j:j + 1, :] = total
                else:
                    o_ref[j] = total

    vmem = pl.BlockSpec(memory_space=pltpu.VMEM)
    return pl.pallas_call(
        body, name=name, in_specs=[vmem] * n, out_specs=[vmem] * len(plan), out_shape=out_shape,
        scratch_shapes=[pltpu.VMEM((N_DEV,) + part.shape, F32) for part in parts]
        + [pltpu.SemaphoreType.DMA((n, N_DEV - 1)), pltpu.SemaphoreType.DMA((n, N_DEV - 1))])(*parts)


def _norm_matmul(x, g, wt, name, carry=None):
    t, d = x.shape
    n = wt.shape[0]
    tm, nc = _token_tile(t), _chunk(n)

    def body(x_ref, g_ref, wt_ref, h_ref, o_ref):
        xh, _ = _rms_stats(x_ref[...])
        h = (xh * g_ref[...]).astype(BF16)
        h_ref[...] = h
        for n0 in range(0, n, nc):
            o_ref[:, n0:n0 + nc] = _dot_nt(h, wt_ref[n0:n0 + nc, :]).astype(BF16)

    return _call(body, name=name, grid=(t // tm,),
                 in_specs=[_rows(tm, d), _whole(g.shape), _resident(wt.shape)],
                 out_specs=[_rows(tm, d), _rows(tm, n)],
                 out_shape=[_sds((t, d), BF16), _sds((t, n), BF16)], args=(x, g, wt), carry=carry)


def _fwd_branch(z, caw, cab, lng, lnb, cbw, dc, name, carry=None):
    t = z.shape[0]
    tm = _token_tile(t)
    ka, kb = caw.shape[0], cbw.shape[0]
    taps_a, taps_b = _fwd_taps(ka), _fwd_taps(kb)
    res_a, res_b = _residues(taps_a), _residues(taps_b)
    span = HALO + tm - SUBLANES

    def body(z_ref, caw_ref, cab_ref, lng_ref, lnb_ref, cbw_ref, ac_ref, act_ref, s_ref, cb_ref,
             rot_a, rot_b, wb_a, wb_b, cb):
        @pl.when(pl.program_id(0) == 0)
        def _():
            rot_a[0, 0:HALO, :] = jnp.zeros((HALO, dc), F32)
            rot_b[0, 0:HALO, :] = jnp.zeros((HALO, dc), F32)
            _broadcast_rows(wb_a, caw_ref, ka)
            wb_a[ka] = jnp.broadcast_to(cab_ref[...], (SUBLANES, dc))
            _broadcast_rows(wb_b, cbw_ref, kb)

        a_val = z_ref[:, 0:dc].astype(F32)
        a_gt = z_ref[:, dc:2 * dc].astype(F32)
        rot_a[0, HALO:HALO + tm, :] = a_val * _sigmoid(a_gt)
        _fill_rotations(rot_a, res_a, span)
        _tap_conv(rot_a, taps_a, wb_a, ac_ref, tm, bias_plane=ka)
        ac = ac_ref[...]
        mu = jnp.mean(ac, axis=-1, keepdims=True)
        xc = ac - mu
        var = jnp.mean(xc * xc, axis=-1, keepdims=True)
        ln = xc * lax.rsqrt(var + NORM_EPS) * lng_ref[...] + lnb_ref[...]
        act_ref[...] = (ln * _sigmoid(ln)).astype(BF16)
        rot_a[0, 0:HALO, :] = rot_a[0, tm:tm + HALO, :]

        sc_c = z_ref[:, 3 * dc:4 * dc].astype(F32)
        sc_v = z_ref[:, 4 * dc:5 * dc].astype(F32)
        rot_b[0, HALO:HALO + tm, :] = sc_c * sc_v
        _fill_rotations(rot_b, res_b, span)
        _tap_conv(rot_b, taps_b, wb_b, cb, tm)
        s_ref[...] = (z_ref[:, 2 * dc:3 * dc].astype(F32) * cb[...]).astype(BF16)
        cb_ref[...] = cb[...].astype(BF16)
        rot_b[0, 0:HALO, :] = rot_b[0, tm:tm + HALO, :]

    return _call(body, name=name, grid=(t // tm,),
                 in_specs=[_rows(tm, 5 * dc), _whole(caw.shape), _whole(cab.shape), _whole(lng.shape),
                           _whole(lnb.shape), _whole(cbw.shape)],
                 out_specs=[_rows(tm, dc), _rows(tm, dc), _rows(tm, dc), _rows(tm, dc)],
                 out_shape=[_sds((t, dc), F32), _sds((t, dc), BF16), _sds((t, dc), BF16), _sds((t, dc), BF16)],
                 scratch=[pltpu.VMEM((len(res_a), HALO + tm, dc), F32), pltpu.VMEM((len(res_b), HALO + tm, dc), F32),
                          pltpu.VMEM((ka + 1, SUBLANES, dc), F32), pltpu.VMEM((kb, SUBLANES, dc), F32),
                          pltpu.VMEM((tm, dc), F32)],
                 args=(z, caw, cab, lng, lnb, cbw), carry=carry)


def _fwd_merge(x, z, bg, a_act, s, wa, wb, wo, name, carry=None):
    t, d = x.shape
    n = z.shape[1]
    dc = a_act.shape[1]
    tm = _token_tile(t)
    o5 = n - 2 * d

    def body(x_ref, z_ref, bg_ref, act_ref, s_ref, wa_ref, wb_ref, wo_ref, o_ref):
        ya = _dot(act_ref[...], wa_ref[...])
        yb = _dot(s_ref[...], wb_ref[...])
        ga = _sigmoid(z_ref[:, o5:o5 + d].astype(F32) + bg_ref[:, 0:d])
        gb = _sigmoid(z_ref[:, o5 + d:n].astype(F32) + bg_ref[:, d:2 * d])
        m = (ga * ya + gb * yb).astype(BF16)
        o_ref[...] = x_ref[...] + _dot(m, wo_ref[...])

    return _call(body, name=name, grid=(t // tm,),
                 in_specs=[_rows(tm, d), _rows(tm, n), _whole(bg.shape), _rows(tm, dc), _rows(tm, dc),
                           _resident(wa.shape), _resident(wb.shape), _resident(wo.shape)],
                 out_specs=[_rows(tm, d)], out_shape=[_sds((t, d), F32)],
                 args=(x, z, bg, a_act, s, wa, wb, wo), carry=carry)[0]


def _fwd_down(x, u, cfw, cfb, wd, name, carry=None):
    t, d = x.shape
    f = u.shape[1] // 2
    tm = _token_tile(t, 256)
    kf = cfw.shape[0]

    taps = _fwd_taps(kf)
    residues = _residues(taps)

    def body(x_ref, u_ref, cfw_ref, cfb_ref, wd_ref, o_ref, act_ref, gl_ref, dgl_ref, rot_u, wb, fg):
        @pl.when(pl.program_id(0) == 0)
        def _():
            rot_u[0, 0:HALO, :] = jnp.zeros((HALO, f), F32)
            _broadcast_rows(wb, cfw_ref, kf)
            wb[kf] = jnp.broadcast_to(cfb_ref[...], (SUBLANES, f))

        rot_u[0, HALO:HALO + tm, :] = u_ref[:, 0:f].astype(F32)
        _fill_rotations(rot_u, residues, HALO + tm - SUBLANES)
        _tap_conv(rot_u, taps, wb, fg, tm, bias_plane=kf)
        gl, dgl = _gelu_tanh(fg[...])
        gl_ref[...] = gl.astype(BF16)
        dgl_ref[...] = dgl.astype(BF16)
        act = (gl * u_ref[:, f:2 * f].astype(F32)).astype(BF16)
        act_ref[...] = act
        o_ref[...] = x_ref[...] + _dot(act, wd_ref[...])
        rot_u[0, 0:HALO, :] = rot_u[0, tm:tm + HALO, :]

    return _call(body, name=name, grid=(t // tm,),
                 in_specs=[_rows(tm, d), _rows(tm, 2 * f), _whole(cfw.shape), _whole(cfb.shape),
                           _resident(wd.shape)],
                 out_specs=[_rows(tm, d), _rows(tm, f), _rows(tm, f), _rows(tm, f)],
                 out_shape=[_sds((t, d), F32), _sds((t, f), BF16), _sds((t, f), BF16), _sds((t, f), BF16)],
                 scratch=[pltpu.VMEM((len(residues), HALO + tm, f), F32), pltpu.VMEM((kf + 1, SUBLANES, f), F32),
                          pltpu.VMEM((tm, f), F32)],
                 args=(x, u, cfw, cfb, wd), carry=carry)


def _fwd_ple(x, g, wpg, p, wple, name, carry=None):
    t, d = x.shape
    pd = p.stack.shape[-1]
    tm = _token_tile(t)

    def body(x_ref, g_ref, wpg_ref, p_ref, wple_ref, o_ref):
        xv = x_ref[...]
        xh, _ = _rms_stats(xv)
        lg = _dot((xh * g_ref[...]).astype(BF16), wpg_ref[...])
        pp = _dot(p_ref[...].astype(BF16), wple_ref[...])
        o_ref[...] = xv + _sigmoid(lg) * pp

    return _call(body, name=name, grid=(t // tm,),
                 in_specs=[_rows(tm, d), _whole(g.shape), _resident(wpg.shape), p.rows(tm),
                           _resident(wple.shape)],
                 out_specs=[_rows(tm, d)], out_shape=[_sds((t, d), F32)],
                 args=(x, g, wpg, p, wple), carry=carry)[0]


def _loss_bwd(x, g, target, name):
    t, d = x.shape
    tm = _token_tile(t)

    def body(x_ref, g_ref, t_ref, dx_ref, dg_ref, loss_ref):
        @pl.when(pl.program_id(0) == 0)
        def _():
            dg_ref[...] = jnp.zeros_like(dg_ref)
            loss_ref[...] = jnp.zeros_like(loss_ref)

        xh, r = _rms_stats(x_ref[...])
        err = xh * g_ref[...] - t_ref[...]
        sq = jnp.sum(jnp.sum(err * err, axis=0, keepdims=True), axis=1, keepdims=True)
        loss_ref[...] += jnp.broadcast_to(0.5 * sq / d, loss_ref.shape)
        dx, dg = _rms_bwd(err / d, xh, r, g_ref[...])
        dx_ref[...] = dx
        dg_ref[...] += dg

    return _call(body, name=name, grid=(t // tm,),
                 in_specs=[_rows(tm, d), _whole(g.shape), _rows(tm, d)],
                 out_specs=[_rows(tm, d), _whole((1, d)), _whole((SUBLANES, LANES))],
                 out_shape=[_sds((t, d), F32), _sds((1, d), F32), _sds((SUBLANES, LANES), F32)],
                 args=(x, g, target))


def _bwd_ple(dy, x, g, wpg, p, wple, name, carry=None):
    t, d = x.shape
    pd = p.stack.shape[-1]
    tm = _token_tile(t)
    nt = t // tm

    def body(dy_ref, x_ref, g_ref, wpg_ref, p_ref, wple_ref, dx_ref, dwpg_ref, dwple_ref, dg_ref,
             acc_pg, acc_ple):
        i = pl.program_id(0)

        @pl.when(i == 0)
        def _():
            acc_pg[...] = jnp.zeros_like(acc_pg)
            acc_ple[...] = jnp.zeros_like(acc_ple)
            dg_ref[...] = jnp.zeros_like(dg_ref)

        dyv = dy_ref[...]
        xh, r = _rms_stats(x_ref[...])
        h = (xh * g_ref[...]).astype(BF16)
        pb = p_ref[...].astype(BF16)
        pg = _sigmoid(_dot(h, wpg_ref[...]))
        pp = _dot(pb, wple_ref[...])
        dpp = (dyv * pg).astype(BF16)
        dlg = (dyv * pp * pg * (1.0 - pg)).astype(BF16)
        acc_ple[...] += _dot_tn(pb, dpp)
        acc_pg[...] += _dot_tn(h, dlg)
        dx, dg = _rms_bwd(_dot_nt(dlg, wpg_ref[...]), xh, r, g_ref[...])
        dx_ref[...] = dyv + dx
        dg_ref[...] += dg

        @pl.when(i == nt - 1)
        def _():
            dwpg_ref[...] = acc_pg[...].astype(BF16)
            dwple_ref[...] = acc_ple[...].astype(BF16)

    return _call(body, name=name, grid=(nt,),
                 in_specs=[_rows(tm, d), _rows(tm, d), _whole(g.shape), _resident(wpg.shape), p.rows(tm),
                           _resident(wple.shape)],
                 out_specs=[_rows(tm, d), _whole((d, d)), _whole((pd, d)), _whole((1, d))],
                 out_shape=[_sds((t, d), F32), _sds((d, d), BF16), _sds((pd, d), BF16), _sds((1, d), F32)],
                 scratch=[pltpu.VMEM((d, d), F32), pltpu.VMEM((pd, d), F32)],
                 args=(dy, x, g, wpg, p, wple), carry=carry)


def _bwd_down(dy, u, gl, dgl, cfw, wd, name, carry=None):
    t, d = dy.shape
    f = u.shape[1] // 2
    tm = _token_tile(t, 256)
    nt = t // tm
    kf = cfw.shape[0]
    bwd = _bwd_taps(kf)
    residues = _residues(bwd)

    def body(dy_ref, u_ref, gl_ref, dgl_ref, cfw_ref, wd_ref, du_ref, dcw_ref, dcb_ref, rot_g, wb, acc, buf):
        i = pl.program_id(0)

        @pl.when(i == 0)
        def _():
            for ref in (dcw_ref, dcb_ref, acc):
                ref[...] = jnp.zeros_like(ref)
            rot_g[0, tm:tm + HALO, :] = jnp.zeros((HALO, f), F32)
            _broadcast_rows(wb, cfw_ref, kf)

        df = _dot_nt(dy_ref[...].astype(BF16), wd_ref[...])
        du_ref[:, f:2 * f] = (df * gl_ref[...].astype(F32)).astype(BF16)
        dfg = df * u_ref[:, f:2 * f].astype(F32) * dgl_ref[...].astype(F32)
        dcb_ref[...] += jnp.sum(dfg, axis=0, keepdims=True)
        rot_g[0, 0:tm, :] = dfg
        _fill_rotations(rot_g, residues, HALO + tm - SUBLANES)
        buf[...] = u_ref[:, 0:f].astype(F32)
        _tap_wgrad(rot_g, bwd, buf, acc, tm)
        _tap_conv(rot_g, bwd, wb, buf, tm)
        du_ref[:, 0:f] = buf[...].astype(BF16)
        rot_g[0, tm:tm + HALO, :] = rot_g[0, 0:HALO, :]

        @pl.when(i == nt - 1)
        def _():
            dcw_ref[0:kf, :] = jnp.sum(acc[...], axis=1)

    return _call(body, name=name, grid=(nt,),
                 in_specs=[_rows_rev(tm, d, nt), _rows_rev(tm, 2 * f, nt), _rows_rev(tm, f, nt),
                           _rows_rev(tm, f, nt), _whole(cfw.shape), _resident(wd.shape)],
                 out_specs=[_rows_rev(tm, 2 * f, nt), _whole((SUBLANES, f)), _whole((1, f))],
                 out_shape=[_sds((t, 2 * f), BF16), _sds((SUBLANES, f), F32), _sds((1, f), F32)],
                 scratch=[pltpu.VMEM((len(residues), HALO + tm, f), F32), pltpu.VMEM((kf, SUBLANES, f), F32),
                          pltpu.VMEM((kf, SUBLANES, f), F32), pltpu.VMEM((tm, f), F32)],
                 args=(dy, u, gl, dgl, cfw, wd), carry=carry)


def _bwd_norm_matmul(dout, wt, x, g, dres, name, carry=None):
    t, d = x.shape
    n = dout.shape[1]
    tm = _token_tile(t)

    def body(do_ref, wt_ref, x_ref, g_ref, dres_ref, dx_ref, dg_ref):
        @pl.when(pl.program_id(0) == 0)
        def _():
            dg_ref[...] = jnp.zeros_like(dg_ref)

        dh = _dot(do_ref[...], wt_ref[...])
        xh, r = _rms_stats(x_ref[...])
        dx, dg = _rms_bwd(dh, xh, r, g_ref[...])
        dx_ref[...] = dres_ref[...] + dx
        dg_ref[...] += dg

    return _call(body, name=name, grid=(t // tm,),
                 in_specs=[_rows(tm, n), _resident(wt.shape), _rows(tm, d), _whole(g.shape), _rows(tm, d)],
                 out_specs=[_rows(tm, d), _whole((1, d))],
                 out_shape=[_sds((t, d), F32), _sds((1, d), F32)],
                 args=(dout, wt, x, g, dres), carry=carry)


def _wgrad_tn(a, b, name, carry=None):
    t, n = a.shape
    d = b.shape[1]
    tt = 1024 if t % 1024 == 0 else _token_tile(t)
    tn = _chunk(n, 1536)
    nt = t // tt

    def body(a_ref, b_ref, o_ref, acc):
        k = pl.program_id(1)

        @pl.when(k == 0)
        def _():
            acc[...] = jnp.zeros_like(acc)

        acc[...] += _dot_tn(a_ref[...].astype(BF16), b_ref[...].astype(BF16))

        @pl.when(k == nt - 1)
        def _():
            o_ref[...] = acc[...].astype(BF16)

    return _call(body, name=name, grid=(n // tn, nt),
                 in_specs=[pl.BlockSpec((tt, tn), lambda j, k: (k, j)), pl.BlockSpec((tt, d), lambda j, k: (k, 0))],
                 out_specs=[pl.BlockSpec((tn, d), lambda j, k: (j, 0))], out_shape=[_sds((n, d), BF16)],
                 scratch=[pltpu.VMEM((tn, d), F32)], args=(a, b), carry=carry)[0]


def _bwd_merge(dy, z, bg, a_act, s, wa, wb, wo, name, carry=None):
    t, d = dy.shape
    n = z.shape[1]
    dc = a_act.shape[1]
    tm = _token_tile(t, 256)
    nt = t // tm
    o5 = n - 2 * d

    def body(dy_ref, z_ref, bg_ref, act_ref, s_ref, wa_ref, wb_ref, wo_ref,
             dact_ref, ds_ref, dgl_ref, dwo_ref, dwa_ref, dwb_ref, dbg_ref, acc_o, acc_a, acc_b):
        i = pl.program_id(0)

        @pl.when(i == 0)
        def _():
            acc_o[...] = jnp.zeros_like(acc_o)
            acc_a[...] = jnp.zeros_like(acc_a)
            acc_b[...] = jnp.zeros_like(acc_b)
            dbg_ref[...] = jnp.zeros_like(dbg_ref)

        dyb = dy_ref[...].astype(BF16)
        dm = _dot_nt(dyb, wo_ref[...])
        ya = _dot(act_ref[...], wa_ref[...])
        yb = _dot(s_ref[...], wb_ref[...])
        ga = _sigmoid(z_ref[:, o5:o5 + d].astype(F32) + bg_ref[:, 0:d])
        gb = _sigmoid(z_ref[:, o5 + d:n].astype(F32) + bg_ref[:, d:2 * d])
        acc_o[...] += _dot_tn((ga * ya + gb * yb).astype(BF16), dyb)
        dya = (dm * ga).astype(BF16)
        dyb2 = (dm * gb).astype(BF16)
        acc_a[...] += _dot_tn(act_ref[...], dya)
        acc_b[...] += _dot_tn(s_ref[...], dyb2)
        dact_ref[...] = _dot_nt(dya, wa_ref[...])
        ds_ref[...] = _dot_nt(dyb2, wb_ref[...])
        dla = dm * ya * ga * (1.0 - ga)
        dlb = dm * yb * gb * (1.0 - gb)
        dgl_ref[:, 0:d] = dla.astype(BF16)
        dgl_ref[:, d:2 * d] = dlb.astype(BF16)
        dbg_ref[:, 0:d] += jnp.sum(dla, axis=0, keepdims=True)
        dbg_ref[:, d:2 * d] += jnp.sum(dlb, axis=0, keepdims=True)

        @pl.when(i == nt - 1)
        def _():
            dwo_ref[...] = acc_o[...].astype(BF16)
            dwa_ref[...] = acc_a[...].astype(BF16)
            dwb_ref[...] = acc_b[...].astype(BF16)

    return _call(body, name=name, grid=(nt,),
                 in_specs=[_rows(tm, d), _rows(tm, n), _whole(bg.shape), _rows(tm, dc), _rows(tm, dc),
                           _resident(wa.shape), _resident(wb.shape), _resident(wo.shape)],
                 out_specs=[_rows(tm, dc), _rows(tm, dc), _rows(tm, 2 * d), _whole((d, d)), _whole((dc, d)),
                            _whole((dc, d)), _whole((1, 2 * d))],
                 out_shape=[_sds((t, dc), F32), _sds((t, dc), F32), _sds((t, 2 * d), BF16), _sds((d, d), BF16),
                            _sds((dc, d), BF16), _sds((dc, d), BF16), _sds((1, 2 * d), F32)],
                 scratch=[pltpu.VMEM((d, d), F32), pltpu.VMEM((dc, d), F32), pltpu.VMEM((dc, d), F32)],
                 args=(dy, z, bg, a_act, s, wa, wb, wo), carry=carry)


def _bwd_branch(dact, ds, z, dgl, a_conv, cb, caw, lng, lnb, cbw, name, carry=None):
    t, n = z.shape
    dc = a_conv.shape[1]
    tm = _token_tile(t, 256)
    nt = t // tm
    ka, kb = caw.shape[0], cbw.shape[0]
    bwd_a, bwd_b = _bwd_taps(ka), _bwd_taps(kb)
    span = HALO + tm - SUBLANES

    def body(dact_ref, ds_ref, z_ref, dgl_ref, ac_ref, cb_ref, caw_ref, lng_ref, lnb_ref, cbw_ref,
             dz_ref, dcaw_ref, dcab_ref, dlng_ref, dlnb_ref, dcbw_ref,
             rot_da, rot_dc, wb_a, wb_b, acc_a, acc_b, buf):
        i = pl.program_id(0)

        @pl.when(i == 0)
        def _():
            for ref in (dcaw_ref, dcab_ref, dlng_ref, dlnb_ref, dcbw_ref, acc_a, acc_b):
                ref[...] = jnp.zeros_like(ref)
            rot_da[0, tm:tm + HALO, :] = jnp.zeros((HALO, dc), F32)
            rot_dc[0, tm:tm + HALO, :] = jnp.zeros((HALO, dc), F32)
            _broadcast_rows(wb_a, caw_ref, ka)
            _broadcast_rows(wb_b, cbw_ref, kb)

        a_val = z_ref[:, 0:dc].astype(F32)
        sg = _sigmoid(z_ref[:, dc:2 * dc].astype(F32))

        ac = ac_ref[...]
        mu = jnp.mean(ac, axis=-1, keepdims=True)
        xc = ac - mu
        rstd = lax.rsqrt(jnp.mean(xc * xc, axis=-1, keepdims=True) + NORM_EPS)
        xh = xc * rstd
        ln = xh * lng_ref[...] + lnb_ref[...]
        sl = _sigmoid(ln)
        dln = dact_ref[...] * (sl * (1.0 + ln * (1.0 - sl)))
        dlng_ref[...] += jnp.sum(dln * xh, axis=0, keepdims=True)
        dlnb_ref[...] += jnp.sum(dln, axis=0, keepdims=True)
        dxh = dln * lng_ref[...]
        dac = rstd * (dxh - jnp.mean(dxh, axis=-1, keepdims=True)
                      - xh * jnp.mean(dxh * xh, axis=-1, keepdims=True))
        dcab_ref[...] += jnp.sum(dac, axis=0, keepdims=True)
        rot_da[0, 0:tm, :] = dac
        _fill_rotations(rot_da, _residues(bwd_a), span)
        buf[...] = a_val * sg
        _tap_wgrad(rot_da, bwd_a, buf, acc_a, tm)
        _tap_conv(rot_da, bwd_a, wb_a, buf, tm)
        rot_da[0, tm:tm + HALO, :] = rot_da[0, 0:HALO, :]
        da = buf[...]
        dz_ref[:, 0:dc] = (da * sg).astype(BF16)
        dz_ref[:, dc:2 * dc] = (da * a_val * sg * (1.0 - sg)).astype(BF16)

        sc_b = z_ref[:, 2 * dc:3 * dc].astype(F32)
        sc_c = z_ref[:, 3 * dc:4 * dc].astype(F32)
        sc_v = z_ref[:, 4 * dc:5 * dc].astype(F32)
        dsv = ds_ref[...]
        dz_ref[:, 2 * dc:3 * dc] = (dsv * cb_ref[...].astype(F32)).astype(BF16)
        rot_dc[0, 0:tm, :] = dsv * sc_b
        _fill_rotations(rot_dc, _residues(bwd_b), span)
        buf[...] = sc_c * sc_v
        _tap_wgrad(rot_dc, bwd_b, buf, acc_b, tm)
        _tap_conv(rot_dc, bwd_b, wb_b, buf, tm)
        rot_dc[0, tm:tm + HALO, :] = rot_dc[0, 0:HALO, :]
        dcv = buf[...]
        dz_ref[:, 3 * dc:4 * dc] = (dcv * sc_v).astype(BF16)
        dz_ref[:, 4 * dc:5 * dc] = (dcv * sc_c).astype(BF16)
        dz_ref[:, 5 * dc:n] = dgl_ref[...]

        @pl.when(i == nt - 1)
        def _():
            dcaw_ref[0:ka, :] = jnp.sum(acc_a[...], axis=1)
            dcbw_ref[0:kb, :] = jnp.sum(acc_b[...], axis=1)

    def planes(taps):
        return pltpu.VMEM((len(_residues(taps)), HALO + tm, dc), F32)

    return _call(body, name=name, grid=(nt,),
                 in_specs=[_rows_rev(tm, dc, nt), _rows_rev(tm, dc, nt), _rows_rev(tm, 5 * dc, nt),
                           _rows_rev(tm, n - 5 * dc, nt), _rows_rev(tm, dc, nt), _rows_rev(tm, dc, nt),
                           _whole(caw.shape), _whole(lng.shape), _whole(lnb.shape), _whole(cbw.shape)],
                 out_specs=[_rows_rev(tm, n, nt), _whole((HALO, dc)), _whole((1, dc)), _whole((1, dc)),
                            _whole((1, dc)), _whole((SUBLANES, dc))],
                 out_shape=[_sds((t, n), BF16), _sds((HALO, dc), F32), _sds((1, dc), F32), _sds((1, dc), F32),
                            _sds((1, dc), F32), _sds((SUBLANES, dc), F32)],
                 scratch=[planes(bwd_a), planes(bwd_b),
                          pltpu.VMEM((ka, SUBLANES, dc), F32), pltpu.VMEM((kb, SUBLANES, dc), F32),
                          pltpu.VMEM((ka, SUBLANES, dc), F32), pltpu.VMEM((kb, SUBLANES, dc), F32),
                          pltpu.VMEM((tm, dc), F32)],
                 args=(dact, ds, z, dgl, a_conv, cb, caw, lng, lnb, cbw), carry=carry)


def _land_specs(depth, nr, tr, cols):
    def spec(k):
        return pl.BlockSpec((N_DEV, tr, cols), lambda i: (0, jnp.clip(i - k * nr, 0, nr - 1), 0))
    return [spec(k) for k in range(depth)]


def _adamw_math(w, g, m, v):
    nm = ADAM_B1 * m + (1.0 - ADAM_B1) * g
    nv = ADAM_B2 * v + (1.0 - ADAM_B2) * (g * g)
    m_hat = nm / (1.0 - ADAM_B1 ** ADAM_STEP)
    v_hat = nv / (1.0 - ADAM_B2 ** ADAM_STEP)
    return -ADAM_LR * (m_hat / (jnp.sqrt(v_hat) + ADAM_EPS) + ADAM_WD * w), nm, nv


def _sum_adamw(lands, w, m, v, name):
    _, rows, cols = lands[0].shape
    tr = _row_tile(rows)
    nr = rows // tr
    depth = len(lands)

    def body(*refs):
        w_ref, m_ref, v_ref, g_ref, d_ref, nm_ref, nv_ref = refs[depth:]
        i = pl.program_id(0)
        for k in range(depth):
            @pl.when(i // nr == k)
            def _(k=k):
                acc = refs[k][0].astype(F32)
                for j in range(1, N_DEV):
                    acc = acc + refs[k][j].astype(F32)
                g_ref[...] = acc
                d_ref[...], nm_ref[...], nv_ref[...] = _adamw_math(w_ref[...], acc, m_ref[...], v_ref[...])

    spec = _rows(tr, cols)
    return _call(body, name=name, grid=(depth * nr,), in_specs=_land_specs(depth, nr, tr, cols) + [spec] * 3,
                 out_specs=[spec] * 4, out_shape=[_sds((depth * rows, cols), F32)] * 4, args=(*lands, w, m, v))


def _adamw_small(ws, gs, ms, vs, name):
    n = len(ws)

    def body(*refs):
        w_refs, g_refs, m_refs, v_refs = refs[:n], refs[n:2 * n], refs[2 * n:3 * n], refs[3 * n:4 * n]
        d_refs, nm_refs, nv_refs = refs[4 * n:5 * n], refs[5 * n:6 * n], refs[6 * n:]
        for k in range(n):
            d_refs[k][...], nm_refs[k][...], nv_refs[k][...] = _adamw_math(
                w_refs[k][...], g_refs[k][...], m_refs[k][...], v_refs[k][...])

    vmem = pl.BlockSpec(memory_space=pltpu.VMEM)
    outs = pl.pallas_call(
        body, name=name, in_specs=[vmem] * (4 * n), out_specs=[vmem] * (3 * n),
        out_shape=[_sds(a.shape, F32) for a in ws] * 3)(*ws, *gs, *ms, *vs)
    return outs[:n], outs[n:2 * n], outs[2 * n:]


def kernel(x, p, g_mix, w_in, b_gate, conv_a_w, conv_a_b, ln_a_g, ln_a_b, w_a_out, conv_b_w, w_b_out, w_o, g_ffn, w_up, conv_f_w, conv_f_b, w_down, g_ple, w_ple, w_ple_gate, g_final, loss_target, m_g_mix, m_w_in, m_b_gate, m_conv_a_w, m_conv_a_b, m_ln_a_g, m_ln_a_b, m_w_a_out, m_conv_b_w, m_w_b_out, m_w_o, m_g_ffn, m_w_up, m_conv_f_w, m_conv_f_b, m_w_down, m_g_ple, m_w_ple, m_w_ple_gate, m_g_final, v_g_mix, v_w_in, v_b_gate, v_conv_a_w, v_conv_a_b, v_ln_a_g, v_ln_a_b, v_w_a_out, v_conv_b_w, v_w_b_out, v_w_o, v_g_ffn, v_w_up, v_conv_f_w, v_conv_f_b, v_w_down, v_g_ple, v_w_ple, v_w_ple_gate, v_g_final):
    w = dict(zip(WEIGHT_NAMES, (g_mix, w_in, b_gate, conv_a_w, conv_a_b, ln_a_g, ln_a_b, w_a_out, conv_b_w,
                                w_b_out, w_o, g_ffn, w_up, conv_f_w, conv_f_b, w_down, g_ple, w_ple,
                                w_ple_gate, g_final)))
    mom = dict(zip(WEIGHT_NAMES, (m_g_mix, m_w_in, m_b_gate, m_conv_a_w, m_conv_a_b, m_ln_a_g, m_ln_a_b,
                                  m_w_a_out, m_conv_b_w, m_w_b_out, m_w_o, m_g_ffn, m_w_up, m_conv_f_w,
                                  m_conv_f_b, m_w_down, m_g_ple, m_w_ple, m_w_ple_gate, m_g_final)))
    var = dict(zip(WEIGHT_NAMES, (v_g_mix, v_w_in, v_b_gate, v_conv_a_w, v_conv_a_b, v_ln_a_g, v_ln_a_b,
                                  v_w_a_out, v_conv_b_w, v_w_b_out, v_w_o, v_g_ffn, v_w_up, v_conv_f_w,
                                  v_conv_f_b, v_w_down, v_g_ple, v_w_ple, v_w_ple_gate, v_g_final)))
    depth = g_mix.shape[0]
    dc = ln_a_g.shape[1]
    me = _my_index()
    x0 = x[0]
    target = loss_target[0]
    big_names = tuple(BIG_AXIS)

    shard = {name: (jnp.swapaxes(w[name], 1, 2) if name in TRANSPOSED else w[name]).astype(BF16)
             for name in big_names}

    def gather_of(layer, *names):
        return _Gather([(shard[name], layer, BIG_AXIS[name]) for name in names])

    def row(name, layer):
        return _Layer(w[name], layer)

    first = _Gather([(shard['w_in'], 0, BIG_AXIS['w_in'])] + [(w[name][None], None, 0) for name in CONV_SHARDED])
    gathered = _run_exchange(first, "gather_first")
    w_in_full = gathered[0]
    conv_full = {name: jnp.transpose(g, (1, 2, 0, 3)).reshape(g.shape[1], g.shape[2], -1)
                 for name, g in zip(CONV_SHARDED, gathered[1:])}
    saved = []
    xc = x0
    for l in range(depth):
        carry = gather_of(l, 'w_a_out', 'w_b_out', 'w_o', 'w_up')
        h, z = _norm_matmul(xc, row('g_mix', l), w_in_full, f"fwd_in_{l}", carry)
        wa_full, wb_full, wo_full, w_up_full = carry.results
        a_conv, a_act, s, cb = _fwd_branch(z, _Layer(conv_full['conv_a_w'], l), row('conv_a_b', l), row('ln_a_g', l),
                                       row('ln_a_b', l), _Layer(conv_full['conv_b_w'], l), dc, f"fwd_branch_{l}")
        x1 = _fwd_merge(xc, z, row('b_gate', l), a_act, s, wa_full, wb_full, wo_full, f"fwd_merge_{l}")
        carry = _Gather([(shard[name], l, BIG_AXIS[name]) for name in ('w_down', 'w_ple', 'w_ple_gate')]
                        + ([(shard['w_in'], l + 1, BIG_AXIS['w_in'])] if l + 1 < depth else []))
        h2, u = _norm_matmul(x1, row('g_ffn', l), w_up_full, f"fwd_up_{l}", carry)
        w_down_full, w_ple_full, w_pg_full = carry.results[:3]
        x2, act, gl, dgl = _fwd_down(x1, u, _Layer(conv_full['conv_f_w'], l), row('conv_f_b', l), w_down_full,
                                     f"fwd_down_{l}")
        x3 = _fwd_ple(x2, row('g_ple', l), w_pg_full, _LayerTokens(p, l), w_ple_full, f"fwd_ple_{l}")
        saved.append((xc, h, z, a_conv, a_act, s, cb, x1, h2, u, act, gl, dgl, x2,
                      dict(w_in=w_in_full, w_a_out=wa_full, w_b_out=wb_full, w_o=wo_full, w_up=w_up_full,
                           w_down=w_down_full, w_ple=w_ple_full, w_ple_gate=w_pg_full)))
        if l + 1 < depth:
            w_in_full = carry.results[3]
        xc = x3

    dx, dg_final, loss_part = _loss_bwd(xc, g_final[None], target, "loss_bwd")
    landed = {name: [None] * depth for name in big_names}
    small = {name: [None] * depth for name in WEIGHT_NAMES if name not in BIG_AXIS and name != 'g_final'}

    def scatter_of(*partials):
        ex = _Scatter([(part, BIG_AXIS[name]) for name, _, part in partials])
        ex.places = [(name, layer) for name, layer, _ in partials]
        return ex

    def keep(ex):
        for (name, layer), land in zip(ex.places, ex.results):
            landed[name][layer] = land

    pending = []
    for l in reversed(range(depth)):
        xin, h, z, a_conv, a_act, s, cb, x1, h2, u, act, gl, dgl, x2, full = saved[l]
        dx2, d_wpg, d_wple, small['g_ple'][l] = _bwd_ple(
            dx, x2, row('g_ple', l), full['w_ple_gate'], _LayerTokens(p, l), full['w_ple'], f"bwd_ple_{l}")
        d_wdown = _wgrad_tn(act, dx2, f"wgrad_down_{l}")
        carry = scatter_of(('w_ple_gate', l, d_wpg), ('w_ple', l, d_wple), ('w_down', l, d_wdown))
        du, small['conv_f_w'][l], small['conv_f_b'][l] = _bwd_down(
            dx2, u, gl, dgl, _Layer(conv_full['conv_f_w'], l), full['w_down'], f"bwd_down_{l}", carry)
        keep(carry)
        carry = scatter_of(*pending) if pending else None
        pending = []
        dx1, small['g_ffn'][l] = _bwd_norm_matmul(du, full['w_up'], x1, row('g_ffn', l), dx2, f"bwd_up_{l}", carry)
        if carry is not None:
            keep(carry)
        d_wup = _wgrad_tn(du, h2, f"wgrad_up_{l}")
        dact, ds, dgate, d_wo, d_wa, d_wb, small['b_gate'][l] = _bwd_merge(
            dx1, z, row('b_gate', l), a_act, s, full['w_a_out'], full['w_b_out'], full['w_o'], f"bwd_merge_{l}")
        carry = scatter_of(('w_up', l, d_wup))
        (dz, small['conv_a_w'][l], small['conv_a_b'][l], small['ln_a_g'][l], small['ln_a_b'][l],
         small['conv_b_w'][l]) = _bwd_branch(
            dact, ds, z, dgate, a_conv, cb, _Layer(conv_full['conv_a_w'], l), row('ln_a_g', l), row('ln_a_b', l),
            _Layer(conv_full['conv_b_w'], l), f"bwd_branch_{l}", carry)
        keep(carry)
        carry = scatter_of(('w_o', l, d_wo), ('w_a_out', l, d_wa), ('w_b_out', l, d_wb))
        d_win = _wgrad_tn(dz, h, f"wgrad_in_{l}", carry)
        keep(carry)
        carry = scatter_of(('w_in', l, d_win)) if l == 0 else None
        if l > 0:
            pending = [('w_in', l, d_win)]
        dx, small['g_mix'][l] = _bwd_norm_matmul(dz, full['w_in'], xin, row('g_mix', l), dx1, f"bwd_in_{l}", carry)
        if carry is not None:
            keep(carry)
    grad_x = dx[None]

    small_names = tuple(small)
    parts = [part for name in small_names for part in small[name]] + [dg_final, loss_part]
    plan = [tuple(range(k * depth, (k + 1) * depth)) for k in range(len(small_names))]
    plan += [(len(parts) - 2,), (len(parts) - 1,)]
    reduced = _all_reduce(parts, plan, "all_reduce_small")
    loss = reduced[-1][0, 0]
    grads = dict(zip(small_names, reduced[:len(small_names)]))
    grads['g_final'] = reduced[len(small_names)].reshape(g_final.shape)
    for name in CONV_SHARDED:
        _, taps, width = w[name].shape
        grads[name] = lax.dynamic_slice(grads[name], (0, 0, me * width), (depth, taps, width))

    delta, new_m, new_v = {}, {}, {}
    for name in big_names:
        view = (lambda a: jnp.swapaxes(a, 1, 2)) if name in TRANSPOSED else (lambda a: a)
        shape = view(w[name]).shape
        flat = lambda a: view(a).reshape(-1, shape[-1])
        lands = [land.reshape(N_DEV, -1, shape[-1]) for land in landed[name]]
        outs = _sum_adamw(lands, flat(w[name]), flat(mom[name]), flat(var[name]), f"adamw_{name}")
        grads[name], delta[name], new_m[name], new_v[name] = [view(a.reshape(shape)) for a in outs]
    rest = tuple(name for name in WEIGHT_NAMES if name not in BIG_AXIS)
    as_2d = lambda a: a.reshape(1, -1) if a.ndim == 1 else a
    outs = _adamw_small(*[[as_2d(src[name]) for name in rest] for src in (w, grads, mom, var)], "adamw_small")
    for dst, values in zip((delta, new_m, new_v), outs):
        dst.update({name: value.reshape(w[name].shape) for name, value in zip(rest, values)})

    return (loss, grad_x, *[grads[n] for n in WEIGHT_NAMES], *[delta[n] for n in WEIGHT_NAMES],
            *[new_m[n] for n in WEIGHT_NAMES], *[new_v[n] for n in WEIGHT_NAMES])
```

```python
import jax
import jax.numpy as jnp
from jax import lax
from jax.experimental import pallas as pl
from jax.experimental.pallas import tpu as pltpu

F32 = jnp.float32
BF16 = jnp.bfloat16
MESH = pl.DeviceIdType.MESH

N_DEV = 8
NORM_EPS = 1e-6
HALO = 32
LANES = 128
SUBLANES = 8
VMEM_LIMIT_BYTES = 56 * 2**20

ADAM_LR = 0.001
ADAM_B1 = 0.9
ADAM_B2 = 0.999
ADAM_EPS = 1e-08
ADAM_WD = 0.01
ADAM_STEP = 10

WEIGHT_NAMES = ('g_mix', 'w_in', 'b_gate', 'conv_a_w', 'conv_a_b', 'ln_a_g', 'ln_a_b', 'w_a_out',
                'conv_b_w', 'w_b_out', 'w_o', 'g_ffn', 'w_up', 'conv_f_w', 'conv_f_b', 'w_down',
                'g_ple', 'w_ple', 'w_ple_gate', 'g_final')
BIG_AXIS = {'w_in': 0, 'w_up': 0, 'w_a_out': 1, 'w_b_out': 1, 'w_o': 0, 'w_down': 0, 'w_ple': 1,
            'w_ple_gate': 0}
TRANSPOSED = ('w_in', 'w_up')
CONV_SHARDED = ('conv_a_w', 'conv_b_w', 'conv_f_w')


def _dot(a, b):
    return jnp.dot(a, b, preferred_element_type=F32)


def _dot_nt(a, b):
    return lax.dot_general(a, b, (((1,), (1,)), ((), ())), preferred_element_type=F32)


def _dot_tn(a, b):
    return lax.dot_general(a, b, (((0,), (0,)), ((), ())), preferred_element_type=F32)


def _sigmoid(v):
    return jax.nn.sigmoid(v)


def _token_tile(t, cap=512):
    return cap if (t % cap == 0 and t > 512) else 128


def _chunk(n, limit=512):
    for c in range(limit - limit % LANES, 0, -LANES):
        if n % c == 0:
            return c
    return n


def _row_tile(rows):
    for c in (512, 256, 128, 64, 32, 16, 8):
        if rows % c == 0:
            return c
    return rows


def _rows(tm, width):
    return pl.BlockSpec((tm, width), lambda i: (i, 0))


def _rows_rev(tm, width, nt):
    return pl.BlockSpec((tm, width), lambda i: (nt - 1 - i, 0))


def _whole(shape):
    nd = len(shape)
    return pl.BlockSpec(tuple(shape), lambda i: (0,) * nd)


def _resident(shape):
    nd = len(shape)
    return pl.BlockSpec(tuple(shape), lambda i: (0,) * nd, pipeline_mode=pl.Buffered(1))


def _sds(shape, dtype):
    return jax.ShapeDtypeStruct(tuple(shape), dtype)


def _rms_stats(xv):
    r = lax.rsqrt(jnp.mean(xv * xv, axis=-1, keepdims=True) + NORM_EPS)
    return xv * r, r


def _rms_bwd(dy, xh, r, g):
    dxh = dy * g
    dx = r * (dxh - xh * jnp.mean(dxh * xh, axis=-1, keepdims=True))
    return dx, jnp.sum(dy * xh, axis=0, keepdims=True)


GELU_C0 = 0.7978845608028654
GELU_C1 = GELU_C0 * 0.044715


def _gelu_tanh(v):
    v2 = v * v
    t = jnp.tanh(v * (GELU_C0 + GELU_C1 * v2))
    q = 1.0 + t
    hv = 0.5 * v
    grad = 0.5 * q + hv * (1.0 - t * t) * (GELU_C0 + (3.0 * GELU_C1) * v2)
    return hv * q, grad


ROW_CHUNK = 32
LANE_CHUNK = 512


def _residues(taps):
    return [0] + sorted({off % SUBLANES for _, off in taps} - {0})


def _fill_rotations(rot_ref, residues, length):
    for plane, r in enumerate(residues):
        if r:
            rot_ref[plane, 0:length, :] = rot_ref[0, pl.ds(r, length), :]


def _broadcast_rows(dst_ref, src_ref, count):
    for k in range(count):
        dst_ref[k] = jnp.broadcast_to(src_ref[k:k + 1, :], dst_ref.shape[1:])


def _lane_chunks(width, chunk=LANE_CHUNK):
    return [(c0, min(chunk, width - c0)) for c0 in range(0, width, chunk)]


def _tap_conv(rot_ref, taps, wb_ref, out_ref, tm, bias_plane=None):
    residues = _residues(taps)
    plane = {r: p for p, r in enumerate(residues)}
    blocks = ROW_CHUNK // SUBLANES
    width = out_ref.shape[1]

    def chunk(c, state):
        r0 = c * ROW_CHUNK
        for c0, cw in _lane_chunks(width):
            accs = [None if bias_plane is None else wb_ref[bias_plane, :, c0:c0 + cw]] * blocks
            for k, off in taps:
                wk = wb_ref[k, :, c0:c0 + cw]
                base = off - off % SUBLANES
                for j in range(blocks):
                    at = pl.multiple_of(r0 + base + SUBLANES * j, SUBLANES)
                    term = wk * rot_ref[plane[off % SUBLANES], pl.ds(at, SUBLANES), c0:c0 + cw]
                    accs[j] = term if accs[j] is None else accs[j] + term
            for j in range(blocks):
                at = pl.multiple_of(r0 + SUBLANES * j, SUBLANES)
                out_ref[pl.ds(at, SUBLANES), c0:c0 + cw] = accs[j]
        return state

    lax.fori_loop(0, tm // ROW_CHUNK, chunk, 0)


def _tap_wgrad(rot_ref, taps, x_ref, acc_ref, tm):
    residues = _residues(taps)
    plane = {r: p for p, r in enumerate(residues)}
    blocks = ROW_CHUNK // SUBLANES
    width = acc_ref.shape[2]

    def chunk(c, state):
        r0 = c * ROW_CHUNK
        for c0, cw in _lane_chunks(width):
            xs = [x_ref[pl.ds(pl.multiple_of(r0 + SUBLANES * j, SUBLANES), SUBLANES), c0:c0 + cw]
                  for j in range(blocks)]
            for k, off in taps:
                base = off - off % SUBLANES
                part = None
                for j in range(blocks):
                    at = pl.multiple_of(r0 + base + SUBLANES * j, SUBLANES)
                    term = xs[j] * rot_ref[plane[off % SUBLANES], pl.ds(at, SUBLANES), c0:c0 + cw]
                    part = term if part is None else part + term
                acc_ref[k, :, c0:c0 + cw] += part
        return state

    lax.fori_loop(0, tm // ROW_CHUNK, chunk, 0)


def _fwd_taps(width):
    return [(k, HALO - (width - 1) + k) for k in range(width)]


def _bwd_taps(width):
    return [(k, width - 1 - k) for k in range(width)]


def _my_index():
    return 4 * lax.axis_index("x") + 2 * lax.axis_index("y") + lax.axis_index("c")


def _mesh_id(idx):
    return (idx // 4, (idx // 2) % 2, idx % 2)


def _slab(ref, axis, idx, width):
    at = [slice(None)] * len(ref.shape)
    at[axis] = pl.ds(pl.multiple_of(idx * width, width), width)
    return ref.at[tuple(at)]


class _Exchange:
    def __init__(self, inputs, out_shape):
        n = len(inputs)
        self.inputs = list(inputs)
        self.out_shape = list(out_shape)
        self.sems = [pltpu.SemaphoreType.DMA((n, N_DEV - 1)), pltpu.SemaphoreType.DMA((n, N_DEV - 1)),
                     pltpu.SemaphoreType.DMA((n,))]
        self.results = None

    def _local(self, ins, outs, k, me):
        raise NotImplementedError

    def _remote(self, ins, outs, k, me, sender, receiver):
        raise NotImplementedError

    def start(self, ins, outs, sems):
        send_sems, recv_sems, local_sems = sems
        me = _my_index()
        for k in range(len(self.inputs)):
            src, dst = self._local(ins, outs, k, me)
            pltpu.make_async_copy(src, dst, local_sems.at[k]).start()
            for dist in range(1, N_DEV):
                peer = (me + dist) % N_DEV
                src, dst = self._remote(ins, outs, k, me, me, peer)
                pltpu.make_async_remote_copy(
                    src_ref=src, dst_ref=dst, send_sem=send_sems.at[k, dist - 1],
                    recv_sem=recv_sems.at[k, dist - 1], device_id=_mesh_id(peer), device_id_type=MESH).start()

    def wait(self, ins, outs, sems):
        send_sems, recv_sems, local_sems = sems
        me = _my_index()
        for k in range(len(self.inputs)):
            for dist in range(1, N_DEV):
                sender = (me + N_DEV - dist) % N_DEV
                src, dst = self._remote(ins, outs, k, me, sender, me)
                cp = pltpu.make_async_remote_copy(
                    src_ref=src, dst_ref=dst, send_sem=send_sems.at[k, dist - 1],
                    recv_sem=recv_sems.at[k, dist - 1], device_id=_mesh_id(sender), device_id_type=MESH)
                cp.wait_send()
                cp.wait_recv()
            src, dst = self._local(ins, outs, k, me)
            pltpu.make_async_copy(src, dst, local_sems.at[k]).wait()

    def forward(self, ins, outs, sems):
        pass


class _Gather(_Exchange):
    FLIPS = ((1, 0), (0, 1), (1, 1))

    def __init__(self, items):
        self.items = list(items)
        out_shape = []
        for shards, layer, axis in self.items:
            shape = list(shards.shape if layer is None else shards.shape[1:])
            shape[axis] *= N_DEV
            out_shape.append(_sds(shape, shards.dtype))
        super().__init__([it[0] for it in self.items], out_shape)

    def _src(self, ins, k):
        layer = self.items[k][1]
        return ins[k] if layer is None else ins[k].at[layer]

    def _place(self, outs, k, idx):
        axis = self.items[k][2]
        return _slab(outs[k], axis, idx, self.out_shape[k].shape[axis] // N_DEV)

    def _copy(self, sems, k, j, src, dst, to):
        return pltpu.make_async_remote_copy(src_ref=src, dst_ref=dst, send_sem=sems[0].at[k, j],
                                            recv_sem=sems[1].at[k, j], device_id=to, device_id_type=MESH)

    @staticmethod
    def _places():
        x, y, c = lax.axis_index("x"), lax.axis_index("y"), lax.axis_index("c")
        chips = [(1 - x if fx else x, 1 - y if fy else y) for fx, fy in _Gather.FLIPS]
        return (x, y, c), (x, y, 1 - c), chips

    @staticmethod
    def _index(place):
        return 4 * place[0] + 2 * place[1] + place[2]

    def start(self, ins, outs, sems):
        me, sibling, chips = self._places()
        for k in range(len(self.inputs)):
            src, mine = self._src(ins, k), self._place(outs, k, self._index(me))
            pltpu.make_async_copy(src, mine, sems[2].at[k]).start()
            self._copy(sems, k, 0, src, mine, sibling).start()
            for j, chip in enumerate(chips):
                self._copy(sems, k, 1 + j, src, mine, (*chip, me[2])).start()

    def forward(self, ins, outs, sems):
        me, sibling, chips = self._places()
        for k in range(len(self.inputs)):
            for j, chip in enumerate(chips):
                got = self._place(outs, k, self._index((*chip, me[2])))
                self._copy(sems, k, 1 + j, got, got, (*chip, me[2])).wait_recv()
                self._copy(sems, k, 4 + j, got, got, sibling).start()

    def wait(self, ins, outs, sems):
        me, sibling, chips = self._places()
        for k in range(len(self.inputs)):
            src, mine = self._src(ins, k), self._place(outs, k, self._index(me))
            self._copy(sems, k, 0, src, self._place(outs, k, self._index(sibling)), sibling).wait_recv()
            for j, chip in enumerate(chips):
                got = self._place(outs, k, self._index((*chip, sibling[2])))
                self._copy(sems, k, 4 + j, got, got, sibling).wait_recv()
            for j in range(N_DEV - 1):
                self._copy(sems, k, j, src, mine, sibling).wait_send()
            pltpu.make_async_copy(src, mine, sems[2].at[k]).wait()


class _Scatter(_Exchange):
    def __init__(self, items):
        self.items = list(items)
        out_shape = []
        for partial, axis in self.items:
            shape = list(partial.shape)
            shape[axis] //= N_DEV
            out_shape.append(_sds([N_DEV] + shape, partial.dtype))
        super().__init__([it[0] for it in self.items], out_shape)

    def _take(self, ins, k, idx):
        axis = self.items[k][1]
        return _slab(ins[k], axis, idx, self.items[k][0].shape[axis] // N_DEV)

    def _local(self, ins, outs, k, me):
        return self._take(ins, k, me), outs[k].at[me]

    def _remote(self, ins, outs, k, me, sender, receiver):
        return self._take(ins, k, receiver), outs[k].at[sender]


def _run_exchange(exchange, name):
    n = len(exchange.inputs)

    def body(*refs):
        ins, outs, sems = refs[:n], refs[n:2 * n], refs[2 * n:]
        exchange.start(ins, outs, sems)
        exchange.forward(ins, outs, sems)
        exchange.wait(ins, outs, sems)

    any_spec = pl.BlockSpec(memory_space=pl.ANY)
    exchange.results = pl.pallas_call(
        body, name=name, in_specs=[any_spec] * n, out_specs=[any_spec] * n, out_shape=exchange.out_shape,
        scratch_shapes=exchange.sems)(*exchange.inputs)
    return exchange.results


class _Layer:
    def __init__(self, stack, index):
        self.stack, self.index = stack, index
        self.shape = (1,) + stack.shape[1:] if stack.ndim == 2 else stack.shape[1:]

    def view(self, ref):
        return ref.at[pl.ds(self.index, 1)] if self.stack.ndim == 2 else ref.at[self.index]


class _LayerTokens:
    def __init__(self, stack, index):
        self.stack, self.index = stack, index

    def rows(self, tm):
        return pl.BlockSpec((None, None, tm, self.stack.shape[-1]), lambda i: (self.index, 0, i, 0))


def _call(body, *, name, grid, in_specs, out_specs, out_shape, args, scratch=(), carry=None):
    in_specs, out_specs, out_shape, scratch = list(in_specs), list(out_specs), list(out_shape), list(scratch)
    args = [a.stack if isinstance(a, _LayerTokens) else a for a in args]
    layers = {k: a for k, a in enumerate(args) if isinstance(a, _Layer)}
    for k, a in layers.items():
        in_specs[k], args[k] = _whole(a.stack.shape), a.stack
    params = pltpu.CompilerParams(dimension_semantics=("arbitrary",) * len(grid),
                                  vmem_limit_bytes=VMEM_LIMIT_BYTES)
    n_in, n_out, n_scr = len(in_specs), len(out_specs), len(scratch)
    n_x = 0 if carry is None else len(carry.inputs)
    steps = 1
    for extent in grid:
        steps *= extent
    assert carry is None or steps >= 3, "a carrier needs a step each for start, second stage and wait"

    def whole_body(*refs):
        core_in, x_in = list(refs[:n_in]), refs[n_in:n_in + n_x]
        refs = refs[n_in + n_x:]
        core_out, x_out = refs[:n_out], refs[n_out:n_out + n_x]
        refs = refs[n_out + n_x:]
        core_scr, sems = refs[:n_scr], refs[n_scr:]
        for k, a in layers.items():
            core_in[k] = a.view(core_in[k])
        if carry is None:
            body(*core_in, *core_out, *core_scr)
            return
        step = pl.program_id(0)
        for axis in range(1, len(grid)):
            step = step * grid[axis] + pl.program_id(axis)

        @pl.when(step == 0)
        def _():
            carry.start(x_in, x_out, sems)

        @pl.when(step == steps - 2)
        def _():
            carry.forward(x_in, x_out, sems)

        body(*core_in, *core_out, *core_scr)

        @pl.when(step == steps - 1)
        def _():
            carry.wait(x_in, x_out, sems)

    any_spec = pl.BlockSpec(memory_space=pl.ANY)
    extra_in = [] if carry is None else carry.inputs
    extra_shape = [] if carry is None else carry.out_shape
    extra_sems = [] if carry is None else carry.sems
    outs = pl.pallas_call(
        whole_body, name=name, grid=grid, in_specs=in_specs + [any_spec] * n_x,
        out_specs=out_specs + [any_spec] * n_x, out_shape=out_shape + extra_shape,
        scratch_shapes=scratch + extra_sems, compiler_params=params)(*args, *extra_in)
    if carry is not None:
        carry.results = outs[n_out:]
    return outs[:n_out]


def _all_reduce(parts, plan, name):
    n = len(parts)
    out_shape = []
    for group in plan:
        shape = parts[group[0]].shape
        if len(group) > 1:
            shape = (len(group),) + (shape[1:] if shape[0] == 1 else shape)
        out_shape.append(_sds(shape, F32))

    def body(*refs):
        ins, outs, lands = refs[:n], refs[n:n + len(plan)], refs[n + len(plan):2 * n + len(plan)]
        send_sems, recv_sems = refs[2 * n + len(plan):]
        me = _my_index()
        for k in range(n):
            lands[k][me] = ins[k][...]
            for dist in range(1, N_DEV):
                pltpu.make_async_remote_copy(
                    src_ref=ins[k], dst_ref=lands[k].at[me],
                    send_sem=send_sems.at[k, dist - 1], recv_sem=recv_sems.at[k, dist - 1],
                    device_id=_mesh_id((me + dist) % N_DEV), device_id_type=MESH).start()
        for k in range(n):
            for dist in range(1, N_DEV):
                sender = (me + N_DEV - dist) % N_DEV
                cp = pltpu.make_async_remote_copy(
                    src_ref=ins[k], dst_ref=lands[k].at[sender],
                    send_sem=send_sems.at[k, dist - 1], recv_sem=recv_sems.at[k, dist - 1],
                    device_id=_mesh_id(sender), device_id_type=MESH)
                cp.wait_send()
                cp.wait_recv()
        for o_ref, group in zip(outs, plan):
            for j, k in enumerate(group):
                total = lands[k][0]
                for dev in range(1, N_DEV):
                    total = total + lands[k][dev]
                if len(group) == 1:
                    o_ref[...] = total
                elif parts[k].shape[0] == 1:
                    o_ref[j:j + 1, :] = total
                else:
                    o_ref[j] = total

    vmem = pl.BlockSpec(memory_space=pltpu.VMEM)
    return pl.pallas_call(
        body, name=name, in_specs=[vmem] * n, out_specs=[vmem] * len(plan), out_shape=out_shape,
        scratch_shapes=[pltpu.VMEM((N_DEV,) + part.shape, F32) for part in parts]
        + [pltpu.SemaphoreType.DMA((n, N_DEV - 1)), pltpu.SemaphoreType.DMA((n, N_DEV - 1))])(*parts)


def _norm_matmul(x, g, wt, name, carry=None):
    t, d = x.shape
    n = wt.shape[0]
    tm, nc = _token_tile(t), _chunk(n)

    def body(x_ref, g_ref, wt_ref, h_ref, o_ref):
        xh, _ = _rms_stats(x_ref[...])
        h = (xh * g_ref[...]).astype(BF16)
        h_ref[...] = h
        for n0 in range(0, n, nc):
            o_ref[:, n0:n0 + nc] = _dot_nt(h, wt_ref[n0:n0 + nc, :]).astype(BF16)

    return _call(body, name=name, grid=(t // tm,),
                 in_specs=[_rows(tm, d), _whole(g.shape), _resident(wt.shape)],
                 out_specs=[_rows(tm, d), _rows(tm, n)],
                 out_shape=[_sds((t, d), BF16), _sds((t, n), BF16)], args=(x, g, wt), carry=carry)


def _fwd_branch(z, caw, cab, lng, lnb, cbw, dc, name, carry=None):
    t = z.shape[0]
    tm = _token_tile(t)
    ka, kb = caw.shape[0], cbw.shape[0]
    taps_a, taps_b = _fwd_taps(ka), _fwd_taps(kb)
    res_a, res_b = _residues(taps_a), _residues(taps_b)
    span = HALO + tm - SUBLANES

    def body(z_ref, caw_ref, cab_ref, lng_ref, lnb_ref, cbw_ref, ac_ref, act_ref, s_ref, cb_ref,
             rot_a, rot_b, wb_a, wb_b, cb):
        @pl.when(pl.program_id(0) == 0)
        def _():
            rot_a[0, 0:HALO, :] = jnp.zeros((HALO, dc), F32)
            rot_b[0, 0:HALO, :] = jnp.zeros((HALO, dc), F32)
            _broadcast_rows(wb_a, caw_ref, ka)
            wb_a[ka] = jnp.broadcast_to(cab_ref[...], (SUBLANES, dc))
            _broadcast_rows(wb_b, cbw_ref, kb)

        a_val = z_ref[:, 0:dc].astype(F32)
        a_gt = z_ref[:, dc:2 * dc].astype(F32)
        rot_a[0, HALO:HALO + tm, :] = a_val * _sigmoid(a_gt)
        _fill_rotations(rot_a, res_a, span)
        _tap_conv(rot_a, taps_a, wb_a, ac_ref, tm, bias_plane=ka)
        ac = ac_ref[...]
        mu = jnp.mean(ac, axis=-1, keepdims=True)
        xc = ac - mu
        var = jnp.mean(xc * xc, axis=-1, keepdims=True)
        ln = xc * lax.rsqrt(var + NORM_EPS) * lng_ref[...] + lnb_ref[...]
        act_ref[...] = (ln * _sigmoid(ln)).astype(BF16)
        rot_a[0, 0:HALO, :] = rot_a[0, tm:tm + HALO, :]

        sc_c = z_ref[:, 3 * dc:4 * dc].astype(F32)
        sc_v = z_ref[:, 4 * dc:5 * dc].astype(F32)
        rot_b[0, HALO:HALO + tm, :] = sc_c * sc_v
        _fill_rotations(rot_b, res_b, span)
        _tap_conv(rot_b, taps_b, wb_b, cb, tm)
        s_ref[...] = (z_ref[:, 2 * dc:3 * dc].astype(F32) * cb[...]).astype(BF16)
        cb_ref[...] = cb[...].astype(BF16)
        rot_b[0, 0:HALO, :] = rot_b[0, tm:tm + HALO, :]

    return _call(body, name=name, grid=(t // tm,),
                 in_specs=[_rows(tm, 5 * dc), _whole(caw.shape), _whole(cab.shape), _whole(lng.shape),
                           _whole(lnb.shape), _whole(cbw.shape)],
                 out_specs=[_rows(tm, dc), _rows(tm, dc), _rows(tm, dc), _rows(tm, dc)],
                 out_shape=[_sds((t, dc), F32), _sds((t, dc), BF16), _sds((t, dc), BF16), _sds((t, dc), BF16)],
                 scratch=[pltpu.VMEM((len(res_a), HALO + tm, dc), F32), pltpu.VMEM((len(res_b), HALO + tm, dc), F32),
                          pltpu.VMEM((ka + 1, SUBLANES, dc), F32), pltpu.VMEM((kb, SUBLANES, dc), F32),
                          pltpu.VMEM((tm, dc), F32)],
                 args=(z, caw, cab, lng, lnb, cbw), carry=carry)


def _fwd_merge(x, z, bg, a_act, s, wa, wb, wo, name, carry=None):
    t, d = x.shape
    n = z.shape[1]
    dc = a_act.shape[1]
    tm = _token_tile(t)
    o5 = n - 2 * d

    def body(x_ref, z_ref, bg_ref, act_ref, s_ref, wa_ref, wb_ref, wo_ref, o_ref):
        ya = _dot(act_ref[...], wa_ref[...])
        yb = _dot(s_ref[...], wb_ref[...])
        ga = _sigmoid(z_ref[:, o5:o5 + d].astype(F32) + bg_ref[:, 0:d])
        gb = _sigmoid(z_ref[:, o5 + d:n].astype(F32) + bg_ref[:, d:2 * d])
        m = (ga * ya + gb * yb).astype(BF16)
        o_ref[...] = x_ref[...] + _dot(m, wo_ref[...])

    return _call(body, name=name, grid=(t // tm,),
                 in_specs=[_rows(tm, d), _rows(tm, n), _whole(bg.shape), _rows(tm, dc), _rows(tm, dc),
                           _resident(wa.shape), _resident(wb.shape), _resident(wo.shape)],
                 out_specs=[_rows(tm, d)], out_shape=[_sds((t, d), F32)],
                 args=(x, z, bg, a_act, s, wa, wb, wo), carry=carry)[0]


def _fwd_down(x, u, cfw, cfb, wd, name, carry=None):
    t, d = x.shape
    f = u.shape[1] // 2
    tm = _token_tile(t, 256)
    kf = cfw.shape[0]

    taps = _fwd_taps(kf)
    residues = _residues(taps)

    def body(x_ref, u_ref, cfw_ref, cfb_ref, wd_ref, o_ref, act_ref, gl_ref, dgl_ref, rot_u, wb, fg):
        @pl.when(pl.program_id(0) == 0)
        def _():
            rot_u[0, 0:HALO, :] = jnp.zeros((HALO, f), F32)
            _broadcast_rows(wb, cfw_ref, kf)
            wb[kf] = jnp.broadcast_to(cfb_ref[...], (SUBLANES, f))

        rot_u[0, HALO:HALO + tm, :] = u_ref[:, 0:f].astype(F32)
        _fill_rotations(rot_u, residues, HALO + tm - SUBLANES)
        _tap_conv(rot_u, taps, wb, fg, tm, bias_plane=kf)
        y = x_ref[...]
        for c0, cw in _lane_chunks(f, 256):
            at = slice(c0, c0 + cw)
            gl, dgl = _gelu_tanh(fg[:, at])
            gl_ref[:, at] = gl.astype(BF16)
            dgl_ref[:, at] = dgl.astype(BF16)
            act = (gl * u_ref[:, f + c0:f + c0 + cw].astype(F32)).astype(BF16)
            act_ref[:, at] = act
            y = y + _dot(act, wd_ref[at, :])
        o_ref[...] = y
        rot_u[0, 0:HALO, :] = rot_u[0, tm:tm + HALO, :]

    return _call(body, name=name, grid=(t // tm,),
                 in_specs=[_rows(tm, d), _rows(tm, 2 * f), _whole(cfw.shape), _whole(cfb.shape),
                           _resident(wd.shape)],
                 out_specs=[_rows(tm, d), _rows(tm, f), _rows(tm, f), _rows(tm, f)],
                 out_shape=[_sds((t, d), F32), _sds((t, f), BF16), _sds((t, f), BF16), _sds((t, f), BF16)],
                 scratch=[pltpu.VMEM((len(residues), HALO + tm, f), F32), pltpu.VMEM((kf + 1, SUBLANES, f), F32),
                          pltpu.VMEM((tm, f), F32)],
                 args=(x, u, cfw, cfb, wd), carry=carry)


def _fwd_ple(x, g, wpg, p, wple, name, carry=None):
    t, d = x.shape
    pd = p.stack.shape[-1]
    tm = _token_tile(t)

    def body(x_ref, g_ref, wpg_ref, p_ref, wple_ref, o_ref):
        xv = x_ref[...]
        xh, _ = _rms_stats(xv)
        lg = _dot((xh * g_ref[...]).astype(BF16), wpg_ref[...])
        pp = _dot(p_ref[...].astype(BF16), wple_ref[...])
        o_ref[...] = xv + _sigmoid(lg) * pp

    return _call(body, name=name, grid=(t // tm,),
                 in_specs=[_rows(tm, d), _whole(g.shape), _resident(wpg.shape), p.rows(tm),
                           _resident(wple.shape)],
                 out_specs=[_rows(tm, d)], out_shape=[_sds((t, d), F32)],
                 args=(x, g, wpg, p, wple), carry=carry)[0]


def _loss_bwd(x, g, target, name):
    t, d = x.shape
    tm = _token_tile(t)

    def body(x_ref, g_ref, t_ref, dx_ref, dg_ref, loss_ref):
        @pl.when(pl.program_id(0) == 0)
        def _():
            dg_ref[...] = jnp.zeros_like(dg_ref)
            loss_ref[...] = jnp.zeros_like(loss_ref)

        xh, r = _rms_stats(x_ref[...])
        err = xh * g_ref[...] - t_ref[...]
        sq = jnp.sum(jnp.sum(err * err, axis=0, keepdims=True), axis=1, keepdims=True)
        loss_ref[...] += jnp.broadcast_to(0.5 * sq / d, loss_ref.shape)
        dx, dg = _rms_bwd(err / d, xh, r, g_ref[...])
        dx_ref[...] = dx
        dg_ref[...] += dg

    return _call(body, name=name, grid=(t // tm,),
                 in_specs=[_rows(tm, d), _whole(g.shape), _rows(tm, d)],
                 out_specs=[_rows(tm, d), _whole((1, d)), _whole((SUBLANES, LANES))],
                 out_shape=[_sds((t, d), F32), _sds((1, d), F32), _sds((SUBLANES, LANES), F32)],
                 args=(x, g, target))


def _bwd_ple(dy, x, g, wpg, p, wple, name, carry=None):
    t, d = x.shape
    pd = p.stack.shape[-1]
    tm = _token_tile(t)
    nt = t // tm

    def body(dy_ref, x_ref, g_ref, wpg_ref, p_ref, wple_ref, dx_ref, dwpg_ref, dwple_ref, dg_ref,
             acc_pg, acc_ple):
        i = pl.program_id(0)

        @pl.when(i == 0)
        def _():
            acc_pg[...] = jnp.zeros_like(acc_pg)
            acc_ple[...] = jnp.zeros_like(acc_ple)
            dg_ref[...] = jnp.zeros_like(dg_ref)

        dyv = dy_ref[...]
        xh, r = _rms_stats(x_ref[...])
        h = (xh * g_ref[...]).astype(BF16)
        pb = p_ref[...].astype(BF16)
        pg = _sigmoid(_dot(h, wpg_ref[...]))
        pp = _dot(pb, wple_ref[...])
        dpp = (dyv * pg).astype(BF16)
        dlg = (dyv * pp * pg * (1.0 - pg)).astype(BF16)
        acc_ple[...] += _dot_tn(pb, dpp)
        acc_pg[...] += _dot_tn(h, dlg)
        dx, dg = _rms_bwd(_dot_nt(dlg, wpg_ref[...]), xh, r, g_ref[...])
        dx_ref[...] = dyv + dx
        dg_ref[...] += dg

        @pl.when(i == nt - 1)
        def _():
            dwpg_ref[...] = acc_pg[...].astype(BF16)
            dwple_ref[...] = acc_ple[...].astype(BF16)

    return _call(body, name=name, grid=(nt,),
                 in_specs=[_rows(tm, d), _rows(tm, d), _whole(g.shape), _resident(wpg.shape), p.rows(tm),
                           _resident(wple.shape)],
                 out_specs=[_rows(tm, d), _whole((d, d)), _whole((pd, d)), _whole((1, d))],
                 out_shape=[_sds((t, d), F32), _sds((d, d), BF16), _sds((pd, d), BF16), _sds((1, d), F32)],
                 scratch=[pltpu.VMEM((d, d), F32), pltpu.VMEM((pd, d), F32)],
                 args=(dy, x, g, wpg, p, wple), carry=carry)


def _bwd_down(dy, u, gl, dgl, cfw, wd, name, carry=None):
    t, d = dy.shape
    f = u.shape[1] // 2
    tm = _token_tile(t, 256)
    nt = t // tm
    kf = cfw.shape[0]
    bwd = _bwd_taps(kf)
    residues = _residues(bwd)

    def body(dy_ref, u_ref, gl_ref, dgl_ref, cfw_ref, wd_ref, du_ref, dcw_ref, dcb_ref, rot_g, wb, acc, buf):
        i = pl.program_id(0)

        @pl.when(i == 0)
        def _():
            for ref in (dcw_ref, dcb_ref, acc):
                ref[...] = jnp.zeros_like(ref)
            rot_g[0, tm:tm + HALO, :] = jnp.zeros((HALO, f), F32)
            _broadcast_rows(wb, cfw_ref, kf)

        dyb = dy_ref[...].astype(BF16)
        for c0, cw in _lane_chunks(f, 256):
            at = slice(c0, c0 + cw)
            df = _dot_nt(dyb, wd_ref[at, :])
            du_ref[:, f + c0:f + c0 + cw] = (df * gl_ref[:, at].astype(F32)).astype(BF16)
            dfg = df * u_ref[:, f + c0:f + c0 + cw].astype(F32) * dgl_ref[:, at].astype(F32)
            dcb_ref[:, at] += jnp.sum(dfg, axis=0, keepdims=True)
            rot_g[0, 0:tm, at] = dfg
        _fill_rotations(rot_g, residues, HALO + tm - SUBLANES)
        buf[...] = u_ref[:, 0:f].astype(F32)
        _tap_wgrad(rot_g, bwd, buf, acc, tm)
        _tap_conv(rot_g, bwd, wb, buf, tm)
        du_ref[:, 0:f] = buf[...].astype(BF16)
        rot_g[0, tm:tm + HALO, :] = rot_g[0, 0:HALO, :]

        @pl.when(i == nt - 1)
        def _():
            dcw_ref[0:kf, :] = jnp.sum(acc[...], axis=1)

    return _call(body, name=name, grid=(nt,),
                 in_specs=[_rows_rev(tm, d, nt), _rows_rev(tm, 2 * f, nt), _rows_rev(tm, f, nt),
                           _rows_rev(tm, f, nt), _whole(cfw.shape), _resident(wd.shape)],
                 out_specs=[_rows_rev(tm, 2 * f, nt), _whole((SUBLANES, f)), _whole((1, f))],
                 out_shape=[_sds((t, 2 * f), BF16), _sds((SUBLANES, f), F32), _sds((1, f), F32)],
                 scratch=[pltpu.VMEM((len(residues), HALO + tm, f), F32), pltpu.VMEM((kf, SUBLANES, f), F32),
                          pltpu.VMEM((kf, SUBLANES, f), F32), pltpu.VMEM((tm, f), F32)],
                 args=(dy, u, gl, dgl, cfw, wd), carry=carry)


def _bwd_norm_matmul(dout, wt, x, g, dres, name, carry=None):
    t, d = x.shape
    n = dout.shape[1]
    tm = _token_tile(t)

    def body(do_ref, wt_ref, x_ref, g_ref, dres_ref, dx_ref, dg_ref):
        @pl.when(pl.program_id(0) == 0)
        def _():
            dg_ref[...] = jnp.zeros_like(dg_ref)

        dh = _dot(do_ref[...], wt_ref[...])
        xh, r = _rms_stats(x_ref[...])
        dx, dg = _rms_bwd(dh, xh, r, g_ref[...])
        dx_ref[...] = dres_ref[...] + dx
        dg_ref[...] += dg

    return _call(body, name=name, grid=(t // tm,),
                 in_specs=[_rows(tm, n), _resident(wt.shape), _rows(tm, d), _whole(g.shape), _rows(tm, d)],
                 out_specs=[_rows(tm, d), _whole((1, d))],
                 out_shape=[_sds((t, d), F32), _sds((1, d), F32)],
                 args=(dout, wt, x, g, dres), carry=carry)


def _wgrad_tn(a, b, name, carry=None):
    t, n = a.shape
    d = b.shape[1]
    tt = 1024 if t % 1024 == 0 else _token_tile(t)
    tn = _chunk(n, 1536)
    nt = t // tt

    def body(a_ref, b_ref, o_ref, acc):
        k = pl.program_id(1)

        @pl.when(k == 0)
        def _():
            acc[...] = jnp.zeros_like(acc)

        acc[...] += _dot_tn(a_ref[...].astype(BF16), b_ref[...].astype(BF16))

        @pl.when(k == nt - 1)
        def _():
            o_ref[...] = acc[...].astype(BF16)

    return _call(body, name=name, grid=(n // tn, nt),
                 in_specs=[pl.BlockSpec((tt, tn), lambda j, k: (k, j)), pl.BlockSpec((tt, d), lambda j, k: (k, 0))],
                 out_specs=[pl.BlockSpec((tn, d), lambda j, k: (j, 0))], out_shape=[_sds((n, d), BF16)],
                 scratch=[pltpu.VMEM((tn, d), F32)], args=(a, b), carry=carry)[0]


def _bwd_merge(dy, z, bg, a_act, s, wa, wb, wo, name, carry=None):
    t, d = dy.shape
    n = z.shape[1]
    dc = a_act.shape[1]
    tm = _token_tile(t)
    nt = t // tm
    o5 = n - 2 * d

    def body(dy_ref, z_ref, bg_ref, act_ref, s_ref, wa_ref, wb_ref, wo_ref,
             dact_ref, ds_ref, dgl_ref, dwo_ref, dwa_ref, dwb_ref, dbg_ref, acc_o, acc_a, acc_b):
        i = pl.program_id(0)

        @pl.when(i == 0)
        def _():
            acc_o[...] = jnp.zeros_like(acc_o)
            acc_a[...] = jnp.zeros_like(acc_a)
            acc_b[...] = jnp.zeros_like(acc_b)
            dbg_ref[...] = jnp.zeros_like(dbg_ref)

        dyb = dy_ref[...].astype(BF16)
        dm = _dot_nt(dyb, wo_ref[...])
        ya = _dot(act_ref[...], wa_ref[...])
        yb = _dot(s_ref[...], wb_ref[...])
        ga = _sigmoid(z_ref[:, o5:o5 + d].astype(F32) + bg_ref[:, 0:d])
        gb = _sigmoid(z_ref[:, o5 + d:n].astype(F32) + bg_ref[:, d:2 * d])
        acc_o[...] += _dot_tn((ga * ya + gb * yb).astype(BF16), dyb)
        dya = (dm * ga).astype(BF16)
        dyb2 = (dm * gb).astype(BF16)
        acc_a[...] += _dot_tn(act_ref[...], dya)
        acc_b[...] += _dot_tn(s_ref[...], dyb2)
        dact_ref[...] = _dot_nt(dya, wa_ref[...])
        ds_ref[...] = _dot_nt(dyb2, wb_ref[...])
        dla = dm * ya * ga * (1.0 - ga)
        dlb = dm * yb * gb * (1.0 - gb)
        dgl_ref[:, 0:d] = dla.astype(BF16)
        dgl_ref[:, d:2 * d] = dlb.astype(BF16)
        dbg_ref[:, 0:d] += jnp.sum(dla, axis=0, keepdims=True)
        dbg_ref[:, d:2 * d] += jnp.sum(dlb, axis=0, keepdims=True)

        @pl.when(i == nt - 1)
        def _():
            dwo_ref[...] = acc_o[...].astype(BF16)
            dwa_ref[...] = acc_a[...].astype(BF16)
            dwb_ref[...] = acc_b[...].astype(BF16)

    return _call(body, name=name, grid=(nt,),
                 in_specs=[_rows(tm, d), _rows(tm, n), _whole(bg.shape), _rows(tm, dc), _rows(tm, dc),
                           _resident(wa.shape), _resident(wb.shape), _resident(wo.shape)],
                 out_specs=[_rows(tm, dc), _rows(tm, dc), _rows(tm, 2 * d), _whole((d, d)), _whole((dc, d)),
                            _whole((dc, d)), _whole((1, 2 * d))],
                 out_shape=[_sds((t, dc), F32), _sds((t, dc), F32), _sds((t, 2 * d), BF16), _sds((d, d), BF16),
                            _sds((dc, d), BF16), _sds((dc, d), BF16), _sds((1, 2 * d), F32)],
                 scratch=[pltpu.VMEM((d, d), F32), pltpu.VMEM((dc, d), F32), pltpu.VMEM((dc, d), F32)],
                 args=(dy, z, bg, a_act, s, wa, wb, wo), carry=carry)


def _bwd_branch(dact, ds, z, dgl, a_conv, cb, caw, lng, lnb, cbw, name, carry=None):
    t, n = z.shape
    dc = a_conv.shape[1]
    tm = _token_tile(t)
    nt = t // tm
    ka, kb = caw.shape[0], cbw.shape[0]
    bwd_a, bwd_b = _bwd_taps(ka), _bwd_taps(kb)
    span = HALO + tm - SUBLANES

    def body(dact_ref, ds_ref, z_ref, dgl_ref, ac_ref, cb_ref, caw_ref, lng_ref, lnb_ref, cbw_ref,
             dz_ref, dcaw_ref, dcab_ref, dlng_ref, dlnb_ref, dcbw_ref,
             rot_da, rot_dc, wb_a, wb_b, acc_a, acc_b, buf):
        i = pl.program_id(0)

        @pl.when(i == 0)
        def _():
            for ref in (dcaw_ref, dcab_ref, dlng_ref, dlnb_ref, dcbw_ref, acc_a, acc_b):
                ref[...] = jnp.zeros_like(ref)
            rot_da[0, tm:tm + HALO, :] = jnp.zeros((HALO, dc), F32)
            rot_dc[0, tm:tm + HALO, :] = jnp.zeros((HALO, dc), F32)
            _broadcast_rows(wb_a, caw_ref, ka)
            _broadcast_rows(wb_b, cbw_ref, kb)

        a_val = z_ref[:, 0:dc].astype(F32)
        sg = _sigmoid(z_ref[:, dc:2 * dc].astype(F32))

        ac = ac_ref[...]
        mu = jnp.mean(ac, axis=-1, keepdims=True)
        xc = ac - mu
        rstd = lax.rsqrt(jnp.mean(xc * xc, axis=-1, keepdims=True) + NORM_EPS)
        xh = xc * rstd
        ln = xh * lng_ref[...] + lnb_ref[...]
        sl = _sigmoid(ln)
        dln = dact_ref[...] * (sl * (1.0 + ln * (1.0 - sl)))
        dlng_ref[...] += jnp.sum(dln * xh, axis=0, keepdims=True)
        dlnb_ref[...] += jnp.sum(dln, axis=0, keepdims=True)
        dxh = dln * lng_ref[...]
        dac = rstd * (dxh - jnp.mean(dxh, axis=-1, keepdims=True)
                      - xh * jnp.mean(dxh * xh, axis=-1, keepdims=True))
        dcab_ref[...] += jnp.sum(dac, axis=0, keepdims=True)
        rot_da[0, 0:tm, :] = dac
        _fill_rotations(rot_da, _residues(bwd_a), span)
        buf[...] = a_val * sg
        _tap_wgrad(rot_da, bwd_a, buf, acc_a, tm)
        _tap_conv(rot_da, bwd_a, wb_a, buf, tm)
        rot_da[0, tm:tm + HALO, :] = rot_da[0, 0:HALO, :]
        da = buf[...]
        dz_ref[:, 0:dc] = (da * sg).astype(BF16)
        dz_ref[:, dc:2 * dc] = (da * a_val * sg * (1.0 - sg)).astype(BF16)

        sc_b = z_ref[:, 2 * dc:3 * dc].astype(F32)
        sc_c = z_ref[:, 3 * dc:4 * dc].astype(F32)
        sc_v = z_ref[:, 4 * dc:5 * dc].astype(F32)
        dsv = ds_ref[...]
        dz_ref[:, 2 * dc:3 * dc] = (dsv * cb_ref[...].astype(F32)).astype(BF16)
        rot_dc[0, 0:tm, :] = dsv * sc_b
        _fill_rotations(rot_dc, _residues(bwd_b), span)
        buf[...] = sc_c * sc_v
        _tap_wgrad(rot_dc, bwd_b, buf, acc_b, tm)
        _tap_conv(rot_dc, bwd_b, wb_b, buf, tm)
        rot_dc[0, tm:tm + HALO, :] = rot_dc[0, 0:HALO, :]
        dcv = buf[...]
        dz_ref[:, 3 * dc:4 * dc] = (dcv * sc_v).astype(BF16)
        dz_ref[:, 4 * dc:5 * dc] = (dcv * sc_c).astype(BF16)
        dz_ref[:, 5 * dc:n] = dgl_ref[...]

        @pl.when(i == nt - 1)
        def _():
            dcaw_ref[0:ka, :] = jnp.sum(acc_a[...], axis=1)
            dcbw_ref[0:kb, :] = jnp.sum(acc_b[...], axis=1)

    def planes(taps):
        return pltpu.VMEM((len(_residues(taps)), HALO + tm, dc), F32)

    return _call(body, name=name, grid=(nt,),
                 in_specs=[_rows_rev(tm, dc, nt), _rows_rev(tm, dc, nt), _rows_rev(tm, 5 * dc, nt),
                           _rows_rev(tm, n - 5 * dc, nt), _rows_rev(tm, dc, nt), _rows_rev(tm, dc, nt),
                           _whole(caw.shape), _whole(lng.shape), _whole(lnb.shape), _whole(cbw.shape)],
                 out_specs=[_rows_rev(tm, n, nt), _whole((HALO, dc)), _whole((1, dc)), _whole((1, dc)),
                            _whole((1, dc)), _whole((SUBLANES, dc))],
                 out_shape=[_sds((t, n), BF16), _sds((HALO, dc), F32), _sds((1, dc), F32), _sds((1, dc), F32),
                            _sds((1, dc), F32), _sds((SUBLANES, dc), F32)],
                 scratch=[planes(bwd_a), planes(bwd_b),
                          pltpu.VMEM((ka, SUBLANES, dc), F32), pltpu.VMEM((kb, SUBLANES, dc), F32),
                          pltpu.VMEM((ka, SUBLANES, dc), F32), pltpu.VMEM((kb, SUBLANES, dc), F32),
                          pltpu.VMEM((tm, dc), F32)],
                 args=(dact, ds, z, dgl, a_conv, cb, caw, lng, lnb, cbw), carry=carry)


def _land_specs(depth, nr, tr, cols):
    def spec(k):
        return pl.BlockSpec((N_DEV, tr, cols), lambda i: (0, jnp.clip(i - k * nr, 0, nr - 1), 0))
    return [spec(k) for k in range(depth)]


def _adamw_math(w, g, m, v):
    nm = ADAM_B1 * m + (1.0 - ADAM_B1) * g
    nv = ADAM_B2 * v + (1.0 - ADAM_B2) * (g * g)
    m_hat = nm / (1.0 - ADAM_B1 ** ADAM_STEP)
    v_hat = nv / (1.0 - ADAM_B2 ** ADAM_STEP)
    return -ADAM_LR * (m_hat / (jnp.sqrt(v_hat) + ADAM_EPS) + ADAM_WD * w), nm, nv


def _sum_adamw(lands, w, m, v, name):
    _, rows, cols = lands[0].shape
    tr = _row_tile(rows)
    nr = rows // tr
    depth = len(lands)

    def body(*refs):
        w_ref, m_ref, v_ref, g_ref, d_ref, nm_ref, nv_ref = refs[depth:]
        i = pl.program_id(0)
        for k in range(depth):
            @pl.when(i // nr == k)
            def _(k=k):
                acc = refs[k][0].astype(F32)
                for j in range(1, N_DEV):
                    acc = acc + refs[k][j].astype(F32)
                g_ref[...] = acc
                d_ref[...], nm_ref[...], nv_ref[...] = _adamw_math(w_ref[...], acc, m_ref[...], v_ref[...])

    spec = _rows(tr, cols)
    return _call(body, name=name, grid=(depth * nr,), in_specs=_land_specs(depth, nr, tr, cols) + [spec] * 3,
                 out_specs=[spec] * 4, out_shape=[_sds((depth * rows, cols), F32)] * 4, args=(*lands, w, m, v))


def _adamw_small(ws, gs, ms, vs, name):
    n = len(ws)

    def body(*refs):
        w_refs, g_refs, m_refs, v_refs = refs[:n], refs[n:2 * n], refs[2 * n:3 * n], refs[3 * n:4 * n]
        d_refs, nm_refs, nv_refs = refs[4 * n:5 * n], refs[5 * n:6 * n], refs[6 * n:]
        for k in range(n):
            d_refs[k][...], nm_refs[k][...], nv_refs[k][...] = _adamw_math(
                w_refs[k][...], g_refs[k][...], m_refs[k][...], v_refs[k][...])

    vmem = pl.BlockSpec(memory_space=pltpu.VMEM)
    outs = pl.pallas_call(
        body, name=name, in_specs=[vmem] * (4 * n), out_specs=[vmem] * (3 * n),
        out_shape=[_sds(a.shape, F32) for a in ws] * 3)(*ws, *gs, *ms, *vs)
    return outs[:n], outs[n:2 * n], outs[2 * n:]


def kernel(x, p, g_mix, w_in, b_gate, conv_a_w, conv_a_b, ln_a_g, ln_a_b, w_a_out, conv_b_w, w_b_out, w_o, g_ffn, w_up, conv_f_w, conv_f_b, w_down, g_ple, w_ple, w_ple_gate, g_final, loss_target, m_g_mix, m_w_in, m_b_gate, m_conv_a_w, m_conv_a_b, m_ln_a_g, m_ln_a_b, m_w_a_out, m_conv_b_w, m_w_b_out, m_w_o, m_g_ffn, m_w_up, m_conv_f_w, m_conv_f_b, m_w_down, m_g_ple, m_w_ple, m_w_ple_gate, m_g_final, v_g_mix, v_w_in, v_b_gate, v_conv_a_w, v_conv_a_b, v_ln_a_g, v_ln_a_b, v_w_a_out, v_conv_b_w, v_w_b_out, v_w_o, v_g_ffn, v_w_up, v_conv_f_w, v_conv_f_b, v_w_down, v_g_ple, v_w_ple, v_w_ple_gate, v_g_final):
    w = dict(zip(WEIGHT_NAMES, (g_mix, w_in, b_gate, conv_a_w, conv_a_b, ln_a_g, ln_a_b, w_a_out, conv_b_w,
                                w_b_out, w_o, g_ffn, w_up, conv_f_w, conv_f_b, w_down, g_ple, w_ple,
                                w_ple_gate, g_final)))
    mom = dict(zip(WEIGHT_NAMES, (m_g_mix, m_w_in, m_b_gate, m_conv_a_w, m_conv_a_b, m_ln_a_g, m_ln_a_b,
                                  m_w_a_out, m_conv_b_w, m_w_b_out, m_w_o, m_g_ffn, m_w_up, m_conv_f_w,
                                  m_conv_f_b, m_w_down, m_g_ple, m_w_ple, m_w_ple_gate, m_g_final)))
    var = dict(zip(WEIGHT_NAMES, (v_g_mix, v_w_in, v_b_gate, v_conv_a_w, v_conv_a_b, v_ln_a_g, v_ln_a_b,
                                  v_w_a_out, v_conv_b_w, v_w_b_out, v_w_o, v_g_ffn, v_w_up, v_conv_f_w,
                                  v_conv_f_b, v_w_down, v_g_ple, v_w_ple, v_w_ple_gate, v_g_final)))
    depth = g_mix.shape[0]
    dc = ln_a_g.shape[1]
    me = _my_index()
    x0 = x[0]
    target = loss_target[0]
    big_names = tuple(BIG_AXIS)

    shard = {name: (jnp.swapaxes(w[name], 1, 2) if name in TRANSPOSED else w[name]).astype(BF16)
             for name in big_names}

    def gather_of(layer, *names):
        return _Gather([(shard[name], layer, BIG_AXIS[name]) for name in names])

    def row(name, layer):
        return _Layer(w[name], layer)

    first = _Gather([(shard['w_in'], 0, BIG_AXIS['w_in'])] + [(w[name][None], None, 0) for name in CONV_SHARDED])
    gathered = _run_exchange(first, "gather_first")
    w_in_full = gathered[0]
    conv_full = {name: jnp.transpose(g, (1, 2, 0, 3)).reshape(g.shape[1], g.shape[2], -1)
                 for name, g in zip(CONV_SHARDED, gathered[1:])}
    saved = []
    xc = x0
    for l in range(depth):
        carry = gather_of(l, 'w_a_out', 'w_b_out', 'w_o', 'w_up')
        h, z = _norm_matmul(xc, row('g_mix', l), w_in_full, f"fwd_in_{l}", carry)
        wa_full, wb_full, wo_full, w_up_full = carry.results
        a_conv, a_act, s, cb = _fwd_branch(z, _Layer(conv_full['conv_a_w'], l), row('conv_a_b', l), row('ln_a_g', l),
                                       row('ln_a_b', l), _Layer(conv_full['conv_b_w'], l), dc, f"fwd_branch_{l}")
        x1 = _fwd_merge(xc, z, row('b_gate', l), a_act, s, wa_full, wb_full, wo_full, f"fwd_merge_{l}")
        carry = _Gather([(shard[name], l, BIG_AXIS[name]) for name in ('w_down', 'w_ple', 'w_ple_gate')]
                        + ([(shard['w_in'], l + 1, BIG_AXIS['w_in'])] if l + 1 < depth else []))
        h2, u = _norm_matmul(x1, row('g_ffn', l), w_up_full, f"fwd_up_{l}", carry)
        w_down_full, w_ple_full, w_pg_full = carry.results[:3]
        x2, act, gl, dgl = _fwd_down(x1, u, _Layer(conv_full['conv_f_w'], l), row('conv_f_b', l), w_down_full,
                                     f"fwd_down_{l}")
        x3 = _fwd_ple(x2, row('g_ple', l), w_pg_full, _LayerTokens(p, l), w_ple_full, f"fwd_ple_{l}")
        saved.append((xc, h, z, a_conv, a_act, s, cb, x1, h2, u, act, gl, dgl, x2,
                      dict(w_in=w_in_full, w_a_out=wa_full, w_b_out=wb_full, w_o=wo_full, w_up=w_up_full,
                           w_down=w_down_full, w_ple=w_ple_full, w_ple_gate=w_pg_full)))
        if l + 1 < depth:
            w_in_full = carry.results[3]
        xc = x3

    dx, dg_final, loss_part = _loss_bwd(xc, g_final[None], target, "loss_bwd")
    landed = {name: [None] * depth for name in big_names}
    small = {name: [None] * depth for name in WEIGHT_NAMES if name not in BIG_AXIS and name != 'g_final'}

    def scatter_of(*partials):
        ex = _Scatter([(part, BIG_AXIS[name]) for name, _, part in partials])
        ex.places = [(name, layer) for name, layer, _ in partials]
        return ex

    def keep(ex):
        for (name, layer), land in zip(ex.places, ex.results):
            landed[name][layer] = land

    pending = []
    for l in reversed(range(depth)):
        xin, h, z, a_conv, a_act, s, cb, x1, h2, u, act, gl, dgl, x2, full = saved[l]
        dx2, d_wpg, d_wple, small['g_ple'][l] = _bwd_ple(
            dx, x2, row('g_ple', l), full['w_ple_gate'], _LayerTokens(p, l), full['w_ple'], f"bwd_ple_{l}")
        d_wdown = _wgrad_tn(act, dx2, f"wgrad_down_{l}")
        carry = scatter_of(('w_ple_gate', l, d_wpg), ('w_ple', l, d_wple), ('w_down', l, d_wdown))
        du, small['conv_f_w'][l], small['conv_f_b'][l] = _bwd_down(
            dx2, u, gl, dgl, _Layer(conv_full['conv_f_w'], l), full['w_down'], f"bwd_down_{l}", carry)
        keep(carry)
        carry = scatter_of(*pending) if pending else None
        pending = []
        dx1, small['g_ffn'][l] = _bwd_norm_matmul(du, full['w_up'], x1, row('g_ffn', l), dx2, f"bwd_up_{l}", carry)
        if carry is not None:
            keep(carry)
        d_wup = _wgrad_tn(du, h2, f"wgrad_up_{l}")
        dact, ds, dgate, d_wo, d_wa, d_wb, small['b_gate'][l] = _bwd_merge(
            dx1, z, row('b_gate', l), a_act, s, full['w_a_out'], full['w_b_out'], full['w_o'], f"bwd_merge_{l}")
        carry = scatter_of(('w_up', l, d_wup))
        (dz, small['conv_a_w'][l], small['conv_a_b'][l], small['ln_a_g'][l], small['ln_a_b'][l],
         small['conv_b_w'][l]) = _bwd_branch(
            dact, ds, z, dgate, a_conv, cb, _Layer(conv_full['conv_a_w'], l), row('ln_a_g', l), row('ln_a_b', l),
            _Layer(conv_full['conv_b_w'], l), f"bwd_branch_{l}", carry)
        keep(carry)
        carry = scatter_of(('w_o', l, d_wo), ('w_a_out', l, d_wa), ('w_b_out', l, d_wb))
        d_win = _wgrad_tn(dz, h, f"wgrad_in_{l}", carry)
        keep(carry)
        carry = scatter_of(('w_in', l, d_win)) if l == 0 else None
        if l > 0:
            pending = [('w_in', l, d_win)]
        dx, small['g_mix'][l] = _bwd_norm_matmul(dz, full['w_in'], xin, row('g_mix', l), dx1, f"bwd_in_{l}", carry)
        if carry is not None:
            keep(carry)
    grad_x = dx[None]

    small_names = tuple(small)
    parts = [part for name in small_names for part in small[name]] + [dg_final, loss_part]
    plan = [tuple(range(k * depth, (k + 1) * depth)) for k in range(len(small_names))]
    plan += [(len(parts) - 2,), (len(parts) - 1,)]
    reduced = _all_reduce(parts, plan, "all_reduce_small")
    loss = reduced[-1][0, 0]
    grads = dict(zip(small_names, reduced[:len(small_names)]))
    grads['g_final'] = reduced[len(small_names)].reshape(g_final.shape)
    for name in CONV_SHARDED:
        _, taps, width = w[name].shape
        grads[name] = lax.dynamic_slice(grads[name], (0, 0, me * width), (depth, taps, width))

    delta, new_m, new_v = {}, {}, {}
    for name in big_names:
        view = (lambda a: jnp.swapaxes(a, 1, 2)) if name in TRANSPOSED else (lambda a: a)
        shape = view(w[name]).shape
        flat = lambda a: view(a).reshape(-1, shape[-1])
        lands = [land.reshape(N_DEV, -1, shape[-1]) for land in landed[name]]
        outs = _sum_adamw(lands, flat(w[name]), flat(mom[name]), flat(var[name]), f"adamw_{name}")
        grads[name], delta[name], new_m[name], new_v[name] = [view(a.reshape(shape)) for a in outs]
    rest = tuple(name for name in WEIGHT_NAMES if name not in BIG_AXIS)
    as_2d = lambda a: a.reshape(1, -1) if a.ndim == 1 else a
    outs = _adamw_small(*[[as_2d(src[name]) for name in rest] for src in (w, grads, mom, var)], "adamw_small")
    for dst, values in zip((delta, new_m, new_v), outs):
        dst.update({name: value.reshape(w[name].shape) for name, value in zip(rest, values)})

    return (loss, grad_x, *[grads[n] for n in WEIGHT_NAMES], *[delta[n] for n in WEIGHT_NAMES],
            *[new_m[n] for n in WEIGHT_NAMES], *[new_v[n] for n in WEIGHT_NAMES])
```

```python
import jax
import jax.numpy as jnp
from jax import lax
from jax.experimental import pallas as pl
from jax.experimental.pallas import tpu as pltpu

F32 = jnp.float32
BF16 = jnp.bfloat16
MESH = pl.DeviceIdType.MESH

N_DEV = 8
NORM_EPS = 1e-6
HALO = 32
LANES = 128
SUBLANES = 8
VMEM_LIMIT_BYTES = 56 * 2**20

ADAM_LR = 0.001
ADAM_B1 = 0.9
ADAM_B2 = 0.999
ADAM_EPS = 1e-08
ADAM_WD = 0.01
ADAM_STEP = 10

WEIGHT_NAMES = ('g_mix', 'w_in', 'b_gate', 'conv_a_w', 'conv_a_b', 'ln_a_g', 'ln_a_b', 'w_a_out',
                'conv_b_w', 'w_b_out', 'w_o', 'g_ffn', 'w_up', 'conv_f_w', 'conv_f_b', 'w_down',
                'g_ple', 'w_ple', 'w_ple_gate', 'g_final')
BIG_AXIS = {'w_in': 0, 'w_up': 0, 'w_a_out': 1, 'w_b_out': 1, 'w_o': 0, 'w_down': 0, 'w_ple': 1,
            'w_ple_gate': 0}
TRANSPOSED = ('w_in', 'w_up')
CONV_SHARDED = ('conv_a_w', 'conv_b_w', 'conv_f_w')


def _dot(a, b):
    return jnp.dot(a, b, preferred_element_type=F32)


def _dot_nt(a, b):
    return lax.dot_general(a, b, (((1,), (1,)), ((), ())), preferred_element_type=F32)


def _dot_tn(a, b):
    return lax.dot_general(a, b, (((0,), (0,)), ((), ())), preferred_element_type=F32)


def _sigmoid(v):
    return jax.nn.sigmoid(v)


def _token_tile(t, cap=512):
    return cap if (t % cap == 0 and t > 512) else 128


def _chunk(n, limit=512):
    for c in range(limit - limit % LANES, 0, -LANES):
        if n % c == 0:
            return c
    return n


def _row_tile(rows):
    for c in (512, 256, 128, 64, 32, 16, 8):
        if rows % c == 0:
            return c
    return rows


def _rows(tm, width):
    return pl.BlockSpec((tm, width), lambda i: (i, 0))


def _rows_rev(tm, width, nt):
    return pl.BlockSpec((tm, width), lambda i: (nt - 1 - i, 0))


def _whole(shape):
    nd = len(shape)
    return pl.BlockSpec(tuple(shape), lambda i: (0,) * nd)


def _resident(shape):
    nd = len(shape)
    return pl.BlockSpec(tuple(shape), lambda i: (0,) * nd, pipeline_mode=pl.Buffered(1))


def _sds(shape, dtype):
    return jax.ShapeDtypeStruct(tuple(shape), dtype)


def _rms_stats(xv):
    r = lax.rsqrt(jnp.mean(xv * xv, axis=-1, keepdims=True) + NORM_EPS)
    return xv * r, r


def _rms_bwd(dy, xh, r, g):
    dxh = dy * g
    dx = r * (dxh - xh * jnp.mean(dxh * xh, axis=-1, keepdims=True))
    return dx, jnp.sum(dy * xh, axis=0, keepdims=True)


GELU_C0 = 0.7978845608028654
GELU_C1 = GELU_C0 * 0.044715


def _gelu_tanh(v):
    v2 = v * v
    t = jnp.tanh(v * (GELU_C0 + GELU_C1 * v2))
    q = 1.0 + t
    hv = 0.5 * v
    grad = 0.5 * q + hv * (1.0 - t * t) * (GELU_C0 + (3.0 * GELU_C1) * v2)
    return hv * q, grad


ROW_CHUNK = 32
LANE_CHUNK = 512
MXU_COLUMNS = 256


def _residues(taps):
    return [0] + sorted({off % SUBLANES for _, off in taps} - {0})


def _fill_rotations(rot_ref, residues, length):
    for plane, r in enumerate(residues):
        if r:
            rot_ref[plane, 0:length, :] = rot_ref[0, pl.ds(r, length), :]


def _broadcast_rows(dst_ref, src_ref, count):
    for k in range(count):
        dst_ref[k] = jnp.broadcast_to(src_ref[k:k + 1, :], dst_ref.shape[1:])


def _lane_chunks(width, chunk=LANE_CHUNK):
    return [(c0, min(chunk, width - c0)) for c0 in range(0, width, chunk)]


def _tap_conv(rot_ref, taps, wb_ref, out_ref, tm, bias_plane=None):
    residues = _residues(taps)
    plane = {r: p for p, r in enumerate(residues)}
    blocks = ROW_CHUNK // SUBLANES
    width = out_ref.shape[1]

    def chunk(c, state):
        r0 = c * ROW_CHUNK
        for c0, cw in _lane_chunks(width):
            accs = [None if bias_plane is None else wb_ref[bias_plane, :, c0:c0 + cw]] * blocks
            for k, off in taps:
                wk = wb_ref[k, :, c0:c0 + cw]
                base = off - off % SUBLANES
                for j in range(blocks):
                    at = pl.multiple_of(r0 + base + SUBLANES * j, SUBLANES)
                    term = wk * rot_ref[plane[off % SUBLANES], pl.ds(at, SUBLANES), c0:c0 + cw]
                    accs[j] = term if accs[j] is None else accs[j] + term
            for j in range(blocks):
                at = pl.multiple_of(r0 + SUBLANES * j, SUBLANES)
                out_ref[pl.ds(at, SUBLANES), c0:c0 + cw] = accs[j]
        return state

    lax.fori_loop(0, tm // ROW_CHUNK, chunk, 0)


def _tap_wgrad(rot_ref, taps, x_ref, acc_ref, tm):
    residues = _residues(taps)
    plane = {r: p for p, r in enumerate(residues)}
    blocks = ROW_CHUNK // SUBLANES
    width = acc_ref.shape[2]

    def chunk(c, state):
        r0 = c * ROW_CHUNK
        for c0, cw in _lane_chunks(width):
            xs = [x_ref[pl.ds(pl.multiple_of(r0 + SUBLANES * j, SUBLANES), SUBLANES), c0:c0 + cw]
                  for j in range(blocks)]
            for k, off in taps:
                base = off - off % SUBLANES
                part = None
                for j in range(blocks):
                    at = pl.multiple_of(r0 + base + SUBLANES * j, SUBLANES)
                    term = xs[j] * rot_ref[plane[off % SUBLANES], pl.ds(at, SUBLANES), c0:c0 + cw]
                    part = term if part is None else part + term
                acc_ref[k, :, c0:c0 + cw] += part
        return state

    lax.fori_loop(0, tm // ROW_CHUNK, chunk, 0)


def _fwd_taps(width):
    return [(k, HALO - (width - 1) + k) for k in range(width)]


def _bwd_taps(width):
    return [(k, width - 1 - k) for k in range(width)]


def _my_index():
    return 4 * lax.axis_index("x") + 2 * lax.axis_index("y") + lax.axis_index("c")


def _mesh_id(idx):
    return (idx // 4, (idx // 2) % 2, idx % 2)


def _slab(ref, axis, idx, width):
    at = [slice(None)] * len(ref.shape)
    at[axis] = pl.ds(pl.multiple_of(idx * width, width), width)
    return ref.at[tuple(at)]


class _Exchange:
    def __init__(self, inputs, out_shape):
        n = len(inputs)
        self.inputs = list(inputs)
        self.out_shape = list(out_shape)
        self.sems = [pltpu.SemaphoreType.DMA((n, N_DEV - 1)), pltpu.SemaphoreType.DMA((n, N_DEV - 1)),
                     pltpu.SemaphoreType.DMA((n,))]
        self.results = None

    def _local(self, ins, outs, k, me):
        raise NotImplementedError

    def _remote(self, ins, outs, k, me, sender, receiver):
        raise NotImplementedError

    def start(self, ins, outs, sems):
        send_sems, recv_sems, local_sems = sems
        me = _my_index()
        for k in range(len(self.inputs)):
            src, dst = self._local(ins, outs, k, me)
            pltpu.make_async_copy(src, dst, local_sems.at[k]).start()
            for dist in range(1, N_DEV):
                peer = (me + dist) % N_DEV
                src, dst = self._remote(ins, outs, k, me, me, peer)
                pltpu.make_async_remote_copy(
                    src_ref=src, dst_ref=dst, send_sem=send_sems.at[k, dist - 1],
                    recv_sem=recv_sems.at[k, dist - 1], device_id=_mesh_id(peer), device_id_type=MESH).start()

    def wait(self, ins, outs, sems):
        send_sems, recv_sems, local_sems = sems
        me = _my_index()
        for k in range(len(self.inputs)):
            for dist in range(1, N_DEV):
                sender = (me + N_DEV - dist) % N_DEV
                src, dst = self._remote(ins, outs, k, me, sender, me)
                cp = pltpu.make_async_remote_copy(
                    src_ref=src, dst_ref=dst, send_sem=send_sems.at[k, dist - 1],
                    recv_sem=recv_sems.at[k, dist - 1], device_id=_mesh_id(sender), device_id_type=MESH)
                cp.wait_send()
                cp.wait_recv()
            src, dst = self._local(ins, outs, k, me)
            pltpu.make_async_copy(src, dst, local_sems.at[k]).wait()

    def forward(self, ins, outs, sems):
        pass


class _Gather(_Exchange):
    FLIPS = ((1, 0), (0, 1), (1, 1))

    def __init__(self, items):
        self.items = list(items)
        out_shape = []
        for shards, layer, axis in self.items:
            shape = list(shards.shape if layer is None else shards.shape[1:])
            shape[axis] *= N_DEV
            out_shape.append(_sds(shape, shards.dtype))
        super().__init__([it[0] for it in self.items], out_shape)

    def _src(self, ins, k):
        layer = self.items[k][1]
        return ins[k] if layer is None else ins[k].at[layer]

    def _place(self, outs, k, idx):
        axis = self.items[k][2]
        return _slab(outs[k], axis, idx, self.out_shape[k].shape[axis] // N_DEV)

    def _copy(self, sems, k, j, src, dst, to):
        return pltpu.make_async_remote_copy(src_ref=src, dst_ref=dst, send_sem=sems[0].at[k, j],
                                            recv_sem=sems[1].at[k, j], device_id=to, device_id_type=MESH)

    @staticmethod
    def _places():
        x, y, c = lax.axis_index("x"), lax.axis_index("y"), lax.axis_index("c")
        chips = [(1 - x if fx else x, 1 - y if fy else y) for fx, fy in _Gather.FLIPS]
        return (x, y, c), (x, y, 1 - c), chips

    @staticmethod
    def _index(place):
        return 4 * place[0] + 2 * place[1] + place[2]

    def start(self, ins, outs, sems):
        me, sibling, chips = self._places()
        for k in range(len(self.inputs)):
            src, mine = self._src(ins, k), self._place(outs, k, self._index(me))
            pltpu.make_async_copy(src, mine, sems[2].at[k]).start()
            self._copy(sems, k, 0, src, mine, sibling).start()
            for j, chip in enumerate(chips):
                self._copy(sems, k, 1 + j, src, mine, (*chip, me[2])).start()

    def forward(self, ins, outs, sems):
        me, sibling, chips = self._places()
        for k in range(len(self.inputs)):
            for j, chip in enumerate(chips):
                got = self._place(outs, k, self._index((*chip, me[2])))
                self._copy(sems, k, 1 + j, got, got, (*chip, me[2])).wait_recv()
                self._copy(sems, k, 4 + j, got, got, sibling).start()

    def wait(self, ins, outs, sems):
        me, sibling, chips = self._places()
        for k in range(len(self.inputs)):
            src, mine = self._src(ins, k), self._place(outs, k, self._index(me))
            self._copy(sems, k, 0, src, self._place(outs, k, self._index(sibling)), sibling).wait_recv()
            for j, chip in enumerate(chips):
                got = self._place(outs, k, self._index((*chip, sibling[2])))
                self._copy(sems, k, 4 + j, got, got, sibling).wait_recv()
            for j in range(N_DEV - 1):
                self._copy(sems, k, j, src, mine, sibling).wait_send()
            pltpu.make_async_copy(src, mine, sems[2].at[k]).wait()


class _Scatter(_Exchange):
    def __init__(self, items):
        self.items = list(items)
        out_shape = []
        for partial, axis in self.items:
            shape = list(partial.shape)
            shape[axis] //= N_DEV
            out_shape.append(_sds([N_DEV] + shape, partial.dtype))
        super().__init__([it[0] for it in self.items], out_shape)

    def _take(self, ins, k, idx):
        axis = self.items[k][1]
        return _slab(ins[k], axis, idx, self.items[k][0].shape[axis] // N_DEV)

    def _local(self, ins, outs, k, me):
        return self._take(ins, k, me), outs[k].at[me]

    def _remote(self, ins, outs, k, me, sender, receiver):
        return self._take(ins, k, receiver), outs[k].at[sender]


def _run_exchange(exchange, name):
    n = len(exchange.inputs)

    def body(*refs):
        ins, outs, sems = refs[:n], refs[n:2 * n], refs[2 * n:]
        exchange.start(ins, outs, sems)
        exchange.forward(ins, outs, sems)
        exchange.wait(ins, outs, sems)

    any_spec = pl.BlockSpec(memory_space=pl.ANY)
    exchange.results = pl.pallas_call(
        body, name=name, in_specs=[any_spec] * n, out_specs=[any_spec] * n, out_shape=exchange.out_shape,
        scratch_shapes=exchange.sems)(*exchange.inputs)
    return exchange.results


class _Layer:
    def __init__(self, stack, index):
        self.stack, self.index = stack, index
        self.shape = (1,) + stack.shape[1:] if stack.ndim == 2 else stack.shape[1:]

    def view(self, ref):
        return ref.at[pl.ds(self.index, 1)] if self.stack.ndim == 2 else ref.at[self.index]


class _LayerTokens:
    def __init__(self, stack, index):
        self.stack, self.index = stack, index

    def rows(self, tm):
        return pl.BlockSpec((None, None, tm, self.stack.shape[-1]), lambda i: (self.index, 0, i, 0))


def _call(body, *, name, grid, in_specs, out_specs, out_shape, args, scratch=(), carry=None):
    in_specs, out_specs, out_shape, scratch = list(in_specs), list(out_specs), list(out_shape), list(scratch)
    args = [a.stack if isinstance(a, _LayerTokens) else a for a in args]
    layers = {k: a for k, a in enumerate(args) if isinstance(a, _Layer)}
    for k, a in layers.items():
        in_specs[k], args[k] = _whole(a.stack.shape), a.stack
    params = pltpu.CompilerParams(dimension_semantics=("arbitrary",) * len(grid),
                                  vmem_limit_bytes=VMEM_LIMIT_BYTES)
    n_in, n_out, n_scr = len(in_specs), len(out_specs), len(scratch)
    n_x = 0 if carry is None else len(carry.inputs)
    steps = 1
    for extent in grid:
        steps *= extent
    assert carry is None or steps >= 3, "a carrier needs a step each for start, second stage and wait"

    def whole_body(*refs):
        core_in, x_in = list(refs[:n_in]), refs[n_in:n_in + n_x]
        refs = refs[n_in + n_x:]
        core_out, x_out = refs[:n_out], refs[n_out:n_out + n_x]
        refs = refs[n_out + n_x:]
        core_scr, sems = refs[:n_scr], refs[n_scr:]
        for k, a in layers.items():
            core_in[k] = a.view(core_in[k])
        if carry is None:
            body(*core_in, *core_out, *core_scr)
            return
        step = pl.program_id(0)
        for axis in range(1, len(grid)):
            step = step * grid[axis] + pl.program_id(axis)

        @pl.when(step == 0)
        def _():
            carry.start(x_in, x_out, sems)

        @pl.when(step == steps - 2)
        def _():
            carry.forward(x_in, x_out, sems)

        body(*core_in, *core_out, *core_scr)

        @pl.when(step == steps - 1)
        def _():
            carry.wait(x_in, x_out, sems)

    any_spec = pl.BlockSpec(memory_space=pl.ANY)
    extra_in = [] if carry is None else carry.inputs
    extra_shape = [] if carry is None else carry.out_shape
    extra_sems = [] if carry is None else carry.sems
    outs = pl.pallas_call(
        whole_body, name=name, grid=grid, in_specs=in_specs + [any_spec] * n_x,
        out_specs=out_specs + [any_spec] * n_x, out_shape=out_shape + extra_shape,
        scratch_shapes=scratch + extra_sems, compiler_params=params)(*args, *extra_in)
    if carry is not None:
        carry.results = outs[n_out:]
    return outs[:n_out]


def _all_reduce(parts, plan, name):
    n = len(parts)
    out_shape = []
    for group in plan:
        shape = parts[group[0]].shape
        if len(group) > 1:
            shape = (len(group),) + (shape[1:] if shape[0] == 1 else shape)
        out_shape.append(_sds(shape, F32))

    def body(*refs):
        ins, outs, lands = refs[:n], refs[n:n + len(plan)], refs[n + len(plan):2 * n + len(plan)]
        send_sems, recv_sems = refs[2 * n + len(plan):]
        me = _my_index()
        for k in range(n):
            lands[k][me] = ins[k][...]
            for dist in range(1, N_DEV):
                pltpu.make_async_remote_copy(
                    src_ref=ins[k], dst_ref=lands[k].at[me],
                    send_sem=send_sems.at[k, dist - 1], recv_sem=recv_sems.at[k, dist - 1],
                    device_id=_mesh_id((me + dist) % N_DEV), device_id_type=MESH).start()
        for k in range(n):
            for dist in range(1, N_DEV):
                sender = (me + N_DEV - dist) % N_DEV
                cp = pltpu.make_async_remote_copy(
                    src_ref=ins[k], dst_ref=lands[k].at[sender],
                    send_sem=send_sems.at[k, dist - 1], recv_sem=recv_sems.at[k, dist - 1],
                    device_id=_mesh_id(sender), device_id_type=MESH)
                cp.wait_send()
                cp.wait_recv()
        for o_ref, group in zip(outs, plan):
            for j, k in enumerate(group):
                total = lands[k][0]
                for dev in range(1, N_DEV):
                    total = total + lands[k][dev]
                if len(group) == 1:
                    o_ref[...] = total
                elif parts[k].shape[0] == 1:
                    o_ref[j:j + 1, :] = total
                else:
                    o_ref[j] = total

    vmem = pl.BlockSpec(memory_space=pltpu.VMEM)
    return pl.pallas_call(
        body, name=name, in_specs=[vmem] * n, out_specs=[vmem] * len(plan), out_shape=out_shape,
        scratch_shapes=[pltpu.VMEM((N_DEV,) + part.shape, F32) for part in parts]
        + [pltpu.SemaphoreType.DMA((n, N_DEV - 1)), pltpu.SemaphoreType.DMA((n, N_DEV - 1))])(*parts)


def _norm_matmul(x, g, wt, name, carry=None):
    t, d = x.shape
    n = wt.shape[0]
    tm, nc = _token_tile(t), _chunk(n)

    def body(x_ref, g_ref, wt_ref, h_ref, o_ref):
        xh, _ = _rms_stats(x_ref[...])
        h = (xh * g_ref[...]).astype(BF16)
        h_ref[...] = h
        for n0 in range(0, n, nc):
            o_ref[:, n0:n0 + nc] = _dot_nt(h, wt_ref[n0:n0 + nc, :]).astype(BF16)

    return _call(body, name=name, grid=(t // tm,),
                 in_specs=[_rows(tm, d), _whole(g.shape), _resident(wt.shape)],
                 out_specs=[_rows(tm, d), _rows(tm, n)],
                 out_shape=[_sds((t, d), BF16), _sds((t, n), BF16)], args=(x, g, wt), carry=carry)


def _fwd_branch(z, caw, cab, lng, lnb, cbw, dc, name, carry=None):
    t = z.shape[0]
    tm = _token_tile(t)
    ka, kb = caw.shape[0], cbw.shape[0]
    taps_a, taps_b = _fwd_taps(ka), _fwd_taps(kb)
    res_a, res_b = _residues(taps_a), _residues(taps_b)
    span = HALO + tm - SUBLANES

    def body(z_ref, caw_ref, cab_ref, lng_ref, lnb_ref, cbw_ref, ac_ref, act_ref, s_ref, cb_ref,
             rot_a, rot_b, wb_a, wb_b, cb):
        @pl.when(pl.program_id(0) == 0)
        def _():
            rot_a[0, 0:HALO, :] = jnp.zeros((HALO, dc), F32)
            rot_b[0, 0:HALO, :] = jnp.zeros((HALO, dc), F32)
            _broadcast_rows(wb_a, caw_ref, ka)
            wb_a[ka] = jnp.broadcast_to(cab_ref[...], (SUBLANES, dc))
            _broadcast_rows(wb_b, cbw_ref, kb)

        a_val = z_ref[:, 0:dc].astype(F32)
        a_gt = z_ref[:, dc:2 * dc].astype(F32)
        rot_a[0, HALO:HALO + tm, :] = a_val * _sigmoid(a_gt)
        _fill_rotations(rot_a, res_a, span)
        _tap_conv(rot_a, taps_a, wb_a, ac_ref, tm, bias_plane=ka)
        ac = ac_ref[...]
        mu = jnp.mean(ac, axis=-1, keepdims=True)
        xc = ac - mu
        var = jnp.mean(xc * xc, axis=-1, keepdims=True)
        ln = xc * lax.rsqrt(var + NORM_EPS) * lng_ref[...] + lnb_ref[...]
        act_ref[...] = (ln * _sigmoid(ln)).astype(BF16)
        rot_a[0, 0:HALO, :] = rot_a[0, tm:tm + HALO, :]

        sc_c = z_ref[:, 3 * dc:4 * dc].astype(F32)
        sc_v = z_ref[:, 4 * dc:5 * dc].astype(F32)
        rot_b[0, HALO:HALO + tm, :] = sc_c * sc_v
        _fill_rotations(rot_b, res_b, span)
        _tap_conv(rot_b, taps_b, wb_b, cb, tm)
        s_ref[...] = (z_ref[:, 2 * dc:3 * dc].astype(F32) * cb[...]).astype(BF16)
        cb_ref[...] = cb[...].astype(BF16)
        rot_b[0, 0:HALO, :] = rot_b[0, tm:tm + HALO, :]

    return _call(body, name=name, grid=(t // tm,),
                 in_specs=[_rows(tm, 5 * dc), _whole(caw.shape), _whole(cab.shape), _whole(lng.shape),
                           _whole(lnb.shape), _whole(cbw.shape)],
                 out_specs=[_rows(tm, dc), _rows(tm, dc), _rows(tm, dc), _rows(tm, dc)],
                 out_shape=[_sds((t, dc), F32), _sds((t, dc), BF16), _sds((t, dc), BF16), _sds((t, dc), BF16)],
                 scratch=[pltpu.VMEM((len(res_a), HALO + tm, dc), F32), pltpu.VMEM((len(res_b), HALO + tm, dc), F32),
                          pltpu.VMEM((ka + 1, SUBLANES, dc), F32), pltpu.VMEM((kb, SUBLANES, dc), F32),
                          pltpu.VMEM((tm, dc), F32)],
                 args=(z, caw, cab, lng, lnb, cbw), carry=carry)


def _fwd_merge(x, z, bg, a_act, s, wa, wb, wo, name, carry=None):
    t, d = x.shape
    n = z.shape[1]
    dc = a_act.shape[1]
    tm = _token_tile(t)
    o5 = n - 2 * d

    def body(x_ref, z_ref, bg_ref, act_ref, s_ref, wa_ref, wb_ref, wo_ref, o_ref):
        ya = _dot(act_ref[...], wa_ref[...])
        yb = _dot(s_ref[...], wb_ref[...])
        ga = _sigmoid(z_ref[:, o5:o5 + d].astype(F32) + bg_ref[:, 0:d])
        gb = _sigmoid(z_ref[:, o5 + d:n].astype(F32) + bg_ref[:, d:2 * d])
        m = (ga * ya + gb * yb).astype(BF16)
        o_ref[...] = x_ref[...] + _dot(m, wo_ref[...])

    return _call(body, name=name, grid=(t // tm,),
                 in_specs=[_rows(tm, d), _rows(tm, n), _whole(bg.shape), _rows(tm, dc), _rows(tm, dc),
                           _resident(wa.shape), _resident(wb.shape), _resident(wo.shape)],
                 out_specs=[_rows(tm, d)], out_shape=[_sds((t, d), F32)],
                 args=(x, z, bg, a_act, s, wa, wb, wo), carry=carry)[0]


def _fwd_down(x, u, cfw, cfb, wd, name, carry=None):
    t, d = x.shape
    f = u.shape[1] // 2
    tm = _token_tile(t, 256)
    kf = cfw.shape[0]

    taps = _fwd_taps(kf)
    residues = _residues(taps)

    def body(x_ref, u_ref, cfw_ref, cfb_ref, wd_ref, o_ref, act_ref, gl_ref, dgl_ref, rot_u, wb, fg):
        @pl.when(pl.program_id(0) == 0)
        def _():
            rot_u[0, 0:HALO, :] = jnp.zeros((HALO, f), F32)
            _broadcast_rows(wb, cfw_ref, kf)
            wb[kf] = jnp.broadcast_to(cfb_ref[...], (SUBLANES, f))

        rot_u[0, HALO:HALO + tm, :] = u_ref[:, 0:f].astype(F32)
        _fill_rotations(rot_u, residues, HALO + tm - SUBLANES)
        _tap_conv(rot_u, taps, wb, fg, tm, bias_plane=kf)
        y = x_ref[...]
        for c0, cw in _lane_chunks(f, MXU_COLUMNS):
            at = slice(c0, c0 + cw)
            gl, dgl = _gelu_tanh(fg[:, at])
            gl_ref[:, at] = gl.astype(BF16)
            dgl_ref[:, at] = dgl.astype(BF16)
            act = (gl * u_ref[:, f + c0:f + c0 + cw].astype(F32)).astype(BF16)
            act_ref[:, at] = act
            y = y + _dot(act, wd_ref[at, :])
        o_ref[...] = y
        rot_u[0, 0:HALO, :] = rot_u[0, tm:tm + HALO, :]

    return _call(body, name=name, grid=(t // tm,),
                 in_specs=[_rows(tm, d), _rows(tm, 2 * f), _whole(cfw.shape), _whole(cfb.shape),
                           _resident(wd.shape)],
                 out_specs=[_rows(tm, d), _rows(tm, f), _rows(tm, f), _rows(tm, f)],
                 out_shape=[_sds((t, d), F32), _sds((t, f), BF16), _sds((t, f), BF16), _sds((t, f), BF16)],
                 scratch=[pltpu.VMEM((len(residues), HALO + tm, f), F32), pltpu.VMEM((kf + 1, SUBLANES, f), F32),
                          pltpu.VMEM((tm, f), F32)],
                 args=(x, u, cfw, cfb, wd), carry=carry)


def _fwd_ple(x, g, wpg, p, wple, name, carry=None):
    t, d = x.shape
    pd = p.stack.shape[-1]
    tm = _token_tile(t)

    def body(x_ref, g_ref, wpg_ref, p_ref, wple_ref, o_ref):
        xv = x_ref[...]
        xh, _ = _rms_stats(xv)
        lg = _dot((xh * g_ref[...]).astype(BF16), wpg_ref[...])
        pp = _dot(p_ref[...].astype(BF16), wple_ref[...])
        o_ref[...] = xv + _sigmoid(lg) * pp

    return _call(body, name=name, grid=(t // tm,),
                 in_specs=[_rows(tm, d), _whole(g.shape), _resident(wpg.shape), p.rows(tm),
                           _resident(wple.shape)],
                 out_specs=[_rows(tm, d)], out_shape=[_sds((t, d), F32)],
                 args=(x, g, wpg, p, wple), carry=carry)[0]


def _loss_bwd(x, g, target, name):
    t, d = x.shape
    tm = _token_tile(t)

    def body(x_ref, g_ref, t_ref, dx_ref, dg_ref, loss_ref):
        @pl.when(pl.program_id(0) == 0)
        def _():
            dg_ref[...] = jnp.zeros_like(dg_ref)
            loss_ref[...] = jnp.zeros_like(loss_ref)

        xh, r = _rms_stats(x_ref[...])
        err = xh * g_ref[...] - t_ref[...]
        sq = jnp.sum(jnp.sum(err * err, axis=0, keepdims=True), axis=1, keepdims=True)
        loss_ref[...] += jnp.broadcast_to(0.5 * sq / d, loss_ref.shape)
        dx, dg = _rms_bwd(err / d, xh, r, g_ref[...])
        dx_ref[...] = dx
        dg_ref[...] += dg

    return _call(body, name=name, grid=(t // tm,),
                 in_specs=[_rows(tm, d), _whole(g.shape), _rows(tm, d)],
                 out_specs=[_rows(tm, d), _whole((1, d)), _whole((SUBLANES, LANES))],
                 out_shape=[_sds((t, d), F32), _sds((1, d), F32), _sds((SUBLANES, LANES), F32)],
                 args=(x, g, target))


def _bwd_ple(dy, x, g, wpg, p, wple, name, carry=None):
    t, d = x.shape
    pd = p.stack.shape[-1]
    tm = _token_tile(t)
    nt = t // tm

    def body(dy_ref, x_ref, g_ref, wpg_ref, p_ref, wple_ref, dx_ref, dwpg_ref, dwple_ref, dg_ref,
             acc_pg, acc_ple):
        i = pl.program_id(0)

        @pl.when(i == 0)
        def _():
            acc_pg[...] = jnp.zeros_like(acc_pg)
            acc_ple[...] = jnp.zeros_like(acc_ple)
            dg_ref[...] = jnp.zeros_like(dg_ref)

        dyv = dy_ref[...]
        xh, r = _rms_stats(x_ref[...])
        h = (xh * g_ref[...]).astype(BF16)
        pb = p_ref[...].astype(BF16)
        pg = _sigmoid(_dot(h, wpg_ref[...]))
        pp = _dot(pb, wple_ref[...])
        dpp = (dyv * pg).astype(BF16)
        dlg = (dyv * pp * pg * (1.0 - pg)).astype(BF16)
        acc_ple[...] += _dot_tn(pb, dpp)
        acc_pg[...] += _dot_tn(h, dlg)
        dx, dg = _rms_bwd(_dot_nt(dlg, wpg_ref[...]), xh, r, g_ref[...])
        dx_ref[...] = dyv + dx
        dg_ref[...] += dg

        @pl.when(i == nt - 1)
        def _():
            dwpg_ref[...] = acc_pg[...].astype(BF16)
            dwple_ref[...] = acc_ple[...].astype(BF16)

    return _call(body, name=name, grid=(nt,),
                 in_specs=[_rows(tm, d), _rows(tm, d), _whole(g.shape), _resident(wpg.shape), p.rows(tm),
                           _resident(wple.shape)],
                 out_specs=[_rows(tm, d), _whole((d, d)), _whole((pd, d)), _whole((1, d))],
                 out_shape=[_sds((t, d), F32), _sds((d, d), BF16), _sds((pd, d), BF16), _sds((1, d), F32)],
                 scratch=[pltpu.VMEM((d, d), F32), pltpu.VMEM((pd, d), F32)],
                 args=(dy, x, g, wpg, p, wple), carry=carry)


def _bwd_down(dy, u, gl, dgl, cfw, wd, name, carry=None):
    t, d = dy.shape
    f = u.shape[1] // 2
    tm = _token_tile(t, 256)
    nt = t // tm
    kf = cfw.shape[0]
    bwd = _bwd_taps(kf)
    residues = _residues(bwd)

    def body(dy_ref, u_ref, gl_ref, dgl_ref, cfw_ref, wd_ref, du_ref, dcw_ref, dcb_ref, rot_g, wb, acc, buf):
        i = pl.program_id(0)

        @pl.when(i == 0)
        def _():
            for ref in (dcw_ref, dcb_ref, acc):
                ref[...] = jnp.zeros_like(ref)
            rot_g[0, tm:tm + HALO, :] = jnp.zeros((HALO, f), F32)
            _broadcast_rows(wb, cfw_ref, kf)

        dyb = dy_ref[...].astype(BF16)
        for c0, cw in _lane_chunks(f, MXU_COLUMNS):
            at = slice(c0, c0 + cw)
            df = _dot_nt(dyb, wd_ref[at, :])
            du_ref[:, f + c0:f + c0 + cw] = (df * gl_ref[:, at].astype(F32)).astype(BF16)
            dfg = df * u_ref[:, f + c0:f + c0 + cw].astype(F32) * dgl_ref[:, at].astype(F32)
            dcb_ref[:, at] += jnp.sum(dfg, axis=0, keepdims=True)
            rot_g[0, 0:tm, at] = dfg
        _fill_rotations(rot_g, residues, HALO + tm - SUBLANES)
        buf[...] = u_ref[:, 0:f].astype(F32)
        _tap_wgrad(rot_g, bwd, buf, acc, tm)
        _tap_conv(rot_g, bwd, wb, buf, tm)
        du_ref[:, 0:f] = buf[...].astype(BF16)
        rot_g[0, tm:tm + HALO, :] = rot_g[0, 0:HALO, :]

        @pl.when(i == nt - 1)
        def _():
            dcw_ref[0:kf, :] = jnp.sum(acc[...], axis=1)

    return _call(body, name=name, grid=(nt,),
                 in_specs=[_rows_rev(tm, d, nt), _rows_rev(tm, 2 * f, nt), _rows_rev(tm, f, nt),
                           _rows_rev(tm, f, nt), _whole(cfw.shape), _resident(wd.shape)],
                 out_specs=[_rows_rev(tm, 2 * f, nt), _whole((SUBLANES, f)), _whole((1, f))],
                 out_shape=[_sds((t, 2 * f), BF16), _sds((SUBLANES, f), F32), _sds((1, f), F32)],
                 scratch=[pltpu.VMEM((len(residues), HALO + tm, f), F32), pltpu.VMEM((kf, SUBLANES, f), F32),
                          pltpu.VMEM((kf, SUBLANES, f), F32), pltpu.VMEM((tm, f), F32)],
                 args=(dy, u, gl, dgl, cfw, wd), carry=carry)


def _bwd_norm_matmul(dout, wt, x, g, dres, name, carry=None):
    t, d = x.shape
    n = dout.shape[1]
    tm = _token_tile(t)

    def body(do_ref, wt_ref, x_ref, g_ref, dres_ref, dx_ref, dg_ref):
        @pl.when(pl.program_id(0) == 0)
        def _():
            dg_ref[...] = jnp.zeros_like(dg_ref)

        dh = _dot(do_ref[...], wt_ref[...])
        xh, r = _rms_stats(x_ref[...])
        dx, dg = _rms_bwd(dh, xh, r, g_ref[...])
        dx_ref[...] = dres_ref[...] + dx
        dg_ref[...] += dg

    return _call(body, name=name, grid=(t // tm,),
                 in_specs=[_rows(tm, n), _resident(wt.shape), _rows(tm, d), _whole(g.shape), _rows(tm, d)],
                 out_specs=[_rows(tm, d), _whole((1, d))],
                 out_shape=[_sds((t, d), F32), _sds((1, d), F32)],
                 args=(dout, wt, x, g, dres), carry=carry)


def _wgrad_tn(a, b, name, carry=None):
    t, n = a.shape
    d = b.shape[1]
    tt = 2048 if t % 2048 == 0 else _token_tile(t)
    tn = _chunk(n, 1536)
    nt = t // tt

    def body(a_ref, b_ref, o_ref, acc):
        k = pl.program_id(1)

        @pl.when(k == 0)
        def _():
            acc[...] = jnp.zeros_like(acc)

        acc[...] += _dot_tn(a_ref[...].astype(BF16), b_ref[...].astype(BF16))

        @pl.when(k == nt - 1)
        def _():
            o_ref[...] = acc[...].astype(BF16)

    return _call(body, name=name, grid=(n // tn, nt),
                 in_specs=[pl.BlockSpec((tt, tn), lambda j, k: (k, j)), pl.BlockSpec((tt, d), lambda j, k: (k, 0))],
                 out_specs=[pl.BlockSpec((tn, d), lambda j, k: (j, 0))], out_shape=[_sds((n, d), BF16)],
                 scratch=[pltpu.VMEM((tn, d), F32)], args=(a, b), carry=carry)[0]


def _bwd_merge(dy, z, bg, a_act, s, wa, wb, wo, name, carry=None):
    t, d = dy.shape
    n = z.shape[1]
    dc = a_act.shape[1]
    tm = _token_tile(t)
    nt = t // tm
    o5 = n - 2 * d

    def body(dy_ref, z_ref, bg_ref, act_ref, s_ref, wa_ref, wb_ref, wo_ref,
             dact_ref, ds_ref, dgl_ref, dwo_ref, dwa_ref, dwb_ref, dbg_ref, acc_o, acc_a, acc_b):
        i = pl.program_id(0)

        @pl.when(i == 0)
        def _():
            acc_o[...] = jnp.zeros_like(acc_o)
            acc_a[...] = jnp.zeros_like(acc_a)
            acc_b[...] = jnp.zeros_like(acc_b)
            dbg_ref[...] = jnp.zeros_like(dbg_ref)

        dyb = dy_ref[...].astype(BF16)
        dm = _dot_nt(dyb, wo_ref[...])
        ya = _dot(act_ref[...], wa_ref[...])
        yb = _dot(s_ref[...], wb_ref[...])
        ga = _sigmoid(z_ref[:, o5:o5 + d].astype(F32) + bg_ref[:, 0:d])
        gb = _sigmoid(z_ref[:, o5 + d:n].astype(F32) + bg_ref[:, d:2 * d])
        acc_o[...] += _dot_tn((ga * ya + gb * yb).astype(BF16), dyb)
        dya = (dm * ga).astype(BF16)
        dyb2 = (dm * gb).astype(BF16)
        acc_a[...] += _dot_tn(act_ref[...], dya)
        acc_b[...] += _dot_tn(s_ref[...], dyb2)
        dact_ref[...] = _dot_nt(dya, wa_ref[...])
        ds_ref[...] = _dot_nt(dyb2, wb_ref[...])
        dla = dm * ya * ga * (1.0 - ga)
        dlb = dm * yb * gb * (1.0 - gb)
        dgl_ref[:, 0:d] = dla.astype(BF16)
        dgl_ref[:, d:2 * d] = dlb.astype(BF16)
        dbg_ref[:, 0:d] += jnp.sum(dla, axis=0, keepdims=True)
        dbg_ref[:, d:2 * d] += jnp.sum(dlb, axis=0, keepdims=True)

        @pl.when(i == nt - 1)
        def _():
            dwo_ref[...] = acc_o[...].astype(BF16)
            dwa_ref[...] = acc_a[...].astype(BF16)
            dwb_ref[...] = acc_b[...].astype(BF16)

    return _call(body, name=name, grid=(nt,),
                 in_specs=[_rows(tm, d), _rows(tm, n), _whole(bg.shape), _rows(tm, dc), _rows(tm, dc),
                           _resident(wa.shape), _resident(wb.shape), _resident(wo.shape)],
                 out_specs=[_rows(tm, dc), _rows(tm, dc), _rows(tm, 2 * d), _whole((d, d)), _whole((dc, d)),
                            _whole((dc, d)), _whole((1, 2 * d))],
                 out_shape=[_sds((t, dc), F32), _sds((t, dc), F32), _sds((t, 2 * d), BF16), _sds((d, d), BF16),
                            _sds((dc, d), BF16), _sds((dc, d), BF16), _sds((1, 2 * d), F32)],
                 scratch=[pltpu.VMEM((d, d), F32), pltpu.VMEM((dc, d), F32), pltpu.VMEM((dc, d), F32)],
                 args=(dy, z, bg, a_act, s, wa, wb, wo), carry=carry)


def _bwd_branch(dact, ds, z, dgl, a_conv, cb, caw, lng, lnb, cbw, name, carry=None):
    t, n = z.shape
    dc = a_conv.shape[1]
    tm = _token_tile(t)
    nt = t // tm
    ka, kb = caw.shape[0], cbw.shape[0]
    bwd_a, bwd_b = _bwd_taps(ka), _bwd_taps(kb)
    span = HALO + tm - SUBLANES

    def body(dact_ref, ds_ref, z_ref, dgl_ref, ac_ref, cb_ref, caw_ref, lng_ref, lnb_ref, cbw_ref,
             dz_ref, dcaw_ref, dcab_ref, dlng_ref, dlnb_ref, dcbw_ref,
             rot_da, rot_dc, wb_a, wb_b, acc_a, acc_b, buf):
        i = pl.program_id(0)

        @pl.when(i == 0)
        def _():
            for ref in (dcaw_ref, dcab_ref, dlng_ref, dlnb_ref, dcbw_ref, acc_a, acc_b):
                ref[...] = jnp.zeros_like(ref)
            rot_da[0, tm:tm + HALO, :] = jnp.zeros((HALO, dc), F32)
            rot_dc[0, tm:tm + HALO, :] = jnp.zeros((HALO, dc), F32)
            _broadcast_rows(wb_a, caw_ref, ka)
            _broadcast_rows(wb_b, cbw_ref, kb)

        a_val = z_ref[:, 0:dc].astype(F32)
        sg = _sigmoid(z_ref[:, dc:2 * dc].astype(F32))

        ac = ac_ref[...]
        mu = jnp.mean(ac, axis=-1, keepdims=True)
        xc = ac - mu
        rstd = lax.rsqrt(jnp.mean(xc * xc, axis=-1, keepdims=True) + NORM_EPS)
        xh = xc * rstd
        ln = xh * lng_ref[...] + lnb_ref[...]
        sl = _sigmoid(ln)
        dln = dact_ref[...] * (sl * (1.0 + ln * (1.0 - sl)))
        dlng_ref[...] += jnp.sum(dln * xh, axis=0, keepdims=True)
        dlnb_ref[...] += jnp.sum(dln, axis=0, keepdims=True)
        dxh = dln * lng_ref[...]
        dac = rstd * (dxh - jnp.mean(dxh, axis=-1, keepdims=True)
                      - xh * jnp.mean(dxh * xh, axis=-1, keepdims=True))
        dcab_ref[...] += jnp.sum(dac, axis=0, keepdims=True)
        rot_da[0, 0:tm, :] = dac
        _fill_rotations(rot_da, _residues(bwd_a), span)
        buf[...] = a_val * sg
        _tap_wgrad(rot_da, bwd_a, buf, acc_a, tm)
        _tap_conv(rot_da, bwd_a, wb_a, buf, tm)
        rot_da[0, tm:tm + HALO, :] = rot_da[0, 0:HALO, :]
        da = buf[...]
        dz_ref[:, 0:dc] = (da * sg).astype(BF16)
        dz_ref[:, dc:2 * dc] = (da * a_val * sg * (1.0 - sg)).astype(BF16)

        sc_b = z_ref[:, 2 * dc:3 * dc].astype(F32)
        sc_c = z_ref[:, 3 * dc:4 * dc].astype(F32)
        sc_v = z_ref[:, 4 * dc:5 * dc].astype(F32)
        dsv = ds_ref[...]
        dz_ref[:, 2 * dc:3 * dc] = (dsv * cb_ref[...].astype(F32)).astype(BF16)
        rot_dc[0, 0:tm, :] = dsv * sc_b
        _fill_rotations(rot_dc, _residues(bwd_b), span)
        buf[...] = sc_c * sc_v
        _tap_wgrad(rot_dc, bwd_b, buf, acc_b, tm)
        _tap_conv(rot_dc, bwd_b, wb_b, buf, tm)
        rot_dc[0, tm:tm + HALO, :] = rot_dc[0, 0:HALO, :]
        dcv = buf[...]
        dz_ref[:, 3 * dc:4 * dc] = (dcv * sc_v).astype(BF16)
        dz_ref[:, 4 * dc:5 * dc] = (dcv * sc_c).astype(BF16)
        dz_ref[:, 5 * dc:n] = dgl_ref[...]

        @pl.when(i == nt - 1)
        def _():
            dcaw_ref[0:ka, :] = jnp.sum(acc_a[...], axis=1)
            dcbw_ref[0:kb, :] = jnp.sum(acc_b[...], axis=1)

    def planes(taps):
        return pltpu.VMEM((len(_residues(taps)), HALO + tm, dc), F32)

    return _call(body, name=name, grid=(nt,),
                 in_specs=[_rows_rev(tm, dc, nt), _rows_rev(tm, dc, nt), _rows_rev(tm, 5 * dc, nt),
                           _rows_rev(tm, n - 5 * dc, nt), _rows_rev(tm, dc, nt), _rows_rev(tm, dc, nt),
                           _whole(caw.shape), _whole(lng.shape), _whole(lnb.shape), _whole(cbw.shape)],
                 out_specs=[_rows_rev(tm, n, nt), _whole((HALO, dc)), _whole((1, dc)), _whole((1, dc)),
                            _whole((1, dc)), _whole((SUBLANES, dc))],
                 out_shape=[_sds((t, n), BF16), _sds((HALO, dc), F32), _sds((1, dc), F32), _sds((1, dc), F32),
                            _sds((1, dc), F32), _sds((SUBLANES, dc), F32)],
                 scratch=[planes(bwd_a), planes(bwd_b),
                          pltpu.VMEM((ka, SUBLANES, dc), F32), pltpu.VMEM((kb, SUBLANES, dc), F32),
                          pltpu.VMEM((ka, SUBLANES, dc), F32), pltpu.VMEM((kb, SUBLANES, dc), F32),
                          pltpu.VMEM((tm, dc), F32)],
                 args=(dact, ds, z, dgl, a_conv, cb, caw, lng, lnb, cbw), carry=carry)


def _land_specs(depth, nr, tr, cols):
    def spec(k):
        return pl.BlockSpec((N_DEV, tr, cols), lambda i: (0, jnp.clip(i - k * nr, 0, nr - 1), 0))
    return [spec(k) for k in range(depth)]


def _adamw_math(w, g, m, v):
    nm = ADAM_B1 * m + (1.0 - ADAM_B1) * g
    nv = ADAM_B2 * v + (1.0 - ADAM_B2) * (g * g)
    m_hat = nm / (1.0 - ADAM_B1 ** ADAM_STEP)
    v_hat = nv / (1.0 - ADAM_B2 ** ADAM_STEP)
    return -ADAM_LR * (m_hat / (jnp.sqrt(v_hat) + ADAM_EPS) + ADAM_WD * w), nm, nv


def _sum_adamw(lands, w, m, v, name):
    _, rows, cols = lands[0].shape
    tr = _row_tile(rows)
    nr = rows // tr
    depth = len(lands)

    def body(*refs):
        w_ref, m_ref, v_ref, g_ref, d_ref, nm_ref, nv_ref = refs[depth:]
        i = pl.program_id(0)
        for k in range(depth):
            @pl.when(i // nr == k)
            def _(k=k):
                acc = refs[k][0].astype(F32)
                for j in range(1, N_DEV):
                    acc = acc + refs[k][j].astype(F32)
                g_ref[...] = acc
                d_ref[...], nm_ref[...], nv_ref[...] = _adamw_math(w_ref[...], acc, m_ref[...], v_ref[...])

    spec = _rows(tr, cols)
    return _call(body, name=name, grid=(depth * nr,), in_specs=_land_specs(depth, nr, tr, cols) + [spec] * 3,
                 out_specs=[spec] * 4, out_shape=[_sds((depth * rows, cols), F32)] * 4, args=(*lands, w, m, v))


def _adamw_small(ws, gs, ms, vs, name):
    n = len(ws)

    def body(*refs):
        w_refs, g_refs, m_refs, v_refs = refs[:n], refs[n:2 * n], refs[2 * n:3 * n], refs[3 * n:4 * n]
        d_refs, nm_refs, nv_refs = refs[4 * n:5 * n], refs[5 * n:6 * n], refs[6 * n:]
        for k in range(n):
            d_refs[k][...], nm_refs[k][...], nv_refs[k][...] = _adamw_math(
                w_refs[k][...], g_refs[k][...], m_refs[k][...], v_refs[k][...])

    vmem = pl.BlockSpec(memory_space=pltpu.VMEM)
    outs = pl.pallas_call(
        body, name=name, in_specs=[vmem] * (4 * n), out_specs=[vmem] * (3 * n),
        out_shape=[_sds(a.shape, F32) for a in ws] * 3)(*ws, *gs, *ms, *vs)
    return outs[:n], outs[n:2 * n], outs[2 * n:]


def kernel(x, p, g_mix, w_in, b_gate, conv_a_w, conv_a_b, ln_a_g, ln_a_b, w_a_out, conv_b_w, w_b_out, w_o, g_ffn, w_up, conv_f_w, conv_f_b, w_down, g_ple, w_ple, w_ple_gate, g_final, loss_target, m_g_mix, m_w_in, m_b_gate, m_conv_a_w, m_conv_a_b, m_ln_a_g, m_ln_a_b, m_w_a_out, m_conv_b_w, m_w_b_out, m_w_o, m_g_ffn, m_w_up, m_conv_f_w, m_conv_f_b, m_w_down, m_g_ple, m_w_ple, m_w_ple_gate, m_g_final, v_g_mix, v_w_in, v_b_gate, v_conv_a_w, v_conv_a_b, v_ln_a_g, v_ln_a_b, v_w_a_out, v_conv_b_w, v_w_b_out, v_w_o, v_g_ffn, v_w_up, v_conv_f_w, v_conv_f_b, v_w_down, v_g_ple, v_w_ple, v_w_ple_gate, v_g_final):
    w = dict(zip(WEIGHT_NAMES, (g_mix, w_in, b_gate, conv_a_w, conv_a_b, ln_a_g, ln_a_b, w_a_out, conv_b_w,
                                w_b_out, w_o, g_ffn, w_up, conv_f_w, conv_f_b, w_down, g_ple, w_ple,
                                w_ple_gate, g_final)))
    mom = dict(zip(WEIGHT_NAMES, (m_g_mix, m_w_in, m_b_gate, m_conv_a_w, m_conv_a_b, m_ln_a_g, m_ln_a_b,
                                  m_w_a_out, m_conv_b_w, m_w_b_out, m_w_o, m_g_ffn, m_w_up, m_conv_f_w,
                                  m_conv_f_b, m_w_down, m_g_ple, m_w_ple, m_w_ple_gate, m_g_final)))
    var = dict(zip(WEIGHT_NAMES, (v_g_mix, v_w_in, v_b_gate, v_conv_a_w, v_conv_a_b, v_ln_a_g, v_ln_a_b,
                                  v_w_a_out, v_conv_b_w, v_w_b_out, v_w_o, v_g_ffn, v_w_up, v_conv_f_w,
                                  v_conv_f_b, v_w_down, v_g_ple, v_w_ple, v_w_ple_gate, v_g_final)))
    depth = g_mix.shape[0]
    dc = ln_a_g.shape[1]
    me = _my_index()
    x0 = x[0]
    target = loss_target[0]
    big_names = tuple(BIG_AXIS)

    shard = {name: (jnp.swapaxes(w[name], 1, 2) if name in TRANSPOSED else w[name]).astype(BF16)
             for name in big_names}

    def gather_of(layer, *names):
        return _Gather([(shard[name], layer, BIG_AXIS[name]) for name in names])

    def row(name, layer):
        return _Layer(w[name], layer)

    first = _Gather([(shard['w_in'], 0, BIG_AXIS['w_in'])] + [(w[name][None], None, 0) for name in CONV_SHARDED])
    gathered = _run_exchange(first, "gather_first")
    w_in_full = gathered[0]
    conv_full = {name: jnp.transpose(g, (1, 2, 0, 3)).reshape(g.shape[1], g.shape[2], -1)
                 for name, g in zip(CONV_SHARDED, gathered[1:])}
    saved = []
    xc = x0
    for l in range(depth):
        carry = gather_of(l, 'w_a_out', 'w_b_out', 'w_o', 'w_up')
        h, z = _norm_matmul(xc, row('g_mix', l), w_in_full, f"fwd_in_{l}", carry)
        wa_full, wb_full, wo_full, w_up_full = carry.results
        a_conv, a_act, s, cb = _fwd_branch(z, _Layer(conv_full['conv_a_w'], l), row('conv_a_b', l), row('ln_a_g', l),
                                       row('ln_a_b', l), _Layer(conv_full['conv_b_w'], l), dc, f"fwd_branch_{l}")
        x1 = _fwd_merge(xc, z, row('b_gate', l), a_act, s, wa_full, wb_full, wo_full, f"fwd_merge_{l}")
        carry = _Gather([(shard[name], l, BIG_AXIS[name]) for name in ('w_down', 'w_ple', 'w_ple_gate')]
                        + ([(shard['w_in'], l + 1, BIG_AXIS['w_in'])] if l + 1 < depth else []))
        h2, u = _norm_matmul(x1, row('g_ffn', l), w_up_full, f"fwd_up_{l}", carry)
        w_down_full, w_ple_full, w_pg_full = carry.results[:3]
        x2, act, gl, dgl = _fwd_down(x1, u, _Layer(conv_full['conv_f_w'], l), row('conv_f_b', l), w_down_full,
                                     f"fwd_down_{l}")
        x3 = _fwd_ple(x2, row('g_ple', l), w_pg_full, _LayerTokens(p, l), w_ple_full, f"fwd_ple_{l}")
        saved.append((xc, h, z, a_conv, a_act, s, cb, x1, h2, u, act, gl, dgl, x2,
                      dict(w_in=w_in_full, w_a_out=wa_full, w_b_out=wb_full, w_o=wo_full, w_up=w_up_full,
                           w_down=w_down_full, w_ple=w_ple_full, w_ple_gate=w_pg_full)))
        if l + 1 < depth:
            w_in_full = carry.results[3]
        xc = x3

    dx, dg_final, loss_part = _loss_bwd(xc, g_final[None], target, "loss_bwd")
    landed = {name: [None] * depth for name in big_names}
    small = {name: [None] * depth for name in WEIGHT_NAMES if name not in BIG_AXIS and name != 'g_final'}

    def scatter_of(*partials):
        ex = _Scatter([(part, BIG_AXIS[name]) for name, _, part in partials])
        ex.places = [(name, layer) for name, layer, _ in partials]
        return ex

    def keep(ex):
        for (name, layer), land in zip(ex.places, ex.results):
            landed[name][layer] = land

    pending = []
    for l in reversed(range(depth)):
        xin, h, z, a_conv, a_act, s, cb, x1, h2, u, act, gl, dgl, x2, full = saved[l]
        dx2, d_wpg, d_wple, small['g_ple'][l] = _bwd_ple(
            dx, x2, row('g_ple', l), full['w_ple_gate'], _LayerTokens(p, l), full['w_ple'], f"bwd_ple_{l}")
        d_wdown = _wgrad_tn(act, dx2, f"wgrad_down_{l}")
        carry = scatter_of(('w_ple_gate', l, d_wpg), ('w_ple', l, d_wple), ('w_down', l, d_wdown))
        du, small['conv_f_w'][l], small['conv_f_b'][l] = _bwd_down(
            dx2, u, gl, dgl, _Layer(conv_full['conv_f_w'], l), full['w_down'], f"bwd_down_{l}", carry)
        keep(carry)
        carry = scatter_of(*pending) if pending else None
        pending = []
        dx1, small['g_ffn'][l] = _bwd_norm_matmul(du, full['w_up'], x1, row('g_ffn', l), dx2, f"bwd_up_{l}", carry)
        if carry is not None:
            keep(carry)
        d_wup = _wgrad_tn(du, h2, f"wgrad_up_{l}")
        dact, ds, dgate, d_wo, d_wa, d_wb, small['b_gate'][l] = _bwd_merge(
            dx1, z, row('b_gate', l), a_act, s, full['w_a_out'], full['w_b_out'], full['w_o'], f"bwd_merge_{l}")
        carry = scatter_of(('w_up', l, d_wup))
        (dz, small['conv_a_w'][l], small['conv_a_b'][l], small['ln_a_g'][l], small['ln_a_b'][l],
         small['conv_b_w'][l]) = _bwd_branch(
            dact, ds, z, dgate, a_conv, cb, _Layer(conv_full['conv_a_w'], l), row('ln_a_g', l), row('ln_a_b', l),
            _Layer(conv_full['conv_b_w'], l), f"bwd_branch_{l}", carry)
        keep(carry)
        carry = scatter_of(('w_o', l, d_wo), ('w_a_out', l, d_wa), ('w_b_out', l, d_wb))
        d_win = _wgrad_tn(dz, h, f"wgrad_in_{l}", carry)
        keep(carry)
        carry = scatter_of(('w_in', l, d_win)) if l == 0 else None
        if l > 0:
            pending = [('w_in', l, d_win)]
        dx, small['g_mix'][l] = _bwd_norm_matmul(dz, full['w_in'], xin, row('g_mix', l), dx1, f"bwd_in_{l}", carry)
        if carry is not None:
            keep(carry)
    grad_x = dx[None]

    small_names = tuple(small)
    parts = [part for name in small_names for part in small[name]] + [dg_final, loss_part]
    plan = [tuple(range(k * depth, (k + 1) * depth)) for k in range(len(small_names))]
    plan += [(len(parts) - 2,), (len(parts) - 1,)]
    reduced = _all_reduce(parts, plan, "all_reduce_small")
    loss = reduced[-1][0, 0]
    grads = dict(zip(small_names, reduced[:len(small_names)]))
    grads['g_final'] = reduced[len(small_names)].reshape(g_final.shape)
    for name in CONV_SHARDED:
        _, taps, width = w[name].shape
        grads[name] = lax.dynamic_slice(grads[name], (0, 0, me * width), (depth, taps, width))

    delta, new_m, new_v = {}, {}, {}
    for name in big_names:
        view = (lambda a: jnp.swapaxes(a, 1, 2)) if name in TRANSPOSED else (lambda a: a)
        shape = view(w[name]).shape
        flat = lambda a: view(a).reshape(-1, shape[-1])
        lands = [land.reshape(N_DEV, -1, shape[-1]) for land in landed[name]]
        outs = _sum_adamw(lands, flat(w[name]), flat(mom[name]), flat(var[name]), f"adamw_{name}")
        grads[name], delta[name], new_m[name], new_v[name] = [view(a.reshape(shape)) for a in outs]
    rest = tuple(name for name in WEIGHT_NAMES if name not in BIG_AXIS)
    as_2d = lambda a: a.reshape(1, -1) if a.ndim == 1 else a
    outs = _adamw_small(*[[as_2d(src[name]) for name in rest] for src in (w, grads, mom, var)], "adamw_small")
    for dst, values in zip((delta, new_m, new_v), outs):
        dst.update({name: value.reshape(w[name].shape) for name, value in zip(rest, values)})

    return (loss, grad_x, *[grads[n] for n in WEIGHT_NAMES], *[delta[n] for n in WEIGHT_NAMES],
            *[new_m[n] for n in WEIGHT_NAMES], *[new_v[n] for n in WEIGHT_NAMES])
```

```python
import jax
import jax.numpy as jnp
from jax import lax
from jax.experimental import pallas as pl
from jax.experimental.pallas import tpu as pltpu

F32 = jnp.float32
BF16 = jnp.bfloat16
MESH = pl.DeviceIdType.MESH

N_DEV = 8
NORM_EPS = 1e-6
HALO = 32
LANES = 128
SUBLANES = 8
VMEM_LIMIT_BYTES = 56 * 2**20

ADAM_LR = 0.001
ADAM_B1 = 0.9
ADAM_B2 = 0.999
ADAM_EPS = 1e-08
ADAM_WD = 0.01
ADAM_STEP = 10

WEIGHT_NAMES = ('g_mix', 'w_in', 'b_gate', 'conv_a_w', 'conv_a_b', 'ln_a_g', 'ln_a_b', 'w_a_out',
                'conv_b_w', 'w_b_out', 'w_o', 'g_ffn', 'w_up', 'conv_f_w', 'conv_f_b', 'w_down',
                'g_ple', 'w_ple', 'w_ple_gate', 'g_final')
BIG_AXIS = {'w_in': 0, 'w_up': 0, 'w_a_out': 1, 'w_b_out': 1, 'w_o': 0, 'w_down': 0, 'w_ple': 1,
            'w_ple_gate': 0}
TRANSPOSED = ('w_in', 'w_up')
CONV_SHARDED = ('conv_a_w', 'conv_b_w', 'conv_f_w')


def _dot(a, b):
    return jnp.dot(a, b, preferred_element_type=F32)


def _dot_nt(a, b):
    return lax.dot_general(a, b, (((1,), (1,)), ((), ())), preferred_element_type=F32)


def _dot_tn(a, b):
    return lax.dot_general(a, b, (((0,), (0,)), ((), ())), preferred_element_type=F32)


def _sigmoid(v):
    return jax.nn.sigmoid(v)


def _token_tile(t, cap=512):
    return cap if (t % cap == 0 and t > 512) else 128


def _chunk(n, limit=512):
    for c in range(limit - limit % LANES, 0, -LANES):
        if n % c == 0:
            return c
    return n


def _row_tile(rows):
    for c in (512, 256, 128, 64, 32, 16, 8):
        if rows % c == 0:
            return c
    return rows


def _rows(tm, width):
    return pl.BlockSpec((tm, width), lambda i: (i, 0))


def _rows_rev(tm, width, nt):
    return pl.BlockSpec((tm, width), lambda i: (nt - 1 - i, 0))


def _whole(shape):
    nd = len(shape)
    return pl.BlockSpec(tuple(shape), lambda i: (0,) * nd)


def _resident(shape):
    nd = len(shape)
    return pl.BlockSpec(tuple(shape), lambda i: (0,) * nd, pipeline_mode=pl.Buffered(1))


def _sds(shape, dtype):
    return jax.ShapeDtypeStruct(tuple(shape), dtype)


def _rms_stats(xv):
    r = lax.rsqrt(jnp.mean(xv * xv, axis=-1, keepdims=True) + NORM_EPS)
    return xv * r, r


def _rms_bwd(dy, xh, r, g):
    dxh = dy * g
    dx = r * (dxh - xh * jnp.mean(dxh * xh, axis=-1, keepdims=True))
    return dx, jnp.sum(dy * xh, axis=0, keepdims=True)


GELU_C0 = 0.7978845608028654
GELU_C1 = GELU_C0 * 0.044715


def _gelu_tanh(v):
    v2 = v * v
    t = jnp.tanh(v * (GELU_C0 + GELU_C1 * v2))
    q = 1.0 + t
    hv = 0.5 * v
    grad = 0.5 * q + hv * (1.0 - t * t) * (GELU_C0 + (3.0 * GELU_C1) * v2)
    return hv * q, grad


ROW_CHUNK = 32
LANE_CHUNK = 512
MXU_COLUMNS = 256


def _residues(taps):
    return [0] + sorted({off % SUBLANES for _, off in taps} - {0})


def _fill_rotations(rot_ref, residues, length):
    for plane, r in enumerate(residues):
        if r:
            rot_ref[plane, 0:length, :] = rot_ref[0, pl.ds(r, length), :]


def _broadcast_rows(dst_ref, src_ref, count):
    for k in range(count):
        dst_ref[k] = jnp.broadcast_to(src_ref[k:k + 1, :], dst_ref.shape[1:])


def _lane_chunks(width, chunk=LANE_CHUNK):
    return [(c0, min(chunk, width - c0)) for c0 in range(0, width, chunk)]


def _tap_conv(rot_ref, taps, wb_ref, out_ref, tm, bias_plane=None):
    residues = _residues(taps)
    plane = {r: p for p, r in enumerate(residues)}
    blocks = ROW_CHUNK // SUBLANES
    width = out_ref.shape[1]

    def chunk(c, state):
        r0 = c * ROW_CHUNK
        for c0, cw in _lane_chunks(width):
            accs = [None if bias_plane is None else wb_ref[bias_plane, :, c0:c0 + cw]] * blocks
            for k, off in taps:
                wk = wb_ref[k, :, c0:c0 + cw]
                base = off - off % SUBLANES
                for j in range(blocks):
                    at = pl.multiple_of(r0 + base + SUBLANES * j, SUBLANES)
                    term = wk * rot_ref[plane[off % SUBLANES], pl.ds(at, SUBLANES), c0:c0 + cw]
                    accs[j] = term if accs[j] is None else accs[j] + term
            for j in range(blocks):
                at = pl.multiple_of(r0 + SUBLANES * j, SUBLANES)
                out_ref[pl.ds(at, SUBLANES), c0:c0 + cw] = accs[j]
        return state

    lax.fori_loop(0, tm // ROW_CHUNK, chunk, 0)


def _tap_wgrad(rot_ref, taps, x_ref, acc_ref, tm):
    residues = _residues(taps)
    plane = {r: p for p, r in enumerate(residues)}
    blocks = ROW_CHUNK // SUBLANES
    width = acc_ref.shape[2]

    def chunk(c, state):
        r0 = c * ROW_CHUNK
        for c0, cw in _lane_chunks(width):
            xs = [x_ref[pl.ds(pl.multiple_of(r0 + SUBLANES * j, SUBLANES), SUBLANES), c0:c0 + cw]
                  for j in range(blocks)]
            for k, off in taps:
                base = off - off % SUBLANES
                part = None
                for j in range(blocks):
                    at = pl.multiple_of(r0 + base + SUBLANES * j, SUBLANES)
                    term = xs[j] * rot_ref[plane[off % SUBLANES], pl.ds(at, SUBLANES), c0:c0 + cw]
                    part = term if part is None else part + term
                acc_ref[k, :, c0:c0 + cw] += part
        return state

    lax.fori_loop(0, tm // ROW_CHUNK, chunk, 0)


def _fwd_taps(width):
    return [(k, HALO - (width - 1) + k) for k in range(width)]


def _bwd_taps(width):
    return [(k, width - 1 - k) for k in range(width)]


def _my_index():
    return 4 * lax.axis_index("x") + 2 * lax.axis_index("y") + lax.axis_index("c")


def _mesh_id(idx):
    return (idx // 4, (idx // 2) % 2, idx % 2)


def _slab(ref, axis, idx, width):
    at = [slice(None)] * len(ref.shape)
    at[axis] = pl.ds(pl.multiple_of(idx * width, width), width)
    return ref.at[tuple(at)]


class _Exchange:
    def __init__(self, inputs, out_shape):
        n = len(inputs)
        self.inputs = list(inputs)
        self.out_shape = list(out_shape)
        self.sems = [pltpu.SemaphoreType.DMA((n, N_DEV - 1)), pltpu.SemaphoreType.DMA((n, N_DEV - 1)),
                     pltpu.SemaphoreType.DMA((n,))]
        self.results = None

    def _local(self, ins, outs, k, me):
        raise NotImplementedError

    def _remote(self, ins, outs, k, me, sender, receiver):
        raise NotImplementedError

    def start(self, ins, outs, sems):
        send_sems, recv_sems, local_sems = sems
        me = _my_index()
        for k in range(len(self.inputs)):
            src, dst = self._local(ins, outs, k, me)
            pltpu.make_async_copy(src, dst, local_sems.at[k]).start()
            for dist in range(1, N_DEV):
                peer = (me + dist) % N_DEV
                src, dst = self._remote(ins, outs, k, me, me, peer)
                pltpu.make_async_remote_copy(
                    src_ref=src, dst_ref=dst, send_sem=send_sems.at[k, dist - 1],
                    recv_sem=recv_sems.at[k, dist - 1], device_id=_mesh_id(peer), device_id_type=MESH).start()

    def wait(self, ins, outs, sems):
        send_sems, recv_sems, local_sems = sems
        me = _my_index()
        for k in range(len(self.inputs)):
            for dist in range(1, N_DEV):
                sender = (me + N_DEV - dist) % N_DEV
                src, dst = self._remote(ins, outs, k, me, sender, me)
                cp = pltpu.make_async_remote_copy(
                    src_ref=src, dst_ref=dst, send_sem=send_sems.at[k, dist - 1],
                    recv_sem=recv_sems.at[k, dist - 1], device_id=_mesh_id(sender), device_id_type=MESH)
                cp.wait_send()
                cp.wait_recv()
            src, dst = self._local(ins, outs, k, me)
            pltpu.make_async_copy(src, dst, local_sems.at[k]).wait()

    def forward(self, ins, outs, sems):
        pass


class _Gather(_Exchange):
    FLIPS = ((1, 0), (0, 1), (1, 1))

    def __init__(self, items):
        self.items = list(items)
        out_shape = []
        for shards, layer, axis in self.items:
            shape = list(shards.shape if layer is None else shards.shape[1:])
            shape[axis] *= N_DEV
            out_shape.append(_sds(shape, shards.dtype))
        super().__init__([it[0] for it in self.items], out_shape)

    def _src(self, ins, k):
        layer = self.items[k][1]
        return ins[k] if layer is None else ins[k].at[layer]

    def _place(self, outs, k, idx):
        axis = self.items[k][2]
        return _slab(outs[k], axis, idx, self.out_shape[k].shape[axis] // N_DEV)

    def _copy(self, sems, k, j, src, dst, to):
        return pltpu.make_async_remote_copy(src_ref=src, dst_ref=dst, send_sem=sems[0].at[k, j],
                                            recv_sem=sems[1].at[k, j], device_id=to, device_id_type=MESH)

    @staticmethod
    def _places():
        x, y, c = lax.axis_index("x"), lax.axis_index("y"), lax.axis_index("c")
        chips = [(1 - x if fx else x, 1 - y if fy else y) for fx, fy in _Gather.FLIPS]
        return (x, y, c), (x, y, 1 - c), chips

    @staticmethod
    def _index(place):
        return 4 * place[0] + 2 * place[1] + place[2]

    def start(self, ins, outs, sems):
        me, sibling, chips = self._places()
        for k in range(len(self.inputs)):
            src, mine = self._src(ins, k), self._place(outs, k, self._index(me))
            pltpu.make_async_copy(src, mine, sems[2].at[k]).start()
            self._copy(sems, k, 0, src, mine, sibling).start()
            for j, chip in enumerate(chips):
                self._copy(sems, k, 1 + j, src, mine, (*chip, me[2])).start()

    def forward(self, ins, outs, sems):
        me, sibling, chips = self._places()
        for k in range(len(self.inputs)):
            for j, chip in enumerate(chips):
                got = self._place(outs, k, self._index((*chip, me[2])))
                self._copy(sems, k, 1 + j, got, got, (*chip, me[2])).wait_recv()
                self._copy(sems, k, 4 + j, got, got, sibling).start()

    def wait(self, ins, outs, sems):
        me, sibling, chips = self._places()
        for k in range(len(self.inputs)):
            src, mine = self._src(ins, k), self._place(outs, k, self._index(me))
            self._copy(sems, k, 0, src, self._place(outs, k, self._index(sibling)), sibling).wait_recv()
            for j, chip in enumerate(chips):
                got = self._place(outs, k, self._index((*chip, sibling[2])))
                self._copy(sems, k, 4 + j, got, got, sibling).wait_recv()
            for j in range(N_DEV - 1):
                self._copy(sems, k, j, src, mine, sibling).wait_send()
            pltpu.make_async_copy(src, mine, sems[2].at[k]).wait()


class _Scatter(_Exchange):
    def __init__(self, items):
        self.items = list(items)
        out_shape = []
        for partial, axis in self.items:
            shape = list(partial.shape)
            shape[axis] //= N_DEV
            out_shape.append(_sds([N_DEV] + shape, partial.dtype))
        super().__init__([it[0] for it in self.items], out_shape)

    def _take(self, ins, k, idx):
        axis = self.items[k][1]
        return _slab(ins[k], axis, idx, self.items[k][0].shape[axis] // N_DEV)

    def _local(self, ins, outs, k, me):
        return self._take(ins, k, me), outs[k].at[me]

    def _remote(self, ins, outs, k, me, sender, receiver):
        return self._take(ins, k, receiver), outs[k].at[sender]


def _run_exchange(exchange, name):
    n = len(exchange.inputs)

    def body(*refs):
        ins, outs, sems = refs[:n], refs[n:2 * n], refs[2 * n:]
        exchange.start(ins, outs, sems)
        exchange.forward(ins, outs, sems)
        exchange.wait(ins, outs, sems)

    any_spec = pl.BlockSpec(memory_space=pl.ANY)
    exchange.results = pl.pallas_call(
        body, name=name, in_specs=[any_spec] * n, out_specs=[any_spec] * n, out_shape=exchange.out_shape,
        scratch_shapes=exchange.sems)(*exchange.inputs)
    return exchange.results


class _Layer:
    def __init__(self, stack, index):
        self.stack, self.index = stack, index
        self.shape = (1,) + stack.shape[1:] if stack.ndim == 2 else stack.shape[1:]

    def view(self, ref):
        return ref.at[pl.ds(self.index, 1)] if self.stack.ndim == 2 else ref.at[self.index]


class _LayerTokens:
    def __init__(self, stack, index):
        self.stack, self.index = stack, index

    def rows(self, tm):
        return pl.BlockSpec((None, None, tm, self.stack.shape[-1]), lambda i: (self.index, 0, i, 0))


def _call(body, *, name, grid, in_specs, out_specs, out_shape, args, scratch=(), carry=None):
    in_specs, out_specs, out_shape, scratch = list(in_specs), list(out_specs), list(out_shape), list(scratch)
    args = [a.stack if isinstance(a, _LayerTokens) else a for a in args]
    layers = {k: a for k, a in enumerate(args) if isinstance(a, _Layer)}
    for k, a in layers.items():
        in_specs[k], args[k] = _whole(a.stack.shape), a.stack
    params = pltpu.CompilerParams(dimension_semantics=("arbitrary",) * len(grid),
                                  vmem_limit_bytes=VMEM_LIMIT_BYTES)
    n_in, n_out, n_scr = len(in_specs), len(out_specs), len(scratch)
    n_x = 0 if carry is None else len(carry.inputs)
    steps = 1
    for extent in grid:
        steps *= extent
    assert carry is None or steps >= 3, "a carrier needs a step each for start, second stage and wait"

    def whole_body(*refs):
        core_in, x_in = list(refs[:n_in]), refs[n_in:n_in + n_x]
        refs = refs[n_in + n_x:]
        core_out, x_out = refs[:n_out], refs[n_out:n_out + n_x]
        refs = refs[n_out + n_x:]
        core_scr, sems = refs[:n_scr], refs[n_scr:]
        for k, a in layers.items():
            core_in[k] = a.view(core_in[k])
        if carry is None:
            body(*core_in, *core_out, *core_scr)
            return
        step = pl.program_id(0)
        for axis in range(1, len(grid)):
            step = step * grid[axis] + pl.program_id(axis)

        @pl.when(step == 0)
        def _():
            carry.start(x_in, x_out, sems)

        @pl.when(step == steps - 2)
        def _():
            carry.forward(x_in, x_out, sems)

        body(*core_in, *core_out, *core_scr)

        @pl.when(step == steps - 1)
        def _():
            carry.wait(x_in, x_out, sems)

    any_spec = pl.BlockSpec(memory_space=pl.ANY)
    extra_in = [] if carry is None else carry.inputs
    extra_shape = [] if carry is None else carry.out_shape
    extra_sems = [] if carry is None else carry.sems
    outs = pl.pallas_call(
        whole_body, name=name, grid=grid, in_specs=in_specs + [any_spec] * n_x,
        out_specs=out_specs + [any_spec] * n_x, out_shape=out_shape + extra_shape,
        scratch_shapes=scratch + extra_sems, compiler_params=params)(*args, *extra_in)
    if carry is not None:
        carry.results = outs[n_out:]
    return outs[:n_out]


def _all_reduce(parts, plan, name):
    n = len(parts)
    out_shape = []
    for group in plan:
        shape = parts[group[0]].shape
        if len(group) > 1:
            shape = (len(group),) + (shape[1:] if shape[0] == 1 else shape)
        out_shape.append(_sds(shape, F32))

    def body(*refs):
        ins, outs, lands = refs[:n], refs[n:n + len(plan)], refs[n + len(plan):2 * n + len(plan)]
        send_sems, recv_sems = refs[2 * n + len(plan):]
        me = _my_index()
        for k in range(n):
            lands[k][me] = ins[k][...]
            for dist in range(1, N_DEV):
                pltpu.make_async_remote_copy(
                    src_ref=ins[k], dst_ref=lands[k].at[me],
                    send_sem=send_sems.at[k, dist - 1], recv_sem=recv_sems.at[k, dist - 1],
                    device_id=_mesh_id((me + dist) % N_DEV), device_id_type=MESH).start()
        for k in range(n):
            for dist in range(1, N_DEV):
                sender = (me + N_DEV - dist) % N_DEV
                cp = pltpu.make_async_remote_copy(
                    src_ref=ins[k], dst_ref=lands[k].at[sender],
                    send_sem=send_sems.at[k, dist - 1], recv_sem=recv_sems.at[k, dist - 1],
                    device_id=_mesh_id(sender), device_id_type=MESH)
                cp.wait_send()
                cp.wait_recv()
        for o_ref, group in zip(outs, plan):
            for j, k in enumerate(group):
                total = lands[k][0]
                for dev in range(1, N_DEV):
                    total = total + lands[k][dev]
                if len(group) == 1:
                    o_ref[...] = total
                elif parts[k].shape[0] == 1:
                    o_ref[j:j + 1, :] = total
                else:
                    o_ref[j] = total

    vmem = pl.BlockSpec(memory_space=pltpu.VMEM)
    return pl.pallas_call(
        body, name=name, in_specs=[vmem] * n, out_specs=[vmem] * len(plan), out_shape=out_shape,
        scratch_shapes=[pltpu.VMEM((N_DEV,) + part.shape, F32) for part in parts]
        + [pltpu.SemaphoreType.DMA((n, N_DEV - 1)), pltpu.SemaphoreType.DMA((n, N_DEV - 1))])(*parts)


def _norm_matmul(x, g, wt, name, carry=None):
    t, d = x.shape
    n = wt.shape[0]
    tm, nc = _token_tile(t), _chunk(n)

    def body(x_ref, g_ref, wt_ref, h_ref, o_ref):
        xh, _ = _rms_stats(x_ref[...])
        h = (xh * g_ref[...]).astype(BF16)
        h_ref[...] = h
        for n0 in range(0, n, nc):
            o_ref[:, n0:n0 + nc] = _dot_nt(h, wt_ref[n0:n0 + nc, :]).astype(BF16)

    return _call(body, name=name, grid=(t // tm,),
                 in_specs=[_rows(tm, d), _whole(g.shape), _resident(wt.shape)],
                 out_specs=[_rows(tm, d), _rows(tm, n)],
                 out_shape=[_sds((t, d), BF16), _sds((t, n), BF16)], args=(x, g, wt), carry=carry)


def _fwd_branch(z, caw, cab, lng, lnb, cbw, dc, name, carry=None):
    t = z.shape[0]
    tm = _token_tile(t)
    ka, kb = caw.shape[0], cbw.shape[0]
    taps_a, taps_b = _fwd_taps(ka), _fwd_taps(kb)
    res_a, res_b = _residues(taps_a), _residues(taps_b)
    span = HALO + tm - SUBLANES

    def body(z_ref, caw_ref, cab_ref, lng_ref, lnb_ref, cbw_ref, ac_ref, act_ref, s_ref, cb_ref,
             rot_a, rot_b, wb_a, wb_b, cb):
        @pl.when(pl.program_id(0) == 0)
        def _():
            rot_a[0, 0:HALO, :] = jnp.zeros((HALO, dc), F32)
            rot_b[0, 0:HALO, :] = jnp.zeros((HALO, dc), F32)
            _broadcast_rows(wb_a, caw_ref, ka)
            wb_a[ka] = jnp.broadcast_to(cab_ref[...], (SUBLANES, dc))
            _broadcast_rows(wb_b, cbw_ref, kb)

        a_val = z_ref[:, 0:dc].astype(F32)
        a_gt = z_ref[:, dc:2 * dc].astype(F32)
        rot_a[0, HALO:HALO + tm, :] = a_val * _sigmoid(a_gt)
        _fill_rotations(rot_a, res_a, span)
        _tap_conv(rot_a, taps_a, wb_a, ac_ref, tm, bias_plane=ka)
        ac = ac_ref[...]
        mu = jnp.mean(ac, axis=-1, keepdims=True)
        xc = ac - mu
        var = jnp.mean(xc * xc, axis=-1, keepdims=True)
        ln = xc * lax.rsqrt(var + NORM_EPS) * lng_ref[...] + lnb_ref[...]
        act_ref[...] = (ln * _sigmoid(ln)).astype(BF16)
        rot_a[0, 0:HALO, :] = rot_a[0, tm:tm + HALO, :]

        sc_c = z_ref[:, 3 * dc:4 * dc].astype(F32)
        sc_v = z_ref[:, 4 * dc:5 * dc].astype(F32)
        rot_b[0, HALO:HALO + tm, :] = sc_c * sc_v
        _fill_rotations(rot_b, res_b, span)
        _tap_conv(rot_b, taps_b, wb_b, cb, tm)
        s_ref[...] = (z_ref[:, 2 * dc:3 * dc].astype(F32) * cb[...]).astype(BF16)
        cb_ref[...] = cb[...].astype(BF16)
        rot_b[0, 0:HALO, :] = rot_b[0, tm:tm + HALO, :]

    return _call(body, name=name, grid=(t // tm,),
                 in_specs=[_rows(tm, 5 * dc), _whole(caw.shape), _whole(cab.shape), _whole(lng.shape),
                           _whole(lnb.shape), _whole(cbw.shape)],
                 out_specs=[_rows(tm, dc), _rows(tm, dc), _rows(tm, dc), _rows(tm, dc)],
                 out_shape=[_sds((t, dc), F32), _sds((t, dc), BF16), _sds((t, dc), BF16), _sds((t, dc), BF16)],
                 scratch=[pltpu.VMEM((len(res_a), HALO + tm, dc), F32), pltpu.VMEM((len(res_b), HALO + tm, dc), F32),
                          pltpu.VMEM((ka + 1, SUBLANES, dc), F32), pltpu.VMEM((kb, SUBLANES, dc), F32),
                          pltpu.VMEM((tm, dc), F32)],
                 args=(z, caw, cab, lng, lnb, cbw), carry=carry)


def _fwd_merge(x, z, bg, a_act, s, wa, wb, wo, name, carry=None):
    t, d = x.shape
    n = z.shape[1]
    dc = a_act.shape[1]
    tm = _token_tile(t)
    o5 = n - 2 * d

    def body(x_ref, z_ref, bg_ref, act_ref, s_ref, wa_ref, wb_ref, wo_ref, o_ref):
        ya = _dot(act_ref[...], wa_ref[...])
        yb = _dot(s_ref[...], wb_ref[...])
        ga = _sigmoid(z_ref[:, o5:o5 + d].astype(F32) + bg_ref[:, 0:d])
        gb = _sigmoid(z_ref[:, o5 + d:n].astype(F32) + bg_ref[:, d:2 * d])
        m = (ga * ya + gb * yb).astype(BF16)
        o_ref[...] = x_ref[...] + _dot(m, wo_ref[...])

    return _call(body, name=name, grid=(t // tm,),
                 in_specs=[_rows(tm, d), _rows(tm, n), _whole(bg.shape), _rows(tm, dc), _rows(tm, dc),
                           _resident(wa.shape), _resident(wb.shape), _resident(wo.shape)],
                 out_specs=[_rows(tm, d)], out_shape=[_sds((t, d), F32)],
                 args=(x, z, bg, a_act, s, wa, wb, wo), carry=carry)[0]


def _fwd_down(x, u, cfw, cfb, wd, name, carry=None):
    t, d = x.shape
    f = u.shape[1] // 2
    tm = _token_tile(t, 256)
    kf = cfw.shape[0]

    taps = _fwd_taps(kf)
    residues = _residues(taps)

    def body(x_ref, u_ref, cfw_ref, cfb_ref, wd_ref, o_ref, act_ref, gl_ref, dgl_ref, rot_u, wb, fg):
        @pl.when(pl.program_id(0) == 0)
        def _():
            rot_u[0, 0:HALO, :] = jnp.zeros((HALO, f), F32)
            _broadcast_rows(wb, cfw_ref, kf)
            wb[kf] = jnp.broadcast_to(cfb_ref[...], (SUBLANES, f))

        rot_u[0, HALO:HALO + tm, :] = u_ref[:, 0:f].astype(F32)
        _fill_rotations(rot_u, residues, HALO + tm - SUBLANES)
        _tap_conv(rot_u, taps, wb, fg, tm, bias_plane=kf)
        y = x_ref[...]
        for c0, cw in _lane_chunks(f, MXU_COLUMNS):
            at = slice(c0, c0 + cw)
            gl, dgl = _gelu_tanh(fg[:, at])
            gl_ref[:, at] = gl.astype(BF16)
            dgl_ref[:, at] = dgl.astype(BF16)
            act = (gl * u_ref[:, f + c0:f + c0 + cw].astype(F32)).astype(BF16)
            act_ref[:, at] = act
            y = y + _dot(act, wd_ref[at, :])
        o_ref[...] = y
        rot_u[0, 0:HALO, :] = rot_u[0, tm:tm + HALO, :]

    return _call(body, name=name, grid=(t // tm,),
                 in_specs=[_rows(tm, d), _rows(tm, 2 * f), _whole(cfw.shape), _whole(cfb.shape),
                           _resident(wd.shape)],
                 out_specs=[_rows(tm, d), _rows(tm, f), _rows(tm, f), _rows(tm, f)],
                 out_shape=[_sds((t, d), F32), _sds((t, f), BF16), _sds((t, f), BF16), _sds((t, f), BF16)],
                 scratch=[pltpu.VMEM((len(residues), HALO + tm, f), F32), pltpu.VMEM((kf + 1, SUBLANES, f), F32),
                          pltpu.VMEM((tm, f), F32)],
                 args=(x, u, cfw, cfb, wd), carry=carry)


def _fwd_ple(x, g, wpg, p, wple, name, head=None):
    t, d = x.shape
    tm = _token_tile(t)

    def body(x_ref, g_ref, wpg_ref, p_ref, wple_ref, *rest):
        xv = x_ref[...]
        xh, _ = _rms_stats(xv)
        lg = _dot((xh * g_ref[...]).astype(BF16), wpg_ref[...])
        pp = _dot(p_ref[...].astype(BF16), wple_ref[...])
        y = xv + _sigmoid(lg) * pp
        if head is None:
            rest[0][...] = y
            return
        gf_ref, t_ref, dy_ref, dg_ref, loss_ref = rest

        @pl.when(pl.program_id(0) == 0)
        def _():
            dg_ref[...] = jnp.zeros_like(dg_ref)
            loss_ref[...] = jnp.zeros_like(loss_ref)

        yh, r = _rms_stats(y)
        err = yh * gf_ref[...] - t_ref[...]
        sq = jnp.sum(jnp.sum(err * err, axis=0, keepdims=True), axis=1, keepdims=True)
        loss_ref[...] += jnp.broadcast_to(0.5 * sq / d, loss_ref.shape)
        dy, dg = _rms_bwd(err / d, yh, r, gf_ref[...])
        dy_ref[...] = dy
        dg_ref[...] += dg

    in_specs = [_rows(tm, d), _whole(g.shape), _resident(wpg.shape), p.rows(tm), _resident(wple.shape)]
    if head is None:
        return _call(body, name=name, grid=(t // tm,), in_specs=in_specs, out_specs=[_rows(tm, d)],
                     out_shape=[_sds((t, d), F32)], args=(x, g, wpg, p, wple))[0]
    return _call(body, name=name, grid=(t // tm,),
                 in_specs=in_specs + [_whole(head[0].shape), _rows(tm, d)],
                 out_specs=[_rows(tm, d), _whole((1, d)), _whole((SUBLANES, LANES))],
                 out_shape=[_sds((t, d), F32), _sds((1, d), F32), _sds((SUBLANES, LANES), F32)],
                 args=(x, g, wpg, p, wple, *head))


def _bwd_ple(dy, x, g, wpg, p, wple, name, above=None):
    t, d = x.shape
    pd = p.stack.shape[-1]
    tm = _token_tile(t)
    nt = t // tm
    n_above = 0 if above is None else len(above)

    def body(*refs):
        head, refs = refs[:max(n_above, 1)], refs[max(n_above, 1):]
        x_ref, g_ref, wpg_ref, p_ref, wple_ref, dx_ref, dwpg_ref, dwple_ref, dg_ref = refs[:9]
        dg_in_ref = refs[9] if above is not None else None
        acc_pg, acc_ple = refs[-2:]
        i = pl.program_id(0)

        @pl.when(i == 0)
        def _():
            acc_pg[...] = jnp.zeros_like(acc_pg)
            acc_ple[...] = jnp.zeros_like(acc_ple)
            dg_ref[...] = jnp.zeros_like(dg_ref)
            if above is not None:
                dg_in_ref[...] = jnp.zeros_like(dg_in_ref)

        if above is None:
            dyv = head[0][...]
        else:
            dz_ref, wt_ref, xin_ref, gin_ref, dres_ref = head
            xh_in, r_in = _rms_stats(xin_ref[...])
            dx_in, dg_in = _rms_bwd(_dot(dz_ref[...], wt_ref[...]), xh_in, r_in, gin_ref[...])
            dyv = dres_ref[...] + dx_in
            dg_in_ref[...] += dg_in
        xh, r = _rms_stats(x_ref[...])
        h = (xh * g_ref[...]).astype(BF16)
        pb = p_ref[...].astype(BF16)
        pg = _sigmoid(_dot(h, wpg_ref[...]))
        pp = _dot(pb, wple_ref[...])
        dpp = (dyv * pg).astype(BF16)
        dlg = (dyv * pp * pg * (1.0 - pg)).astype(BF16)
        acc_ple[...] += _dot_tn(pb, dpp)
        acc_pg[...] += _dot_tn(h, dlg)
        dx, dg = _rms_bwd(_dot_nt(dlg, wpg_ref[...]), xh, r, g_ref[...])
        dx_ref[...] = dyv + dx
        dg_ref[...] += dg

        @pl.when(i == nt - 1)
        def _():
            dwpg_ref[...] = acc_pg[...].astype(BF16)
            dwple_ref[...] = acc_ple[...].astype(BF16)

    if above is None:
        head_specs, head_args, more_specs, more_shape = [_rows(tm, d)], (dy,), [], []
    else:
        dz, wt, x_in, g_in, dres = above
        head_specs = [_rows(tm, dz.shape[1]), _resident(wt.shape), _rows(tm, d), _whole(g_in.shape), _rows(tm, d)]
        head_args, more_specs, more_shape = above, [_whole((1, d))], [_sds((1, d), F32)]
    return _call(body, name=name, grid=(nt,),
                 in_specs=head_specs + [_rows(tm, d), _whole(g.shape), _resident(wpg.shape), p.rows(tm),
                                        _resident(wple.shape)],
                 out_specs=[_rows(tm, d), _whole((d, d)), _whole((pd, d)), _whole((1, d))] + more_specs,
                 out_shape=[_sds((t, d), F32), _sds((d, d), BF16), _sds((pd, d), BF16), _sds((1, d), F32)]
                 + more_shape,
                 scratch=[pltpu.VMEM((d, d), F32), pltpu.VMEM((pd, d), F32)],
                 args=(*head_args, x, g, wpg, p, wple))


def _bwd_down(dy, u, gl, dgl, cfw, wd, name, carry=None):
    t, d = dy.shape
    f = u.shape[1] // 2
    tm = _token_tile(t, 256)
    nt = t // tm
    kf = cfw.shape[0]
    bwd = _bwd_taps(kf)
    residues = _residues(bwd)

    def body(dy_ref, u_ref, gl_ref, dgl_ref, cfw_ref, wd_ref, du_ref, dcw_ref, dcb_ref, rot_g, wb, acc, buf):
        i = pl.program_id(0)

        @pl.when(i == 0)
        def _():
            for ref in (dcw_ref, dcb_ref, acc):
                ref[...] = jnp.zeros_like(ref)
            rot_g[0, tm:tm + HALO, :] = jnp.zeros((HALO, f), F32)
            _broadcast_rows(wb, cfw_ref, kf)

        dyb = dy_ref[...].astype(BF16)
        for c0, cw in _lane_chunks(f, MXU_COLUMNS):
            at = slice(c0, c0 + cw)
            df = _dot_nt(dyb, wd_ref[at, :])
            du_ref[:, f + c0:f + c0 + cw] = (df * gl_ref[:, at].astype(F32)).astype(BF16)
            dfg = df * u_ref[:, f + c0:f + c0 + cw].astype(F32) * dgl_ref[:, at].astype(F32)
            dcb_ref[:, at] += jnp.sum(dfg, axis=0, keepdims=True)
            rot_g[0, 0:tm, at] = dfg
        _fill_rotations(rot_g, residues, HALO + tm - SUBLANES)
        buf[...] = u_ref[:, 0:f].astype(F32)
        _tap_wgrad(rot_g, bwd, buf, acc, tm)
        _tap_conv(rot_g, bwd, wb, buf, tm)
        du_ref[:, 0:f] = buf[...].astype(BF16)
        rot_g[0, tm:tm + HALO, :] = rot_g[0, 0:HALO, :]

        @pl.when(i == nt - 1)
        def _():
            dcw_ref[0:kf, :] = jnp.sum(acc[...], axis=1)

    return _call(body, name=name, grid=(nt,),
                 in_specs=[_rows_rev(tm, d, nt), _rows_rev(tm, 2 * f, nt), _rows_rev(tm, f, nt),
                           _rows_rev(tm, f, nt), _whole(cfw.shape), _resident(wd.shape)],
                 out_specs=[_rows_rev(tm, 2 * f, nt), _whole((SUBLANES, f)), _whole((1, f))],
                 out_shape=[_sds((t, 2 * f), BF16), _sds((SUBLANES, f), F32), _sds((1, f), F32)],
                 scratch=[pltpu.VMEM((len(residues), HALO + tm, f), F32), pltpu.VMEM((kf, SUBLANES, f), F32),
                          pltpu.VMEM((kf, SUBLANES, f), F32), pltpu.VMEM((tm, f), F32)],
                 args=(dy, u, gl, dgl, cfw, wd), carry=carry)


def _bwd_norm_matmul(dout, wt, x, g, dres, name, carry=None):
    t, d = x.shape
    n = dout.shape[1]
    tm = _token_tile(t)

    def body(do_ref, wt_ref, x_ref, g_ref, dres_ref, dx_ref, dg_ref):
        @pl.when(pl.program_id(0) == 0)
        def _():
            dg_ref[...] = jnp.zeros_like(dg_ref)

        dh = _dot(do_ref[...], wt_ref[...])
        xh, r = _rms_stats(x_ref[...])
        dx, dg = _rms_bwd(dh, xh, r, g_ref[...])
        dx_ref[...] = dres_ref[...] + dx
        dg_ref[...] += dg

    return _call(body, name=name, grid=(t // tm,),
                 in_specs=[_rows(tm, n), _resident(wt.shape), _rows(tm, d), _whole(g.shape), _rows(tm, d)],
                 out_specs=[_rows(tm, d), _whole((1, d))],
                 out_shape=[_sds((t, d), F32), _sds((1, d), F32)],
                 args=(dout, wt, x, g, dres), carry=carry)


def _wgrad_tn(a, b, name, carry=None):
    t, n = a.shape
    d = b.shape[1]
    tt = 2048 if t % 2048 == 0 else _token_tile(t)
    tn = _chunk(n, 1536)
    nt = t // tt

    def body(a_ref, b_ref, o_ref, acc):
        k = pl.program_id(1)

        @pl.when(k == 0)
        def _():
            acc[...] = jnp.zeros_like(acc)

        acc[...] += _dot_tn(a_ref[...].astype(BF16), b_ref[...].astype(BF16))

        @pl.when(k == nt - 1)
        def _():
            o_ref[...] = acc[...].astype(BF16)

    return _call(body, name=name, grid=(n // tn, nt),
                 in_specs=[pl.BlockSpec((tt, tn), lambda j, k: (k, j)), pl.BlockSpec((tt, d), lambda j, k: (k, 0))],
                 out_specs=[pl.BlockSpec((tn, d), lambda j, k: (j, 0))], out_shape=[_sds((n, d), BF16)],
                 scratch=[pltpu.VMEM((tn, d), F32)], args=(a, b), carry=carry)[0]


def _bwd_merge(dy, z, bg, a_act, s, wa, wb, wo, name, carry=None):
    t, d = dy.shape
    n = z.shape[1]
    dc = a_act.shape[1]
    tm = _token_tile(t)
    nt = t // tm
    o5 = n - 2 * d

    def body(dy_ref, z_ref, bg_ref, act_ref, s_ref, wa_ref, wb_ref, wo_ref,
             dact_ref, ds_ref, dgl_ref, dwo_ref, dwa_ref, dwb_ref, dbg_ref, acc_o, acc_a, acc_b):
        i = pl.program_id(0)

        @pl.when(i == 0)
        def _():
            acc_o[...] = jnp.zeros_like(acc_o)
            acc_a[...] = jnp.zeros_like(acc_a)
            acc_b[...] = jnp.zeros_like(acc_b)
            dbg_ref[...] = jnp.zeros_like(dbg_ref)

        dyb = dy_ref[...].astype(BF16)
        dm = _dot_nt(dyb, wo_ref[...])
        ya = _dot(act_ref[...], wa_ref[...])
        yb = _dot(s_ref[...], wb_ref[...])
        ga = _sigmoid(z_ref[:, o5:o5 + d].astype(F32) + bg_ref[:, 0:d])
        gb = _sigmoid(z_ref[:, o5 + d:n].astype(F32) + bg_ref[:, d:2 * d])
        acc_o[...] += _dot_tn((ga * ya + gb * yb).astype(BF16), dyb)
        dya = (dm * ga).astype(BF16)
        dyb2 = (dm * gb).astype(BF16)
        acc_a[...] += _dot_tn(act_ref[...], dya)
        acc_b[...] += _dot_tn(s_ref[...], dyb2)
        dact_ref[...] = _dot_nt(dya, wa_ref[...])
        ds_ref[...] = _dot_nt(dyb2, wb_ref[...])
        dla = dm * ya * ga * (1.0 - ga)
        dlb = dm * yb * gb * (1.0 - gb)
        dgl_ref[:, 0:d] = dla.astype(BF16)
        dgl_ref[:, d:2 * d] = dlb.astype(BF16)
        dbg_ref[:, 0:d] += jnp.sum(dla, axis=0, keepdims=True)
        dbg_ref[:, d:2 * d] += jnp.sum(dlb, axis=0, keepdims=True)

        @pl.when(i == nt - 1)
        def _():
            dwo_ref[...] = acc_o[...].astype(BF16)
            dwa_ref[...] = acc_a[...].astype(BF16)
            dwb_ref[...] = acc_b[...].astype(BF16)

    return _call(body, name=name, grid=(nt,),
                 in_specs=[_rows(tm, d), _rows(tm, n), _whole(bg.shape), _rows(tm, dc), _rows(tm, dc),
                           _resident(wa.shape), _resident(wb.shape), _resident(wo.shape)],
                 out_specs=[_rows(tm, dc), _rows(tm, dc), _rows(tm, 2 * d), _whole((d, d)), _whole((dc, d)),
                            _whole((dc, d)), _whole((1, 2 * d))],
                 out_shape=[_sds((t, dc), F32), _sds((t, dc), F32), _sds((t, 2 * d), BF16), _sds((d, d), BF16),
                            _sds((dc, d), BF16), _sds((dc, d), BF16), _sds((1, 2 * d), F32)],
                 scratch=[pltpu.VMEM((d, d), F32), pltpu.VMEM((dc, d), F32), pltpu.VMEM((dc, d), F32)],
                 args=(dy, z, bg, a_act, s, wa, wb, wo), carry=carry)


def _bwd_branch(dact, ds, z, dgl, a_conv, cb, caw, lng, lnb, cbw, name, carry=None):
    t, n = z.shape
    dc = a_conv.shape[1]
    tm = _token_tile(t)
    nt = t // tm
    ka, kb = caw.shape[0], cbw.shape[0]
    bwd_a, bwd_b = _bwd_taps(ka), _bwd_taps(kb)
    span = HALO + tm - SUBLANES

    def body(dact_ref, ds_ref, z_ref, dgl_ref, ac_ref, cb_ref, caw_ref, lng_ref, lnb_ref, cbw_ref,
             dz_ref, dcaw_ref, dcab_ref, dlng_ref, dlnb_ref, dcbw_ref,
             rot_da, rot_dc, wb_a, wb_b, acc_a, acc_b, buf):
        i = pl.program_id(0)

        @pl.when(i == 0)
        def _():
            for ref in (dcaw_ref, dcab_ref, dlng_ref, dlnb_ref, dcbw_ref, acc_a, acc_b):
                ref[...] = jnp.zeros_like(ref)
            rot_da[0, tm:tm + HALO, :] = jnp.zeros((HALO, dc), F32)
            rot_dc[0, tm:tm + HALO, :] = jnp.zeros((HALO, dc), F32)
            _broadcast_rows(wb_a, caw_ref, ka)
            _broadcast_rows(wb_b, cbw_ref, kb)

        a_val = z_ref[:, 0:dc].astype(F32)
        sg = _sigmoid(z_ref[:, dc:2 * dc].astype(F32))

        ac = ac_ref[...]
        mu = jnp.mean(ac, axis=-1, keepdims=True)
        xc = ac - mu
        rstd = lax.rsqrt(jnp.mean(xc * xc, axis=-1, keepdims=True) + NORM_EPS)
        xh = xc * rstd
        ln = xh * lng_ref[...] + lnb_ref[...]
        sl = _sigmoid(ln)
        dln = dact_ref[...] * (sl * (1.0 + ln * (1.0 - sl)))
        dlng_ref[...] += jnp.sum(dln * xh, axis=0, keepdims=True)
        dlnb_ref[...] += jnp.sum(dln, axis=0, keepdims=True)
        dxh = dln * lng_ref[...]
        dac = rstd * (dxh - jnp.mean(dxh, axis=-1, keepdims=True)
                      - xh * jnp.mean(dxh * xh, axis=-1, keepdims=True))
        dcab_ref[...] += jnp.sum(dac, axis=0, keepdims=True)
        rot_da[0, 0:tm, :] = dac
        _fill_rotations(rot_da, _residues(bwd_a), span)
        buf[...] = a_val * sg
        _tap_wgrad(rot_da, bwd_a, buf, acc_a, tm)
        _tap_conv(rot_da, bwd_a, wb_a, buf, tm)
        rot_da[0, tm:tm + HALO, :] = rot_da[0, 0:HALO, :]
        da = buf[...]
        dz_ref[:, 0:dc] = (da * sg).astype(BF16)
        dz_ref[:, dc:2 * dc] = (da * a_val * sg * (1.0 - sg)).astype(BF16)

        sc_b = z_ref[:, 2 * dc:3 * dc].astype(F32)
        sc_c = z_ref[:, 3 * dc:4 * dc].astype(F32)
        sc_v = z_ref[:, 4 * dc:5 * dc].astype(F32)
        dsv = ds_ref[...]
        dz_ref[:, 2 * dc:3 * dc] = (dsv * cb_ref[...].astype(F32)).astype(BF16)
        rot_dc[0, 0:tm, :] = dsv * sc_b
        _fill_rotations(rot_dc, _residues(bwd_b), span)
        buf[...] = sc_c * sc_v
        _tap_wgrad(rot_dc, bwd_b, buf, acc_b, tm)
        _tap_conv(rot_dc, bwd_b, wb_b, buf, tm)
        rot_dc[0, tm:tm + HALO, :] = rot_dc[0, 0:HALO, :]
        dcv = buf[...]
        dz_ref[:, 3 * dc:4 * dc] = (dcv * sc_v).astype(BF16)
        dz_ref[:, 4 * dc:5 * dc] = (dcv * sc_c).astype(BF16)
        dz_ref[:, 5 * dc:n] = dgl_ref[...]

        @pl.when(i == nt - 1)
        def _():
            dcaw_ref[0:ka, :] = jnp.sum(acc_a[...], axis=1)
            dcbw_ref[0:kb, :] = jnp.sum(acc_b[...], axis=1)

    def planes(taps):
        return pltpu.VMEM((len(_residues(taps)), HALO + tm, dc), F32)

    return _call(body, name=name, grid=(nt,),
                 in_specs=[_rows_rev(tm, dc, nt), _rows_rev(tm, dc, nt), _rows_rev(tm, 5 * dc, nt),
                           _rows_rev(tm, n - 5 * dc, nt), _rows_rev(tm, dc, nt), _rows_rev(tm, dc, nt),
                           _whole(caw.shape), _whole(lng.shape), _whole(lnb.shape), _whole(cbw.shape)],
                 out_specs=[_rows_rev(tm, n, nt), _whole((HALO, dc)), _whole((1, dc)), _whole((1, dc)),
                            _whole((1, dc)), _whole((SUBLANES, dc))],
                 out_shape=[_sds((t, n), BF16), _sds((HALO, dc), F32), _sds((1, dc), F32), _sds((1, dc), F32),
                            _sds((1, dc), F32), _sds((SUBLANES, dc), F32)],
                 scratch=[planes(bwd_a), planes(bwd_b),
                          pltpu.VMEM((ka, SUBLANES, dc), F32), pltpu.VMEM((kb, SUBLANES, dc), F32),
                          pltpu.VMEM((ka, SUBLANES, dc), F32), pltpu.VMEM((kb, SUBLANES, dc), F32),
                          pltpu.VMEM((tm, dc), F32)],
                 args=(dact, ds, z, dgl, a_conv, cb, caw, lng, lnb, cbw), carry=carry)


def _land_specs(depth, nr, tr, cols):
    def spec(k):
        return pl.BlockSpec((N_DEV, tr, cols), lambda i: (0, jnp.clip(i - k * nr, 0, nr - 1), 0))
    return [spec(k) for k in range(depth)]


def _adamw_math(w, g, m, v):
    nm = ADAM_B1 * m + (1.0 - ADAM_B1) * g
    nv = ADAM_B2 * v + (1.0 - ADAM_B2) * (g * g)
    m_hat = nm / (1.0 - ADAM_B1 ** ADAM_STEP)
    v_hat = nv / (1.0 - ADAM_B2 ** ADAM_STEP)
    return -ADAM_LR * (m_hat / (jnp.sqrt(v_hat) + ADAM_EPS) + ADAM_WD * w), nm, nv


def _sum_adamw(lands, w, m, v, name):
    _, rows, cols = lands[0].shape
    tr = _row_tile(rows)
    nr = rows // tr
    depth = len(lands)

    def body(*refs):
        w_ref, m_ref, v_ref, g_ref, d_ref, nm_ref, nv_ref = refs[depth:]
        i = pl.program_id(0)
        for k in range(depth):
            @pl.when(i // nr == k)
            def _(k=k):
                acc = refs[k][0].astype(F32)
                for j in range(1, N_DEV):
                    acc = acc + refs[k][j].astype(F32)
                g_ref[...] = acc
                d_ref[...], nm_ref[...], nv_ref[...] = _adamw_math(w_ref[...], acc, m_ref[...], v_ref[...])

    spec = _rows(tr, cols)
    return _call(body, name=name, grid=(depth * nr,), in_specs=_land_specs(depth, nr, tr, cols) + [spec] * 3,
                 out_specs=[spec] * 4, out_shape=[_sds((depth * rows, cols), F32)] * 4, args=(*lands, w, m, v))


def _adamw_small(ws, gs, ms, vs, name):
    n = len(ws)

    def body(*refs):
        w_refs, g_refs, m_refs, v_refs = refs[:n], refs[n:2 * n], refs[2 * n:3 * n], refs[3 * n:4 * n]
        d_refs, nm_refs, nv_refs = refs[4 * n:5 * n], refs[5 * n:6 * n], refs[6 * n:]
        for k in range(n):
            d_refs[k][...], nm_refs[k][...], nv_refs[k][...] = _adamw_math(
                w_refs[k][...], g_refs[k][...], m_refs[k][...], v_refs[k][...])

    vmem = pl.BlockSpec(memory_space=pltpu.VMEM)
    outs = pl.pallas_call(
        body, name=name, in_specs=[vmem] * (4 * n), out_specs=[vmem] * (3 * n),
        out_shape=[_sds(a.shape, F32) for a in ws] * 3)(*ws, *gs, *ms, *vs)
    return outs[:n], outs[n:2 * n], outs[2 * n:]


def kernel(x, p, g_mix, w_in, b_gate, conv_a_w, conv_a_b, ln_a_g, ln_a_b, w_a_out, conv_b_w, w_b_out, w_o, g_ffn, w_up, conv_f_w, conv_f_b, w_down, g_ple, w_ple, w_ple_gate, g_final, loss_target, m_g_mix, m_w_in, m_b_gate, m_conv_a_w, m_conv_a_b, m_ln_a_g, m_ln_a_b, m_w_a_out, m_conv_b_w, m_w_b_out, m_w_o, m_g_ffn, m_w_up, m_conv_f_w, m_conv_f_b, m_w_down, m_g_ple, m_w_ple, m_w_ple_gate, m_g_final, v_g_mix, v_w_in, v_b_gate, v_conv_a_w, v_conv_a_b, v_ln_a_g, v_ln_a_b, v_w_a_out, v_conv_b_w, v_w_b_out, v_w_o, v_g_ffn, v_w_up, v_conv_f_w, v_conv_f_b, v_w_down, v_g_ple, v_w_ple, v_w_ple_gate, v_g_final):
    w = dict(zip(WEIGHT_NAMES, (g_mix, w_in, b_gate, conv_a_w, conv_a_b, ln_a_g, ln_a_b, w_a_out, conv_b_w,
                                w_b_out, w_o, g_ffn, w_up, conv_f_w, conv_f_b, w_down, g_ple, w_ple,
                                w_ple_gate, g_final)))
    mom = dict(zip(WEIGHT_NAMES, (m_g_mix, m_w_in, m_b_gate, m_conv_a_w, m_conv_a_b, m_ln_a_g, m_ln_a_b,
                                  m_w_a_out, m_conv_b_w, m_w_b_out, m_w_o, m_g_ffn, m_w_up, m_conv_f_w,
                                  m_conv_f_b, m_w_down, m_g_ple, m_w_ple, m_w_ple_gate, m_g_final)))
    var = dict(zip(WEIGHT_NAMES, (v_g_mix, v_w_in, v_b_gate, v_conv_a_w, v_conv_a_b, v_ln_a_g, v_ln_a_b,
                                  v_w_a_out, v_conv_b_w, v_w_b_out, v_w_o, v_g_ffn, v_w_up, v_conv_f_w,
                                  v_conv_f_b, v_w_down, v_g_ple, v_w_ple, v_w_ple_gate, v_g_final)))
    depth = g_mix.shape[0]
    dc = ln_a_g.shape[1]
    me = _my_index()
    x0 = x[0]
    target = loss_target[0]
    big_names = tuple(BIG_AXIS)

    shard = {name: (jnp.swapaxes(w[name], 1, 2) if name in TRANSPOSED else w[name]).astype(BF16)
             for name in big_names}

    def gather_of(layer, *names):
        return _Gather([(shard[name], layer, BIG_AXIS[name]) for name in names])

    def row(name, layer):
        return _Layer(w[name], layer)

    first = _Gather([(shard['w_in'], 0, BIG_AXIS['w_in'])] + [(w[name][None], None, 0) for name in CONV_SHARDED])
    gathered = _run_exchange(first, "gather_first")
    w_in_full = gathered[0]
    conv_full = {name: jnp.transpose(g, (1, 2, 0, 3)).reshape(g.shape[1], g.shape[2], -1)
                 for name, g in zip(CONV_SHARDED, gathered[1:])}
    saved = []
    xc = x0
    for l in range(depth):
        carry = gather_of(l, 'w_a_out', 'w_b_out', 'w_o', 'w_up')
        h, z = _norm_matmul(xc, row('g_mix', l), w_in_full, f"fwd_in_{l}", carry)
        wa_full, wb_full, wo_full, w_up_full = carry.results
        a_conv, a_act, s, cb = _fwd_branch(z, _Layer(conv_full['conv_a_w'], l), row('conv_a_b', l), row('ln_a_g', l),
                                       row('ln_a_b', l), _Layer(conv_full['conv_b_w'], l), dc, f"fwd_branch_{l}")
        x1 = _fwd_merge(xc, z, row('b_gate', l), a_act, s, wa_full, wb_full, wo_full, f"fwd_merge_{l}")
        carry = _Gather([(shard[name], l, BIG_AXIS[name]) for name in ('w_down', 'w_ple', 'w_ple_gate')]
                        + ([(shard['w_in'], l + 1, BIG_AXIS['w_in'])] if l + 1 < depth else []))
        h2, u = _norm_matmul(x1, row('g_ffn', l), w_up_full, f"fwd_up_{l}", carry)
        w_down_full, w_ple_full, w_pg_full = carry.results[:3]
        x2, act, gl, dgl = _fwd_down(x1, u, _Layer(conv_full['conv_f_w'], l), row('conv_f_b', l), w_down_full,
                                     f"fwd_down_{l}")
        last = _fwd_ple(x2, row('g_ple', l), w_pg_full, _LayerTokens(p, l), w_ple_full, f"fwd_ple_{l}",
                        head=(g_final[None], target) if l + 1 == depth else None)
        saved.append((xc, h, z, a_conv, a_act, s, cb, x1, h2, u, act, gl, dgl, x2,
                      dict(w_in=w_in_full, w_a_out=wa_full, w_b_out=wb_full, w_o=wo_full, w_up=w_up_full,
                           w_down=w_down_full, w_ple=w_ple_full, w_ple_gate=w_pg_full)))
        if l + 1 < depth:
            w_in_full = carry.results[3]
        xc = last

    dx, dg_final, loss_part = last
    above = None
    landed = {name: [None] * depth for name in big_names}
    small = {name: [None] * depth for name in WEIGHT_NAMES if name not in BIG_AXIS and name != 'g_final'}

    def scatter_of(*partials):
        ex = _Scatter([(part, BIG_AXIS[name]) for name, _, part in partials])
        ex.places = [(name, layer) for name, layer, _ in partials]
        return ex

    def keep(ex):
        for (name, layer), land in zip(ex.places, ex.results):
            landed[name][layer] = land

    pending = []
    for l in reversed(range(depth)):
        xin, h, z, a_conv, a_act, s, cb, x1, h2, u, act, gl, dgl, x2, full = saved[l]
        outs = _bwd_ple(dx, x2, row('g_ple', l), full['w_ple_gate'], _LayerTokens(p, l), full['w_ple'],
                        f"bwd_ple_{l}", above)
        dx2, d_wpg, d_wple, small['g_ple'][l] = outs[:4]
        if above is not None:
            small['g_mix'][l + 1] = outs[4]
        d_wdown = _wgrad_tn(act, dx2, f"wgrad_down_{l}")
        carry = scatter_of(('w_ple_gate', l, d_wpg), ('w_ple', l, d_wple), ('w_down', l, d_wdown))
        du, small['conv_f_w'][l], small['conv_f_b'][l] = _bwd_down(
            dx2, u, gl, dgl, _Layer(conv_full['conv_f_w'], l), full['w_down'], f"bwd_down_{l}", carry)
        keep(carry)
        carry = scatter_of(*pending) if pending else None
        pending = []
        dx1, small['g_ffn'][l] = _bwd_norm_matmul(du, full['w_up'], x1, row('g_ffn', l), dx2, f"bwd_up_{l}", carry)
        if carry is not None:
            keep(carry)
        d_wup = _wgrad_tn(du, h2, f"wgrad_up_{l}")
        dact, ds, dgate, d_wo, d_wa, d_wb, small['b_gate'][l] = _bwd_merge(
            dx1, z, row('b_gate', l), a_act, s, full['w_a_out'], full['w_b_out'], full['w_o'], f"bwd_merge_{l}")
        carry = scatter_of(('w_up', l, d_wup))
        (dz, small['conv_a_w'][l], small['conv_a_b'][l], small['ln_a_g'][l], small['ln_a_b'][l],
         small['conv_b_w'][l]) = _bwd_branch(
            dact, ds, z, dgate, a_conv, cb, _Layer(conv_full['conv_a_w'], l), row('ln_a_g', l), row('ln_a_b', l),
            _Layer(conv_full['conv_b_w'], l), f"bwd_branch_{l}", carry)
        keep(carry)
        carry = scatter_of(('w_o', l, d_wo), ('w_a_out', l, d_wa), ('w_b_out', l, d_wb))
        d_win = _wgrad_tn(dz, h, f"wgrad_in_{l}", carry)
        keep(carry)
        if l > 0:
            pending = [('w_in', l, d_win)]
            above, dx = (dz, full['w_in'], xin, row('g_mix', l), dx1), None
        else:
            carry = scatter_of(('w_in', l, d_win))
            dx, small['g_mix'][l] = _bwd_norm_matmul(dz, full['w_in'], xin, row('g_mix', l), dx1, f"bwd_in_{l}",
                                                     carry)
            keep(carry)
    grad_x = dx[None]

    small_names = tuple(small)
    parts = [part for name in small_names for part in small[name]] + [dg_final, loss_part]
    plan = [tuple(range(k * depth, (k + 1) * depth)) for k in range(len(small_names))]
    plan += [(len(parts) - 2,), (len(parts) - 1,)]
    reduced = _all_reduce(parts, plan, "all_reduce_small")
    loss = reduced[-1][0, 0]
    grads = dict(zip(small_names, reduced[:len(small_names)]))
    grads['g_final'] = reduced[len(small_names)].reshape(g_final.shape)
    for name in CONV_SHARDED:
        _, taps, width = w[name].shape
        grads[name] = lax.dynamic_slice(grads[name], (0, 0, me * width), (depth, taps, width))

    delta, new_m, new_v = {}, {}, {}
    for name in big_names:
        view = (lambda a: jnp.swapaxes(a, 1, 2)) if name in TRANSPOSED else (lambda a: a)
        shape = view(w[name]).shape
        flat = lambda a: view(a).reshape(-1, shape[-1])
        lands = [land.reshape(N_DEV, -1, shape[-1]) for land in landed[name]]
        outs = _sum_adamw(lands, flat(w[name]), flat(mom[name]), flat(var[name]), f"adamw_{name}")
        grads[name], delta[name], new_m[name], new_v[name] = [view(a.reshape(shape)) for a in outs]
    rest = tuple(name for name in WEIGHT_NAMES if name not in BIG_AXIS)
    as_2d = lambda a: a.reshape(1, -1) if a.ndim == 1 else a
    outs = _adamw_small(*[[as_2d(src[name]) for name in rest] for src in (w, grads, mom, var)], "adamw_small")
    for dst, values in zip((delta, new_m, new_v), outs):
        dst.update({name: value.reshape(w[name].shape) for name, value in zip(rest, values)})

    return (loss, grad_x, *[grads[n] for n in WEIGHT_NAMES], *[delta[n] for n in WEIGHT_NAMES],
            *[new_m[n] for n in WEIGHT_NAMES], *[new_v[n] for n in WEIGHT_NAMES])
```

```python
import jax
import jax.numpy as jnp
from jax import lax
from jax.experimental import pallas as pl
from jax.experimental.pallas import tpu as pltpu

F32 = jnp.float32
BF16 = jnp.bfloat16
MESH = pl.DeviceIdType.MESH

N_DEV = 8
NORM_EPS = 1e-6
HALO = 32
LANES = 128
SUBLANES = 8
VMEM_LIMIT_BYTES = 56 * 2**20

ADAM_LR = 0.001
ADAM_B1 = 0.9
ADAM_B2 = 0.999
ADAM_EPS = 1e-08
ADAM_WD = 0.01
ADAM_STEP = 10

WEIGHT_NAMES = ('g_mix', 'w_in', 'b_gate', 'conv_a_w', 'conv_a_b', 'ln_a_g', 'ln_a_b', 'w_a_out',
                'conv_b_w', 'w_b_out', 'w_o', 'g_ffn', 'w_up', 'conv_f_w', 'conv_f_b', 'w_down',
                'g_ple', 'w_ple', 'w_ple_gate', 'g_final')
BIG_AXIS = {'w_in': 0, 'w_up': 0, 'w_a_out': 1, 'w_b_out': 1, 'w_o': 0, 'w_down': 0, 'w_ple': 1,
            'w_ple_gate': 0}
TRANSPOSED = ('w_in', 'w_up')
CONV_SHARDED = ('conv_a_w', 'conv_b_w', 'conv_f_w')


def _dot(a, b):
    return jnp.dot(a, b, preferred_element_type=F32)


def _dot_nt(a, b):
    return lax.dot_general(a, b, (((1,), (1,)), ((), ())), preferred_element_type=F32)


def _dot_tn(a, b):
    return lax.dot_general(a, b, (((0,), (0,)), ((), ())), preferred_element_type=F32)


def _sigmoid(v):
    return jax.nn.sigmoid(v)


def _token_tile(t, cap=512):
    return cap if (t % cap == 0 and t > 512) else 128


def _chunk(n, limit=512):
    for c in range(limit - limit % LANES, 0, -LANES):
        if n % c == 0:
            return c
    return n


def _row_tile(rows):
    for c in (512, 256, 128, 64, 32, 16, 8):
        if rows % c == 0:
            return c
    return rows


def _rows(tm, width):
    return pl.BlockSpec((tm, width), lambda i: (i, 0))


def _rows_rev(tm, width, nt):
    return pl.BlockSpec((tm, width), lambda i: (nt - 1 - i, 0))


def _whole(shape):
    nd = len(shape)
    return pl.BlockSpec(tuple(shape), lambda i: (0,) * nd)


def _resident(shape):
    nd = len(shape)
    return pl.BlockSpec(tuple(shape), lambda i: (0,) * nd, pipeline_mode=pl.Buffered(1))


def _sds(shape, dtype):
    return jax.ShapeDtypeStruct(tuple(shape), dtype)


def _rms_stats(xv):
    r = lax.rsqrt(jnp.mean(xv * xv, axis=-1, keepdims=True) + NORM_EPS)
    return xv * r, r


def _rms_bwd(dy, xh, r, g):
    dxh = dy * g
    dx = r * (dxh - xh * jnp.mean(dxh * xh, axis=-1, keepdims=True))
    return dx, jnp.sum(dy * xh, axis=0, keepdims=True)


GELU_C0 = 0.7978845608028654
GELU_C1 = GELU_C0 * 0.044715


def _gelu_tanh(v):
    v2 = v * v
    t = jnp.tanh(v * (GELU_C0 + GELU_C1 * v2))
    q = 1.0 + t
    hv = 0.5 * v
    grad = 0.5 * q + hv * (1.0 - t * t) * (GELU_C0 + (3.0 * GELU_C1) * v2)
    return hv * q, grad


ROW_CHUNK = 32
LANE_CHUNK = 512
MXU_COLUMNS = 256


def _residues(taps):
    return [0] + sorted({off % SUBLANES for _, off in taps} - {0})


def _fill_rotations(rot_ref, residues, length):
    for plane, r in enumerate(residues):
        if r:
            rot_ref[plane, 0:length, :] = rot_ref[0, pl.ds(r, length), :]


def _broadcast_rows(dst_ref, src_ref, count):
    for k in range(count):
        dst_ref[k] = jnp.broadcast_to(src_ref[k:k + 1, :], dst_ref.shape[1:])


def _lane_chunks(width, chunk=LANE_CHUNK):
    return [(c0, min(chunk, width - c0)) for c0 in range(0, width, chunk)]


def _tap_conv(rot_ref, taps, wb_ref, out_ref, tm, bias_plane=None):
    residues = _residues(taps)
    plane = {r: p for p, r in enumerate(residues)}
    blocks = ROW_CHUNK // SUBLANES
    width = out_ref.shape[1]

    def chunk(c, state):
        r0 = c * ROW_CHUNK
        for c0, cw in _lane_chunks(width):
            accs = [None if bias_plane is None else wb_ref[bias_plane, :, c0:c0 + cw]] * blocks
            for k, off in taps:
                wk = wb_ref[k, :, c0:c0 + cw]
                base = off - off % SUBLANES
                for j in range(blocks):
                    at = pl.multiple_of(r0 + base + SUBLANES * j, SUBLANES)
                    term = wk * rot_ref[plane[off % SUBLANES], pl.ds(at, SUBLANES), c0:c0 + cw]
                    accs[j] = term if accs[j] is None else accs[j] + term
            for j in range(blocks):
                at = pl.multiple_of(r0 + SUBLANES * j, SUBLANES)
                out_ref[pl.ds(at, SUBLANES), c0:c0 + cw] = accs[j]
        return state

    lax.fori_loop(0, tm // ROW_CHUNK, chunk, 0)


def _tap_wgrad(rot_ref, taps, x_ref, acc_ref, tm):
    residues = _residues(taps)
    plane = {r: p for p, r in enumerate(residues)}
    blocks = ROW_CHUNK // SUBLANES
    width = acc_ref.shape[2]

    def chunk(c, state):
        r0 = c * ROW_CHUNK
        for c0, cw in _lane_chunks(width):
            xs = [x_ref[pl.ds(pl.multiple_of(r0 + SUBLANES * j, SUBLANES), SUBLANES), c0:c0 + cw]
                  for j in range(blocks)]
            for k, off in taps:
                base = off - off % SUBLANES
                part = None
                for j in range(blocks):
                    at = pl.multiple_of(r0 + base + SUBLANES * j, SUBLANES)
                    term = xs[j] * rot_ref[plane[off % SUBLANES], pl.ds(at, SUBLANES), c0:c0 + cw]
                    part = term if part is None else part + term
                acc_ref[k, :, c0:c0 + cw] += part
        return state

    lax.fori_loop(0, tm // ROW_CHUNK, chunk, 0)


def _fwd_taps(width):
    return [(k, HALO - (width - 1) + k) for k in range(width)]


def _bwd_taps(width):
    return [(k, width - 1 - k) for k in range(width)]


def _my_index():
    return 4 * lax.axis_index("x") + 2 * lax.axis_index("y") + lax.axis_index("c")


def _mesh_id(idx):
    return (idx // 4, (idx // 2) % 2, idx % 2)


def _slab(ref, axis, idx, width):
    at = [slice(None)] * len(ref.shape)
    at[axis] = pl.ds(pl.multiple_of(idx * width, width), width)
    return ref.at[tuple(at)]


class _Exchange:
    def __init__(self, inputs, out_shape):
        n = len(inputs)
        self.inputs = list(inputs)
        self.out_shape = list(out_shape)
        self.sems = [pltpu.SemaphoreType.DMA((n, N_DEV - 1)), pltpu.SemaphoreType.DMA((n, N_DEV - 1)),
                     pltpu.SemaphoreType.DMA((n,))]
        self.results = None

    def _local(self, ins, outs, k, me):
        raise NotImplementedError

    def _remote(self, ins, outs, k, me, sender, receiver):
        raise NotImplementedError

    def start(self, ins, outs, sems):
        send_sems, recv_sems, local_sems = sems
        me = _my_index()
        for k in range(len(self.inputs)):
            src, dst = self._local(ins, outs, k, me)
            pltpu.make_async_copy(src, dst, local_sems.at[k]).start()
            for dist in range(1, N_DEV):
                peer = (me + dist) % N_DEV
                src, dst = self._remote(ins, outs, k, me, me, peer)
                pltpu.make_async_remote_copy(
                    src_ref=src, dst_ref=dst, send_sem=send_sems.at[k, dist - 1],
                    recv_sem=recv_sems.at[k, dist - 1], device_id=_mesh_id(peer), device_id_type=MESH).start()

    def wait(self, ins, outs, sems):
        send_sems, recv_sems, local_sems = sems
        me = _my_index()
        for k in range(len(self.inputs)):
            for dist in range(1, N_DEV):
                sender = (me + N_DEV - dist) % N_DEV
                src, dst = self._remote(ins, outs, k, me, sender, me)
                cp = pltpu.make_async_remote_copy(
                    src_ref=src, dst_ref=dst, send_sem=send_sems.at[k, dist - 1],
                    recv_sem=recv_sems.at[k, dist - 1], device_id=_mesh_id(sender), device_id_type=MESH)
                cp.wait_send()
                cp.wait_recv()
            src, dst = self._local(ins, outs, k, me)
            pltpu.make_async_copy(src, dst, local_sems.at[k]).wait()

    def forward(self, ins, outs, sems):
        pass


class _Gather(_Exchange):
    FLIPS = ((1, 0), (0, 1), (1, 1))

    def __init__(self, items):
        self.items = list(items)
        out_shape = []
        for shards, layer, axis in self.items:
            shape = list(shards.shape if layer is None else shards.shape[1:])
            shape[axis] *= N_DEV
            out_shape.append(_sds(shape, shards.dtype))
        super().__init__([it[0] for it in self.items], out_shape)

    def _src(self, ins, k):
        layer = self.items[k][1]
        return ins[k] if layer is None else ins[k].at[layer]

    def _place(self, outs, k, idx):
        axis = self.items[k][2]
        return _slab(outs[k], axis, idx, self.out_shape[k].shape[axis] // N_DEV)

    def _copy(self, sems, k, j, src, dst, to):
        return pltpu.make_async_remote_copy(src_ref=src, dst_ref=dst, send_sem=sems[0].at[k, j],
                                            recv_sem=sems[1].at[k, j], device_id=to, device_id_type=MESH)

    @staticmethod
    def _places():
        x, y, c = lax.axis_index("x"), lax.axis_index("y"), lax.axis_index("c")
        chips = [(1 - x if fx else x, 1 - y if fy else y) for fx, fy in _Gather.FLIPS]
        return (x, y, c), (x, y, 1 - c), chips

    @staticmethod
    def _index(place):
        return 4 * place[0] + 2 * place[1] + place[2]

    def start(self, ins, outs, sems):
        me, sibling, chips = self._places()
        for k in range(len(self.inputs)):
            src, mine = self._src(ins, k), self._place(outs, k, self._index(me))
            pltpu.make_async_copy(src, mine, sems[2].at[k]).start()
            self._copy(sems, k, 0, src, mine, sibling).start()
            for j, chip in enumerate(chips):
                self._copy(sems, k, 1 + j, src, mine, (*chip, me[2])).start()

    def forward(self, ins, outs, sems):
        me, sibling, chips = self._places()
        for k in range(len(self.inputs)):
            for j, chip in enumerate(chips):
                got = self._place(outs, k, self._index((*chip, me[2])))
                self._copy(sems, k, 1 + j, got, got, (*chip, me[2])).wait_recv()
                self._copy(sems, k, 4 + j, got, got, sibling).start()

    def wait(self, ins, outs, sems):
        me, sibling, chips = self._places()
        for k in range(len(self.inputs)):
            src, mine = self._src(ins, k), self._place(outs, k, self._index(me))
            self._copy(sems, k, 0, src, self._place(outs, k, self._index(sibling)), sibling).wait_recv()
            for j, chip in enumerate(chips):
                got = self._place(outs, k, self._index((*chip, sibling[2])))
                self._copy(sems, k, 4 + j, got, got, sibling).wait_recv()
            for j in range(N_DEV - 1):
                self._copy(sems, k, j, src, mine, sibling).wait_send()
            pltpu.make_async_copy(src, mine, sems[2].at[k]).wait()


class _Scatter(_Exchange):
    def __init__(self, items):
        self.items = list(items)
        out_shape = []
        for partial, axis in self.items:
            shape = list(partial.shape)
            shape[axis] //= N_DEV
            out_shape.append(_sds([N_DEV] + shape, partial.dtype))
        super().__init__([it[0] for it in self.items], out_shape)

    def _take(self, ins, k, idx):
        axis = self.items[k][1]
        return _slab(ins[k], axis, idx, self.items[k][0].shape[axis] // N_DEV)

    def _local(self, ins, outs, k, me):
        return self._take(ins, k, me), outs[k].at[me]

    def _remote(self, ins, outs, k, me, sender, receiver):
        return self._take(ins, k, receiver), outs[k].at[sender]


def _run_exchange(exchange, name):
    n = len(exchange.inputs)

    def body(*refs):
        ins, outs, sems = refs[:n], refs[n:2 * n], refs[2 * n:]
        exchange.start(ins, outs, sems)
        exchange.forward(ins, outs, sems)
        exchange.wait(ins, outs, sems)

    any_spec = pl.BlockSpec(memory_space=pl.ANY)
    exchange.results = pl.pallas_call(
        body, name=name, in_specs=[any_spec] * n, out_specs=[any_spec] * n, out_shape=exchange.out_shape,
        scratch_shapes=exchange.sems)(*exchange.inputs)
    return exchange.results


class _Layer:
    def __init__(self, stack, index):
        self.stack, self.index = stack, index
        self.shape = (1,) + stack.shape[1:] if stack.ndim == 2 else stack.shape[1:]

    def view(self, ref):
        return ref.at[pl.ds(self.index, 1)] if self.stack.ndim == 2 else ref.at[self.index]


class _LayerTokens:
    def __init__(self, stack, index):
        self.stack, self.index = stack, index

    def rows(self, tm):
        return pl.BlockSpec((None, None, tm, self.stack.shape[-1]), lambda i: (self.index, 0, i, 0))


def _call(body, *, name, grid, in_specs, out_specs, out_shape, args, scratch=(), carry=None):
    in_specs, out_specs, out_shape, scratch = list(in_specs), list(out_specs), list(out_shape), list(scratch)
    args = [a.stack if isinstance(a, _LayerTokens) else a for a in args]
    layers = {k: a for k, a in enumerate(args) if isinstance(a, _Layer)}
    for k, a in layers.items():
        in_specs[k], args[k] = _whole(a.stack.shape), a.stack
    params = pltpu.CompilerParams(dimension_semantics=("arbitrary",) * len(grid),
                                  vmem_limit_bytes=VMEM_LIMIT_BYTES)
    n_in, n_out, n_scr = len(in_specs), len(out_specs), len(scratch)
    n_x = 0 if carry is None else len(carry.inputs)
    steps = 1
    for extent in grid:
        steps *= extent
    assert carry is None or steps >= 3, "a carrier needs a step each for start, second stage and wait"

    def whole_body(*refs):
        core_in, x_in = list(refs[:n_in]), refs[n_in:n_in + n_x]
        refs = refs[n_in + n_x:]
        core_out, x_out = refs[:n_out], refs[n_out:n_out + n_x]
        refs = refs[n_out + n_x:]
        core_scr, sems = refs[:n_scr], refs[n_scr:]
        for k, a in layers.items():
            core_in[k] = a.view(core_in[k])
        if carry is None:
            body(*core_in, *core_out, *core_scr)
            return
        step = pl.program_id(0)
        for axis in range(1, len(grid)):
            step = step * grid[axis] + pl.program_id(axis)

        @pl.when(step == 0)
        def _():
            carry.start(x_in, x_out, sems)

        @pl.when(step == steps - 2)
        def _():
            carry.forward(x_in, x_out, sems)

        body(*core_in, *core_out, *core_scr)

        @pl.when(step == steps - 1)
        def _():
            carry.wait(x_in, x_out, sems)

    any_spec = pl.BlockSpec(memory_space=pl.ANY)
    extra_in = [] if carry is None else carry.inputs
    extra_shape = [] if carry is None else carry.out_shape
    extra_sems = [] if carry is None else carry.sems
    outs = pl.pallas_call(
        whole_body, name=name, grid=grid, in_specs=in_specs + [any_spec] * n_x,
        out_specs=out_specs + [any_spec] * n_x, out_shape=out_shape + extra_shape,
        scratch_shapes=scratch + extra_sems, compiler_params=params)(*args, *extra_in)
    if carry is not None:
        carry.results = outs[n_out:]
    return outs[:n_out]


def _all_reduce(parts, plan, name):
    n = len(parts)
    out_shape = []
    for group in plan:
        shape = parts[group[0]].shape
        if len(group) > 1:
            shape = (len(group),) + (shape[1:] if shape[0] == 1 else shape)
        out_shape.append(_sds(shape, F32))

    def body(*refs):
        ins, outs, lands = refs[:n], refs[n:n + len(plan)], refs[n + len(plan):2 * n + len(plan)]
        send_sems, recv_sems = refs[2 * n + len(plan):]
        me = _my_index()
        for k in range(n):
            lands[k][me] = ins[k][...]
            for dist in range(1, N_DEV):
                pltpu.make_async_remote_copy(
                    src_ref=ins[k], dst_ref=lands[k].at[me],
                    send_sem=send_sems.at[k, dist - 1], recv_sem=recv_sems.at[k, dist - 1],
                    device_id=_mesh_id((me + dist) % N_DEV), device_id_type=MESH).start()
        for k in range(n):
            for dist in range(1, N_DEV):
                sender = (me + N_DEV - dist) % N_DEV
                cp = pltpu.make_async_remote_copy(
                    src_ref=ins[k], dst_ref=lands[k].at[sender],
                    send_sem=send_sems.at[k, dist - 1], recv_sem=recv_sems.at[k, dist - 1],
                    device_id=_mesh_id(sender), device_id_type=MESH)
                cp.wait_send()
                cp.wait_recv()
        for o_ref, group in zip(outs, plan):
            for j, k in enumerate(group):
                total = lands[k][0]
                for dev in range(1, N_DEV):
                    total = total + lands[k][dev]
                if len(group) == 1:
                    o_ref[...] = total
                elif parts[k].shape[0] == 1:
                    o_ref[j:j + 1, :] = total
                else:
                    o_ref[j] = total

    vmem = pl.BlockSpec(memory_space=pltpu.VMEM)
    return pl.pallas_call(
        body, name=name, in_specs=[vmem] * n, out_specs=[vmem] * len(plan), out_shape=out_shape,
        scratch_shapes=[pltpu.VMEM((N_DEV,) + part.shape, F32) for part in parts]
        + [pltpu.SemaphoreType.DMA((n, N_DEV - 1)), pltpu.SemaphoreType.DMA((n, N_DEV - 1))])(*parts)


def _ple_out(xv, g_ref, wpg_ref, p_ref, wple_ref):
    xh, _ = _rms_stats(xv)
    lg = _dot((xh * g_ref[...]).astype(BF16), wpg_ref[...])
    return xv + _sigmoid(lg) * _dot(p_ref[...].astype(BF16), wple_ref[...])


def _norm_matmul(x, g, wt, name, carry=None, ple=None):
    t, d = x.shape
    n = wt.shape[0]
    tm, nc = _token_tile(t), _chunk(n)

    def body(x_ref, g_ref, wt_ref, *rest):
        xv = x_ref[...]
        if ple is not None:
            xv = _ple_out(xv, *rest[:4])
            rest[4][...] = xv
        h_ref, o_ref = rest[-2:]
        xh, _ = _rms_stats(xv)
        h = (xh * g_ref[...]).astype(BF16)
        h_ref[...] = h
        for n0 in range(0, n, nc):
            o_ref[:, n0:n0 + nc] = _dot_nt(h, wt_ref[n0:n0 + nc, :]).astype(BF16)

    in_specs = [_rows(tm, d), _whole(g.shape), _resident(wt.shape)]
    out_specs, out_shape, args = [_rows(tm, d), _rows(tm, n)], [_sds((t, d), BF16), _sds((t, n), BF16)], (x, g, wt)
    if ple is not None:
        g_ple, wpg, p, wple = ple
        in_specs += [_whole(g_ple.shape), _resident(wpg.shape), p.rows(tm), _resident(wple.shape)]
        out_specs, out_shape, args = [_rows(tm, d)] + out_specs, [_sds((t, d), F32)] + out_shape, args + ple
    return _call(body, name=name, grid=(t // tm,), in_specs=in_specs, out_specs=out_specs, out_shape=out_shape,
                 args=args, carry=carry)


def _fwd_branch(z, caw, cab, lng, lnb, cbw, dc, name, carry=None):
    t = z.shape[0]
    tm = _token_tile(t)
    ka, kb = caw.shape[0], cbw.shape[0]
    taps_a, taps_b = _fwd_taps(ka), _fwd_taps(kb)
    res_a, res_b = _residues(taps_a), _residues(taps_b)
    span = HALO + tm - SUBLANES

    def body(z_ref, caw_ref, cab_ref, lng_ref, lnb_ref, cbw_ref, ac_ref, act_ref, s_ref, cb_ref,
             rot_a, rot_b, wb_a, wb_b, cb):
        @pl.when(pl.program_id(0) == 0)
        def _():
            rot_a[0, 0:HALO, :] = jnp.zeros((HALO, dc), F32)
            rot_b[0, 0:HALO, :] = jnp.zeros((HALO, dc), F32)
            _broadcast_rows(wb_a, caw_ref, ka)
            wb_a[ka] = jnp.broadcast_to(cab_ref[...], (SUBLANES, dc))
            _broadcast_rows(wb_b, cbw_ref, kb)

        a_val = z_ref[:, 0:dc].astype(F32)
        a_gt = z_ref[:, dc:2 * dc].astype(F32)
        rot_a[0, HALO:HALO + tm, :] = a_val * _sigmoid(a_gt)
        _fill_rotations(rot_a, res_a, span)
        _tap_conv(rot_a, taps_a, wb_a, ac_ref, tm, bias_plane=ka)
        ac = ac_ref[...]
        mu = jnp.mean(ac, axis=-1, keepdims=True)
        xc = ac - mu
        var = jnp.mean(xc * xc, axis=-1, keepdims=True)
        ln = xc * lax.rsqrt(var + NORM_EPS) * lng_ref[...] + lnb_ref[...]
        act_ref[...] = (ln * _sigmoid(ln)).astype(BF16)
        rot_a[0, 0:HALO, :] = rot_a[0, tm:tm + HALO, :]

        sc_c = z_ref[:, 3 * dc:4 * dc].astype(F32)
        sc_v = z_ref[:, 4 * dc:5 * dc].astype(F32)
        rot_b[0, HALO:HALO + tm, :] = sc_c * sc_v
        _fill_rotations(rot_b, res_b, span)
        _tap_conv(rot_b, taps_b, wb_b, cb, tm)
        s_ref[...] = (z_ref[:, 2 * dc:3 * dc].astype(F32) * cb[...]).astype(BF16)
        cb_ref[...] = cb[...].astype(BF16)
        rot_b[0, 0:HALO, :] = rot_b[0, tm:tm + HALO, :]

    return _call(body, name=name, grid=(t // tm,),
                 in_specs=[_rows(tm, 5 * dc), _whole(caw.shape), _whole(cab.shape), _whole(lng.shape),
                           _whole(lnb.shape), _whole(cbw.shape)],
                 out_specs=[_rows(tm, dc), _rows(tm, dc), _rows(tm, dc), _rows(tm, dc)],
                 out_shape=[_sds((t, dc), F32), _sds((t, dc), BF16), _sds((t, dc), BF16), _sds((t, dc), BF16)],
                 scratch=[pltpu.VMEM((len(res_a), HALO + tm, dc), F32), pltpu.VMEM((len(res_b), HALO + tm, dc), F32),
                          pltpu.VMEM((ka + 1, SUBLANES, dc), F32), pltpu.VMEM((kb, SUBLANES, dc), F32),
                          pltpu.VMEM((tm, dc), F32)],
                 args=(z, caw, cab, lng, lnb, cbw), carry=carry)


def _fwd_merge(x, z, bg, a_act, s, wa, wb, wo, name, carry=None):
    t, d = x.shape
    n = z.shape[1]
    dc = a_act.shape[1]
    tm = _token_tile(t)
    o5 = n - 2 * d

    def body(x_ref, z_ref, bg_ref, act_ref, s_ref, wa_ref, wb_ref, wo_ref, o_ref):
        ya = _dot(act_ref[...], wa_ref[...])
        yb = _dot(s_ref[...], wb_ref[...])
        ga = _sigmoid(z_ref[:, o5:o5 + d].astype(F32) + bg_ref[:, 0:d])
        gb = _sigmoid(z_ref[:, o5 + d:n].astype(F32) + bg_ref[:, d:2 * d])
        m = (ga * ya + gb * yb).astype(BF16)
        o_ref[...] = x_ref[...] + _dot(m, wo_ref[...])

    return _call(body, name=name, grid=(t // tm,),
                 in_specs=[_rows(tm, d), _rows(tm, n), _whole(bg.shape), _rows(tm, dc), _rows(tm, dc),
                           _resident(wa.shape), _resident(wb.shape), _resident(wo.shape)],
                 out_specs=[_rows(tm, d)], out_shape=[_sds((t, d), F32)],
                 args=(x, z, bg, a_act, s, wa, wb, wo), carry=carry)[0]


def _fwd_down(x, u, cfw, cfb, wd, name, carry=None):
    t, d = x.shape
    f = u.shape[1] // 2
    tm = _token_tile(t, 256)
    kf = cfw.shape[0]

    taps = _fwd_taps(kf)
    residues = _residues(taps)

    def body(x_ref, u_ref, cfw_ref, cfb_ref, wd_ref, o_ref, act_ref, gl_ref, dgl_ref, rot_u, wb, fg):
        @pl.when(pl.program_id(0) == 0)
        def _():
            rot_u[0, 0:HALO, :] = jnp.zeros((HALO, f), F32)
            _broadcast_rows(wb, cfw_ref, kf)
            wb[kf] = jnp.broadcast_to(cfb_ref[...], (SUBLANES, f))

        rot_u[0, HALO:HALO + tm, :] = u_ref[:, 0:f].astype(F32)
        _fill_rotations(rot_u, residues, HALO + tm - SUBLANES)
        _tap_conv(rot_u, taps, wb, fg, tm, bias_plane=kf)
        y = x_ref[...]
        for c0, cw in _lane_chunks(f, MXU_COLUMNS):
            at = slice(c0, c0 + cw)
            gl, dgl = _gelu_tanh(fg[:, at])
            gl_ref[:, at] = gl.astype(BF16)
            dgl_ref[:, at] = dgl.astype(BF16)
            act = (gl * u_ref[:, f + c0:f + c0 + cw].astype(F32)).astype(BF16)
            act_ref[:, at] = act
            y = y + _dot(act, wd_ref[at, :])
        o_ref[...] = y
        rot_u[0, 0:HALO, :] = rot_u[0, tm:tm + HALO, :]

    return _call(body, name=name, grid=(t // tm,),
                 in_specs=[_rows(tm, d), _rows(tm, 2 * f), _whole(cfw.shape), _whole(cfb.shape),
                           _resident(wd.shape)],
                 out_specs=[_rows(tm, d), _rows(tm, f), _rows(tm, f), _rows(tm, f)],
                 out_shape=[_sds((t, d), F32), _sds((t, f), BF16), _sds((t, f), BF16), _sds((t, f), BF16)],
                 scratch=[pltpu.VMEM((len(residues), HALO + tm, f), F32), pltpu.VMEM((kf + 1, SUBLANES, f), F32),
                          pltpu.VMEM((tm, f), F32)],
                 args=(x, u, cfw, cfb, wd), carry=carry)


def _fwd_ple_loss(x, g, wpg, p, wple, g_final, target, name):
    t, d = x.shape
    tm = _token_tile(t)

    def body(x_ref, g_ref, wpg_ref, p_ref, wple_ref, gf_ref, t_ref, dy_ref, dg_ref, loss_ref):
        y = _ple_out(x_ref[...], g_ref, wpg_ref, p_ref, wple_ref)

        @pl.when(pl.program_id(0) == 0)
        def _():
            dg_ref[...] = jnp.zeros_like(dg_ref)
            loss_ref[...] = jnp.zeros_like(loss_ref)

        yh, r = _rms_stats(y)
        err = yh * gf_ref[...] - t_ref[...]
        sq = jnp.sum(jnp.sum(err * err, axis=0, keepdims=True), axis=1, keepdims=True)
        loss_ref[...] += jnp.broadcast_to(0.5 * sq / d, loss_ref.shape)
        dy, dg = _rms_bwd(err / d, yh, r, gf_ref[...])
        dy_ref[...] = dy
        dg_ref[...] += dg

    return _call(body, name=name, grid=(t // tm,),
                 in_specs=[_rows(tm, d), _whole(g.shape), _resident(wpg.shape), p.rows(tm), _resident(wple.shape),
                           _whole(g_final.shape), _rows(tm, d)],
                 out_specs=[_rows(tm, d), _whole((1, d)), _whole((SUBLANES, LANES))],
                 out_shape=[_sds((t, d), F32), _sds((1, d), F32), _sds((SUBLANES, LANES), F32)],
                 args=(x, g, wpg, p, wple, g_final, target))


def _bwd_ple(dy, x, g, wpg, p, wple, name, above=None):
    t, d = x.shape
    pd = p.stack.shape[-1]
    tm = _token_tile(t)
    nt = t // tm
    n_above = 0 if above is None else len(above)

    def body(*refs):
        head, refs = refs[:max(n_above, 1)], refs[max(n_above, 1):]
        x_ref, g_ref, wpg_ref, p_ref, wple_ref, dx_ref, dwpg_ref, dwple_ref, dg_ref = refs[:9]
        dg_in_ref = refs[9] if above is not None else None
        acc_pg, acc_ple = refs[-2:]
        i = pl.program_id(0)

        @pl.when(i == 0)
        def _():
            acc_pg[...] = jnp.zeros_like(acc_pg)
            acc_ple[...] = jnp.zeros_like(acc_ple)
            dg_ref[...] = jnp.zeros_like(dg_ref)
            if above is not None:
                dg_in_ref[...] = jnp.zeros_like(dg_in_ref)

        if above is None:
            dyv = head[0][...]
        else:
            dz_ref, wt_ref, xin_ref, gin_ref, dres_ref = head
            xh_in, r_in = _rms_stats(xin_ref[...])
            dx_in, dg_in = _rms_bwd(_dot(dz_ref[...], wt_ref[...]), xh_in, r_in, gin_ref[...])
            dyv = dres_ref[...] + dx_in
            dg_in_ref[...] += dg_in
        xh, r = _rms_stats(x_ref[...])
        h = (xh * g_ref[...]).astype(BF16)
        pb = p_ref[...].astype(BF16)
        pg = _sigmoid(_dot(h, wpg_ref[...]))
        pp = _dot(pb, wple_ref[...])
        dpp = (dyv * pg).astype(BF16)
        dlg = (dyv * pp * pg * (1.0 - pg)).astype(BF16)
        acc_ple[...] += _dot_tn(pb, dpp)
        acc_pg[...] += _dot_tn(h, dlg)
        dx, dg = _rms_bwd(_dot_nt(dlg, wpg_ref[...]), xh, r, g_ref[...])
        dx_ref[...] = dyv + dx
        dg_ref[...] += dg

        @pl.when(i == nt - 1)
        def _():
            dwpg_ref[...] = acc_pg[...].astype(BF16)
            dwple_ref[...] = acc_ple[...].astype(BF16)

    if above is None:
        head_specs, head_args, more_specs, more_shape = [_rows(tm, d)], (dy,), [], []
    else:
        dz, wt, x_in, g_in, dres = above
        head_specs = [_rows(tm, dz.shape[1]), _resident(wt.shape), _rows(tm, d), _whole(g_in.shape), _rows(tm, d)]
        head_args, more_specs, more_shape = above, [_whole((1, d))], [_sds((1, d), F32)]
    return _call(body, name=name, grid=(nt,),
                 in_specs=head_specs + [_rows(tm, d), _whole(g.shape), _resident(wpg.shape), p.rows(tm),
                                        _resident(wple.shape)],
                 out_specs=[_rows(tm, d), _whole((d, d)), _whole((pd, d)), _whole((1, d))] + more_specs,
                 out_shape=[_sds((t, d), F32), _sds((d, d), BF16), _sds((pd, d), BF16), _sds((1, d), F32)]
                 + more_shape,
                 scratch=[pltpu.VMEM((d, d), F32), pltpu.VMEM((pd, d), F32)],
                 args=(*head_args, x, g, wpg, p, wple))


def _bwd_down(dy, u, gl, dgl, cfw, wd, name, carry=None):
    t, d = dy.shape
    f = u.shape[1] // 2
    tm = _token_tile(t, 256)
    nt = t // tm
    kf = cfw.shape[0]
    bwd = _bwd_taps(kf)
    residues = _residues(bwd)

    def body(dy_ref, u_ref, gl_ref, dgl_ref, cfw_ref, wd_ref, du_ref, dcw_ref, dcb_ref, rot_g, wb, acc, buf):
        i = pl.program_id(0)

        @pl.when(i == 0)
        def _():
            for ref in (dcw_ref, dcb_ref, acc):
                ref[...] = jnp.zeros_like(ref)
            rot_g[0, tm:tm + HALO, :] = jnp.zeros((HALO, f), F32)
            _broadcast_rows(wb, cfw_ref, kf)

        dyb = dy_ref[...].astype(BF16)
        for c0, cw in _lane_chunks(f, MXU_COLUMNS):
            at = slice(c0, c0 + cw)
            df = _dot_nt(dyb, wd_ref[at, :])
            du_ref[:, f + c0:f + c0 + cw] = (df * gl_ref[:, at].astype(F32)).astype(BF16)
            dfg = df * u_ref[:, f + c0:f + c0 + cw].astype(F32) * dgl_ref[:, at].astype(F32)
            dcb_ref[:, at] += jnp.sum(dfg, axis=0, keepdims=True)
            rot_g[0, 0:tm, at] = dfg
        _fill_rotations(rot_g, residues, HALO + tm - SUBLANES)
        buf[...] = u_ref[:, 0:f].astype(F32)
        _tap_wgrad(rot_g, bwd, buf, acc, tm)
        _tap_conv(rot_g, bwd, wb, buf, tm)
        du_ref[:, 0:f] = buf[...].astype(BF16)
        rot_g[0, tm:tm + HALO, :] = rot_g[0, 0:HALO, :]

        @pl.when(i == nt - 1)
        def _():
            dcw_ref[0:kf, :] = jnp.sum(acc[...], axis=1)

    return _call(body, name=name, grid=(nt,),
                 in_specs=[_rows_rev(tm, d, nt), _rows_rev(tm, 2 * f, nt), _rows_rev(tm, f, nt),
                           _rows_rev(tm, f, nt), _whole(cfw.shape), _resident(wd.shape)],
                 out_specs=[_rows_rev(tm, 2 * f, nt), _whole((SUBLANES, f)), _whole((1, f))],
                 out_shape=[_sds((t, 2 * f), BF16), _sds((SUBLANES, f), F32), _sds((1, f), F32)],
                 scratch=[pltpu.VMEM((len(residues), HALO + tm, f), F32), pltpu.VMEM((kf, SUBLANES, f), F32),
                          pltpu.VMEM((kf, SUBLANES, f), F32), pltpu.VMEM((tm, f), F32)],
                 args=(dy, u, gl, dgl, cfw, wd), carry=carry)


def _bwd_norm_matmul(dout, wt, x, g, dres, name, carry=None):
    t, d = x.shape
    n = dout.shape[1]
    tm = _token_tile(t)

    def body(do_ref, wt_ref, x_ref, g_ref, dres_ref, dx_ref, dg_ref):
        @pl.when(pl.program_id(0) == 0)
        def _():
            dg_ref[...] = jnp.zeros_like(dg_ref)

        dh = _dot(do_ref[...], wt_ref[...])
        xh, r = _rms_stats(x_ref[...])
        dx, dg = _rms_bwd(dh, xh, r, g_ref[...])
        dx_ref[...] = dres_ref[...] + dx
        dg_ref[...] += dg

    return _call(body, name=name, grid=(t // tm,),
                 in_specs=[_rows(tm, n), _resident(wt.shape), _rows(tm, d), _whole(g.shape), _rows(tm, d)],
                 out_specs=[_rows(tm, d), _whole((1, d))],
                 out_shape=[_sds((t, d), F32), _sds((1, d), F32)],
                 args=(dout, wt, x, g, dres), carry=carry)


def _wgrad_tn(a, b, name, carry=None):
    t, n = a.shape
    d = b.shape[1]
    tt = 2048 if t % 2048 == 0 else _token_tile(t)
    tn = _chunk(n, 1536)
    nt = t // tt

    def body(a_ref, b_ref, o_ref, acc):
        k = pl.program_id(1)

        @pl.when(k == 0)
        def _():
            acc[...] = jnp.zeros_like(acc)

        acc[...] += _dot_tn(a_ref[...].astype(BF16), b_ref[...].astype(BF16))

        @pl.when(k == nt - 1)
        def _():
            o_ref[...] = acc[...].astype(BF16)

    return _call(body, name=name, grid=(n // tn, nt),
                 in_specs=[pl.BlockSpec((tt, tn), lambda j, k: (k, j)), pl.BlockSpec((tt, d), lambda j, k: (k, 0))],
                 out_specs=[pl.BlockSpec((tn, d), lambda j, k: (j, 0))], out_shape=[_sds((n, d), BF16)],
                 scratch=[pltpu.VMEM((tn, d), F32)], args=(a, b), carry=carry)[0]


def _bwd_merge(dy, z, bg, a_act, s, wa, wb, wo, name, carry=None):
    t, d = dy.shape
    n = z.shape[1]
    dc = a_act.shape[1]
    tm = _token_tile(t)
    nt = t // tm
    o5 = n - 2 * d

    def body(dy_ref, z_ref, bg_ref, act_ref, s_ref, wa_ref, wb_ref, wo_ref,
             dact_ref, ds_ref, dgl_ref, dwo_ref, dwa_ref, dwb_ref, dbg_ref, acc_o, acc_a, acc_b):
        i = pl.program_id(0)

        @pl.when(i == 0)
        def _():
            acc_o[...] = jnp.zeros_like(acc_o)
            acc_a[...] = jnp.zeros_like(acc_a)
            acc_b[...] = jnp.zeros_like(acc_b)
            dbg_ref[...] = jnp.zeros_like(dbg_ref)

        dyb = dy_ref[...].astype(BF16)
        dm = _dot_nt(dyb, wo_ref[...])
        ya = _dot(act_ref[...], wa_ref[...])
        yb = _dot(s_ref[...], wb_ref[...])
        ga = _sigmoid(z_ref[:, o5:o5 + d].astype(F32) + bg_ref[:, 0:d])
        gb = _sigmoid(z_ref[:, o5 + d:n].astype(F32) + bg_ref[:, d:2 * d])
        acc_o[...] += _dot_tn((ga * ya + gb * yb).astype(BF16), dyb)
        dya = (dm * ga).astype(BF16)
        dyb2 = (dm * gb).astype(BF16)
        acc_a[...] += _dot_tn(act_ref[...], dya)
        acc_b[...] += _dot_tn(s_ref[...], dyb2)
        dact_ref[...] = _dot_nt(dya, wa_ref[...])
        ds_ref[...] = _dot_nt(dyb2, wb_ref[...])
        dla = dm * ya * ga * (1.0 - ga)
        dlb = dm * yb * gb * (1.0 - gb)
        dgl_ref[:, 0:d] = dla.astype(BF16)
        dgl_ref[:, d:2 * d] = dlb.astype(BF16)
        dbg_ref[:, 0:d] += jnp.sum(dla, axis=0, keepdims=True)
        dbg_ref[:, d:2 * d] += jnp.sum(dlb, axis=0, keepdims=True)

        @pl.when(i == nt - 1)
        def _():
            dwo_ref[...] = acc_o[...].astype(BF16)
            dwa_ref[...] = acc_a[...].astype(BF16)
            dwb_ref[...] = acc_b[...].astype(BF16)

    return _call(body, name=name, grid=(nt,),
                 in_specs=[_rows(tm, d), _rows(tm, n), _whole(bg.shape), _rows(tm, dc), _rows(tm, dc),
                           _resident(wa.shape), _resident(wb.shape), _resident(wo.shape)],
                 out_specs=[_rows(tm, dc), _rows(tm, dc), _rows(tm, 2 * d), _whole((d, d)), _whole((dc, d)),
                            _whole((dc, d)), _whole((1, 2 * d))],
                 out_shape=[_sds((t, dc), F32), _sds((t, dc), F32), _sds((t, 2 * d), BF16), _sds((d, d), BF16),
                            _sds((dc, d), BF16), _sds((dc, d), BF16), _sds((1, 2 * d), F32)],
                 scratch=[pltpu.VMEM((d, d), F32), pltpu.VMEM((dc, d), F32), pltpu.VMEM((dc, d), F32)],
                 args=(dy, z, bg, a_act, s, wa, wb, wo), carry=carry)


def _bwd_branch(dact, ds, z, dgl, a_conv, cb, caw, lng, lnb, cbw, name, carry=None):
    t, n = z.shape
    dc = a_conv.shape[1]
    tm = _token_tile(t)
    nt = t // tm
    ka, kb = caw.shape[0], cbw.shape[0]
    bwd_a, bwd_b = _bwd_taps(ka), _bwd_taps(kb)
    span = HALO + tm - SUBLANES

    def body(dact_ref, ds_ref, z_ref, dgl_ref, ac_ref, cb_ref, caw_ref, lng_ref, lnb_ref, cbw_ref,
             dz_ref, dcaw_ref, dcab_ref, dlng_ref, dlnb_ref, dcbw_ref,
             rot_da, rot_dc, wb_a, wb_b, acc_a, acc_b, buf):
        i = pl.program_id(0)

        @pl.when(i == 0)
        def _():
            for ref in (dcaw_ref, dcab_ref, dlng_ref, dlnb_ref, dcbw_ref, acc_a, acc_b):
                ref[...] = jnp.zeros_like(ref)
            rot_da[0, tm:tm + HALO, :] = jnp.zeros((HALO, dc), F32)
            rot_dc[0, tm:tm + HALO, :] = jnp.zeros((HALO, dc), F32)
            _broadcast_rows(wb_a, caw_ref, ka)
            _broadcast_rows(wb_b, cbw_ref, kb)

        a_val = z_ref[:, 0:dc].astype(F32)
        sg = _sigmoid(z_ref[:, dc:2 * dc].astype(F32))

        ac = ac_ref[...]
        mu = jnp.mean(ac, axis=-1, keepdims=True)
        xc = ac - mu
        rstd = lax.rsqrt(jnp.mean(xc * xc, axis=-1, keepdims=True) + NORM_EPS)
        xh = xc * rstd
        ln = xh * lng_ref[...] + lnb_ref[...]
        sl = _sigmoid(ln)
        dln = dact_ref[...] * (sl * (1.0 + ln * (1.0 - sl)))
        dlng_ref[...] += jnp.sum(dln * xh, axis=0, keepdims=True)
        dlnb_ref[...] += jnp.sum(dln, axis=0, keepdims=True)
        dxh = dln * lng_ref[...]
        dac = rstd * (dxh - jnp.mean(dxh, axis=-1, keepdims=True)
                      - xh * jnp.mean(dxh * xh, axis=-1, keepdims=True))
        dcab_ref[...] += jnp.sum(dac, axis=0, keepdims=True)
        rot_da[0, 0:tm, :] = dac
        _fill_rotations(rot_da, _residues(bwd_a), span)
        buf[...] = a_val * sg
        _tap_wgrad(rot_da, bwd_a, buf, acc_a, tm)
        _tap_conv(rot_da, bwd_a, wb_a, buf, tm)
        rot_da[0, tm:tm + HALO, :] = rot_da[0, 0:HALO, :]
        da = buf[...]
        dz_ref[:, 0:dc] = (da * sg).astype(BF16)
        dz_ref[:, dc:2 * dc] = (da * a_val * sg * (1.0 - sg)).astype(BF16)

        sc_b = z_ref[:, 2 * dc:3 * dc].astype(F32)
        sc_c = z_ref[:, 3 * dc:4 * dc].astype(F32)
        sc_v = z_ref[:, 4 * dc:5 * dc].astype(F32)
        dsv = ds_ref[...]
        dz_ref[:, 2 * dc:3 * dc] = (dsv * cb_ref[...].astype(F32)).astype(BF16)
        rot_dc[0, 0:tm, :] = dsv * sc_b
        _fill_rotations(rot_dc, _residues(bwd_b), span)
        buf[...] = sc_c * sc_v
        _tap_wgrad(rot_dc, bwd_b, buf, acc_b, tm)
        _tap_conv(rot_dc, bwd_b, wb_b, buf, tm)
        rot_dc[0, tm:tm + HALO, :] = rot_dc[0, 0:HALO, :]
        dcv = buf[...]
        dz_ref[:, 3 * dc:4 * dc] = (dcv * sc_v).astype(BF16)
        dz_ref[:, 4 * dc:5 * dc] = (dcv * sc_c).astype(BF16)
        dz_ref[:, 5 * dc:n] = dgl_ref[...]

        @pl.when(i == nt - 1)
        def _():
            dcaw_ref[0:ka, :] = jnp.sum(acc_a[...], axis=1)
            dcbw_ref[0:kb, :] = jnp.sum(acc_b[...], axis=1)

    def planes(taps):
        return pltpu.VMEM((len(_residues(taps)), HALO + tm, dc), F32)

    return _call(body, name=name, grid=(nt,),
                 in_specs=[_rows_rev(tm, dc, nt), _rows_rev(tm, dc, nt), _rows_rev(tm, 5 * dc, nt),
                           _rows_rev(tm, n - 5 * dc, nt), _rows_rev(tm, dc, nt), _rows_rev(tm, dc, nt),
                           _whole(caw.shape), _whole(lng.shape), _whole(lnb.shape), _whole(cbw.shape)],
                 out_specs=[_rows_rev(tm, n, nt), _whole((HALO, dc)), _whole((1, dc)), _whole((1, dc)),
                            _whole((1, dc)), _whole((SUBLANES, dc))],
                 out_shape=[_sds((t, n), BF16), _sds((HALO, dc), F32), _sds((1, dc), F32), _sds((1, dc), F32),
                            _sds((1, dc), F32), _sds((SUBLANES, dc), F32)],
                 scratch=[planes(bwd_a), planes(bwd_b),
                          pltpu.VMEM((ka, SUBLANES, dc), F32), pltpu.VMEM((kb, SUBLANES, dc), F32),
                          pltpu.VMEM((ka, SUBLANES, dc), F32), pltpu.VMEM((kb, SUBLANES, dc), F32),
                          pltpu.VMEM((tm, dc), F32)],
                 args=(dact, ds, z, dgl, a_conv, cb, caw, lng, lnb, cbw), carry=carry)


def _land_specs(depth, nr, tr, cols):
    def spec(k):
        return pl.BlockSpec((N_DEV, tr, cols), lambda i: (0, jnp.clip(i - k * nr, 0, nr - 1), 0))
    return [spec(k) for k in range(depth)]


def _adamw_math(w, g, m, v):
    nm = ADAM_B1 * m + (1.0 - ADAM_B1) * g
    nv = ADAM_B2 * v + (1.0 - ADAM_B2) * (g * g)
    m_hat = nm / (1.0 - ADAM_B1 ** ADAM_STEP)
    v_hat = nv / (1.0 - ADAM_B2 ** ADAM_STEP)
    return -ADAM_LR * (m_hat / (jnp.sqrt(v_hat) + ADAM_EPS) + ADAM_WD * w), nm, nv


def _sum_adamw(lands, w, m, v, name):
    _, rows, cols = lands[0].shape
    tr = _row_tile(rows)
    nr = rows // tr
    depth = len(lands)

    def body(*refs):
        w_ref, m_ref, v_ref, g_ref, d_ref, nm_ref, nv_ref = refs[depth:]
        i = pl.program_id(0)
        for k in range(depth):
            @pl.when(i // nr == k)
            def _(k=k):
                acc = refs[k][0].astype(F32)
                for j in range(1, N_DEV):
                    acc = acc + refs[k][j].astype(F32)
                g_ref[...] = acc
                d_ref[...], nm_ref[...], nv_ref[...] = _adamw_math(w_ref[...], acc, m_ref[...], v_ref[...])

    spec = _rows(tr, cols)
    return _call(body, name=name, grid=(depth * nr,), in_specs=_land_specs(depth, nr, tr, cols) + [spec] * 3,
                 out_specs=[spec] * 4, out_shape=[_sds((depth * rows, cols), F32)] * 4, args=(*lands, w, m, v))


def _adamw_small(ws, gs, ms, vs, name):
    n = len(ws)

    def body(*refs):
        w_refs, g_refs, m_refs, v_refs = refs[:n], refs[n:2 * n], refs[2 * n:3 * n], refs[3 * n:4 * n]
        d_refs, nm_refs, nv_refs = refs[4 * n:5 * n], refs[5 * n:6 * n], refs[6 * n:]
        for k in range(n):
            d_refs[k][...], nm_refs[k][...], nv_refs[k][...] = _adamw_math(
                w_refs[k][...], g_refs[k][...], m_refs[k][...], v_refs[k][...])

    vmem = pl.BlockSpec(memory_space=pltpu.VMEM)
    outs = pl.pallas_call(
        body, name=name, in_specs=[vmem] * (4 * n), out_specs=[vmem] * (3 * n),
        out_shape=[_sds(a.shape, F32) for a in ws] * 3)(*ws, *gs, *ms, *vs)
    return outs[:n], outs[n:2 * n], outs[2 * n:]


def kernel(x, p, g_mix, w_in, b_gate, conv_a_w, conv_a_b, ln_a_g, ln_a_b, w_a_out, conv_b_w, w_b_out, w_o, g_ffn, w_up, conv_f_w, conv_f_b, w_down, g_ple, w_ple, w_ple_gate, g_final, loss_target, m_g_mix, m_w_in, m_b_gate, m_conv_a_w, m_conv_a_b, m_ln_a_g, m_ln_a_b, m_w_a_out, m_conv_b_w, m_w_b_out, m_w_o, m_g_ffn, m_w_up, m_conv_f_w, m_conv_f_b, m_w_down, m_g_ple, m_w_ple, m_w_ple_gate, m_g_final, v_g_mix, v_w_in, v_b_gate, v_conv_a_w, v_conv_a_b, v_ln_a_g, v_ln_a_b, v_w_a_out, v_conv_b_w, v_w_b_out, v_w_o, v_g_ffn, v_w_up, v_conv_f_w, v_conv_f_b, v_w_down, v_g_ple, v_w_ple, v_w_ple_gate, v_g_final):
    w = dict(zip(WEIGHT_NAMES, (g_mix, w_in, b_gate, conv_a_w, conv_a_b, ln_a_g, ln_a_b, w_a_out, conv_b_w,
                                w_b_out, w_o, g_ffn, w_up, conv_f_w, conv_f_b, w_down, g_ple, w_ple,
                                w_ple_gate, g_final)))
    mom = dict(zip(WEIGHT_NAMES, (m_g_mix, m_w_in, m_b_gate, m_conv_a_w, m_conv_a_b, m_ln_a_g, m_ln_a_b,
                                  m_w_a_out, m_conv_b_w, m_w_b_out, m_w_o, m_g_ffn, m_w_up, m_conv_f_w,
                                  m_conv_f_b, m_w_down, m_g_ple, m_w_ple, m_w_ple_gate, m_g_final)))
    var = dict(zip(WEIGHT_NAMES, (v_g_mix, v_w_in, v_b_gate, v_conv_a_w, v_conv_a_b, v_ln_a_g, v_ln_a_b,
                                  v_w_a_out, v_conv_b_w, v_w_b_out, v_w_o, v_g_ffn, v_w_up, v_conv_f_w,
                                  v_conv_f_b, v_w_down, v_g_ple, v_w_ple, v_w_ple_gate, v_g_final)))
    depth = g_mix.shape[0]
    dc = ln_a_g.shape[1]
    me = _my_index()
    x0 = x[0]
    target = loss_target[0]
    big_names = tuple(BIG_AXIS)

    shard = {name: (jnp.swapaxes(w[name], 1, 2) if name in TRANSPOSED else w[name]).astype(BF16)
             for name in big_names}

    def gather_of(layer, *names):
        return _Gather([(shard[name], layer, BIG_AXIS[name]) for name in names])

    def row(name, layer):
        return _Layer(w[name], layer)

    first = _Gather([(shard['w_in'], 0, BIG_AXIS['w_in'])] + [(w[name][None], None, 0) for name in CONV_SHARDED])
    gathered = _run_exchange(first, "gather_first")
    w_in_full = gathered[0]
    conv_full = {name: jnp.transpose(g, (1, 2, 0, 3)).reshape(g.shape[1], g.shape[2], -1)
                 for name, g in zip(CONV_SHARDED, gathered[1:])}
    saved = []
    xc = x0
    below = None
    for l in range(depth):
        carry = gather_of(l, 'w_a_out', 'w_b_out', 'w_o', 'w_up')
        outs = _norm_matmul(xc, row('g_mix', l), w_in_full, f"fwd_in_{l}", carry, below)
        if below is not None:
            xc = outs[0]
        h, z = outs[-2:]
        wa_full, wb_full, wo_full, w_up_full = carry.results
        a_conv, a_act, s, cb = _fwd_branch(z, _Layer(conv_full['conv_a_w'], l), row('conv_a_b', l), row('ln_a_g', l),
                                       row('ln_a_b', l), _Layer(conv_full['conv_b_w'], l), dc, f"fwd_branch_{l}")
        x1 = _fwd_merge(xc, z, row('b_gate', l), a_act, s, wa_full, wb_full, wo_full, f"fwd_merge_{l}")
        carry = _Gather([(shard[name], l, BIG_AXIS[name]) for name in ('w_down', 'w_ple', 'w_ple_gate')]
                        + ([(shard['w_in'], l + 1, BIG_AXIS['w_in'])] if l + 1 < depth else []))
        h2, u = _norm_matmul(x1, row('g_ffn', l), w_up_full, f"fwd_up_{l}", carry)
        w_down_full, w_ple_full, w_pg_full = carry.results[:3]
        x2, act, gl, dgl = _fwd_down(x1, u, _Layer(conv_full['conv_f_w'], l), row('conv_f_b', l), w_down_full,
                                     f"fwd_down_{l}")
        saved.append((xc, h, z, a_conv, a_act, s, cb, x1, h2, u, act, gl, dgl, x2,
                      dict(w_in=w_in_full, w_a_out=wa_full, w_b_out=wb_full, w_o=wo_full, w_up=w_up_full,
                           w_down=w_down_full, w_ple=w_ple_full, w_ple_gate=w_pg_full)))
        if l + 1 < depth:
            w_in_full = carry.results[3]
            xc, below = x2, (row('g_ple', l), w_pg_full, _LayerTokens(p, l), w_ple_full)

    dx, dg_final, loss_part = _fwd_ple_loss(x2, row('g_ple', depth - 1), w_pg_full, _LayerTokens(p, depth - 1),
                                            w_ple_full, g_final[None], target, "fwd_ple_loss")
    above = None
    landed = {name: [None] * depth for name in big_names}
    small = {name: [None] * depth for name in WEIGHT_NAMES if name not in BIG_AXIS and name != 'g_final'}

    def scatter_of(*partials):
        ex = _Scatter([(part, BIG_AXIS[name]) for name, _, part in partials])
        ex.places = [(name, layer) for name, layer, _ in partials]
        return ex

    def keep(ex):
        for (name, layer), land in zip(ex.places, ex.results):
            landed[name][layer] = land

    pending = []
    for l in reversed(range(depth)):
        xin, h, z, a_conv, a_act, s, cb, x1, h2, u, act, gl, dgl, x2, full = saved[l]
        outs = _bwd_ple(dx, x2, row('g_ple', l), full['w_ple_gate'], _LayerTokens(p, l), full['w_ple'],
                        f"bwd_ple_{l}", above)
        dx2, d_wpg, d_wple, small['g_ple'][l] = outs[:4]
        if above is not None:
            small['g_mix'][l + 1] = outs[4]
        d_wdown = _wgrad_tn(act, dx2, f"wgrad_down_{l}")
        carry = scatter_of(('w_ple_gate', l, d_wpg), ('w_ple', l, d_wple), ('w_down', l, d_wdown))
        du, small['conv_f_w'][l], small['conv_f_b'][l] = _bwd_down(
            dx2, u, gl, dgl, _Layer(conv_full['conv_f_w'], l), full['w_down'], f"bwd_down_{l}", carry)
        keep(carry)
        carry = scatter_of(*pending) if pending else None
        pending = []
        dx1, small['g_ffn'][l] = _bwd_norm_matmul(du, full['w_up'], x1, row('g_ffn', l), dx2, f"bwd_up_{l}", carry)
        if carry is not None:
            keep(carry)
        d_wup = _wgrad_tn(du, h2, f"wgrad_up_{l}")
        dact, ds, dgate, d_wo, d_wa, d_wb, small['b_gate'][l] = _bwd_merge(
            dx1, z, row('b_gate', l), a_act, s, full['w_a_out'], full['w_b_out'], full['w_o'], f"bwd_merge_{l}")
        carry = scatter_of(('w_up', l, d_wup))
        (dz, small['conv_a_w'][l], small['conv_a_b'][l], small['ln_a_g'][l], small['ln_a_b'][l],
         small['conv_b_w'][l]) = _bwd_branch(
            dact, ds, z, dgate, a_conv, cb, _Layer(conv_full['conv_a_w'], l), row('ln_a_g', l), row('ln_a_b', l),
            _Layer(conv_full['conv_b_w'], l), f"bwd_branch_{l}", carry)
        keep(carry)
        carry = scatter_of(('w_o', l, d_wo), ('w_a_out', l, d_wa), ('w_b_out', l, d_wb))
        d_win = _wgrad_tn(dz, h, f"wgrad_in_{l}", carry)
        keep(carry)
        if l > 0:
            pending = [('w_in', l, d_win)]
            above, dx = (dz, full['w_in'], xin, row('g_mix', l), dx1), None
        else:
            carry = scatter_of(('w_in', l, d_win))
            dx, small['g_mix'][l] = _bwd_norm_matmul(dz, full['w_in'], xin, row('g_mix', l), dx1, f"bwd_in_{l}",
                                                     carry)
            keep(carry)
    grad_x = dx[None]

    small_names = tuple(small)
    parts = [part for name in small_names for part in small[name]] + [dg_final, loss_part]
    plan = [tuple(range(k * depth, (k + 1) * depth)) for k in range(len(small_names))]
    plan += [(len(parts) - 2,), (len(parts) - 1,)]
    reduced = _all_reduce(parts, plan, "all_reduce_small")
    loss = reduced[-1][0, 0]
    grads = dict(zip(small_names, reduced[:len(small_names)]))
    grads['g_final'] = reduced[len(small_names)].reshape(g_final.shape)
    for name in CONV_SHARDED:
        _, taps, width = w[name].shape
        grads[name] = lax.dynamic_slice(grads[name], (0, 0, me * width), (depth, taps, width))

    delta, new_m, new_v = {}, {}, {}
    for name in big_names:
        view = (lambda a: jnp.swapaxes(a, 1, 2)) if name in TRANSPOSED else (lambda a: a)
        shape = view(w[name]).shape
        flat = lambda a: view(a).reshape(-1, shape[-1])
        lands = [land.reshape(N_DEV, -1, shape[-1]) for land in landed[name]]
        outs = _sum_adamw(lands, flat(w[name]), flat(mom[name]), flat(var[name]), f"adamw_{name}")
        grads[name], delta[name], new_m[name], new_v[name] = [view(a.reshape(shape)) for a in outs]
    rest = tuple(name for name in WEIGHT_NAMES if name not in BIG_AXIS)
    as_2d = lambda a: a.reshape(1, -1) if a.ndim == 1 else a
    outs = _adamw_small(*[[as_2d(src[name]) for name in rest] for src in (w, grads, mom, var)], "adamw_small")
    for dst, values in zip((delta, new_m, new_v), outs):
        dst.update({name: value.reshape(w[name].shape) for name, value in zip(rest, values)})

    return (loss, grad_x, *[grads[n] for n in WEIGHT_NAMES], *[delta[n] for n in WEIGHT_NAMES],
            *[new_m[n] for n in WEIGHT_NAMES], *[new_v[n] for n in WEIGHT_NAMES])
```

```python
import jax
import jax.numpy as jnp
from jax import lax
from jax.experimental import pallas as pl
from jax.experimental.pallas import tpu as pltpu

F32 = jnp.float32
BF16 = jnp.bfloat16
MESH = pl.DeviceIdType.MESH

N_DEV = 8
NORM_EPS = 1e-6
HALO = 32
LANES = 128
SUBLANES = 8
VMEM_LIMIT_BYTES = 56 * 2**20

ADAM_LR = 0.001
ADAM_B1 = 0.9
ADAM_B2 = 0.999
ADAM_EPS = 1e-08
ADAM_WD = 0.01
ADAM_STEP = 10

WEIGHT_NAMES = ('g_mix', 'w_in', 'b_gate', 'conv_a_w', 'conv_a_b', 'ln_a_g', 'ln_a_b', 'w_a_out',
                'conv_b_w', 'w_b_out', 'w_o', 'g_ffn', 'w_up', 'conv_f_w', 'conv_f_b', 'w_down',
                'g_ple', 'w_ple', 'w_ple_gate', 'g_final')
BIG_AXIS = {'w_in': 0, 'w_up': 0, 'w_a_out': 1, 'w_b_out': 1, 'w_o': 0, 'w_down': 0, 'w_ple': 1,
            'w_ple_gate': 0}
TRANSPOSED = ('w_in', 'w_up')
CONV_SHARDED = ('conv_a_w', 'conv_b_w', 'conv_f_w')
BRANCH_SMALL_ROWS = {'conv_a_w': (0, HALO), 'conv_a_b': (HALO, 1), 'ln_a_g': (HALO + 1, 1), 'ln_a_b': (HALO + 2, 1),
                     'conv_b_w': (HALO + SUBLANES, SUBLANES)}
DOWN_SMALL_ROWS = {'conv_f_w': (0, SUBLANES), 'conv_f_b': (SUBLANES, 1)}


def _dot(a, b):
    return jnp.dot(a, b, preferred_element_type=F32)


def _dot_nt(a, b):
    return lax.dot_general(a, b, (((1,), (1,)), ((), ())), preferred_element_type=F32)


def _dot_tn(a, b):
    return lax.dot_general(a, b, (((0,), (0,)), ((), ())), preferred_element_type=F32)


def _sigmoid(v):
    return jax.nn.sigmoid(v)


def _token_tile(t, cap=512):
    return cap if (t % cap == 0 and t > 512) else 128


def _chunk(n, limit=512):
    for c in range(limit - limit % LANES, 0, -LANES):
        if n % c == 0:
            return c
    return n


def _row_tile(rows):
    for c in (512, 256, 128, 64, 32, 16, 8):
        if rows % c == 0:
            return c
    return rows


def _rows(tm, width):
    return pl.BlockSpec((tm, width), lambda i: (i, 0))


def _rows_rev(tm, width, nt):
    return pl.BlockSpec((tm, width), lambda i: (nt - 1 - i, 0))


def _whole(shape):
    nd = len(shape)
    return pl.BlockSpec(tuple(shape), lambda i: (0,) * nd)


def _resident(shape):
    nd = len(shape)
    return pl.BlockSpec(tuple(shape), lambda i: (0,) * nd, pipeline_mode=pl.Buffered(1))


def _sds(shape, dtype):
    return jax.ShapeDtypeStruct(tuple(shape), dtype)


def _rms_stats(xv):
    r = lax.rsqrt(jnp.mean(xv * xv, axis=-1, keepdims=True) + NORM_EPS)
    return xv * r, r


def _rms_bwd(dy, xh, r, g):
    dxh = dy * g
    dx = r * (dxh - xh * jnp.mean(dxh * xh, axis=-1, keepdims=True))
    return dx, jnp.sum(dy * xh, axis=0, keepdims=True)


GELU_C0 = 0.7978845608028654
GELU_C1 = GELU_C0 * 0.044715


def _gelu_tanh(v):
    v2 = v * v
    t = jnp.tanh(v * (GELU_C0 + GELU_C1 * v2))
    q = 1.0 + t
    hv = 0.5 * v
    grad = 0.5 * q + hv * (1.0 - t * t) * (GELU_C0 + (3.0 * GELU_C1) * v2)
    return hv * q, grad


ROW_CHUNK = 32
LANE_CHUNK = 512
MXU_COLUMNS = 256


def _residues(taps):
    return [0] + sorted({off % SUBLANES for _, off in taps} - {0})


def _fill_rotations(rot_ref, residues, length):
    for plane, r in enumerate(residues):
        if r:
            rot_ref[plane, 0:length, :] = rot_ref[0, pl.ds(r, length), :]


def _broadcast_rows(dst_ref, src_ref, count):
    for k in range(count):
        dst_ref[k] = jnp.broadcast_to(src_ref[k:k + 1, :], dst_ref.shape[1:])


def _lane_chunks(width, chunk=LANE_CHUNK):
    return [(c0, min(chunk, width - c0)) for c0 in range(0, width, chunk)]


def _tap_conv(rot_ref, taps, wb_ref, out_ref, tm, bias_plane=None):
    residues = _residues(taps)
    plane = {r: p for p, r in enumerate(residues)}
    blocks = ROW_CHUNK // SUBLANES
    width = out_ref.shape[1]

    def chunk(c, state):
        r0 = c * ROW_CHUNK
        for c0, cw in _lane_chunks(width):
            accs = [None if bias_plane is None else wb_ref[bias_plane, :, c0:c0 + cw]] * blocks
            for k, off in taps:
                wk = wb_ref[k, :, c0:c0 + cw]
                base = off - off % SUBLANES
                for j in range(blocks):
                    at = pl.multiple_of(r0 + base + SUBLANES * j, SUBLANES)
                    term = wk * rot_ref[plane[off % SUBLANES], pl.ds(at, SUBLANES), c0:c0 + cw]
                    accs[j] = term if accs[j] is None else accs[j] + term
            for j in range(blocks):
                at = pl.multiple_of(r0 + SUBLANES * j, SUBLANES)
                out_ref[pl.ds(at, SUBLANES), c0:c0 + cw] = accs[j]
        return state

    lax.fori_loop(0, tm // ROW_CHUNK, chunk, 0)


def _tap_wgrad(rot_ref, taps, x_ref, acc_ref, tm):
    residues = _residues(taps)
    plane = {r: p for p, r in enumerate(residues)}
    blocks = ROW_CHUNK // SUBLANES
    width = acc_ref.shape[2]

    def chunk(c, state):
        r0 = c * ROW_CHUNK
        for c0, cw in _lane_chunks(width):
            xs = [x_ref[pl.ds(pl.multiple_of(r0 + SUBLANES * j, SUBLANES), SUBLANES), c0:c0 + cw]
                  for j in range(blocks)]
            for k, off in taps:
                base = off - off % SUBLANES
                part = None
                for j in range(blocks):
                    at = pl.multiple_of(r0 + base + SUBLANES * j, SUBLANES)
                    term = xs[j] * rot_ref[plane[off % SUBLANES], pl.ds(at, SUBLANES), c0:c0 + cw]
                    part = term if part is None else part + term
                acc_ref[k, :, c0:c0 + cw] += part
        return state

    lax.fori_loop(0, tm // ROW_CHUNK, chunk, 0)


def _fwd_taps(width):
    return [(k, HALO - (width - 1) + k) for k in range(width)]


def _bwd_taps(width):
    return [(k, width - 1 - k) for k in range(width)]


def _my_index():
    return 4 * lax.axis_index("x") + 2 * lax.axis_index("y") + lax.axis_index("c")


def _mesh_id(idx):
    return (idx // 4, (idx // 2) % 2, idx % 2)


def _slab(ref, axis, idx, width):
    at = [slice(None)] * len(ref.shape)
    at[axis] = pl.ds(pl.multiple_of(idx * width, width), width)
    return ref.at[tuple(at)]


class _Exchange:
    def __init__(self, inputs, out_shape):
        n = len(inputs)
        self.inputs = list(inputs)
        self.out_shape = list(out_shape)
        self.sems = [pltpu.SemaphoreType.DMA((n, N_DEV - 1)), pltpu.SemaphoreType.DMA((n, N_DEV - 1)),
                     pltpu.SemaphoreType.DMA((n,))]
        self.results = None

    def _local(self, ins, outs, k, me):
        raise NotImplementedError

    def _remote(self, ins, outs, k, me, sender, receiver):
        raise NotImplementedError

    def start(self, ins, outs, sems):
        send_sems, recv_sems, local_sems = sems
        me = _my_index()
        for k in range(len(self.inputs)):
            src, dst = self._local(ins, outs, k, me)
            pltpu.make_async_copy(src, dst, local_sems.at[k]).start()
            for dist in range(1, N_DEV):
                peer = (me + dist) % N_DEV
                src, dst = self._remote(ins, outs, k, me, me, peer)
                pltpu.make_async_remote_copy(
                    src_ref=src, dst_ref=dst, send_sem=send_sems.at[k, dist - 1],
                    recv_sem=recv_sems.at[k, dist - 1], device_id=_mesh_id(peer), device_id_type=MESH).start()

    def wait(self, ins, outs, sems):
        send_sems, recv_sems, local_sems = sems
        me = _my_index()
        for k in range(len(self.inputs)):
            for dist in range(1, N_DEV):
                sender = (me + N_DEV - dist) % N_DEV
                src, dst = self._remote(ins, outs, k, me, sender, me)
                cp = pltpu.make_async_remote_copy(
                    src_ref=src, dst_ref=dst, send_sem=send_sems.at[k, dist - 1],
                    recv_sem=recv_sems.at[k, dist - 1], device_id=_mesh_id(sender), device_id_type=MESH)
                cp.wait_send()
                cp.wait_recv()
            src, dst = self._local(ins, outs, k, me)
            pltpu.make_async_copy(src, dst, local_sems.at[k]).wait()

    def forward(self, ins, outs, sems):
        pass


class _Gather(_Exchange):
    FLIPS = ((1, 0), (0, 1), (1, 1))

    def __init__(self, items):
        self.items = list(items)
        out_shape = []
        for shards, layer, axis in self.items:
            shape = list(shards.shape if layer is None else shards.shape[1:])
            shape[axis] *= N_DEV
            out_shape.append(_sds(shape, shards.dtype))
        super().__init__([it[0] for it in self.items], out_shape)

    def _src(self, ins, k):
        layer = self.items[k][1]
        return ins[k] if layer is None else ins[k].at[layer]

    def _place(self, outs, k, idx):
        axis = self.items[k][2]
        return _slab(outs[k], axis, idx, self.out_shape[k].shape[axis] // N_DEV)

    def _copy(self, sems, k, j, src, dst, to):
        return pltpu.make_async_remote_copy(src_ref=src, dst_ref=dst, send_sem=sems[0].at[k, j],
                                            recv_sem=sems[1].at[k, j], device_id=to, device_id_type=MESH)

    @staticmethod
    def _places():
        x, y, c = lax.axis_index("x"), lax.axis_index("y"), lax.axis_index("c")
        chips = [(1 - x if fx else x, 1 - y if fy else y) for fx, fy in _Gather.FLIPS]
        return (x, y, c), (x, y, 1 - c), chips

    @staticmethod
    def _index(place):
        return 4 * place[0] + 2 * place[1] + place[2]

    def start(self, ins, outs, sems):
        me, sibling, chips = self._places()
        for k in range(len(self.inputs)):
            src, mine = self._src(ins, k), self._place(outs, k, self._index(me))
            pltpu.make_async_copy(src, mine, sems[2].at[k]).start()
            self._copy(sems, k, 0, src, mine, sibling).start()
            for j, chip in enumerate(chips):
                self._copy(sems, k, 1 + j, src, mine, (*chip, me[2])).start()

    def forward(self, ins, outs, sems):
        me, sibling, chips = self._places()
        for k in range(len(self.inputs)):
            for j, chip in enumerate(chips):
                got = self._place(outs, k, self._index((*chip, me[2])))
                self._copy(sems, k, 1 + j, got, got, (*chip, me[2])).wait_recv()
                self._copy(sems, k, 4 + j, got, got, sibling).start()

    def wait(self, ins, outs, sems):
        me, sibling, chips = self._places()
        for k in range(len(self.inputs)):
            src, mine = self._src(ins, k), self._place(outs, k, self._index(me))
            self._copy(sems, k, 0, src, self._place(outs, k, self._index(sibling)), sibling).wait_recv()
            for j, chip in enumerate(chips):
                got = self._place(outs, k, self._index((*chip, sibling[2])))
                self._copy(sems, k, 4 + j, got, got, sibling).wait_recv()
            for j in range(N_DEV - 1):
                self._copy(sems, k, j, src, mine, sibling).wait_send()
            pltpu.make_async_copy(src, mine, sems[2].at[k]).wait()


class _Scatter(_Exchange):
    def __init__(self, items):
        self.items = list(items)
        out_shape = []
        for partial, axis in self.items:
            shape = list(partial.shape)
            shape[axis] //= N_DEV
            out_shape.append(_sds([N_DEV] + shape, partial.dtype))
        super().__init__([it[0] for it in self.items], out_shape)

    def _take(self, ins, k, idx):
        axis = self.items[k][1]
        return _slab(ins[k], axis, idx, self.items[k][0].shape[axis] // N_DEV)

    def _local(self, ins, outs, k, me):
        return self._take(ins, k, me), outs[k].at[me]

    def _remote(self, ins, outs, k, me, sender, receiver):
        return self._take(ins, k, receiver), outs[k].at[sender]


def _run_exchange(exchange, name):
    n = len(exchange.inputs)

    def body(*refs):
        ins, outs, sems = refs[:n], refs[n:2 * n], refs[2 * n:]
        exchange.start(ins, outs, sems)
        exchange.forward(ins, outs, sems)
        exchange.wait(ins, outs, sems)

    any_spec = pl.BlockSpec(memory_space=pl.ANY)
    exchange.results = pl.pallas_call(
        body, name=name, in_specs=[any_spec] * n, out_specs=[any_spec] * n, out_shape=exchange.out_shape,
        scratch_shapes=exchange.sems)(*exchange.inputs)
    return exchange.results


class _Layer:
    def __init__(self, stack, index):
        self.stack, self.index = stack, index
        self.shape = (1,) + stack.shape[1:] if stack.ndim == 2 else stack.shape[1:]

    def view(self, ref):
        return ref.at[pl.ds(self.index, 1)] if self.stack.ndim == 2 else ref.at[self.index]


class _LayerTokens:
    def __init__(self, stack, index):
        self.stack, self.index = stack, index

    def rows(self, tm):
        return pl.BlockSpec((None, None, tm, self.stack.shape[-1]), lambda i: (self.index, 0, i, 0))


def _call(body, *, name, grid, in_specs, out_specs, out_shape, args, scratch=(), carry=None):
    in_specs, out_specs, out_shape, scratch = list(in_specs), list(out_specs), list(out_shape), list(scratch)
    args = [a.stack if isinstance(a, _LayerTokens) else a for a in args]
    layers = {k: a for k, a in enumerate(args) if isinstance(a, _Layer)}
    for k, a in layers.items():
        in_specs[k], args[k] = _whole(a.stack.shape), a.stack
    params = pltpu.CompilerParams(dimension_semantics=("arbitrary",) * len(grid),
                                  vmem_limit_bytes=VMEM_LIMIT_BYTES)
    n_in, n_out, n_scr = len(in_specs), len(out_specs), len(scratch)
    n_x = 0 if carry is None else len(carry.inputs)
    steps = 1
    for extent in grid:
        steps *= extent
    assert carry is None or steps >= 3, "a carrier needs a step each for start, second stage and wait"

    def whole_body(*refs):
        core_in, x_in = list(refs[:n_in]), refs[n_in:n_in + n_x]
        refs = refs[n_in + n_x:]
        core_out, x_out = refs[:n_out], refs[n_out:n_out + n_x]
        refs = refs[n_out + n_x:]
        core_scr, sems = refs[:n_scr], refs[n_scr:]
        for k, a in layers.items():
            core_in[k] = a.view(core_in[k])
        if carry is None:
            body(*core_in, *core_out, *core_scr)
            return
        step = pl.program_id(0)
        for axis in range(1, len(grid)):
            step = step * grid[axis] + pl.program_id(axis)

        @pl.when(step == 0)
        def _():
            carry.start(x_in, x_out, sems)

        @pl.when(step == steps - 2)
        def _():
            carry.forward(x_in, x_out, sems)

        body(*core_in, *core_out, *core_scr)

        @pl.when(step == steps - 1)
        def _():
            carry.wait(x_in, x_out, sems)

    any_spec = pl.BlockSpec(memory_space=pl.ANY)
    extra_in = [] if carry is None else carry.inputs
    extra_shape = [] if carry is None else carry.out_shape
    extra_sems = [] if carry is None else carry.sems
    outs = pl.pallas_call(
        whole_body, name=name, grid=grid, in_specs=in_specs + [any_spec] * n_x,
        out_specs=out_specs + [any_spec] * n_x, out_shape=out_shape + extra_shape,
        scratch_shapes=scratch + extra_sems, compiler_params=params)(*args, *extra_in)
    if carry is not None:
        carry.results = outs[n_out:]
    return outs[:n_out]


def _all_reduce(parts, plan, name):
    n = len(parts)
    out_shape = []
    for group in plan:
        k, _, rows = group[0]
        shape = (rows, parts[k].shape[1])
        if len(group) > 1:
            shape = (len(group),) + (shape[1:] if rows == 1 else shape)
        out_shape.append(_sds(shape, F32))

    def body(*refs):
        ins, outs, lands = refs[:n], refs[n:n + len(plan)], refs[n + len(plan):2 * n + len(plan)]
        send_sems, recv_sems = refs[2 * n + len(plan):]
        me = _my_index()
        for k in range(n):
            lands[k][me] = ins[k][...]
            for dist in range(1, N_DEV):
                pltpu.make_async_remote_copy(
                    src_ref=ins[k], dst_ref=lands[k].at[me],
                    send_sem=send_sems.at[k, dist - 1], recv_sem=recv_sems.at[k, dist - 1],
                    device_id=_mesh_id((me + dist) % N_DEV), device_id_type=MESH).start()
        for k in range(n):
            for dist in range(1, N_DEV):
                sender = (me + N_DEV - dist) % N_DEV
                cp = pltpu.make_async_remote_copy(
                    src_ref=ins[k], dst_ref=lands[k].at[sender],
                    send_sem=send_sems.at[k, dist - 1], recv_sem=recv_sems.at[k, dist - 1],
                    device_id=_mesh_id(sender), device_id_type=MESH)
                cp.wait_send()
                cp.wait_recv()
        for k in range(n):
            total = lands[k][0]
            for dev in range(1, N_DEV):
                total = total + lands[k][dev]
            lands[k][0] = total
        for o_ref, group in zip(outs, plan):
            for j, (k, row0, rows) in enumerate(group):
                picked = lands[k][0, row0:row0 + rows, :]
                if len(group) == 1:
                    o_ref[...] = picked
                elif rows == 1:
                    o_ref[j:j + 1, :] = picked
                else:
                    o_ref[j] = picked

    vmem = pl.BlockSpec(memory_space=pltpu.VMEM)
    return pl.pallas_call(
        body, name=name, in_specs=[vmem] * n, out_specs=[vmem] * len(plan), out_shape=out_shape,
        scratch_shapes=[pltpu.VMEM((N_DEV,) + part.shape, F32) for part in parts]
        + [pltpu.SemaphoreType.DMA((n, N_DEV - 1)), pltpu.SemaphoreType.DMA((n, N_DEV - 1))])(*parts)


def _ple_out(xv, g_ref, wpg_ref, p_ref, wple_ref):
    xh, _ = _rms_stats(xv)
    lg = _dot((xh * g_ref[...]).astype(BF16), wpg_ref[...])
    return xv + _sigmoid(lg) * _dot(p_ref[...].astype(BF16), wple_ref[...])


def _norm_matmul(x, g, wt, name, carry=None, ple=None):
    t, d = x.shape
    n = wt.shape[0]
    tm, nc = _token_tile(t), _chunk(n)

    def body(x_ref, g_ref, wt_ref, *rest):
        xv = x_ref[...]
        if ple is not None:
            xv = _ple_out(xv, *rest[:4])
            rest[4][...] = xv
        h_ref, o_ref = rest[-2:]
        xh, _ = _rms_stats(xv)
        h = (xh * g_ref[...]).astype(BF16)
        h_ref[...] = h
        for n0 in range(0, n, nc):
            o_ref[:, n0:n0 + nc] = _dot_nt(h, wt_ref[n0:n0 + nc, :]).astype(BF16)

    in_specs = [_rows(tm, d), _whole(g.shape), _resident(wt.shape)]
    out_specs, out_shape, args = [_rows(tm, d), _rows(tm, n)], [_sds((t, d), BF16), _sds((t, n), BF16)], (x, g, wt)
    if ple is not None:
        g_ple, wpg, p, wple = ple
        in_specs += [_whole(g_ple.shape), _resident(wpg.shape), p.rows(tm), _resident(wple.shape)]
        out_specs, out_shape, args = [_rows(tm, d)] + out_specs, [_sds((t, d), F32)] + out_shape, args + ple
    return _call(body, name=name, grid=(t // tm,), in_specs=in_specs, out_specs=out_specs, out_shape=out_shape,
                 args=args, carry=carry)


def _fwd_branch(z, caw, cab, lng, lnb, cbw, dc, name, carry=None):
    t = z.shape[0]
    tm = _token_tile(t)
    ka, kb = caw.shape[0], cbw.shape[0]
    taps_a, taps_b = _fwd_taps(ka), _fwd_taps(kb)
    res_a, res_b = _residues(taps_a), _residues(taps_b)
    span = HALO + tm - SUBLANES

    def body(z_ref, caw_ref, cab_ref, lng_ref, lnb_ref, cbw_ref, ac_ref, act_ref, s_ref, cb_ref,
             rot_a, rot_b, wb_a, wb_b, cb):
        @pl.when(pl.program_id(0) == 0)
        def _():
            rot_a[0, 0:HALO, :] = jnp.zeros((HALO, dc), F32)
            rot_b[0, 0:HALO, :] = jnp.zeros((HALO, dc), F32)
            _broadcast_rows(wb_a, caw_ref, ka)
            wb_a[ka] = jnp.broadcast_to(cab_ref[...], (SUBLANES, dc))
            _broadcast_rows(wb_b, cbw_ref, kb)

        a_val = z_ref[:, 0:dc].astype(F32)
        a_gt = z_ref[:, dc:2 * dc].astype(F32)
        rot_a[0, HALO:HALO + tm, :] = a_val * _sigmoid(a_gt)
        _fill_rotations(rot_a, res_a, span)
        _tap_conv(rot_a, taps_a, wb_a, ac_ref, tm, bias_plane=ka)
        ac = ac_ref[...]
        mu = jnp.mean(ac, axis=-1, keepdims=True)
        xc = ac - mu
        var = jnp.mean(xc * xc, axis=-1, keepdims=True)
        ln = xc * lax.rsqrt(var + NORM_EPS) * lng_ref[...] + lnb_ref[...]
        act_ref[...] = (ln * _sigmoid(ln)).astype(BF16)
        rot_a[0, 0:HALO, :] = rot_a[0, tm:tm + HALO, :]

        sc_c = z_ref[:, 3 * dc:4 * dc].astype(F32)
        sc_v = z_ref[:, 4 * dc:5 * dc].astype(F32)
        rot_b[0, HALO:HALO + tm, :] = sc_c * sc_v
        _fill_rotations(rot_b, res_b, span)
        _tap_conv(rot_b, taps_b, wb_b, cb, tm)
        s_ref[...] = (z_ref[:, 2 * dc:3 * dc].astype(F32) * cb[...]).astype(BF16)
        cb_ref[...] = cb[...].astype(BF16)
        rot_b[0, 0:HALO, :] = rot_b[0, tm:tm + HALO, :]

    return _call(body, name=name, grid=(t // tm,),
                 in_specs=[_rows(tm, 5 * dc), _whole(caw.shape), _whole(cab.shape), _whole(lng.shape),
                           _whole(lnb.shape), _whole(cbw.shape)],
                 out_specs=[_rows(tm, dc), _rows(tm, dc), _rows(tm, dc), _rows(tm, dc)],
                 out_shape=[_sds((t, dc), F32), _sds((t, dc), BF16), _sds((t, dc), BF16), _sds((t, dc), BF16)],
                 scratch=[pltpu.VMEM((len(res_a), HALO + tm, dc), F32), pltpu.VMEM((len(res_b), HALO + tm, dc), F32),
                          pltpu.VMEM((ka + 1, SUBLANES, dc), F32), pltpu.VMEM((kb, SUBLANES, dc), F32),
                          pltpu.VMEM((tm, dc), F32)],
                 args=(z, caw, cab, lng, lnb, cbw), carry=carry)


def _fwd_merge(x, z, bg, a_act, s, wa, wb, wo, name, carry=None):
    t, d = x.shape
    n = z.shape[1]
    dc = a_act.shape[1]
    tm = _token_tile(t)
    o5 = n - 2 * d

    def body(x_ref, z_ref, bg_ref, act_ref, s_ref, wa_ref, wb_ref, wo_ref, o_ref):
        ya = _dot(act_ref[...], wa_ref[...])
        yb = _dot(s_ref[...], wb_ref[...])
        ga = _sigmoid(z_ref[:, o5:o5 + d].astype(F32) + bg_ref[:, 0:d])
        gb = _sigmoid(z_ref[:, o5 + d:n].astype(F32) + bg_ref[:, d:2 * d])
        m = (ga * ya + gb * yb).astype(BF16)
        o_ref[...] = x_ref[...] + _dot(m, wo_ref[...])

    return _call(body, name=name, grid=(t // tm,),
                 in_specs=[_rows(tm, d), _rows(tm, n), _whole(bg.shape), _rows(tm, dc), _rows(tm, dc),
                           _resident(wa.shape), _resident(wb.shape), _resident(wo.shape)],
                 out_specs=[_rows(tm, d)], out_shape=[_sds((t, d), F32)],
                 args=(x, z, bg, a_act, s, wa, wb, wo), carry=carry)[0]


def _fwd_down(x, u, cfw, cfb, wd, name, carry=None):
    t, d = x.shape
    f = u.shape[1] // 2
    tm = _token_tile(t, 256)
    kf = cfw.shape[0]

    taps = _fwd_taps(kf)
    residues = _residues(taps)

    def body(x_ref, u_ref, cfw_ref, cfb_ref, wd_ref, o_ref, act_ref, gl_ref, dgl_ref, rot_u, wb, fg):
        @pl.when(pl.program_id(0) == 0)
        def _():
            rot_u[0, 0:HALO, :] = jnp.zeros((HALO, f), F32)
            _broadcast_rows(wb, cfw_ref, kf)
            wb[kf] = jnp.broadcast_to(cfb_ref[...], (SUBLANES, f))

        rot_u[0, HALO:HALO + tm, :] = u_ref[:, 0:f].astype(F32)
        _fill_rotations(rot_u, residues, HALO + tm - SUBLANES)
        _tap_conv(rot_u, taps, wb, fg, tm, bias_plane=kf)
        y = x_ref[...]
        for c0, cw in _lane_chunks(f, MXU_COLUMNS):
            at = slice(c0, c0 + cw)
            gl, dgl = _gelu_tanh(fg[:, at])
            gl_ref[:, at] = gl.astype(BF16)
            dgl_ref[:, at] = dgl.astype(BF16)
            act = (gl * u_ref[:, f + c0:f + c0 + cw].astype(F32)).astype(BF16)
            act_ref[:, at] = act
            y = y + _dot(act, wd_ref[at, :])
        o_ref[...] = y
        rot_u[0, 0:HALO, :] = rot_u[0, tm:tm + HALO, :]

    return _call(body, name=name, grid=(t // tm,),
                 in_specs=[_rows(tm, d), _rows(tm, 2 * f), _whole(cfw.shape), _whole(cfb.shape),
                           _resident(wd.shape)],
                 out_specs=[_rows(tm, d), _rows(tm, f), _rows(tm, f), _rows(tm, f)],
                 out_shape=[_sds((t, d), F32), _sds((t, f), BF16), _sds((t, f), BF16), _sds((t, f), BF16)],
                 scratch=[pltpu.VMEM((len(residues), HALO + tm, f), F32), pltpu.VMEM((kf + 1, SUBLANES, f), F32),
                          pltpu.VMEM((tm, f), F32)],
                 args=(x, u, cfw, cfb, wd), carry=carry)


def _fwd_ple_loss(x, g, wpg, p, wple, g_final, target, name):
    t, d = x.shape
    tm = _token_tile(t)

    def body(x_ref, g_ref, wpg_ref, p_ref, wple_ref, gf_ref, t_ref, dy_ref, dg_ref, loss_ref):
        y = _ple_out(x_ref[...], g_ref, wpg_ref, p_ref, wple_ref)

        @pl.when(pl.program_id(0) == 0)
        def _():
            dg_ref[...] = jnp.zeros_like(dg_ref)
            loss_ref[...] = jnp.zeros_like(loss_ref)

        yh, r = _rms_stats(y)
        err = yh * gf_ref[...] - t_ref[...]
        sq = jnp.sum(jnp.sum(err * err, axis=0, keepdims=True), axis=1, keepdims=True)
        loss_ref[...] += jnp.broadcast_to(0.5 * sq / d, loss_ref.shape)
        dy, dg = _rms_bwd(err / d, yh, r, gf_ref[...])
        dy_ref[...] = dy
        dg_ref[...] += dg

    return _call(body, name=name, grid=(t // tm,),
                 in_specs=[_rows(tm, d), _whole(g.shape), _resident(wpg.shape), p.rows(tm), _resident(wple.shape),
                           _whole(g_final.shape), _rows(tm, d)],
                 out_specs=[_rows(tm, d), _whole((1, d)), _whole((SUBLANES, LANES))],
                 out_shape=[_sds((t, d), F32), _sds((1, d), F32), _sds((SUBLANES, LANES), F32)],
                 args=(x, g, wpg, p, wple, g_final, target))


def _bwd_ple(dy, x, g, wpg, p, wple, name, above=None):
    t, d = x.shape
    pd = p.stack.shape[-1]
    tm = _token_tile(t)
    nt = t // tm
    n_above = 0 if above is None else len(above)

    def body(*refs):
        head, refs = refs[:max(n_above, 1)], refs[max(n_above, 1):]
        x_ref, g_ref, wpg_ref, p_ref, wple_ref, dx_ref, dwpg_ref, dwple_ref, dg_ref = refs[:9]
        dg_in_ref = refs[9] if above is not None else None
        acc_pg, acc_ple = refs[-2:]
        i = pl.program_id(0)

        @pl.when(i == 0)
        def _():
            acc_pg[...] = jnp.zeros_like(acc_pg)
            acc_ple[...] = jnp.zeros_like(acc_ple)
            dg_ref[...] = jnp.zeros_like(dg_ref)
            if above is not None:
                dg_in_ref[...] = jnp.zeros_like(dg_in_ref)

        if above is None:
            dyv = head[0][...]
        else:
            dz_ref, wt_ref, xin_ref, gin_ref, dres_ref = head
            xh_in, r_in = _rms_stats(xin_ref[...])
            dx_in, dg_in = _rms_bwd(_dot(dz_ref[...], wt_ref[...]), xh_in, r_in, gin_ref[...])
            dyv = dres_ref[...] + dx_in
            dg_in_ref[...] += dg_in
        xh, r = _rms_stats(x_ref[...])
        h = (xh * g_ref[...]).astype(BF16)
        pb = p_ref[...].astype(BF16)
        pg = _sigmoid(_dot(h, wpg_ref[...]))
        pp = _dot(pb, wple_ref[...])
        dpp = (dyv * pg).astype(BF16)
        dlg = (dyv * pp * pg * (1.0 - pg)).astype(BF16)
        acc_ple[...] += _dot_tn(pb, dpp)
        acc_pg[...] += _dot_tn(h, dlg)
        dx, dg = _rms_bwd(_dot_nt(dlg, wpg_ref[...]), xh, r, g_ref[...])
        dx_ref[...] = dyv + dx
        dg_ref[...] += dg

        @pl.when(i == nt - 1)
        def _():
            dwpg_ref[...] = acc_pg[...].astype(BF16)
            dwple_ref[...] = acc_ple[...].astype(BF16)

    if above is None:
        head_specs, head_args, more_specs, more_shape = [_rows(tm, d)], (dy,), [], []
    else:
        dz, wt, x_in, g_in, dres = above
        head_specs = [_rows(tm, dz.shape[1]), _resident(wt.shape), _rows(tm, d), _whole(g_in.shape), _rows(tm, d)]
        head_args, more_specs, more_shape = above, [_whole((1, d))], [_sds((1, d), F32)]
    return _call(body, name=name, grid=(nt,),
                 in_specs=head_specs + [_rows(tm, d), _whole(g.shape), _resident(wpg.shape), p.rows(tm),
                                        _resident(wple.shape)],
                 out_specs=[_rows(tm, d), _whole((d, d)), _whole((pd, d)), _whole((1, d))] + more_specs,
                 out_shape=[_sds((t, d), F32), _sds((d, d), BF16), _sds((pd, d), BF16), _sds((1, d), F32)]
                 + more_shape,
                 scratch=[pltpu.VMEM((d, d), F32), pltpu.VMEM((pd, d), F32)],
                 args=(*head_args, x, g, wpg, p, wple))


def _bwd_down(dy, u, gl, dgl, cfw, wd, name, carry=None):
    t, d = dy.shape
    f = u.shape[1] // 2
    tm = _token_tile(t, 256)
    nt = t // tm
    kf = cfw.shape[0]
    bwd = _bwd_taps(kf)
    residues = _residues(bwd)

    def body(dy_ref, u_ref, gl_ref, dgl_ref, cfw_ref, wd_ref, du_ref, small_ref, rot_g, wb, acc, buf):
        i = pl.program_id(0)
        dcw_ref = small_ref.at[pl.ds(0, SUBLANES)]
        dcb_ref = small_ref.at[pl.ds(SUBLANES, 1)]

        @pl.when(i == 0)
        def _():
            for ref in (small_ref, acc):
                ref[...] = jnp.zeros_like(ref)
            rot_g[0, tm:tm + HALO, :] = jnp.zeros((HALO, f), F32)
            _broadcast_rows(wb, cfw_ref, kf)

        dyb = dy_ref[...].astype(BF16)
        for c0, cw in _lane_chunks(f, MXU_COLUMNS):
            at = slice(c0, c0 + cw)
            df = _dot_nt(dyb, wd_ref[at, :])
            du_ref[:, f + c0:f + c0 + cw] = (df * gl_ref[:, at].astype(F32)).astype(BF16)
            dfg = df * u_ref[:, f + c0:f + c0 + cw].astype(F32) * dgl_ref[:, at].astype(F32)
            dcb_ref[:, at] += jnp.sum(dfg, axis=0, keepdims=True)
            rot_g[0, 0:tm, at] = dfg
        _fill_rotations(rot_g, residues, HALO + tm - SUBLANES)
        buf[...] = u_ref[:, 0:f].astype(F32)
        _tap_wgrad(rot_g, bwd, buf, acc, tm)
        _tap_conv(rot_g, bwd, wb, buf, tm)
        du_ref[:, 0:f] = buf[...].astype(BF16)
        rot_g[0, tm:tm + HALO, :] = rot_g[0, 0:HALO, :]

        @pl.when(i == nt - 1)
        def _():
            dcw_ref[0:kf, :] = jnp.sum(acc[...], axis=1)

    return _call(body, name=name, grid=(nt,),
                 in_specs=[_rows_rev(tm, d, nt), _rows_rev(tm, 2 * f, nt), _rows_rev(tm, f, nt),
                           _rows_rev(tm, f, nt), _whole(cfw.shape), _resident(wd.shape)],
                 out_specs=[_rows_rev(tm, 2 * f, nt), _whole((2 * SUBLANES, f))],
                 out_shape=[_sds((t, 2 * f), BF16), _sds((2 * SUBLANES, f), F32)],
                 scratch=[pltpu.VMEM((len(residues), HALO + tm, f), F32), pltpu.VMEM((kf, SUBLANES, f), F32),
                          pltpu.VMEM((kf, SUBLANES, f), F32), pltpu.VMEM((tm, f), F32)],
                 args=(dy, u, gl, dgl, cfw, wd), carry=carry)


def _bwd_norm_matmul(dout, wt, x, g, dres, name, carry=None):
    t, d = x.shape
    n = dout.shape[1]
    tm = _token_tile(t)

    def body(do_ref, wt_ref, x_ref, g_ref, dres_ref, dx_ref, dg_ref):
        @pl.when(pl.program_id(0) == 0)
        def _():
            dg_ref[...] = jnp.zeros_like(dg_ref)

        dh = _dot(do_ref[...], wt_ref[...])
        xh, r = _rms_stats(x_ref[...])
        dx, dg = _rms_bwd(dh, xh, r, g_ref[...])
        dx_ref[...] = dres_ref[...] + dx
        dg_ref[...] += dg

    return _call(body, name=name, grid=(t // tm,),
                 in_specs=[_rows(tm, n), _resident(wt.shape), _rows(tm, d), _whole(g.shape), _rows(tm, d)],
                 out_specs=[_rows(tm, d), _whole((1, d))],
                 out_shape=[_sds((t, d), F32), _sds((1, d), F32)],
                 args=(dout, wt, x, g, dres), carry=carry)


def _wgrad_tn(a, b, name, carry=None):
    t, n = a.shape
    d = b.shape[1]
    tt = 2048 if t % 2048 == 0 else _token_tile(t)
    tn = _chunk(n, 1536)
    nt = t // tt

    def body(a_ref, b_ref, o_ref, acc):
        k = pl.program_id(1)

        @pl.when(k == 0)
        def _():
            acc[...] = jnp.zeros_like(acc)

        acc[...] += _dot_tn(a_ref[...].astype(BF16), b_ref[...].astype(BF16))

        @pl.when(k == nt - 1)
        def _():
            o_ref[...] = acc[...].astype(BF16)

    return _call(body, name=name, grid=(n // tn, nt),
                 in_specs=[pl.BlockSpec((tt, tn), lambda j, k: (k, j)), pl.BlockSpec((tt, d), lambda j, k: (k, 0))],
                 out_specs=[pl.BlockSpec((tn, d), lambda j, k: (j, 0))], out_shape=[_sds((n, d), BF16)],
                 scratch=[pltpu.VMEM((tn, d), F32)], args=(a, b), carry=carry)[0]


def _bwd_merge(dy, z, bg, a_act, s, wa, wb, wo, name, carry=None):
    t, d = dy.shape
    n = z.shape[1]
    dc = a_act.shape[1]
    tm = _token_tile(t)
    nt = t // tm
    o5 = n - 2 * d

    def body(dy_ref, z_ref, bg_ref, act_ref, s_ref, wa_ref, wb_ref, wo_ref,
             dact_ref, ds_ref, dgl_ref, dwo_ref, dwa_ref, dwb_ref, dbg_ref, acc_o, acc_a, acc_b):
        i = pl.program_id(0)

        @pl.when(i == 0)
        def _():
            acc_o[...] = jnp.zeros_like(acc_o)
            acc_a[...] = jnp.zeros_like(acc_a)
            acc_b[...] = jnp.zeros_like(acc_b)
            dbg_ref[...] = jnp.zeros_like(dbg_ref)

        dyb = dy_ref[...].astype(BF16)
        dm = _dot_nt(dyb, wo_ref[...])
        ya = _dot(act_ref[...], wa_ref[...])
        yb = _dot(s_ref[...], wb_ref[...])
        ga = _sigmoid(z_ref[:, o5:o5 + d].astype(F32) + bg_ref[:, 0:d])
        gb = _sigmoid(z_ref[:, o5 + d:n].astype(F32) + bg_ref[:, d:2 * d])
        acc_o[...] += _dot_tn((ga * ya + gb * yb).astype(BF16), dyb)
        dya = (dm * ga).astype(BF16)
        dyb2 = (dm * gb).astype(BF16)
        acc_a[...] += _dot_tn(act_ref[...], dya)
        acc_b[...] += _dot_tn(s_ref[...], dyb2)
        dact_ref[...] = _dot_nt(dya, wa_ref[...])
        ds_ref[...] = _dot_nt(dyb2, wb_ref[...])
        dla = dm * ya * ga * (1.0 - ga)
        dlb = dm * yb * gb * (1.0 - gb)
        dgl_ref[:, 0:d] = dla.astype(BF16)
        dgl_ref[:, d:2 * d] = dlb.astype(BF16)
        dbg_ref[:, 0:d] += jnp.sum(dla, axis=0, keepdims=True)
        dbg_ref[:, d:2 * d] += jnp.sum(dlb, axis=0, keepdims=True)

        @pl.when(i == nt - 1)
        def _():
            dwo_ref[...] = acc_o[...].astype(BF16)
            dwa_ref[...] = acc_a[...].astype(BF16)
            dwb_ref[...] = acc_b[...].astype(BF16)

    return _call(body, name=name, grid=(nt,),
                 in_specs=[_rows(tm, d), _rows(tm, n), _whole(bg.shape), _rows(tm, dc), _rows(tm, dc),
                           _resident(wa.shape), _resident(wb.shape), _resident(wo.shape)],
                 out_specs=[_rows(tm, dc), _rows(tm, dc), _rows(tm, 2 * d), _whole((d, d)), _whole((dc, d)),
                            _whole((dc, d)), _whole((1, 2 * d))],
                 out_shape=[_sds((t, dc), F32), _sds((t, dc), F32), _sds((t, 2 * d), BF16), _sds((d, d), BF16),
                            _sds((dc, d), BF16), _sds((dc, d), BF16), _sds((1, 2 * d), F32)],
                 scratch=[pltpu.VMEM((d, d), F32), pltpu.VMEM((dc, d), F32), pltpu.VMEM((dc, d), F32)],
                 args=(dy, z, bg, a_act, s, wa, wb, wo), carry=carry)


def _bwd_branch(dact, ds, z, dgl, a_conv, cb, caw, lng, lnb, cbw, name, carry=None):
    t, n = z.shape
    dc = a_conv.shape[1]
    tm = _token_tile(t)
    nt = t // tm
    ka, kb = caw.shape[0], cbw.shape[0]
    bwd_a, bwd_b = _bwd_taps(ka), _bwd_taps(kb)
    span = HALO + tm - SUBLANES

    def body(dact_ref, ds_ref, z_ref, dgl_ref, ac_ref, cb_ref, caw_ref, lng_ref, lnb_ref, cbw_ref,
             dz_ref, small_ref, rot_da, rot_dc, wb_a, wb_b, acc_a, acc_b, buf):
        i = pl.program_id(0)
        dcaw_ref = small_ref.at[pl.ds(0, HALO)]
        dcab_ref = small_ref.at[pl.ds(HALO, 1)]
        dlng_ref = small_ref.at[pl.ds(HALO + 1, 1)]
        dlnb_ref = small_ref.at[pl.ds(HALO + 2, 1)]
        dcbw_ref = small_ref.at[pl.ds(HALO + SUBLANES, SUBLANES)]

        @pl.when(i == 0)
        def _():
            for ref in (small_ref, acc_a, acc_b):
                ref[...] = jnp.zeros_like(ref)
            rot_da[0, tm:tm + HALO, :] = jnp.zeros((HALO, dc), F32)
            rot_dc[0, tm:tm + HALO, :] = jnp.zeros((HALO, dc), F32)
            _broadcast_rows(wb_a, caw_ref, ka)
            _broadcast_rows(wb_b, cbw_ref, kb)

        a_val = z_ref[:, 0:dc].astype(F32)
        sg = _sigmoid(z_ref[:, dc:2 * dc].astype(F32))

        ac = ac_ref[...]
        mu = jnp.mean(ac, axis=-1, keepdims=True)
        xc = ac - mu
        rstd = lax.rsqrt(jnp.mean(xc * xc, axis=-1, keepdims=True) + NORM_EPS)
        xh = xc * rstd
        ln = xh * lng_ref[...] + lnb_ref[...]
        sl = _sigmoid(ln)
        dln = dact_ref[...] * (sl * (1.0 + ln * (1.0 - sl)))
        dlng_ref[...] += jnp.sum(dln * xh, axis=0, keepdims=True)
        dlnb_ref[...] += jnp.sum(dln, axis=0, keepdims=True)
        dxh = dln * lng_ref[...]
        dac = rstd * (dxh - jnp.mean(dxh, axis=-1, keepdims=True)
                      - xh * jnp.mean(dxh * xh, axis=-1, keepdims=True))
        dcab_ref[...] += jnp.sum(dac, axis=0, keepdims=True)
        rot_da[0, 0:tm, :] = dac
        _fill_rotations(rot_da, _residues(bwd_a), span)
        buf[...] = a_val * sg
        _tap_wgrad(rot_da, bwd_a, buf, acc_a, tm)
        _tap_conv(rot_da, bwd_a, wb_a, buf, tm)
        rot_da[0, tm:tm + HALO, :] = rot_da[0, 0:HALO, :]
        da = buf[...]
        dz_ref[:, 0:dc] = (da * sg).astype(BF16)
        dz_ref[:, dc:2 * dc] = (da * a_val * sg * (1.0 - sg)).astype(BF16)

        sc_b = z_ref[:, 2 * dc:3 * dc].astype(F32)
        sc_c = z_ref[:, 3 * dc:4 * dc].astype(F32)
        sc_v = z_ref[:, 4 * dc:5 * dc].astype(F32)
        dsv = ds_ref[...]
        dz_ref[:, 2 * dc:3 * dc] = (dsv * cb_ref[...].astype(F32)).astype(BF16)
        rot_dc[0, 0:tm, :] = dsv * sc_b
        _fill_rotations(rot_dc, _residues(bwd_b), span)
        buf[...] = sc_c * sc_v
        _tap_wgrad(rot_dc, bwd_b, buf, acc_b, tm)
        _tap_conv(rot_dc, bwd_b, wb_b, buf, tm)
        rot_dc[0, tm:tm + HALO, :] = rot_dc[0, 0:HALO, :]
        dcv = buf[...]
        dz_ref[:, 3 * dc:4 * dc] = (dcv * sc_v).astype(BF16)
        dz_ref[:, 4 * dc:5 * dc] = (dcv * sc_c).astype(BF16)
        dz_ref[:, 5 * dc:n] = dgl_ref[...]

        @pl.when(i == nt - 1)
        def _():
            dcaw_ref[0:ka, :] = jnp.sum(acc_a[...], axis=1)
            dcbw_ref[0:kb, :] = jnp.sum(acc_b[...], axis=1)

    def planes(taps):
        return pltpu.VMEM((len(_residues(taps)), HALO + tm, dc), F32)

    return _call(body, name=name, grid=(nt,),
                 in_specs=[_rows_rev(tm, dc, nt), _rows_rev(tm, dc, nt), _rows_rev(tm, 5 * dc, nt),
                           _rows_rev(tm, n - 5 * dc, nt), _rows_rev(tm, dc, nt), _rows_rev(tm, dc, nt),
                           _whole(caw.shape), _whole(lng.shape), _whole(lnb.shape), _whole(cbw.shape)],
                 out_specs=[_rows_rev(tm, n, nt), _whole((HALO + 2 * SUBLANES, dc))],
                 out_shape=[_sds((t, n), BF16), _sds((HALO + 2 * SUBLANES, dc), F32)],
                 scratch=[planes(bwd_a), planes(bwd_b),
                          pltpu.VMEM((ka, SUBLANES, dc), F32), pltpu.VMEM((kb, SUBLANES, dc), F32),
                          pltpu.VMEM((ka, SUBLANES, dc), F32), pltpu.VMEM((kb, SUBLANES, dc), F32),
                          pltpu.VMEM((tm, dc), F32)],
                 args=(dact, ds, z, dgl, a_conv, cb, caw, lng, lnb, cbw), carry=carry)


def _land_specs(depth, nr, tr, cols):
    def spec(k):
        return pl.BlockSpec((N_DEV, tr, cols), lambda i: (0, jnp.clip(i - k * nr, 0, nr - 1), 0))
    return [spec(k) for k in range(depth)]


def _adamw_math(w, g, m, v):
    nm = ADAM_B1 * m + (1.0 - ADAM_B1) * g
    nv = ADAM_B2 * v + (1.0 - ADAM_B2) * (g * g)
    m_hat = nm / (1.0 - ADAM_B1 ** ADAM_STEP)
    v_hat = nv / (1.0 - ADAM_B2 ** ADAM_STEP)
    return -ADAM_LR * (m_hat / (jnp.sqrt(v_hat) + ADAM_EPS) + ADAM_WD * w), nm, nv


def _sum_adamw(lands, w, m, v, name):
    _, rows, cols = lands[0].shape
    tr = _row_tile(rows)
    nr = rows // tr
    depth = len(lands)

    def body(*refs):
        w_ref, m_ref, v_ref, g_ref, d_ref, nm_ref, nv_ref = refs[depth:]
        i = pl.program_id(0)
        for k in range(depth):
            @pl.when(i // nr == k)
            def _(k=k):
                acc = refs[k][0].astype(F32)
                for j in range(1, N_DEV):
                    acc = acc + refs[k][j].astype(F32)
                g_ref[...] = acc
                d_ref[...], nm_ref[...], nv_ref[...] = _adamw_math(w_ref[...], acc, m_ref[...], v_ref[...])

    spec = _rows(tr, cols)
    return _call(body, name=name, grid=(depth * nr,), in_specs=_land_specs(depth, nr, tr, cols) + [spec] * 3,
                 out_specs=[spec] * 4, out_shape=[_sds((depth * rows, cols), F32)] * 4, args=(*lands, w, m, v))


def _adamw_small(ws, gs, ms, vs, name):
    n = len(ws)

    def body(*refs):
        w_refs, g_refs, m_refs, v_refs = refs[:n], refs[n:2 * n], refs[2 * n:3 * n], refs[3 * n:4 * n]
        d_refs, nm_refs, nv_refs = refs[4 * n:5 * n], refs[5 * n:6 * n], refs[6 * n:]
        for k in range(n):
            d_refs[k][...], nm_refs[k][...], nv_refs[k][...] = _adamw_math(
                w_refs[k][...], g_refs[k][...], m_refs[k][...], v_refs[k][...])

    vmem = pl.BlockSpec(memory_space=pltpu.VMEM)
    outs = pl.pallas_call(
        body, name=name, in_specs=[vmem] * (4 * n), out_specs=[vmem] * (3 * n),
        out_shape=[_sds(a.shape, F32) for a in ws] * 3)(*ws, *gs, *ms, *vs)
    return outs[:n], outs[n:2 * n], outs[2 * n:]


def kernel(x, p, g_mix, w_in, b_gate, conv_a_w, conv_a_b, ln_a_g, ln_a_b, w_a_out, conv_b_w, w_b_out, w_o, g_ffn, w_up, conv_f_w, conv_f_b, w_down, g_ple, w_ple, w_ple_gate, g_final, loss_target, m_g_mix, m_w_in, m_b_gate, m_conv_a_w, m_conv_a_b, m_ln_a_g, m_ln_a_b, m_w_a_out, m_conv_b_w, m_w_b_out, m_w_o, m_g_ffn, m_w_up, m_conv_f_w, m_conv_f_b, m_w_down, m_g_ple, m_w_ple, m_w_ple_gate, m_g_final, v_g_mix, v_w_in, v_b_gate, v_conv_a_w, v_conv_a_b, v_ln_a_g, v_ln_a_b, v_w_a_out, v_conv_b_w, v_w_b_out, v_w_o, v_g_ffn, v_w_up, v_conv_f_w, v_conv_f_b, v_w_down, v_g_ple, v_w_ple, v_w_ple_gate, v_g_final):
    w = dict(zip(WEIGHT_NAMES, (g_mix, w_in, b_gate, conv_a_w, conv_a_b, ln_a_g, ln_a_b, w_a_out, conv_b_w,
                                w_b_out, w_o, g_ffn, w_up, conv_f_w, conv_f_b, w_down, g_ple, w_ple,
                                w_ple_gate, g_final)))
    mom = dict(zip(WEIGHT_NAMES, (m_g_mix, m_w_in, m_b_gate, m_conv_a_w, m_conv_a_b, m_ln_a_g, m_ln_a_b,
                                  m_w_a_out, m_conv_b_w, m_w_b_out, m_w_o, m_g_ffn, m_w_up, m_conv_f_w,
                                  m_conv_f_b, m_w_down, m_g_ple, m_w_ple, m_w_ple_gate, m_g_final)))
    var = dict(zip(WEIGHT_NAMES, (v_g_mix, v_w_in, v_b_gate, v_conv_a_w, v_conv_a_b, v_ln_a_g, v_ln_a_b,
                                  v_w_a_out, v_conv_b_w, v_w_b_out, v_w_o, v_g_ffn, v_w_up, v_conv_f_w,
                                  v_conv_f_b, v_w_down, v_g_ple, v_w_ple, v_w_ple_gate, v_g_final)))
    depth = g_mix.shape[0]
    dc = ln_a_g.shape[1]
    me = _my_index()
    x0 = x[0]
    target = loss_target[0]
    big_names = tuple(BIG_AXIS)

    shard = {name: (jnp.swapaxes(w[name], 1, 2) if name in TRANSPOSED else w[name]).astype(BF16)
             for name in big_names}

    def gather_of(layer, *names):
        return _Gather([(shard[name], layer, BIG_AXIS[name]) for name in names])

    def row(name, layer):
        return _Layer(w[name], layer)

    first = _Gather([(shard['w_in'], 0, BIG_AXIS['w_in'])] + [(w[name][None], None, 0) for name in CONV_SHARDED])
    gathered = _run_exchange(first, "gather_first")
    w_in_full = gathered[0]
    conv_full = {name: jnp.transpose(g, (1, 2, 0, 3)).reshape(g.shape[1], g.shape[2], -1)
                 for name, g in zip(CONV_SHARDED, gathered[1:])}
    saved = []
    xc = x0
    below = None
    for l in range(depth):
        carry = gather_of(l, 'w_a_out', 'w_b_out', 'w_o', 'w_up')
        outs = _norm_matmul(xc, row('g_mix', l), w_in_full, f"fwd_in_{l}", carry, below)
        if below is not None:
            xc = outs[0]
        h, z = outs[-2:]
        wa_full, wb_full, wo_full, w_up_full = carry.results
        a_conv, a_act, s, cb = _fwd_branch(z, _Layer(conv_full['conv_a_w'], l), row('conv_a_b', l), row('ln_a_g', l),
                                       row('ln_a_b', l), _Layer(conv_full['conv_b_w'], l), dc, f"fwd_branch_{l}")
        x1 = _fwd_merge(xc, z, row('b_gate', l), a_act, s, wa_full, wb_full, wo_full, f"fwd_merge_{l}")
        carry = _Gather([(shard[name], l, BIG_AXIS[name]) for name in ('w_down', 'w_ple', 'w_ple_gate')]
                        + ([(shard['w_in'], l + 1, BIG_AXIS['w_in'])] if l + 1 < depth else []))
        h2, u = _norm_matmul(x1, row('g_ffn', l), w_up_full, f"fwd_up_{l}", carry)
        w_down_full, w_ple_full, w_pg_full = carry.results[:3]
        x2, act, gl, dgl = _fwd_down(x1, u, _Layer(conv_full['conv_f_w'], l), row('conv_f_b', l), w_down_full,
                                     f"fwd_down_{l}")
        saved.append((xc, h, z, a_conv, a_act, s, cb, x1, h2, u, act, gl, dgl, x2,
                      dict(w_in=w_in_full, w_a_out=wa_full, w_b_out=wb_full, w_o=wo_full, w_up=w_up_full,
                           w_down=w_down_full, w_ple=w_ple_full, w_ple_gate=w_pg_full)))
        if l + 1 < depth:
            w_in_full = carry.results[3]
            xc, below = x2, (row('g_ple', l), w_pg_full, _LayerTokens(p, l), w_ple_full)

    dx, dg_final, loss_part = _fwd_ple_loss(x2, row('g_ple', depth - 1), w_pg_full, _LayerTokens(p, depth - 1),
                                            w_ple_full, g_final[None], target, "fwd_ple_loss")
    above = None
    landed = {name: [None] * depth for name in big_names}
    small = {name: [None] * depth for name in WEIGHT_NAMES if name not in BIG_AXIS and name != 'g_final'}

    def scatter_of(*partials):
        ex = _Scatter([(part, BIG_AXIS[name]) for name, _, part in partials])
        ex.places = [(name, layer) for name, layer, _ in partials]
        return ex

    def keep(ex):
        for (name, layer), land in zip(ex.places, ex.results):
            landed[name][layer] = land

    pending = []
    for l in reversed(range(depth)):
        xin, h, z, a_conv, a_act, s, cb, x1, h2, u, act, gl, dgl, x2, full = saved[l]
        outs = _bwd_ple(dx, x2, row('g_ple', l), full['w_ple_gate'], _LayerTokens(p, l), full['w_ple'],
                        f"bwd_ple_{l}", above)
        dx2, d_wpg, d_wple, small['g_ple'][l] = outs[:4]
        if above is not None:
            small['g_mix'][l + 1] = outs[4]
        d_wdown = _wgrad_tn(act, dx2, f"wgrad_down_{l}")
        carry = scatter_of(('w_ple_gate', l, d_wpg), ('w_ple', l, d_wple), ('w_down', l, d_wdown))
        du, down_small = _bwd_down(
            dx2, u, gl, dgl, _Layer(conv_full['conv_f_w'], l), full['w_down'], f"bwd_down_{l}", carry)
        keep(carry)
        for name, (row0, rows) in DOWN_SMALL_ROWS.items():
            small[name][l] = (down_small, row0, rows)
        carry = scatter_of(*pending) if pending else None
        pending = []
        dx1, small['g_ffn'][l] = _bwd_norm_matmul(du, full['w_up'], x1, row('g_ffn', l), dx2, f"bwd_up_{l}", carry)
        if carry is not None:
            keep(carry)
        d_wup = _wgrad_tn(du, h2, f"wgrad_up_{l}")
        dact, ds, dgate, d_wo, d_wa, d_wb, small['b_gate'][l] = _bwd_merge(
            dx1, z, row('b_gate', l), a_act, s, full['w_a_out'], full['w_b_out'], full['w_o'], f"bwd_merge_{l}")
        carry = scatter_of(('w_up', l, d_wup))
        dz, branch_small = _bwd_branch(
            dact, ds, z, dgate, a_conv, cb, _Layer(conv_full['conv_a_w'], l), row('ln_a_g', l), row('ln_a_b', l),
            _Layer(conv_full['conv_b_w'], l), f"bwd_branch_{l}", carry)
        keep(carry)
        for name, (row0, rows) in BRANCH_SMALL_ROWS.items():
            small[name][l] = (branch_small, row0, rows)
        carry = scatter_of(('w_o', l, d_wo), ('w_a_out', l, d_wa), ('w_b_out', l, d_wb))
        d_win = _wgrad_tn(dz, h, f"wgrad_in_{l}", carry)
        keep(carry)
        if l > 0:
            pending = [('w_in', l, d_win)]
            above, dx = (dz, full['w_in'], xin, row('g_mix', l), dx1), None
        else:
            carry = scatter_of(('w_in', l, d_win))
            dx, small['g_mix'][l] = _bwd_norm_matmul(dz, full['w_in'], xin, row('g_mix', l), dx1, f"bwd_in_{l}",
                                                     carry)
            keep(carry)
    grad_x = dx[None]

    small_names = tuple(small)
    parts, plan = [], []
    for group in [small[name] for name in small_names] + [[dg_final], [loss_part]]:
        entries = []
        for item in group:
            part, row0, rows = item if isinstance(item, tuple) else (item, 0, item.shape[0])
            if not any(part is seen for seen in parts):
                parts.append(part)
            entries.append((next(k for k, seen in enumerate(parts) if seen is part), row0, rows))
        plan.append(entries)
    reduced = _all_reduce(parts, plan, "all_reduce_small")
    loss = reduced[-1][0, 0]
    grads = dict(zip(small_names, reduced[:len(small_names)]))
    grads['g_final'] = reduced[len(small_names)].reshape(g_final.shape)
    for name in CONV_SHARDED:
        _, taps, width = w[name].shape
        grads[name] = lax.dynamic_slice(grads[name], (0, 0, me * width), (depth, taps, width))

    delta, new_m, new_v = {}, {}, {}
    for name in big_names:
        view = (lambda a: jnp.swapaxes(a, 1, 2)) if name in TRANSPOSED else (lambda a: a)
        shape = view(w[name]).shape
        flat = lambda a: view(a).reshape(-1, shape[-1])
        lands = [land.reshape(N_DEV, -1, shape[-1]) for land in landed[name]]
        outs = _sum_adamw(lands, flat(w[name]), flat(mom[name]), flat(var[name]), f"adamw_{name}")
        grads[name], delta[name], new_m[name], new_v[name] = [view(a.reshape(shape)) for a in outs]
    rest = tuple(name for name in WEIGHT_NAMES if name not in BIG_AXIS)
    as_2d = lambda a: a.reshape(1, -1) if a.ndim == 1 else a
    outs = _adamw_small(*[[as_2d(src[name]) for name in rest] for src in (w, grads, mom, var)], "adamw_small")
    for dst, values in zip((delta, new_m, new_v), outs):
        dst.update({name: value.reshape(w[name].shape) for name, value in zip(rest, values)})

    return (loss, grad_x, *[grads[n] for n in WEIGHT_NAMES], *[delta[n] for n in WEIGHT_NAMES],
            *[new_m[n] for n in WEIGHT_NAMES], *[new_v[n] for n in WEIGHT_NAMES])
```

```python
import jax
import jax.numpy as jnp
from jax import lax
from jax.experimental import pallas as pl
from jax.experimental.pallas import tpu as pltpu

F32 = jnp.float32
BF16 = jnp.bfloat16
MESH = pl.DeviceIdType.MESH

N_DEV = 8
NORM_EPS = 1e-6
HALO = 32
LANES = 128
SUBLANES = 8
VMEM_LIMIT_BYTES = 56 * 2**20

ADAM_LR = 0.001
ADAM_B1 = 0.9
ADAM_B2 = 0.999
ADAM_EPS = 1e-08
ADAM_WD = 0.01
ADAM_STEP = 10

WEIGHT_NAMES = ('g_mix', 'w_in', 'b_gate', 'conv_a_w', 'conv_a_b', 'ln_a_g', 'ln_a_b', 'w_a_out',
                'conv_b_w', 'w_b_out', 'w_o', 'g_ffn', 'w_up', 'conv_f_w', 'conv_f_b', 'w_down',
                'g_ple', 'w_ple', 'w_ple_gate', 'g_final')
BIG_AXIS = {'w_in': 0, 'w_up': 0, 'w_a_out': 1, 'w_b_out': 1, 'w_o': 0, 'w_down': 0, 'w_ple': 1,
            'w_ple_gate': 0}
TRANSPOSED = ('w_in', 'w_up')
CONV_SHARDED = ('conv_a_w', 'conv_b_w', 'conv_f_w')


def _dot(a, b):
    return jnp.dot(a, b, preferred_element_type=F32)


def _dot_nt(a, b):
    return lax.dot_general(a, b, (((1,), (1,)), ((), ())), preferred_element_type=F32)


def _dot_tn(a, b):
    return lax.dot_general(a, b, (((0,), (0,)), ((), ())), preferred_element_type=F32)


def _sigmoid(v):
    return jax.nn.sigmoid(v)


def _token_tile(t, cap=512):
    return cap if (t % cap == 0 and t > 512) else 128


def _chunk(n, limit=512):
    for c in range(limit - limit % LANES, 0, -LANES):
        if n % c == 0:
            return c
    return n


def _row_tile(rows):
    for c in (512, 256, 128, 64, 32, 16, 8):
        if rows % c == 0:
            return c
    return rows


DEEP = 3


def _rows(tm, width):
    return pl.BlockSpec((tm, width), lambda i: (i, 0))


def _rows_rev(tm, width, nt):
    return pl.BlockSpec((tm, width), lambda i: (nt - 1 - i, 0))


def _whole(shape):
    nd = len(shape)
    return pl.BlockSpec(tuple(shape), lambda i: (0,) * nd)


def _resident(shape):
    nd = len(shape)
    return pl.BlockSpec(tuple(shape), lambda i: (0,) * nd, pipeline_mode=pl.Buffered(1))


def _sds(shape, dtype):
    return jax.ShapeDtypeStruct(tuple(shape), dtype)


def _rms_stats(xv):
    r = lax.rsqrt(jnp.mean(xv * xv, axis=-1, keepdims=True) + NORM_EPS)
    return xv * r, r


def _rms_bwd(dy, xh, r, g):
    dxh = dy * g
    dx = r * (dxh - xh * jnp.mean(dxh * xh, axis=-1, keepdims=True))
    return dx, jnp.sum(dy * xh, axis=0, keepdims=True)


GELU_C0 = 0.7978845608028654
GELU_C1 = GELU_C0 * 0.044715


def _gelu_tanh(v):
    v2 = v * v
    t = jnp.tanh(v * (GELU_C0 + GELU_C1 * v2))
    q = 1.0 + t
    hv = 0.5 * v
    grad = 0.5 * q + hv * (1.0 - t * t) * (GELU_C0 + (3.0 * GELU_C1) * v2)
    return hv * q, grad


ROW_CHUNK = 32
LANE_CHUNK = 512
MXU_COLUMNS = 256


def _residues(taps):
    return [0] + sorted({off % SUBLANES for _, off in taps} - {0})


def _fill_rotations(rot_ref, residues, length):
    for plane, r in enumerate(residues):
        if r:
            rot_ref[plane, 0:length, :] = rot_ref[0, pl.ds(r, length), :]


def _broadcast_rows(dst_ref, src_ref, count):
    for k in range(count):
        dst_ref[k] = jnp.broadcast_to(src_ref[k:k + 1, :], dst_ref.shape[1:])


def _lane_chunks(width, chunk=LANE_CHUNK):
    return [(c0, min(chunk, width - c0)) for c0 in range(0, width, chunk)]


def _tap_conv(rot_ref, taps, wb_ref, out_ref, tm, bias_plane=None):
    residues = _residues(taps)
    plane = {r: p for p, r in enumerate(residues)}
    blocks = ROW_CHUNK // SUBLANES
    width = out_ref.shape[1]

    def chunk(c, state):
        r0 = c * ROW_CHUNK
        for c0, cw in _lane_chunks(width):
            accs = [None if bias_plane is None else wb_ref[bias_plane, :, c0:c0 + cw]] * blocks
            for k, off in taps:
                wk = wb_ref[k, :, c0:c0 + cw]
                base = off - off % SUBLANES
                for j in range(blocks):
                    at = pl.multiple_of(r0 + base + SUBLANES * j, SUBLANES)
                    term = wk * rot_ref[plane[off % SUBLANES], pl.ds(at, SUBLANES), c0:c0 + cw]
                    accs[j] = term if accs[j] is None else accs[j] + term
            for j in range(blocks):
                at = pl.multiple_of(r0 + SUBLANES * j, SUBLANES)
                out_ref[pl.ds(at, SUBLANES), c0:c0 + cw] = accs[j]
        return state

    lax.fori_loop(0, tm // ROW_CHUNK, chunk, 0)


def _tap_wgrad(rot_ref, taps, x_ref, acc_ref, tm):
    residues = _residues(taps)
    plane = {r: p for p, r in enumerate(residues)}
    blocks = ROW_CHUNK // SUBLANES
    width = acc_ref.shape[2]

    def chunk(c, state):
        r0 = c * ROW_CHUNK
        for c0, cw in _lane_chunks(width):
            xs = [x_ref[pl.ds(pl.multiple_of(r0 + SUBLANES * j, SUBLANES), SUBLANES), c0:c0 + cw]
                  for j in range(blocks)]
            for k, off in taps:
                base = off - off % SUBLANES
                part = None
                for j in range(blocks):
                    at = pl.multiple_of(r0 + base + SUBLANES * j, SUBLANES)
                    term = xs[j] * rot_ref[plane[off % SUBLANES], pl.ds(at, SUBLANES), c0:c0 + cw]
                    part = term if part is None else part + term
                acc_ref[k, :, c0:c0 + cw] += part
        return state

    lax.fori_loop(0, tm // ROW_CHUNK, chunk, 0)


def _fwd_taps(width):
    return [(k, HALO - (width - 1) + k) for k in range(width)]


def _bwd_taps(width):
    return [(k, width - 1 - k) for k in range(width)]


def _my_index():
    return 4 * lax.axis_index("x") + 2 * lax.axis_index("y") + lax.axis_index("c")


def _mesh_id(idx):
    return (idx // 4, (idx // 2) % 2, idx % 2)


def _slab(ref, axis, idx, width):
    at = [slice(None)] * len(ref.shape)
    at[axis] = pl.ds(pl.multiple_of(idx * width, width), width)
    return ref.at[tuple(at)]


class _Exchange:
    def __init__(self, inputs, out_shape):
        n = len(inputs)
        self.inputs = list(inputs)
        self.out_shape = list(out_shape)
        self.sems = [pltpu.SemaphoreType.DMA((n, N_DEV - 1)), pltpu.SemaphoreType.DMA((n, N_DEV - 1)),
                     pltpu.SemaphoreType.DMA((n,))]
        self.results = None

    def _local(self, ins, outs, k, me):
        raise NotImplementedError

    def _remote(self, ins, outs, k, me, sender, receiver):
        raise NotImplementedError

    def start(self, ins, outs, sems):
        send_sems, recv_sems, local_sems = sems
        me = _my_index()
        for k in range(len(self.inputs)):
            src, dst = self._local(ins, outs, k, me)
            pltpu.make_async_copy(src, dst, local_sems.at[k]).start()
            for dist in range(1, N_DEV):
                peer = (me + dist) % N_DEV
                src, dst = self._remote(ins, outs, k, me, me, peer)
                pltpu.make_async_remote_copy(
                    src_ref=src, dst_ref=dst, send_sem=send_sems.at[k, dist - 1],
                    recv_sem=recv_sems.at[k, dist - 1], device_id=_mesh_id(peer), device_id_type=MESH).start()

    def wait(self, ins, outs, sems):
        send_sems, recv_sems, local_sems = sems
        me = _my_index()
        for k in range(len(self.inputs)):
            for dist in range(1, N_DEV):
                sender = (me + N_DEV - dist) % N_DEV
                src, dst = self._remote(ins, outs, k, me, sender, me)
                cp = pltpu.make_async_remote_copy(
                    src_ref=src, dst_ref=dst, send_sem=send_sems.at[k, dist - 1],
                    recv_sem=recv_sems.at[k, dist - 1], device_id=_mesh_id(sender), device_id_type=MESH)
                cp.wait_send()
                cp.wait_recv()
            src, dst = self._local(ins, outs, k, me)
            pltpu.make_async_copy(src, dst, local_sems.at[k]).wait()

    def forward(self, ins, outs, sems):
        pass


class _Gather(_Exchange):
    FLIPS = ((1, 0), (0, 1), (1, 1))

    def __init__(self, items):
        self.items = list(items)
        out_shape = []
        for shards, layer, axis in self.items:
            shape = list(shards.shape if layer is None else shards.shape[1:])
            shape[axis] *= N_DEV
            out_shape.append(_sds(shape, shards.dtype))
        super().__init__([it[0] for it in self.items], out_shape)

    def _src(self, ins, k):
        layer = self.items[k][1]
        return ins[k] if layer is None else ins[k].at[layer]

    def _place(self, outs, k, idx):
        axis = self.items[k][2]
        return _slab(outs[k], axis, idx, self.out_shape[k].shape[axis] // N_DEV)

    def _copy(self, sems, k, j, src, dst, to):
        return pltpu.make_async_remote_copy(src_ref=src, dst_ref=dst, send_sem=sems[0].at[k, j],
                                            recv_sem=sems[1].at[k, j], device_id=to, device_id_type=MESH)

    @staticmethod
    def _places():
        x, y, c = lax.axis_index("x"), lax.axis_index("y"), lax.axis_index("c")
        chips = [(1 - x if fx else x, 1 - y if fy else y) for fx, fy in _Gather.FLIPS]
        return (x, y, c), (x, y, 1 - c), chips

    @staticmethod
    def _index(place):
        return 4 * place[0] + 2 * place[1] + place[2]

    def start(self, ins, outs, sems):
        me, sibling, chips = self._places()
        for k in range(len(self.inputs)):
            src, mine = self._src(ins, k), self._place(outs, k, self._index(me))
            pltpu.make_async_copy(src, mine, sems[2].at[k]).start()
            self._copy(sems, k, 0, src, mine, sibling).start()
            for j, chip in enumerate(chips):
                self._copy(sems, k, 1 + j, src, mine, (*chip, me[2])).start()

    def forward(self, ins, outs, sems):
        me, sibling, chips = self._places()
        for k in range(len(self.inputs)):
            for j, chip in enumerate(chips):
                got = self._place(outs, k, self._index((*chip, me[2])))
                self._copy(sems, k, 1 + j, got, got, (*chip, me[2])).wait_recv()
                self._copy(sems, k, 4 + j, got, got, sibling).start()

    def wait(self, ins, outs, sems):
        me, sibling, chips = self._places()
        for k in range(len(self.inputs)):
            src, mine = self._src(ins, k), self._place(outs, k, self._index(me))
            self._copy(sems, k, 0, src, self._place(outs, k, self._index(sibling)), sibling).wait_recv()
            for j, chip in enumerate(chips):
                got = self._place(outs, k, self._index((*chip, sibling[2])))
                self._copy(sems, k, 4 + j, got, got, sibling).wait_recv()
            for j in range(N_DEV - 1):
                self._copy(sems, k, j, src, mine, sibling).wait_send()
            pltpu.make_async_copy(src, mine, sems[2].at[k]).wait()


class _Scatter(_Exchange):
    def __init__(self, items):
        self.items = list(items)
        out_shape = []
        for partial, axis in self.items:
            shape = list(partial.shape)
            shape[axis] //= N_DEV
            out_shape.append(_sds([N_DEV] + shape, partial.dtype))
        super().__init__([it[0] for it in self.items], out_shape)

    def _take(self, ins, k, idx):
        axis = self.items[k][1]
        return _slab(ins[k], axis, idx, self.items[k][0].shape[axis] // N_DEV)

    def _local(self, ins, outs, k, me):
        return self._take(ins, k, me), outs[k].at[me]

    def _remote(self, ins, outs, k, me, sender, receiver):
        return self._take(ins, k, receiver), outs[k].at[sender]


def _run_exchange(exchange, name):
    n = len(exchange.inputs)

    def body(*refs):
        ins, outs, sems = refs[:n], refs[n:2 * n], refs[2 * n:]
        exchange.start(ins, outs, sems)
        exchange.forward(ins, outs, sems)
        exchange.wait(ins, outs, sems)

    any_spec = pl.BlockSpec(memory_space=pl.ANY)
    exchange.results = pl.pallas_call(
        body, name=name, in_specs=[any_spec] * n, out_specs=[any_spec] * n, out_shape=exchange.out_shape,
        scratch_shapes=exchange.sems)(*exchange.inputs)
    return exchange.results


class _Layer:
    def __init__(self, stack, index):
        self.stack, self.index = stack, index
        self.shape = (1,) + stack.shape[1:] if stack.ndim == 2 else stack.shape[1:]

    def view(self, ref):
        return ref.at[pl.ds(self.index, 1)] if self.stack.ndim == 2 else ref.at[self.index]


class _LayerTokens:
    def __init__(self, stack, index):
        self.stack, self.index = stack, index

    def rows(self, tm):
        return pl.BlockSpec((None, None, tm, self.stack.shape[-1]), lambda i: (self.index, 0, i, 0))


def _call(body, *, name, grid, in_specs, out_specs, out_shape, args, scratch=(), carry=None):
    in_specs, out_specs, out_shape, scratch = list(in_specs), list(out_specs), list(out_shape), list(scratch)
    args = [a.stack if isinstance(a, _LayerTokens) else a for a in args]
    layers = {k: a for k, a in enumerate(args) if isinstance(a, _Layer)}
    for k, a in layers.items():
        in_specs[k], args[k] = _whole(a.stack.shape), a.stack
    params = pltpu.CompilerParams(dimension_semantics=("arbitrary",) * len(grid),
                                  vmem_limit_bytes=VMEM_LIMIT_BYTES)
    n_in, n_out, n_scr = len(in_specs), len(out_specs), len(scratch)
    n_x = 0 if carry is None else len(carry.inputs)
    steps = 1
    for extent in grid:
        steps *= extent
    assert carry is None or steps >= 3, "a carrier needs a step each for start, second stage and wait"

    def whole_body(*refs):
        core_in, x_in = list(refs[:n_in]), refs[n_in:n_in + n_x]
        refs = refs[n_in + n_x:]
        core_out, x_out = refs[:n_out], refs[n_out:n_out + n_x]
        refs = refs[n_out + n_x:]
        core_scr, sems = refs[:n_scr], refs[n_scr:]
        for k, a in layers.items():
            core_in[k] = a.view(core_in[k])
        if carry is None:
            body(*core_in, *core_out, *core_scr)
            return
        step = pl.program_id(0)
        for axis in range(1, len(grid)):
            step = step * grid[axis] + pl.program_id(axis)

        @pl.when(step == 0)
        def _():
            carry.start(x_in, x_out, sems)

        @pl.when(step == steps - 2)
        def _():
            carry.forward(x_in, x_out, sems)

        body(*core_in, *core_out, *core_scr)

        @pl.when(step == steps - 1)
        def _():
            carry.wait(x_in, x_out, sems)

    any_spec = pl.BlockSpec(memory_space=pl.ANY)
    extra_in = [] if carry is None else carry.inputs
    extra_shape = [] if carry is None else carry.out_shape
    extra_sems = [] if carry is None else carry.sems
    outs = pl.pallas_call(
        whole_body, name=name, grid=grid, in_specs=in_specs + [any_spec] * n_x,
        out_specs=out_specs + [any_spec] * n_x, out_shape=out_shape + extra_shape,
        scratch_shapes=scratch + extra_sems, compiler_params=params)(*args, *extra_in)
    if carry is not None:
        carry.results = outs[n_out:]
    return outs[:n_out]


def _all_reduce(parts, plan, name):
    n = len(parts)
    out_shape = []
    for group in plan:
        shape = parts[group[0]].shape
        if len(group) > 1:
            shape = (len(group),) + (shape[1:] if shape[0] == 1 else shape)
        out_shape.append(_sds(shape, F32))

    def body(*refs):
        ins, outs, lands = refs[:n], refs[n:n + len(plan)], refs[n + len(plan):2 * n + len(plan)]
        send_sems, recv_sems = refs[2 * n + len(plan):]
        me = _my_index()
        for k in range(n):
            lands[k][me] = ins[k][...]
            for dist in range(1, N_DEV):
                pltpu.make_async_remote_copy(
                    src_ref=ins[k], dst_ref=lands[k].at[me],
                    send_sem=send_sems.at[k, dist - 1], recv_sem=recv_sems.at[k, dist - 1],
                    device_id=_mesh_id((me + dist) % N_DEV), device_id_type=MESH).start()
        for k in range(n):
            for dist in range(1, N_DEV):
                sender = (me + N_DEV - dist) % N_DEV
                cp = pltpu.make_async_remote_copy(
                    src_ref=ins[k], dst_ref=lands[k].at[sender],
                    send_sem=send_sems.at[k, dist - 1], recv_sem=recv_sems.at[k, dist - 1],
                    device_id=_mesh_id(sender), device_id_type=MESH)
                cp.wait_send()
                cp.wait_recv()
        for o_ref, group in zip(outs, plan):
            for j, k in enumerate(group):
                total = lands[k][0]
                for dev in range(1, N_DEV):
                    total = total + lands[k][dev]
                if len(group) == 1:
                    o_ref[...] = total
                elif parts[k].shape[0] == 1:
                    o_ref[j:j + 1, :] = total
                else:
                    o_ref[j] = total

    vmem = pl.BlockSpec(memory_space=pltpu.VMEM)
    return pl.pallas_call(
        body, name=name, in_specs=[vmem] * n, out_specs=[vmem] * len(plan), out_shape=out_shape,
        scratch_shapes=[pltpu.VMEM((N_DEV,) + part.shape, F32) for part in parts]
        + [pltpu.SemaphoreType.DMA((n, N_DEV - 1)), pltpu.SemaphoreType.DMA((n, N_DEV - 1))])(*parts)


def _ple_out(xv, g_ref, wpg_ref, p_ref, wple_ref):
    xh, _ = _rms_stats(xv)
    lg = _dot((xh * g_ref[...]).astype(BF16), wpg_ref[...])
    return xv + _sigmoid(lg) * _dot(p_ref[...].astype(BF16), wple_ref[...])


def _norm_matmul(x, g, wt, name, carry=None, ple=None):
    t, d = x.shape
    n = wt.shape[0]
    tm, nc = _token_tile(t), _chunk(n)

    def body(x_ref, g_ref, wt_ref, *rest):
        xv = x_ref[...]
        if ple is not None:
            xv = _ple_out(xv, *rest[:4])
            rest[4][...] = xv
        h_ref, o_ref = rest[-2:]
        xh, _ = _rms_stats(xv)
        h = (xh * g_ref[...]).astype(BF16)
        h_ref[...] = h
        for n0 in range(0, n, nc):
            o_ref[:, n0:n0 + nc] = _dot_nt(h, wt_ref[n0:n0 + nc, :]).astype(BF16)

    in_specs = [_rows(tm, d), _whole(g.shape), _resident(wt.shape)]
    out_specs, out_shape, args = [_rows(tm, d), _rows(tm, n)], [_sds((t, d), BF16), _sds((t, n), BF16)], (x, g, wt)
    if ple is not None:
        g_ple, wpg, p, wple = ple
        in_specs += [_whole(g_ple.shape), _resident(wpg.shape), p.rows(tm), _resident(wple.shape)]
        out_specs, out_shape, args = [_rows(tm, d)] + out_specs, [_sds((t, d), F32)] + out_shape, args + ple
    return _call(body, name=name, grid=(t // tm,), in_specs=in_specs, out_specs=out_specs, out_shape=out_shape,
                 args=args, carry=carry)


def _fwd_branch(z, caw, cab, lng, lnb, cbw, dc, name, carry=None):
    t = z.shape[0]
    tm = _token_tile(t)
    ka, kb = caw.shape[0], cbw.shape[0]
    taps_a, taps_b = _fwd_taps(ka), _fwd_taps(kb)
    res_a, res_b = _residues(taps_a), _residues(taps_b)
    span = HALO + tm - SUBLANES

    def body(z_ref, caw_ref, cab_ref, lng_ref, lnb_ref, cbw_ref, ac_ref, act_ref, s_ref, cb_ref,
             rot_a, rot_b, wb_a, wb_b, cb):
        @pl.when(pl.program_id(0) == 0)
        def _():
            rot_a[0, 0:HALO, :] = jnp.zeros((HALO, dc), F32)
            rot_b[0, 0:HALO, :] = jnp.zeros((HALO, dc), F32)
            _broadcast_rows(wb_a, caw_ref, ka)
            wb_a[ka] = jnp.broadcast_to(cab_ref[...], (SUBLANES, dc))
            _broadcast_rows(wb_b, cbw_ref, kb)

        a_val = z_ref[:, 0:dc].astype(F32)
        a_gt = z_ref[:, dc:2 * dc].astype(F32)
        rot_a[0, HALO:HALO + tm, :] = a_val * _sigmoid(a_gt)
        _fill_rotations(rot_a, res_a, span)
        _tap_conv(rot_a, taps_a, wb_a, ac_ref, tm, bias_plane=ka)
        ac = ac_ref[...]
        mu = jnp.mean(ac, axis=-1, keepdims=True)
        xc = ac - mu
        var = jnp.mean(xc * xc, axis=-1, keepdims=True)
        ln = xc * lax.rsqrt(var + NORM_EPS) * lng_ref[...] + lnb_ref[...]
        act_ref[...] = (ln * _sigmoid(ln)).astype(BF16)
        rot_a[0, 0:HALO, :] = rot_a[0, tm:tm + HALO, :]

        sc_c = z_ref[:, 3 * dc:4 * dc].astype(F32)
        sc_v = z_ref[:, 4 * dc:5 * dc].astype(F32)
        rot_b[0, HALO:HALO + tm, :] = sc_c * sc_v
        _fill_rotations(rot_b, res_b, span)
        _tap_conv(rot_b, taps_b, wb_b, cb, tm)
        s_ref[...] = (z_ref[:, 2 * dc:3 * dc].astype(F32) * cb[...]).astype(BF16)
        cb_ref[...] = cb[...].astype(BF16)
        rot_b[0, 0:HALO, :] = rot_b[0, tm:tm + HALO, :]

    return _call(body, name=name, grid=(t // tm,),
                 in_specs=[_rows(tm, 5 * dc), _whole(caw.shape), _whole(cab.shape), _whole(lng.shape),
                           _whole(lnb.shape), _whole(cbw.shape)],
                 out_specs=[_rows(tm, dc), _rows(tm, dc), _rows(tm, dc), _rows(tm, dc)],
                 out_shape=[_sds((t, dc), F32), _sds((t, dc), BF16), _sds((t, dc), BF16), _sds((t, dc), BF16)],
                 scratch=[pltpu.VMEM((len(res_a), HALO + tm, dc), F32), pltpu.VMEM((len(res_b), HALO + tm, dc), F32),
                          pltpu.VMEM((ka + 1, SUBLANES, dc), F32), pltpu.VMEM((kb, SUBLANES, dc), F32),
                          pltpu.VMEM((tm, dc), F32)],
                 args=(z, caw, cab, lng, lnb, cbw), carry=carry)


def _fwd_merge(x, z, bg, a_act, s, wa, wb, wo, name, carry=None):
    t, d = x.shape
    n = z.shape[1]
    dc = a_act.shape[1]
    tm = _token_tile(t)
    o5 = n - 2 * d

    def body(x_ref, z_ref, bg_ref, act_ref, s_ref, wa_ref, wb_ref, wo_ref, o_ref):
        ya = _dot(act_ref[...], wa_ref[...])
        yb = _dot(s_ref[...], wb_ref[...])
        ga = _sigmoid(z_ref[:, o5:o5 + d].astype(F32) + bg_ref[:, 0:d])
        gb = _sigmoid(z_ref[:, o5 + d:n].astype(F32) + bg_ref[:, d:2 * d])
        m = (ga * ya + gb * yb).astype(BF16)
        o_ref[...] = x_ref[...] + _dot(m, wo_ref[...])

    return _call(body, name=name, grid=(t // tm,),
                 in_specs=[_rows(tm, d), _rows(tm, n), _whole(bg.shape), _rows(tm, dc), _rows(tm, dc),
                           _resident(wa.shape), _resident(wb.shape), _resident(wo.shape)],
                 out_specs=[_rows(tm, d)], out_shape=[_sds((t, d), F32)],
                 args=(x, z, bg, a_act, s, wa, wb, wo), carry=carry)[0]


def _fwd_down(x, u, cfw, cfb, wd, name, carry=None):
    t, d = x.shape
    f = u.shape[1] // 2
    tm = _token_tile(t, 256)
    kf = cfw.shape[0]

    taps = _fwd_taps(kf)
    residues = _residues(taps)

    def body(x_ref, u_ref, cfw_ref, cfb_ref, wd_ref, o_ref, act_ref, gl_ref, dgl_ref, rot_u, wb, fg):
        @pl.when(pl.program_id(0) == 0)
        def _():
            rot_u[0, 0:HALO, :] = jnp.zeros((HALO, f), F32)
            _broadcast_rows(wb, cfw_ref, kf)
            wb[kf] = jnp.broadcast_to(cfb_ref[...], (SUBLANES, f))

        rot_u[0, HALO:HALO + tm, :] = u_ref[:, 0:f].astype(F32)
        _fill_rotations(rot_u, residues, HALO + tm - SUBLANES)
        _tap_conv(rot_u, taps, wb, fg, tm, bias_plane=kf)
        y = x_ref[...]
        for c0, cw in _lane_chunks(f, MXU_COLUMNS):
            at = slice(c0, c0 + cw)
            gl, dgl = _gelu_tanh(fg[:, at])
            gl_ref[:, at] = gl.astype(BF16)
            dgl_ref[:, at] = dgl.astype(BF16)
            act = (gl * u_ref[:, f + c0:f + c0 + cw].astype(F32)).astype(BF16)
            act_ref[:, at] = act
            y = y + _dot(act, wd_ref[at, :])
        o_ref[...] = y
        rot_u[0, 0:HALO, :] = rot_u[0, tm:tm + HALO, :]

    return _call(body, name=name, grid=(t // tm,),
                 in_specs=[_rows(tm, d), _rows(tm, 2 * f), _whole(cfw.shape), _whole(cfb.shape),
                           _resident(wd.shape)],
                 out_specs=[_rows(tm, d), _rows(tm, f), _rows(tm, f), _rows(tm, f)],
                 out_shape=[_sds((t, d), F32), _sds((t, f), BF16), _sds((t, f), BF16), _sds((t, f), BF16)],
                 scratch=[pltpu.VMEM((len(residues), HALO + tm, f), F32), pltpu.VMEM((kf + 1, SUBLANES, f), F32),
                          pltpu.VMEM((tm, f), F32)],
                 args=(x, u, cfw, cfb, wd), carry=carry)


def _fwd_ple_loss(x, g, wpg, p, wple, g_final, target, name):
    t, d = x.shape
    tm = _token_tile(t)

    def body(x_ref, g_ref, wpg_ref, p_ref, wple_ref, gf_ref, t_ref, dy_ref, dg_ref, loss_ref):
        y = _ple_out(x_ref[...], g_ref, wpg_ref, p_ref, wple_ref)

        @pl.when(pl.program_id(0) == 0)
        def _():
            dg_ref[...] = jnp.zeros_like(dg_ref)
            loss_ref[...] = jnp.zeros_like(loss_ref)

        yh, r = _rms_stats(y)
        err = yh * gf_ref[...] - t_ref[...]
        sq = jnp.sum(jnp.sum(err * err, axis=0, keepdims=True), axis=1, keepdims=True)
        loss_ref[...] += jnp.broadcast_to(0.5 * sq / d, loss_ref.shape)
        dy, dg = _rms_bwd(err / d, yh, r, gf_ref[...])
        dy_ref[...] = dy
        dg_ref[...] += dg

    return _call(body, name=name, grid=(t // tm,),
                 in_specs=[_rows(tm, d), _whole(g.shape), _resident(wpg.shape), p.rows(tm), _resident(wple.shape),
                           _whole(g_final.shape), _rows(tm, d)],
                 out_specs=[_rows(tm, d), _whole((1, d)), _whole((SUBLANES, LANES))],
                 out_shape=[_sds((t, d), F32), _sds((1, d), F32), _sds((SUBLANES, LANES), F32)],
                 args=(x, g, wpg, p, wple, g_final, target))


def _bwd_ple(dy, x, g, wpg, p, wple, name, above=None):
    t, d = x.shape
    pd = p.stack.shape[-1]
    tm = _token_tile(t)
    nt = t // tm
    n_above = 0 if above is None else len(above)

    def body(*refs):
        head, refs = refs[:max(n_above, 1)], refs[max(n_above, 1):]
        x_ref, g_ref, wpg_ref, p_ref, wple_ref, dx_ref, dwpg_ref, dwple_ref, dg_ref = refs[:9]
        dg_in_ref = refs[9] if above is not None else None
        acc_pg, acc_ple = refs[-2:]
        i = pl.program_id(0)

        @pl.when(i == 0)
        def _():
            acc_pg[...] = jnp.zeros_like(acc_pg)
            acc_ple[...] = jnp.zeros_like(acc_ple)
            dg_ref[...] = jnp.zeros_like(dg_ref)
            if above is not None:
                dg_in_ref[...] = jnp.zeros_like(dg_in_ref)

        if above is None:
            dyv = head[0][...]
        else:
            dz_ref, wt_ref, xin_ref, gin_ref, dres_ref = head
            xh_in, r_in = _rms_stats(xin_ref[...])
            dx_in, dg_in = _rms_bwd(_dot(dz_ref[...], wt_ref[...]), xh_in, r_in, gin_ref[...])
            dyv = dres_ref[...] + dx_in
            dg_in_ref[...] += dg_in
        xh, r = _rms_stats(x_ref[...])
        h = (xh * g_ref[...]).astype(BF16)
        pb = p_ref[...].astype(BF16)
        pg = _sigmoid(_dot(h, wpg_ref[...]))
        pp = _dot(pb, wple_ref[...])
        dpp = (dyv * pg).astype(BF16)
        dlg = (dyv * pp * pg * (1.0 - pg)).astype(BF16)
        acc_ple[...] += _dot_tn(pb, dpp)
        acc_pg[...] += _dot_tn(h, dlg)
        dx, dg = _rms_bwd(_dot_nt(dlg, wpg_ref[...]), xh, r, g_ref[...])
        dx_ref[...] = dyv + dx
        dg_ref[...] += dg

        @pl.when(i == nt - 1)
        def _():
            dwpg_ref[...] = acc_pg[...].astype(BF16)
            dwple_ref[...] = acc_ple[...].astype(BF16)

    if above is None:
        head_specs, head_args, more_specs, more_shape = [_rows(tm, d)], (dy,), [], []
    else:
        dz, wt, x_in, g_in, dres = above
        head_specs = [_rows(tm, dz.shape[1]), _resident(wt.shape), _rows(tm, d), _whole(g_in.shape), _rows(tm, d)]
        head_args, more_specs, more_shape = above, [_whole((1, d))], [_sds((1, d), F32)]
    return _call(body, name=name, grid=(nt,),
                 in_specs=head_specs + [_rows(tm, d), _whole(g.shape), _resident(wpg.shape), p.rows(tm),
                                        _resident(wple.shape)],
                 out_specs=[_rows(tm, d), _whole((d, d)), _whole((pd, d)), _whole((1, d))] + more_specs,
                 out_shape=[_sds((t, d), F32), _sds((d, d), BF16), _sds((pd, d), BF16), _sds((1, d), F32)]
                 + more_shape,
                 scratch=[pltpu.VMEM((d, d), F32), pltpu.VMEM((pd, d), F32)],
                 args=(*head_args, x, g, wpg, p, wple))


def _bwd_down(dy, u, gl, dgl, cfw, wd, name, carry=None):
    t, d = dy.shape
    f = u.shape[1] // 2
    tm = _token_tile(t, 256)
    nt = t // tm
    kf = cfw.shape[0]
    bwd = _bwd_taps(kf)
    residues = _residues(bwd)

    def body(dy_ref, u_hbm, gl_ref, dgl_ref, cfw_ref, wd_ref, du_ref, dcw_ref, dcb_ref,
             rot_g, wb, acc, buf, u_buf, u_sem):
        i = pl.program_id(0)

        def fetch(step):
            first = (nt - 1 - step) * tm
            rows = pl.ds(first if isinstance(step, int) else pl.multiple_of(first, tm), tm)
            return pltpu.make_async_copy(u_hbm.at[rows, :], u_buf.at[step % DEEP], u_sem.at[step % DEEP])

        @pl.when(i == 0)
        def _():
            for step in range(min(DEEP, nt)):
                fetch(step).start()
            for ref in (dcw_ref, dcb_ref, acc):
                ref[...] = jnp.zeros_like(ref)
            rot_g[0, tm:tm + HALO, :] = jnp.zeros((HALO, f), F32)
            _broadcast_rows(wb, cfw_ref, kf)

        fetch(i).wait()
        u_ref = u_buf.at[i % DEEP]
        dyb = dy_ref[...].astype(BF16)
        for c0, cw in _lane_chunks(f, MXU_COLUMNS):
            at = slice(c0, c0 + cw)
            df = _dot_nt(dyb, wd_ref[at, :])
            du_ref[:, f + c0:f + c0 + cw] = (df * gl_ref[:, at].astype(F32)).astype(BF16)
            dfg = df * u_ref[:, f + c0:f + c0 + cw].astype(F32) * dgl_ref[:, at].astype(F32)
            dcb_ref[:, at] += jnp.sum(dfg, axis=0, keepdims=True)
            rot_g[0, 0:tm, at] = dfg
        _fill_rotations(rot_g, residues, HALO + tm - SUBLANES)
        buf[...] = u_ref[:, 0:f].astype(F32)

        @pl.when(i + DEEP < nt)
        def _():
            fetch(i + DEEP).start()

        _tap_wgrad(rot_g, bwd, buf, acc, tm)
        _tap_conv(rot_g, bwd, wb, buf, tm)
        du_ref[:, 0:f] = buf[...].astype(BF16)
        rot_g[0, tm:tm + HALO, :] = rot_g[0, 0:HALO, :]

        @pl.when(i == nt - 1)
        def _():
            dcw_ref[0:kf, :] = jnp.sum(acc[...], axis=1)

    return _call(body, name=name, grid=(nt,),
                 in_specs=[_rows_rev(tm, d, nt), pl.BlockSpec(memory_space=pl.ANY), _rows_rev(tm, f, nt),
                           _rows_rev(tm, f, nt), _whole(cfw.shape), _resident(wd.shape)],
                 out_specs=[_rows_rev(tm, 2 * f, nt), _whole((SUBLANES, f)), _whole((1, f))],
                 out_shape=[_sds((t, 2 * f), BF16), _sds((SUBLANES, f), F32), _sds((1, f), F32)],
                 scratch=[pltpu.VMEM((len(residues), HALO + tm, f), F32), pltpu.VMEM((kf, SUBLANES, f), F32),
                          pltpu.VMEM((kf, SUBLANES, f), F32), pltpu.VMEM((tm, f), F32),
                          pltpu.VMEM((DEEP, tm, 2 * f), BF16), pltpu.SemaphoreType.DMA((DEEP,))],
                 args=(dy, u, gl, dgl, cfw, wd), carry=carry)


def _bwd_norm_matmul(dout, wt, x, g, dres, name, carry=None):
    t, d = x.shape
    n = dout.shape[1]
    tm = _token_tile(t)

    def body(do_ref, wt_ref, x_ref, g_ref, dres_ref, dx_ref, dg_ref):
        @pl.when(pl.program_id(0) == 0)
        def _():
            dg_ref[...] = jnp.zeros_like(dg_ref)

        dh = _dot(do_ref[...], wt_ref[...])
        xh, r = _rms_stats(x_ref[...])
        dx, dg = _rms_bwd(dh, xh, r, g_ref[...])
        dx_ref[...] = dres_ref[...] + dx
        dg_ref[...] += dg

    return _call(body, name=name, grid=(t // tm,),
                 in_specs=[_rows(tm, n), _resident(wt.shape), _rows(tm, d), _whole(g.shape), _rows(tm, d)],
                 out_specs=[_rows(tm, d), _whole((1, d))],
                 out_shape=[_sds((t, d), F32), _sds((1, d), F32)],
                 args=(dout, wt, x, g, dres), carry=carry)


def _wgrad_tn(a, b, name, carry=None):
    t, n = a.shape
    d = b.shape[1]
    tt = 2048 if t % 2048 == 0 else _token_tile(t)
    tn = _chunk(n, 1536)
    nt = t // tt

    def body(a_ref, b_ref, o_ref, acc):
        k = pl.program_id(1)

        @pl.when(k == 0)
        def _():
            acc[...] = jnp.zeros_like(acc)

        acc[...] += _dot_tn(a_ref[...].astype(BF16), b_ref[...].astype(BF16))

        @pl.when(k == nt - 1)
        def _():
            o_ref[...] = acc[...].astype(BF16)

    return _call(body, name=name, grid=(n // tn, nt),
                 in_specs=[pl.BlockSpec((tt, tn), lambda j, k: (k, j)), pl.BlockSpec((tt, d), lambda j, k: (k, 0))],
                 out_specs=[pl.BlockSpec((tn, d), lambda j, k: (j, 0))], out_shape=[_sds((n, d), BF16)],
                 scratch=[pltpu.VMEM((tn, d), F32)], args=(a, b), carry=carry)[0]


def _bwd_merge(dy, z, bg, a_act, s, wa, wb, wo, name, carry=None):
    t, d = dy.shape
    n = z.shape[1]
    dc = a_act.shape[1]
    tm = _token_tile(t)
    nt = t // tm
    o5 = n - 2 * d

    def body(dy_ref, z_ref, bg_ref, act_ref, s_ref, wa_ref, wb_ref, wo_ref,
             dact_ref, ds_ref, dgl_ref, dwo_ref, dwa_ref, dwb_ref, dbg_ref, acc_o, acc_a, acc_b):
        i = pl.program_id(0)

        @pl.when(i == 0)
        def _():
            acc_o[...] = jnp.zeros_like(acc_o)
            acc_a[...] = jnp.zeros_like(acc_a)
            acc_b[...] = jnp.zeros_like(acc_b)
            dbg_ref[...] = jnp.zeros_like(dbg_ref)

        dyb = dy_ref[...].astype(BF16)
        dm = _dot_nt(dyb, wo_ref[...])
        ya = _dot(act_ref[...], wa_ref[...])
        yb = _dot(s_ref[...], wb_ref[...])
        ga = _sigmoid(z_ref[:, o5:o5 + d].astype(F32) + bg_ref[:, 0:d])
        gb = _sigmoid(z_ref[:, o5 + d:n].astype(F32) + bg_ref[:, d:2 * d])
        acc_o[...] += _dot_tn((ga * ya + gb * yb).astype(BF16), dyb)
        dya = (dm * ga).astype(BF16)
        dyb2 = (dm * gb).astype(BF16)
        acc_a[...] += _dot_tn(act_ref[...], dya)
        acc_b[...] += _dot_tn(s_ref[...], dyb2)
        dact_ref[...] = _dot_nt(dya, wa_ref[...])
        ds_ref[...] = _dot_nt(dyb2, wb_ref[...])
        dla = dm * ya * ga * (1.0 - ga)
        dlb = dm * yb * gb * (1.0 - gb)
        dgl_ref[:, 0:d] = dla.astype(BF16)
        dgl_ref[:, d:2 * d] = dlb.astype(BF16)
        dbg_ref[:, 0:d] += jnp.sum(dla, axis=0, keepdims=True)
        dbg_ref[:, d:2 * d] += jnp.sum(dlb, axis=0, keepdims=True)

        @pl.when(i == nt - 1)
        def _():
            dwo_ref[...] = acc_o[...].astype(BF16)
            dwa_ref[...] = acc_a[...].astype(BF16)
            dwb_ref[...] = acc_b[...].astype(BF16)

    return _call(body, name=name, grid=(nt,),
                 in_specs=[_rows(tm, d), _rows(tm, n), _whole(bg.shape), _rows(tm, dc), _rows(tm, dc),
                           _resident(wa.shape), _resident(wb.shape), _resident(wo.shape)],
                 out_specs=[_rows(tm, dc), _rows(tm, dc), _rows(tm, 2 * d), _whole((d, d)), _whole((dc, d)),
                            _whole((dc, d)), _whole((1, 2 * d))],
                 out_shape=[_sds((t, dc), F32), _sds((t, dc), F32), _sds((t, 2 * d), BF16), _sds((d, d), BF16),
                            _sds((dc, d), BF16), _sds((dc, d), BF16), _sds((1, 2 * d), F32)],
                 scratch=[pltpu.VMEM((d, d), F32), pltpu.VMEM((dc, d), F32), pltpu.VMEM((dc, d), F32)],
                 args=(dy, z, bg, a_act, s, wa, wb, wo), carry=carry)


def _bwd_branch(dact, ds, z, dgl, a_conv, cb, caw, lng, lnb, cbw, name, carry=None):
    t, n = z.shape
    dc = a_conv.shape[1]
    tm = _token_tile(t)
    nt = t // tm
    ka, kb = caw.shape[0], cbw.shape[0]
    bwd_a, bwd_b = _bwd_taps(ka), _bwd_taps(kb)
    span = HALO + tm - SUBLANES

    def body(dact_ref, ds_ref, z_ref, dgl_ref, ac_ref, cb_ref, caw_ref, lng_ref, lnb_ref, cbw_ref,
             dz_ref, dcaw_ref, dcab_ref, dlng_ref, dlnb_ref, dcbw_ref,
             rot_da, rot_dc, wb_a, wb_b, acc_a, acc_b, buf):
        i = pl.program_id(0)

        @pl.when(i == 0)
        def _():
            for ref in (dcaw_ref, dcab_ref, dlng_ref, dlnb_ref, dcbw_ref, acc_a, acc_b):
                ref[...] = jnp.zeros_like(ref)
            rot_da[0, tm:tm + HALO, :] = jnp.zeros((HALO, dc), F32)
            rot_dc[0, tm:tm + HALO, :] = jnp.zeros((HALO, dc), F32)
            _broadcast_rows(wb_a, caw_ref, ka)
            _broadcast_rows(wb_b, cbw_ref, kb)

        a_val = z_ref[:, 0:dc].astype(F32)
        sg = _sigmoid(z_ref[:, dc:2 * dc].astype(F32))

        ac = ac_ref[...]
        mu = jnp.mean(ac, axis=-1, keepdims=True)
        xc = ac - mu
        rstd = lax.rsqrt(jnp.mean(xc * xc, axis=-1, keepdims=True) + NORM_EPS)
        xh = xc * rstd
        ln = xh * lng_ref[...] + lnb_ref[...]
        sl = _sigmoid(ln)
        dln = dact_ref[...] * (sl * (1.0 + ln * (1.0 - sl)))
        dlng_ref[...] += jnp.sum(dln * xh, axis=0, keepdims=True)
        dlnb_ref[...] += jnp.sum(dln, axis=0, keepdims=True)
        dxh = dln * lng_ref[...]
        dac = rstd * (dxh - jnp.mean(dxh, axis=-1, keepdims=True)
                      - xh * jnp.mean(dxh * xh, axis=-1, keepdims=True))
        dcab_ref[...] += jnp.sum(dac, axis=0, keepdims=True)
        rot_da[0, 0:tm, :] = dac
        _fill_rotations(rot_da, _residues(bwd_a), span)
        buf[...] = a_val * sg
        _tap_wgrad(rot_da, bwd_a, buf, acc_a, tm)
        _tap_conv(rot_da, bwd_a, wb_a, buf, tm)
        rot_da[0, tm:tm + HALO, :] = rot_da[0, 0:HALO, :]
        da = buf[...]
        dz_ref[:, 0:dc] = (da * sg).astype(BF16)
        dz_ref[:, dc:2 * dc] = (da * a_val * sg * (1.0 - sg)).astype(BF16)

        sc_b = z_ref[:, 2 * dc:3 * dc].astype(F32)
        sc_c = z_ref[:, 3 * dc:4 * dc].astype(F32)
        sc_v = z_ref[:, 4 * dc:5 * dc].astype(F32)
        dsv = ds_ref[...]
        dz_ref[:, 2 * dc:3 * dc] = (dsv * cb_ref[...].astype(F32)).astype(BF16)
        rot_dc[0, 0:tm, :] = dsv * sc_b
        _fill_rotations(rot_dc, _residues(bwd_b), span)
        buf[...] = sc_c * sc_v
        _tap_wgrad(rot_dc, bwd_b, buf, acc_b, tm)
        _tap_conv(rot_dc, bwd_b, wb_b, buf, tm)
        rot_dc[0, tm:tm + HALO, :] = rot_dc[0, 0:HALO, :]
        dcv = buf[...]
        dz_ref[:, 3 * dc:4 * dc] = (dcv * sc_v).astype(BF16)
        dz_ref[:, 4 * dc:5 * dc] = (dcv * sc_c).astype(BF16)
        dz_ref[:, 5 * dc:n] = dgl_ref[...]

        @pl.when(i == nt - 1)
        def _():
            dcaw_ref[0:ka, :] = jnp.sum(acc_a[...], axis=1)
            dcbw_ref[0:kb, :] = jnp.sum(acc_b[...], axis=1)

    def planes(taps):
        return pltpu.VMEM((len(_residues(taps)), HALO + tm, dc), F32)

    return _call(body, name=name, grid=(nt,),
                 in_specs=[_rows_rev(tm, dc, nt), _rows_rev(tm, dc, nt), _rows_rev(tm, 5 * dc, nt),
                           _rows_rev(tm, n - 5 * dc, nt), _rows_rev(tm, dc, nt), _rows_rev(tm, dc, nt),
                           _whole(caw.shape), _whole(lng.shape), _whole(lnb.shape), _whole(cbw.shape)],
                 out_specs=[_rows_rev(tm, n, nt), _whole((HALO, dc)), _whole((1, dc)), _whole((1, dc)),
                            _whole((1, dc)), _whole((SUBLANES, dc))],
                 out_shape=[_sds((t, n), BF16), _sds((HALO, dc), F32), _sds((1, dc), F32), _sds((1, dc), F32),
                            _sds((1, dc), F32), _sds((SUBLANES, dc), F32)],
                 scratch=[planes(bwd_a), planes(bwd_b),
                          pltpu.VMEM((ka, SUBLANES, dc), F32), pltpu.VMEM((kb, SUBLANES, dc), F32),
                          pltpu.VMEM((ka, SUBLANES, dc), F32), pltpu.VMEM((kb, SUBLANES, dc), F32),
                          pltpu.VMEM((tm, dc), F32)],
                 args=(dact, ds, z, dgl, a_conv, cb, caw, lng, lnb, cbw), carry=carry)


def _land_specs(depth, nr, tr, cols):
    def spec(k):
        return pl.BlockSpec((N_DEV, tr, cols), lambda i: (0, jnp.clip(i - k * nr, 0, nr - 1), 0))
    return [spec(k) for k in range(depth)]


def _adamw_math(w, g, m, v):
    nm = ADAM_B1 * m + (1.0 - ADAM_B1) * g
    nv = ADAM_B2 * v + (1.0 - ADAM_B2) * (g * g)
    m_hat = nm / (1.0 - ADAM_B1 ** ADAM_STEP)
    v_hat = nv / (1.0 - ADAM_B2 ** ADAM_STEP)
    return -ADAM_LR * (m_hat / (jnp.sqrt(v_hat) + ADAM_EPS) + ADAM_WD * w), nm, nv


def _sum_adamw(lands, w, m, v, name):
    _, rows, cols = lands[0].shape
    tr = _row_tile(rows)
    nr = rows // tr
    depth = len(lands)

    def body(*refs):
        w_ref, m_ref, v_ref, g_ref, d_ref, nm_ref, nv_ref = refs[depth:]
        i = pl.program_id(0)
        for k in range(depth):
            @pl.when(i // nr == k)
            def _(k=k):
                acc = refs[k][0].astype(F32)
                for j in range(1, N_DEV):
                    acc = acc + refs[k][j].astype(F32)
                g_ref[...] = acc
                d_ref[...], nm_ref[...], nv_ref[...] = _adamw_math(w_ref[...], acc, m_ref[...], v_ref[...])

    spec = _rows(tr, cols)
    return _call(body, name=name, grid=(depth * nr,), in_specs=_land_specs(depth, nr, tr, cols) + [spec] * 3,
                 out_specs=[spec] * 4, out_shape=[_sds((depth * rows, cols), F32)] * 4, args=(*lands, w, m, v))


def _adamw_small(ws, gs, ms, vs, name):
    n = len(ws)

    def body(*refs):
        w_refs, g_refs, m_refs, v_refs = refs[:n], refs[n:2 * n], refs[2 * n:3 * n], refs[3 * n:4 * n]
        d_refs, nm_refs, nv_refs = refs[4 * n:5 * n], refs[5 * n:6 * n], refs[6 * n:]
        for k in range(n):
            d_refs[k][...], nm_refs[k][...], nv_refs[k][...] = _adamw_math(
                w_refs[k][...], g_refs[k][...], m_refs[k][...], v_refs[k][...])

    vmem = pl.BlockSpec(memory_space=pltpu.VMEM)
    outs = pl.pallas_call(
        body, name=name, in_specs=[vmem] * (4 * n), out_specs=[vmem] * (3 * n),
        out_shape=[_sds(a.shape, F32) for a in ws] * 3)(*ws, *gs, *ms, *vs)
    return outs[:n], outs[n:2 * n], outs[2 * n:]


def kernel(x, p, g_mix, w_in, b_gate, conv_a_w, conv_a_b, ln_a_g, ln_a_b, w_a_out, conv_b_w, w_b_out, w_o, g_ffn, w_up, conv_f_w, conv_f_b, w_down, g_ple, w_ple, w_ple_gate, g_final, loss_target, m_g_mix, m_w_in, m_b_gate, m_conv_a_w, m_conv_a_b, m_ln_a_g, m_ln_a_b, m_w_a_out, m_conv_b_w, m_w_b_out, m_w_o, m_g_ffn, m_w_up, m_conv_f_w, m_conv_f_b, m_w_down, m_g_ple, m_w_ple, m_w_ple_gate, m_g_final, v_g_mix, v_w_in, v_b_gate, v_conv_a_w, v_conv_a_b, v_ln_a_g, v_ln_a_b, v_w_a_out, v_conv_b_w, v_w_b_out, v_w_o, v_g_ffn, v_w_up, v_conv_f_w, v_conv_f_b, v_w_down, v_g_ple, v_w_ple, v_w_ple_gate, v_g_final):
    w = dict(zip(WEIGHT_NAMES, (g_mix, w_in, b_gate, conv_a_w, conv_a_b, ln_a_g, ln_a_b, w_a_out, conv_b_w,
                                w_b_out, w_o, g_ffn, w_up, conv_f_w, conv_f_b, w_down, g_ple, w_ple,
                                w_ple_gate, g_final)))
    mom = dict(zip(WEIGHT_NAMES, (m_g_mix, m_w_in, m_b_gate, m_conv_a_w, m_conv_a_b, m_ln_a_g, m_ln_a_b,
                                  m_w_a_out, m_conv_b_w, m_w_b_out, m_w_o, m_g_ffn, m_w_up, m_conv_f_w,
                                  m_conv_f_b, m_w_down, m_g_ple, m_w_ple, m_w_ple_gate, m_g_final)))
    var = dict(zip(WEIGHT_NAMES, (v_g_mix, v_w_in, v_b_gate, v_conv_a_w, v_conv_a_b, v_ln_a_g, v_ln_a_b,
                                  v_w_a_out, v_conv_b_w, v_w_b_out, v_w_o, v_g_ffn, v_w_up, v_conv_f_w,
                                  v_conv_f_b, v_w_down, v_g_ple, v_w_ple, v_w_ple_gate, v_g_final)))
    depth = g_mix.shape[0]
    dc = ln_a_g.shape[1]
    me = _my_index()
    x0 = x[0]
    target = loss_target[0]
    big_names = tuple(BIG_AXIS)

    shard = {name: (jnp.swapaxes(w[name], 1, 2) if name in TRANSPOSED else w[name]).astype(BF16)
             for name in big_names}

    def gather_of(layer, *names):
        return _Gather([(shard[name], layer, BIG_AXIS[name]) for name in names])

    def row(name, layer):
        return _Layer(w[name], layer)

    first = _Gather([(shard['w_in'], 0, BIG_AXIS['w_in'])] + [(w[name][None], None, 0) for name in CONV_SHARDED])
    gathered = _run_exchange(first, "gather_first")
    w_in_full = gathered[0]
    conv_full = {name: jnp.transpose(g, (1, 2, 0, 3)).reshape(g.shape[1], g.shape[2], -1)
                 for name, g in zip(CONV_SHARDED, gathered[1:])}
    saved = []
    xc = x0
    below = None
    for l in range(depth):
        carry = gather_of(l, 'w_a_out', 'w_b_out', 'w_o', 'w_up')
        outs = _norm_matmul(xc, row('g_mix', l), w_in_full, f"fwd_in_{l}", carry, below)
        if below is not None:
            xc = outs[0]
        h, z = outs[-2:]
        wa_full, wb_full, wo_full, w_up_full = carry.results
        a_conv, a_act, s, cb = _fwd_branch(z, _Layer(conv_full['conv_a_w'], l), row('conv_a_b', l), row('ln_a_g', l),
                                       row('ln_a_b', l), _Layer(conv_full['conv_b_w'], l), dc, f"fwd_branch_{l}")
        x1 = _fwd_merge(xc, z, row('b_gate', l), a_act, s, wa_full, wb_full, wo_full, f"fwd_merge_{l}")
        carry = _Gather([(shard[name], l, BIG_AXIS[name]) for name in ('w_down', 'w_ple', 'w_ple_gate')]
                        + ([(shard['w_in'], l + 1, BIG_AXIS['w_in'])] if l + 1 < depth else []))
        h2, u = _norm_matmul(x1, row('g_ffn', l), w_up_full, f"fwd_up_{l}", carry)
        w_down_full, w_ple_full, w_pg_full = carry.results[:3]
        x2, act, gl, dgl = _fwd_down(x1, u, _Layer(conv_full['conv_f_w'], l), row('conv_f_b', l), w_down_full,
                                     f"fwd_down_{l}")
        saved.append((xc, h, z, a_conv, a_act, s, cb, x1, h2, u, act, gl, dgl, x2,
                      dict(w_in=w_in_full, w_a_out=wa_full, w_b_out=wb_full, w_o=wo_full, w_up=w_up_full,
                           w_down=w_down_full, w_ple=w_ple_full, w_ple_gate=w_pg_full)))
        if l + 1 < depth:
            w_in_full = carry.results[3]
            xc, below = x2, (row('g_ple', l), w_pg_full, _LayerTokens(p, l), w_ple_full)

    dx, dg_final, loss_part = _fwd_ple_loss(x2, row('g_ple', depth - 1), w_pg_full, _LayerTokens(p, depth - 1),
                                            w_ple_full, g_final[None], target, "fwd_ple_loss")
    above = None
    landed = {name: [None] * depth for name in big_names}
    small = {name: [None] * depth for name in WEIGHT_NAMES if name not in BIG_AXIS and name != 'g_final'}

    def scatter_of(*partials):
        ex = _Scatter([(part, BIG_AXIS[name]) for name, _, part in partials])
        ex.places = [(name, layer) for name, layer, _ in partials]
        return ex

    def keep(ex):
        for (name, layer), land in zip(ex.places, ex.results):
            landed[name][layer] = land

    pending = []
    for l in reversed(range(depth)):
        xin, h, z, a_conv, a_act, s, cb, x1, h2, u, act, gl, dgl, x2, full = saved[l]
        outs = _bwd_ple(dx, x2, row('g_ple', l), full['w_ple_gate'], _LayerTokens(p, l), full['w_ple'],
                        f"bwd_ple_{l}", above)
        dx2, d_wpg, d_wple, small['g_ple'][l] = outs[:4]
        if above is not None:
            small['g_mix'][l + 1] = outs[4]
        d_wdown = _wgrad_tn(act, dx2, f"wgrad_down_{l}")
        carry = scatter_of(('w_ple_gate', l, d_wpg), ('w_ple', l, d_wple), ('w_down', l, d_wdown))
        du, small['conv_f_w'][l], small['conv_f_b'][l] = _bwd_down(
            dx2, u, gl, dgl, _Layer(conv_full['conv_f_w'], l), full['w_down'], f"bwd_down_{l}", carry)
        keep(carry)
        carry = scatter_of(*pending) if pending else None
        pending = []
        dx1, small['g_ffn'][l] = _bwd_norm_matmul(du, full['w_up'], x1, row('g_ffn', l), dx2, f"bwd_up_{l}", carry)
        if carry is not None:
            keep(carry)
        d_wup = _wgrad_tn(du, h2, f"wgrad_up_{l}")
        dact, ds, dgate, d_wo, d_wa, d_wb, small['b_gate'][l] = _bwd_merge(
            dx1, z, row('b_gate', l), a_act, s, full['w_a_out'], full['w_b_out'], full['w_o'], f"bwd_merge_{l}")
        carry = scatter_of(('w_up', l, d_wup))
        (dz, small['conv_a_w'][l], small['conv_a_b'][l], small['ln_a_g'][l], small['ln_a_b'][l],
         small['conv_b_w'][l]) = _bwd_branch(
            dact, ds, z, dgate, a_conv, cb, _Layer(conv_full['conv_a_w'], l), row('ln_a_g', l), row('ln_a_b', l),
            _Layer(conv_full['conv_b_w'], l), f"bwd_branch_{l}", carry)
        keep(carry)
        carry = scatter_of(('w_o', l, d_wo), ('w_a_out', l, d_wa), ('w_b_out', l, d_wb))
        d_win = _wgrad_tn(dz, h, f"wgrad_in_{l}", carry)
        keep(carry)
        if l > 0:
            pending = [('w_in', l, d_win)]
            above, dx = (dz, full['w_in'], xin, row('g_mix', l), dx1), None
        else:
            carry = scatter_of(('w_in', l, d_win))
            dx, small['g_mix'][l] = _bwd_norm_matmul(dz, full['w_in'], xin, row('g_mix', l), dx1, f"bwd_in_{l}",
                                                     carry)
            keep(carry)
    grad_x = dx[None]

    small_names = tuple(small)
    parts = [part for name in small_names for part in small[name]] + [dg_final, loss_part]
    plan = [tuple(range(k * depth, (k + 1) * depth)) for k in range(len(small_names))]
    plan += [(len(parts) - 2,), (len(parts) - 1,)]
    reduced = _all_reduce(parts, plan, "all_reduce_small")
    loss = reduced[-1][0, 0]
    grads = dict(zip(small_names, reduced[:len(small_names)]))
    grads['g_final'] = reduced[len(small_names)].reshape(g_final.shape)
    for name in CONV_SHARDED:
        _, taps, width = w[name].shape
        grads[name] = lax.dynamic_slice(grads[name], (0, 0, me * width), (depth, taps, width))

    delta, new_m, new_v = {}, {}, {}
    for name in big_names:
        view = (lambda a: jnp.swapaxes(a, 1, 2)) if name in TRANSPOSED else (lambda a: a)
        shape = view(w[name]).shape
        flat = lambda a: view(a).reshape(-1, shape[-1])
        lands = [land.reshape(N_DEV, -1, shape[-1]) for land in landed[name]]
        outs = _sum_adamw(lands, flat(w[name]), flat(mom[name]), flat(var[name]), f"adamw_{name}")
        grads[name], delta[name], new_m[name], new_v[name] = [view(a.reshape(shape)) for a in outs]
    rest = tuple(name for name in WEIGHT_NAMES if name not in BIG_AXIS)
    as_2d = lambda a: a.reshape(1, -1) if a.ndim == 1 else a
    outs = _adamw_small(*[[as_2d(src[name]) for name in rest] for src in (w, grads, mom, var)], "adamw_small")
    for dst, values in zip((delta, new_m, new_v), outs):
        dst.update({name: value.reshape(w[name].shape) for name, value in zip(rest, values)})

    return (loss, grad_x, *[grads[n] for n in WEIGHT_NAMES], *[delta[n] for n in WEIGHT_NAMES],
            *[new_m[n] for n in WEIGHT_NAMES], *[new_v[n] for n in WEIGHT_NAMES])
```
